```python
import math
import jax
import jax.numpy as jnp
from jax import lax
import numpy as np

D_MODEL = 1024
BATCH = 16
SEQ = 4096
DEPTH = 4

GRID_W = 64
CTX_LEN = 256
N_MIXERS = 4

DN_ALPHA = (2 * DEPTH) ** 0.25
DN_BETA = (8 * DEPTH) ** -0.25
LN_EPS = 1e-6

D_RNN = (D_MODEL * 4 // 3) // 128 * 128
RG_BLOCK = 128
RG_BLOCKS = D_RNN // RG_BLOCK
RG_CONV = 4
LRU_C = 8.0

HY_ORDER = 2
HY_SHORT = 3
FILTER_EMB = 33
FILTER_HIDDEN = 64

DA_HEAD_DIM = 64
DA_HEADS = D_MODEL // (2 * DA_HEAD_DIM)
ATTN_BLOCK = 128
ROPE_THETA = 10000.0

S5_GROUP = 16
S5_GROUPS = D_MODEL // S5_GROUP
S5_STATE = 64

N_EXPERTS = 256
TOP_K = 8
N_EXPERT_GROUPS = 8
TOPK_GROUPS = 4
D_EXPERT = 256
ROUTED_SCALE = 2.5
MOE_BLOCK = 256

kernel_name = 'hybrid_rglru_hyena_diffattn_s5_moe_trunk'


def _count(kind):
    return len(range(kind, DEPTH, N_MIXERS))


def layer_norm(x, g, b):
    xf = x.astype(jnp.float32)
    mu = xf.mean(-1, keepdims=True)
    var = jnp.square(xf - mu).mean(-1, keepdims=True)
    return ((xf - mu) * lax.rsqrt(var + LN_EPS) * g + b).astype(x.dtype)


def dwconv(x, w, b, pad):
    y = lax.conv_general_dilated(x, w[:, None, :].astype(x.dtype), (1,), [pad],
                                 dimension_numbers=('NWC', 'WIO', 'NWC'),
                                 feature_group_count=x.shape[-1])
    return y + b


def lin_comb(e1, e2):
    a1, b1 = e1
    a2, b2 = e2
    return a1 * a2, a2 * b1 + b2


def assoc_linear_scan(a, b, axis, reverse):
    return lax.associative_scan(lin_comb, (a, b), reverse=reverse, axis=axis)[1]


def swiglu(x, w_gu, w_down):
    g, u = jnp.split(x @ w_gu, 2, -1)
    return (jax.nn.silu(g) * u) @ w_down


def lru_coeffs(x, gate_w, gate_b, lam):
    B, L, _ = x.shape
    f32 = jnp.float32
    xf = x.astype(f32)
    xb = xf.reshape(B, L, RG_BLOCKS, RG_BLOCK)
    gates = jnp.einsum('blni,gnij->gblnj', xb, gate_w.astype(f32)).reshape(2, B, L, D_RNN)
    gates = jax.nn.sigmoid(gates + gate_b.astype(f32)[:, None, None, :])
    log_a = -LRU_C * gates[0] * jax.nn.softplus(-lam.astype(f32))
    a = jnp.exp(log_a)
    b = jnp.sqrt(-jnp.expm1(2.0 * log_a)) * gates[1] * xf
    return a, b


def lru_direction(rc, rl, gate_w, gate_b, lam, reverse):
    ac, bc = lru_coeffs(rc, gate_w, gate_b, lam)
    hc = assoc_linear_scan(ac, bc, 1, reverse)
    h0 = hc[:, 0] if reverse else hc[:, -1]
    al, bl = lru_coeffs(rl, gate_w, gate_b, lam)
    e = rl.shape[1] - 1 if reverse else 0
    bl = bl.at[:, e].add(al[:, e] * h0)
    hl = assoc_linear_scan(al, bl, 1, reverse)
    return hc, hl


def rglru_mixer(hc, hl, w_in, conv_w, conv_b, gate_w, gate_b, lam, w_out, ctx_out):
    pad = (RG_CONV // 2, RG_CONV - 1 - RG_CONV // 2)

    def branches(h):
        g, r = jnp.split(h @ w_in, 2, -1)
        return jax.nn.gelu(g), dwconv(r, conv_w, conv_b, pad)

    gc, rc = branches(hc)
    gl, rl = branches(hl)
    sum_c = 0.0
    sum_l = 0.0
    for d in range(2):
        h_c, h_l = lru_direction(rc, rl, gate_w[d], gate_b[d], lam[d], d == 1)
        sum_l = sum_l + h_l
        if ctx_out:
            sum_c = sum_c + h_c
    yl = (gl * sum_l).astype(hl.dtype) @ w_out
    yc = (gc * sum_c).astype(hc.dtype) @ w_out if ctx_out else None
    return yc, yl


def hyena_filter_spectrum(L, w1, b1, w2, b2, w3, b3, w4, freq, decay):
    f32 = jnp.float32
    t = jnp.arange(L, dtype=f32)
    t_norm = t / max(L - 1, 1)
    bands = (FILTER_EMB - 1) // 2
    fr = jnp.linspace(1e-4, bands - 1, bands, dtype=f32)
    ang = (2.0 * math.pi / L) * t[:, None] * fr[None, :]
    z = jnp.concatenate([t_norm[:, None], jnp.cos(ang), -jnp.sin(ang)], -1)
    fq = freq.astype(f32)
    h = jnp.sin(fq * (z @ w1.astype(f32) + b1.astype(f32)))
    h = jnp.sin(fq * (h @ w2.astype(f32) + b2.astype(f32)))
    h = jnp.sin(fq * (h @ w3.astype(f32) + b3.astype(f32)))
    h = (h @ w4.astype(f32)).reshape(L, 2, HY_ORDER, -1)
    h = h * jnp.exp(-t_norm[:, None, None, None] * jnp.abs(decay.astype(f32)))
    k = jnp.concatenate([h[:, 0], jnp.zeros_like(h[:1, 0]), h[:0:-1, 1]], 0)
    k = k * lax.rsqrt(jnp.sum(k * k, 0, keepdims=True) + 1e-6)
    return jnp.fft.rfft(k, axis=0)


def fftconv(z, kf, bias):
    L = z.shape[1]
    y = jnp.fft.irfft(jnp.fft.rfft(z, n=2 * L, axis=1) * kf, n=2 * L, axis=1)[:, :L]
    return y + z * bias


def hyena_seq(h, w_in, b_in, short_w, short_b, fw1, fb1, fw2, fb2, fw3, fb3, fw4,
              ffreq, fdecay, fbias, w_out, b_out):
    f32 = jnp.float32
    L = h.shape[1]
    half = (HY_SHORT - 1) // 2
    u = dwconv(h @ w_in + b_in, short_w, short_b, (half, half))
    v, x1, x2 = jnp.split(u.astype(f32), 3, -1)
    kf = hyena_filter_spectrum(L, fw1, fb1, fw2, fb2, fw3, fb3, fw4, ffreq, fdecay)
    fb = fbias.astype(f32)
    z = x1 * fftconv(v, kf[:, 0], fb[0])
    z = x2 * fftconv(z, kf[:, 1], fb[1])
    return z.astype(h.dtype) @ w_out + b_out


def axial_rope_tables(seq_len):
    t = jnp.arange(seq_len)
    row = (t // GRID_W).astype(jnp.float32)
    col = (t % GRID_W).astype(jnp.float32)
    axis_dim = DA_HEAD_DIM // 2
    inv = ROPE_THETA ** (-jnp.arange(0, axis_dim, 2, dtype=jnp.float32) / axis_dim)
    ang_r = row[:, None] * inv
    ang_c = col[:, None] * inv
    return jnp.cos(ang_r), jnp.sin(ang_r), jnp.cos(ang_c), jnp.sin(ang_c)


def _rotate(x, cos, sin):
    h = x.shape[-1] // 2
    x1, x2 = x[..., :h], x[..., h:]
    return jnp.concatenate([x1 * cos - x2 * sin, x1 * sin + x2 * cos], -1)


def apply_axial_rope(x, tables):
    cr, sr, cc, sc = [t[None, :, None, None, :].astype(x.dtype) for t in tables]
    h = x.shape[-1] // 2
    return jnp.concatenate([_rotate(x[..., :h], cr, sr), _rotate(x[..., h:], cc, sc)], -1)


def diff_attend(q, k, v, lam, subln_w, lam_init):
    s = jnp.einsum('bqhcd,bkhcd->bhcqk', q, k, preferred_element_type=jnp.float32) * (DA_HEAD_DIM ** -0.5)
    p = jax.nn.softmax(s, axis=-1)
    a = p[:, :, 0] - lam * p[:, :, 1]
    o = jnp.einsum('bhqk,bkhe->bqhe', a, v.astype(jnp.float32))
    o = o * lax.rsqrt(jnp.mean(o * o, -1, keepdims=True) + 1e-5) * subln_w.astype(jnp.float32) * (1.0 - lam_init)
    return o.astype(q.dtype)


def diff_attention_mixer(hc, hl, w_in, lam_p, subln_w, w_out, layer_idx, rope, ctx_out):
    B, S, _ = hl.shape

    def proj(h):
        L = h.shape[1]
        q, k, v = jnp.split(h @ w_in, 3, -1)
        return (q.reshape(B, L, DA_HEADS, 2, DA_HEAD_DIM),
                k.reshape(B, L, DA_HEADS, 2, DA_HEAD_DIM),
                v.reshape(B, L, DA_HEADS, 2 * DA_HEAD_DIM))

    qc, kc, vc = proj(hc)
    ql, kl, vl = proj(hl)
    ql = apply_axial_rope(ql, rope)
    kl = apply_axial_rope(kl, rope)
    lam_init = 0.8 - 0.6 * math.exp(-0.3 * layer_idx)
    lp = lam_p.astype(jnp.float32)
    lam = jnp.exp(jnp.sum(lp[0] * lp[1])) - jnp.exp(jnp.sum(lp[2] * lp[3])) + lam_init
    k_all = jnp.concatenate([kc, kl], 1)
    v_all = jnp.concatenate([vc, vl], 1)
    nb = S // ATTN_BLOCK
    qb = jnp.moveaxis(ql.reshape(B, nb, ATTN_BLOCK, DA_HEADS, 2, DA_HEAD_DIM), 1, 0)
    ob = lax.map(lambda qi: diff_attend(qi, k_all, v_all, lam, subln_w, lam_init), qb)
    yl = jnp.moveaxis(ob, 0, 1).reshape(B, S, -1) @ w_out
    yc = diff_attend(qc, kc, vc, lam, subln_w, lam_init).reshape(B, hc.shape[1], -1) @ w_out if ctx_out else None
    return yc, yl


def s5_mixer(hc, hl, a_re, a_im, log_step, b_re, b_im, c_re, c_im, d_skip, w_glu, b_glu, ctx_out):
    f32 = jnp.float32
    lam = lax.complex(jnp.minimum(a_re.astype(f32), -1e-4), a_im.astype(f32))
    step = jnp.exp(log_step.astype(f32))[..., None]
    abar = jnp.exp(lam * step)
    bbar = ((abar - 1.0) / lam)[..., None] * lax.complex(b_re.astype(f32), b_im.astype(f32))
    cmat = lax.complex(c_re.astype(f32), c_im.astype(f32))

    def ssm_states(u, d, s0=None):
        L = u.shape[0]
        ug = u.reshape(L, S5_GROUPS, S5_GROUP).astype(jnp.complex64)
        bu = jnp.einsum('lgh,gph->lgp', ug, bbar[d])
        a = jnp.broadcast_to(abar[d], bu.shape)
        rev = d == 1
        if s0 is not None:
            e = L - 1 if rev else 0
            bu = bu.at[e].add(abar[d] * s0)
        return assoc_linear_scan(a, bu, 0, rev)

    def readout(s, d):
        return jnp.einsum('lgp,ghp->lgh', s, cmat[d]).real.reshape(s.shape[0], -1)

    def per_sample(args):
        uc, ul = args
        yc = 0.0
        yl = 0.0
        for d in range(2):
            sc = ssm_states(uc, d)
            sl = ssm_states(ul, d, sc[0] if d == 1 else sc[-1])
            yl = yl + readout(sl, d)
            if ctx_out:
                yc = yc + readout(sc, d)
        return (yc, yl) if ctx_out else yl

    uc = hc.astype(f32)
    ul = hl.astype(f32)
    out = lax.map(per_sample, (uc, ul))

    def glu(u, y):
        g = jax.nn.gelu(y + d_skip.astype(f32) * u)
        val, gate = jnp.split(g.astype(w_glu.dtype) @ w_glu + b_glu, 2, -1)
        return val * jax.nn.sigmoid(gate)

    if ctx_out:
        return glu(uc, out[0]).astype(hc.dtype), glu(ul, out[1]).astype(hl.dtype)
    return None, glu(ul, out).astype(hl.dtype)


def route(xt, w_router, bias):
    T = xt.shape[0]
    scores = jax.nn.sigmoid((xt @ w_router).astype(jnp.float32))
    biased = scores + bias.astype(jnp.float32)
    grp = biased.reshape(T, N_EXPERT_GROUPS, N_EXPERTS // N_EXPERT_GROUPS)
    grp_score = lax.top_k(grp, 2)[0].sum(-1)
    _, gidx = lax.top_k(grp_score, TOPK_GROUPS)
    gmask = (gidx[:, :, None] == jnp.arange(N_EXPERT_GROUPS)).any(1)
    emask = jnp.repeat(gmask, N_EXPERTS // N_EXPERT_GROUPS, axis=1)
    _, idx = lax.top_k(jnp.where(emask, biased, -jnp.inf), TOP_K)
    w = jnp.take_along_axis(scores, idx, 1)
    w = w / (w.sum(-1, keepdims=True) + 1e-20) * ROUTED_SCALE
    return idx, w


def moe_ffn(xt, w_router, bias, w_gu, w_down, sh_gu, sh_down):
    T, D = xt.shape
    idx, w = route(xt, w_router, bias)
    n_assign = T * TOP_K
    n_blocks = -(-(n_assign + N_EXPERTS * (MOE_BLOCK - 1)) // MOE_BLOCK)
    n_rows = n_blocks * MOE_BLOCK
    flat_e = idx.reshape(-1)
    order = jnp.argsort(flat_e)
    se = flat_e[order]
    counts = jnp.bincount(flat_e, length=N_EXPERTS)
    starts = jnp.cumsum(counts) - counts
    pcounts = (counts + MOE_BLOCK - 1) // MOE_BLOCK * MOE_BLOCK
    pends = jnp.cumsum(pcounts)
    dest = pends[se] - pcounts[se] + jnp.arange(n_assign) - starts[se]
    row_tok = jnp.full((n_rows,), T, jnp.int32).at[dest].set((order // TOP_K).astype(jnp.int32))
    row_w = jnp.zeros((n_rows,), xt.dtype).at[dest].set(w.reshape(-1)[order].astype(xt.dtype))
    blk_e = jnp.minimum(jnp.searchsorted(pends, jnp.arange(n_blocks) * MOE_BLOCK, side='right'), N_EXPERTS - 1)
    x_pad = jnp.concatenate([xt, jnp.zeros((1, D), xt.dtype)], 0)

    def step(acc, blk):
        rows, wts, e = blk
        yb = swiglu(x_pad[rows], w_gu[e], w_down[e]) * wts[:, None]
        return acc.at[rows].add(yb), None

    acc, _ = lax.scan(step, jnp.zeros_like(x_pad),
                      (row_tok.reshape(n_blocks, MOE_BLOCK), row_w.reshape(n_blocks, MOE_BLOCK), blk_e))
    return swiglu(xt, sh_gu, sh_down) + acc[:T]


def setup_inputs(seed: int = 0) -> dict:
    key = jax.random.key(seed)
    ks = iter(jax.random.split(key, 80))
    f32 = jnp.float32

    def nrm(shape, std):
        return std * jax.random.normal(next(ks), shape, f32)

    D = D_MODEL
    F = D_EXPERT
    n_a, n_b, n_c, n_d = (_count(m) for m in range(N_MIXERS))
    x = nrm((BATCH, SEQ, D), 1.0)
    c = nrm((BATCH, D), 1.0)
    ctx = nrm((BATCH, CTX_LEN, D), 1.0)
    c_ctx = nrm((D,), 1.0)
    mod_w = nrm((DEPTH, D, 6 * D), 0.5 * D ** -0.5)
    mod_b = nrm((DEPTH, 6 * D), 0.02)
    ln_g = 1.0 + nrm((DEPTH, 2, D), 0.02)
    ln_b = nrm((DEPTH, 2, D), 0.02)
    rg_w_in = nrm((n_a, D, 2 * D_RNN), D ** -0.5)
    rg_conv_w = nrm((n_a, RG_CONV, D_RNN), RG_CONV ** -0.5)
    rg_conv_b = nrm((n_a, D_RNN), 0.02)
    rg_gate_w = nrm((n_a, 2, 2, RG_BLOCKS, RG_BLOCK, RG_BLOCK), RG_BLOCK ** -0.5)
    rg_gate_b = nrm((n_a, 2, 2, D_RNN), 0.02)
    lru_p = jax.random.uniform(next(ks), (n_a, 2, D_RNN), f32, minval=0.9, maxval=0.999) ** (1.0 / LRU_C)
    rg_lam = jnp.log(lru_p) - jnp.log1p(-lru_p)
    rg_w_out = nrm((n_a, D_RNN, D), DN_BETA * D_RNN ** -0.5)
    hy_w_in = nrm((n_b, D, 3 * D), D ** -0.5)
    hy_b_in = nrm((n_b, 3 * D), 0.02)
    hy_short_w = nrm((n_b, HY_SHORT, 3 * D), HY_SHORT ** -0.5)
    hy_short_b = nrm((n_b, 3 * D), 0.02)
    hy_f_w1 = nrm((n_b, FILTER_EMB, FILTER_HIDDEN), FILTER_EMB ** -0.5)
    hy_f_b1 = nrm((n_b, FILTER_HIDDEN), 0.1)
    hy_f_w2 = nrm((n_b, FILTER_HIDDEN, FILTER_HIDDEN), FILTER_HIDDEN ** -0.5)
    hy_f_b2 = nrm((n_b, FILTER_HIDDEN), 0.1)
    hy_f_w3 = nrm((n_b, FILTER_HIDDEN, FILTER_HIDDEN), FILTER_HIDDEN ** -0.5)
    hy_f_b3 = nrm((n_b, FILTER_HIDDEN), 0.1)
    hy_f_w4 = nrm((n_b, FILTER_HIDDEN, 2 * HY_ORDER * D), FILTER_HIDDEN ** -0.5)
    hy_f_freq = 1.0 + nrm((n_b, FILTER_HIDDEN), 0.02)
    decay0 = jnp.linspace(abs(math.log(1e-2) / 1.5), abs(math.log(1e-2) / 0.3), D, dtype=f32)
    hy_f_decay = decay0 * (1.0 + nrm((n_b, 2, HY_ORDER, D), 0.02))
    hy_f_bias = nrm((n_b, HY_ORDER, D), 0.5)
    hy_w_out = nrm((n_b, D, D), DN_BETA * D ** -0.5)
    hy_b_out = nrm((n_b, D), 0.02)
    da_w_in = nrm((n_c, D, 3 * D), D ** -0.5)
    da_lam = nrm((n_c, 4, DA_HEAD_DIM), 0.1)
    da_subln = 1.0 + nrm((n_c, 2 * DA_HEAD_DIM), 0.02)
    da_w_out = nrm((n_c, D, D), DN_BETA * D ** -0.5)
    s5_a_re = -0.5 + nrm((n_d, 2, S5_GROUPS, S5_STATE), 0.01)
    s5_a_im = math.pi * jnp.arange(S5_STATE, dtype=f32) + nrm((n_d, 2, S5_GROUPS, S5_STATE), 0.01)
    s5_log_step = jax.random.uniform(next(ks), (n_d, 2, S5_GROUPS), f32,
                                     minval=math.log(1e-3), maxval=math.log(1e-1))
    s5_b_re = nrm((n_d, 2, S5_GROUPS, S5_STATE, S5_GROUP), (S5_GROUP ** -0.5) / math.sqrt(2.0))
    s5_b_im = nrm((n_d, 2, S5_GROUPS, S5_STATE, S5_GROUP), (S5_GROUP ** -0.5) / math.sqrt(2.0))
    s5_c_re = nrm((n_d, 2, S5_GROUPS, S5_GROUP, S5_STATE), (S5_STATE ** -0.5) / math.sqrt(2.0))
    s5_c_im = nrm((n_d, 2, S5_GROUPS, S5_GROUP, S5_STATE), (S5_STATE ** -0.5) / math.sqrt(2.0))
    s5_d = nrm((n_d, D), 1.0)
    s5_w_glu = jnp.concatenate([nrm((n_d, D, D), DN_BETA * D ** -0.5), nrm((n_d, D, D), D ** -0.5)], -1)
    s5_b_glu = nrm((n_d, 2 * D), 0.02)
    moe_w_router = nrm((DEPTH, D, N_EXPERTS), D ** -0.5)
    moe_bias = nrm((DEPTH, N_EXPERTS), 0.01)
    moe_w_gu = nrm((DEPTH, N_EXPERTS, D, 2 * F), D ** -0.5)
    moe_w_down = nrm((DEPTH, N_EXPERTS, F, D), DN_BETA * F ** -0.5)
    moe_sh_gu = nrm((DEPTH, D, 2 * F), D ** -0.5)
    moe_sh_down = nrm((DEPTH, F, D), DN_BETA * F ** -0.5)
    return {
        'x': x, 'c': c, 'ctx': ctx, 'c_ctx': c_ctx,
        'mod_w': mod_w, 'mod_b': mod_b, 'ln_g': ln_g, 'ln_b': ln_b,
        'rg_w_in': rg_w_in, 'rg_conv_w': rg_conv_w, 'rg_conv_b': rg_conv_b, 'rg_gate_w': rg_gate_w,
        'rg_gate_b': rg_gate_b, 'rg_lam': rg_lam, 'rg_w_out': rg_w_out,
        'hy_w_in': hy_w_in, 'hy_b_in': hy_b_in, 'hy_short_w': hy_short_w, 'hy_short_b': hy_short_b,
        'hy_f_w1': hy_f_w1, 'hy_f_b1': hy_f_b1, 'hy_f_w2': hy_f_w2, 'hy_f_b2': hy_f_b2,
        'hy_f_w3': hy_f_w3, 'hy_f_b3': hy_f_b3, 'hy_f_w4': hy_f_w4, 'hy_f_freq': hy_f_freq,
        'hy_f_decay': hy_f_decay, 'hy_f_bias': hy_f_bias, 'hy_w_out': hy_w_out, 'hy_b_out': hy_b_out,
        'da_w_in': da_w_in, 'da_lam': da_lam, 'da_subln': da_subln, 'da_w_out': da_w_out,
        's5_a_re': s5_a_re, 's5_a_im': s5_a_im, 's5_log_step': s5_log_step, 's5_b_re': s5_b_re,
        's5_b_im': s5_b_im, 's5_c_re': s5_c_re, 's5_c_im': s5_c_im, 's5_d': s5_d,
        's5_w_glu': s5_w_glu, 's5_b_glu': s5_b_glu,
        'moe_w_router': moe_w_router, 'moe_bias': moe_bias, 'moe_w_gu': moe_w_gu,
        'moe_w_down': moe_w_down, 'moe_sh_gu': moe_sh_gu, 'moe_sh_down': moe_sh_down,
    }


def reference(x, c, ctx, c_ctx, mod_w, mod_b, ln_g, ln_b,
              rg_w_in, rg_conv_w, rg_conv_b, rg_gate_w, rg_gate_b, rg_lam, rg_w_out,
              hy_w_in, hy_b_in, hy_short_w, hy_short_b, hy_f_w1, hy_f_b1, hy_f_w2, hy_f_b2,
              hy_f_w3, hy_f_b3, hy_f_w4, hy_f_freq, hy_f_decay, hy_f_bias, hy_w_out, hy_b_out,
              da_w_in, da_lam, da_subln, da_w_out,
              s5_a_re, s5_a_im, s5_log_step, s5_b_re, s5_b_im, s5_c_re, s5_c_im, s5_d,
              s5_w_glu, s5_b_glu,
              moe_w_router, moe_bias, moe_w_gu, moe_w_down, moe_sh_gu, moe_sh_down):
    B, S, D = x.shape
    silu_c = jax.nn.silu(c)
    silu_cc = jax.nn.silu(c_ctx)
    xl, xc = x, ctx
    for i in range(DEPTH):
        kind, j = i % N_MIXERS, i // N_MIXERS
        keep_ctx = i < DEPTH - 1
        ml = jnp.split((silu_c @ mod_w[i] + mod_b[i])[:, None, :], 6, -1)
        mc = jnp.split(silu_cc @ mod_w[i] + mod_b[i], 6, -1)
        hl = xl * (1 + ml[1]) + ml[0]
        hc = xc * (1 + mc[1]) + mc[0]
        if kind == 0:
            yc, yl = rglru_mixer(hc, hl, rg_w_in[j], rg_conv_w[j], rg_conv_b[j], rg_gate_w[j],
                                 rg_gate_b[j], rg_lam[j], rg_w_out[j], keep_ctx)
        elif kind == 1:
            hy = [p[j] for p in (hy_w_in, hy_b_in, hy_short_w, hy_short_b, hy_f_w1, hy_f_b1,
                                 hy_f_w2, hy_f_b2, hy_f_w3, hy_f_b3, hy_f_w4, hy_f_freq,
                                 hy_f_decay, hy_f_bias, hy_w_out, hy_b_out)]
            yl = hyena_seq(hl, *hy)
            yc = hyena_seq(hc, *hy) if keep_ctx else None
        elif kind == 2:
            yc, yl = diff_attention_mixer(hc, hl, da_w_in[j], da_lam[j], da_subln[j], da_w_out[j],
                                          i, axial_rope_tables(S), keep_ctx)
        else:
            yc, yl = s5_mixer(hc, hl, s5_a_re[j], s5_a_im[j], s5_log_step[j], s5_b_re[j], s5_b_im[j],
                              s5_c_re[j], s5_c_im[j], s5_d[j], s5_w_glu[j], s5_b_glu[j], keep_ctx)
        xl = layer_norm(DN_ALPHA * xl + ml[2] * yl, ln_g[i, 0], ln_b[i, 0])
        fl = (xl * (1 + ml[4]) + ml[3]).reshape(-1, D)
        moe_p = (moe_w_router[i], moe_bias[i], moe_w_gu[i], moe_w_down[i], moe_sh_gu[i], moe_sh_down[i])
        if keep_ctx:
            xc = layer_norm(DN_ALPHA * xc + mc[2] * yc, ln_g[i, 0], ln_b[i, 0])
            fc = (xc * (1 + mc[4]) + mc[3]).reshape(-1, D)
            f = moe_ffn(jnp.concatenate([fl, fc], 0), *moe_p)
            xc = layer_norm(DN_ALPHA * xc + mc[5] * f[B * S:].reshape(xc.shape), ln_g[i, 1], ln_b[i, 1])
            f = f[:B * S]
        else:
            f = moe_ffn(fl, *moe_p)
        xl = layer_norm(DN_ALPHA * xl + ml[5] * f.reshape(xl.shape), ln_g[i, 1], ln_b[i, 1])
    return xl
```

```python
import functools
import math

import jax
import jax.numpy as jnp
from jax import lax
from jax.experimental import pallas as pl
from jax.experimental.pallas import tpu as pltpu

F32 = jnp.float32
BF16 = jnp.bfloat16
HIGHEST = lax.Precision.HIGHEST

N_MIXERS = 4
LN_EPS = 1e-6
LRU_C = 8.0
RG_BLOCK = 128
GRID_W = 64
DA_HEAD_DIM = 64
ROPE_THETA = 10000.0
S5_GROUP = 16
S5_CHUNK = 16
TOP_K = 8
N_EXPERT_GROUPS = 8
TOPK_GROUPS = 4
ROUTED_SCALE = 2.5
MOE_BLOCK = 256

LANES = 128
SUBLANES = 8
VMEM_LIMIT = 56 * 1024 * 1024


def _cp(n_grid):
    return pltpu.CompilerParams(dimension_semantics=("arbitrary",) * n_grid,
                                vmem_limit_bytes=VMEM_LIMIT)


def _silu(x):
    return x * jax.nn.sigmoid(x)


def _mod_table_kernel(c_ref, w_ref, b_ref, o_ref):
    s = _silu(c_ref[...])
    o_ref[...] = jnp.dot(s, w_ref[...], precision=HIGHEST, preferred_element_type=F32) + b_ref[...]


def _mod_table(cc, mod_w, mod_b):
    depth, D, _ = mod_w.shape
    R = cc.shape[0]
    return pl.pallas_call(
        _mod_table_kernel,
        grid=(depth, 6),
        in_specs=[pl.BlockSpec((R, D), lambda i, k: (0, 0)),
                  pl.BlockSpec((None, D, D), lambda i, k: (i, 0, k)),
                  pl.BlockSpec((None, None, 1, D), lambda i, k: (i, k, 0, 0))],
        out_specs=pl.BlockSpec((None, None, R, D), lambda i, k: (i, k, 0, 0)),
        out_shape=jax.ShapeDtypeStruct((depth, 6, R, D), F32),
        compiler_params=_cp(2), name="mod_table",
    )(cc, mod_w, mod_b.reshape(depth, 6, 1, D))


class _Geo:
    def __init__(self, B, S, C, D):
        self.B, self.S, self.C, self.D = B, S, C, D
        self.P = S + C
        self.TM = math.gcd(S, C)
        while self.TM > 256:
            self.TM //= 2
        self.nt = self.P // self.TM
        self.n_lat = S // self.TM

    def mod_spec(self, k, width=None, col=None):
        D, B, n_lat = self.D, self.B, self.n_lat
        width = D if width is None else width
        if col is None:
            return pl.BlockSpec((None, None, 1, width),
                                lambda b, i, *_: (k, jnp.where(i < n_lat, b, B), 0, 0))
        return pl.BlockSpec((None, None, 1, width),
                            lambda b, i, j, *_: (k, jnp.where(i < n_lat, b, B), 0, col(j)))

    def row_spec(self, width, col=0):
        return pl.BlockSpec((1, self.TM, width), lambda b, i, *_: (b, i, col))


def _full_spec(shape):
    nd = len(shape)
    return pl.BlockSpec(shape, lambda *_: (0,) * nd)


def _layer_norm(z, g, b):
    mu = jnp.mean(z, axis=-1, keepdims=True)
    zc = z - mu
    var = jnp.mean(zc * zc, axis=-1, keepdims=True)
    return zc * lax.rsqrt(var + LN_EPS) * g + b


def _finish_mixer(y, x, gate, lng, lnb, sh, sc, wr_ref, xm_ref, f_ref, s_ref, alpha):
    xn = _layer_norm(alpha * x + gate * y, lng, lnb)
    xm_ref[0] = xn
    f = xn * (1.0 + sc) + sh
    f_ref[0] = f.astype(f_ref.dtype)
    logits = jnp.dot(f, wr_ref[...], precision=HIGHEST, preferred_element_type=F32)
    s_ref[0] = jax.nn.sigmoid(logits)


def _post_kernel(*refs, alpha, has_bias):
    if has_bias:
        (y_ref, w_ref, b_ref, x_ref, gate_ref, lng_ref, lnb_ref, sh_ref, sc_ref, wr_ref,
         xm_ref, f_ref, s_ref) = refs
    else:
        (y_ref, w_ref, x_ref, gate_ref, lng_ref, lnb_ref, sh_ref, sc_ref, wr_ref,
         xm_ref, f_ref, s_ref) = refs
    y = jnp.dot(y_ref[0].astype(BF16), w_ref[...], preferred_element_type=F32)
    if has_bias:
        y = y + b_ref[...]
    _finish_mixer(y, x_ref[0], gate_ref[...], lng_ref[...], lnb_ref[...], sh_ref[...], sc_ref[...],
                  wr_ref, xm_ref, f_ref, s_ref, alpha)


def _post_outs(geo, E):
    B, P, D = geo.B, geo.P, geo.D
    out_specs = [geo.row_spec(D), geo.row_spec(D), geo.row_spec(E)]
    out_shape = [jax.ShapeDtypeStruct((B, P, D), F32), jax.ShapeDtypeStruct((B, P, D), BF16),
                 jax.ShapeDtypeStruct((B, P, E), F32)]
    return out_specs, out_shape


def _post_mixer(geo, y, w_out, b_out, xs, modl, lng, lnb, w_router, alpha):
    D = geo.D
    Kd = y.shape[-1]
    E = w_router.shape[-1]
    has_bias = b_out is not None
    ins = [y, w_out.astype(BF16)]
    specs = [geo.row_spec(Kd), _full_spec((Kd, D))]
    if has_bias:
        ins.append(b_out.reshape(1, D))
        specs.append(_full_spec((1, D)))
    ins += [xs, modl, lng.reshape(1, D), lnb.reshape(1, D), modl, modl, w_router]
    specs += [geo.row_spec(D), geo.mod_spec(2), _full_spec((1, D)), _full_spec((1, D)),
              geo.mod_spec(3), geo.mod_spec(4), _full_spec((D, E))]
    out_specs, out_shape = _post_outs(geo, E)
    return pl.pallas_call(
        functools.partial(_post_kernel, alpha=alpha, has_bias=has_bias),
        grid=(geo.B, geo.nt), in_specs=specs, out_specs=out_specs, out_shape=out_shape,
        compiler_params=_cp(2), name="post_mixer",
    )(*ins)


def _dwconv_seg(r, cw, cb, S, C, lo):
    P = S + C
    row = lax.broadcasted_iota(jnp.int32, r.shape, 0)
    tl = jnp.where(row < S, row, row - S)
    sl = jnp.where(row < S, S, C)
    acc = jnp.zeros_like(r) + cb
    for k in range(cw.shape[0]):
        off = k - lo
        if off == 0:
            term = r
        else:
            shifted = pltpu.roll(r, (-off) % P, 0)
            valid = jnp.logical_and(tl + off >= 0, tl + off < sl)
            term = jnp.where(valid, shifted, 0.0)
        acc = acc + cw[k:k + 1, :] * term
    return acc


def _rg_in_kernel(x_ref, sh_ref, sc_ref, wg_ref, wr_ref, g_ref, r_ref):
    h = (x_ref[0] * (1.0 + sc_ref[...]) + sh_ref[...]).astype(BF16)
    g = jnp.dot(h, wg_ref[...], preferred_element_type=F32)
    g_ref[0] = jax.nn.gelu(g).astype(g_ref.dtype)
    r_ref[0] = jnp.dot(h, wr_ref[...], preferred_element_type=F32)


def _rg_scan_kernel(r_ref, g_ref, cw_ref, cb_ref, gw_ref, gb_ref, lam_ref, o_ref,
                    a_scr, b_scr, h_scr, *, S, C):
    P = S + C
    n = r_ref.shape[-1]
    r = r_ref[0]
    rc = _dwconv_seg(r, cw_ref[...], cb_ref[...], S, C, cw_ref.shape[0] // 2)
    rcb = rc.astype(BF16)
    row = lax.broadcasted_iota(jnp.int32, (P, n), 0)
    sub = jnp.bitwise_and(row, SUBLANES - 1)
    for d in range(2):
        rev = d == 1
        gr = jax.nn.sigmoid(jnp.dot(rcb, gw_ref[d, 0], preferred_element_type=F32) + gb_ref[d, 0])
        gi = jax.nn.sigmoid(jnp.dot(rcb, gw_ref[d, 1], preferred_element_type=F32) + gb_ref[d, 1])
        nl = -lam_ref[d]
        sp = jnp.maximum(nl, 0.0) + jnp.log1p(jnp.exp(-jnp.abs(nl)))
        a = jnp.exp(-LRU_C * gr * sp)
        bb = jnp.sqrt(1.0 - a * a) * gi * rc
        for s in (1, 2, 4):
            shift = (P - s) if rev else s
            a_sh = pltpu.roll(a, shift, 0)
            b_sh = pltpu.roll(bb, shift, 0)
            m = (sub < SUBLANES - s) if rev else (sub >= s)
            bb = jnp.where(m, a * b_sh + bb, bb)
            a = jnp.where(m, a * a_sh, a)
        a_scr[...] = a
        b_scr[...] = bb

        def chain(lo_tile, n_tiles, c0):
            def body(i, c):
                t = (lo_tile + n_tiles - 1 - i) if rev else (lo_tile + i)
                off = pl.multiple_of(t * SUBLANES, SUBLANES)
                h = b_scr[pl.ds(off, SUBLANES), :] + a_scr[pl.ds(off, SUBLANES), :] * c
                if d == 0:
                    h_scr[pl.ds(off, SUBLANES), :] = h
                else:
                    h_scr[pl.ds(off, SUBLANES), :] = h_scr[pl.ds(off, SUBLANES), :] + h
                edge = h[0:1] if rev else h[SUBLANES - 1:SUBLANES]
                return jnp.broadcast_to(edge, (SUBLANES, n))
            return lax.fori_loop(0, n_tiles, body, c0)

        c_ctx = chain(S // SUBLANES, C // SUBLANES, jnp.zeros((SUBLANES, n), F32))
        chain(0, S // SUBLANES, c_ctx)
    o_ref[0] = (g_ref[0].astype(F32) * h_scr[...]).astype(o_ref.dtype)


def _rglru_mixer(geo, xs, modl, w_in, conv_w, conv_b, gate_w, gate_b, lam):
    B, P, D, S, C = geo.B, geo.P, geo.D, geo.S, geo.C
    R = w_in.shape[1] // 2
    nb = R // RG_BLOCK
    w_in = w_in.astype(BF16)
    g, r = pl.pallas_call(
        _rg_in_kernel, grid=(B, geo.nt),
        in_specs=[geo.row_spec(D), geo.mod_spec(0), geo.mod_spec(1),
                  pl.BlockSpec((D, R), lambda b, i: (0, 0)), pl.BlockSpec((D, R), lambda b, i: (0, 1))],
        out_specs=[geo.row_spec(R), geo.row_spec(R)],
        out_shape=[jax.ShapeDtypeStruct((B, P, R), BF16), jax.ShapeDtypeStruct((B, P, R), F32)],
        compiler_params=_cp(2), name="rg_in",
    )(xs, modl, modl, w_in, w_in)
    K = conv_w.shape[0]
    seq_spec = pl.BlockSpec((1, P, RG_BLOCK), lambda b, n: (b, 0, n))
    y = pl.pallas_call(
        functools.partial(_rg_scan_kernel, S=S, C=C), grid=(B, nb),
        in_specs=[seq_spec, seq_spec,
                  pl.BlockSpec((K, RG_BLOCK), lambda b, n: (0, n)),
                  pl.BlockSpec((1, RG_BLOCK), lambda b, n: (0, n)),
                  pl.BlockSpec((2, 2, None, RG_BLOCK, RG_BLOCK), lambda b, n: (0, 0, n, 0, 0)),
                  pl.BlockSpec((2, 2, 1, RG_BLOCK), lambda b, n: (0, 0, 0, n)),
                  pl.BlockSpec((2, 1, RG_BLOCK), lambda b, n: (0, 0, n))],
        out_specs=seq_spec,
        out_shape=jax.ShapeDtypeStruct((B, P, R), BF16),
        scratch_shapes=[pltpu.VMEM((P, RG_BLOCK), F32)] * 3,
        compiler_params=_cp(2), name="rg_scan",
    )(r, g, conv_w, conv_b.reshape(1, R), gate_w.astype(BF16), gate_b.reshape(2, 2, 1, R),
      lam.reshape(2, 1, R))
    return y


def _mm_bias_kernel(x_ref, sh_ref, sc_ref, w_ref, b_ref, o_ref):
    h = (x_ref[0] * (1.0 + sc_ref[...]) + sh_ref[...]).astype(BF16)
    o_ref[0] = (jnp.dot(h, w_ref[...], preferred_element_type=F32) + b_ref[...]).astype(o_ref.dtype)


def _short_conv_kernel(u_ref, cw_ref, cb_ref, o_ref, ob_ref, *, S, C):
    y = _dwconv_seg(u_ref[0], cw_ref[...], cb_ref[...], S, C, (cw_ref.shape[0] - 1) // 2)
    o_ref[0] = y
    ob_ref[0] = y.astype(BF16)


def _dft_table_kernel(c_ref, s_ref, st_ref, *, L, TF):
    i = pl.program_id(0)
    N = 2 * L
    f = lax.broadcasted_iota(jnp.int32, (TF, L), 0) + i * TF
    t = lax.broadcasted_iota(jnp.int32, (TF, L), 1)
    ang = jnp.bitwise_and(f * t, N - 1).astype(F32) * (2.0 * math.pi / N)
    c_ref[...] = jnp.cos(ang).astype(BF16)
    nyq_t = (1 - 2 * jnp.bitwise_and(t, 1)).astype(F32)
    s_ref[...] = jnp.where(f == 0, nyq_t, jnp.sin(ang)).astype(BF16)
    nyq_f = (1 - 2 * jnp.bitwise_and(f, 1)).astype(F32)
    st_ref[...] = jnp.where(t == 0, nyq_f, jnp.sin(ang)).astype(BF16)


def _dft_tables(L):
    TF = min(L, 256)
    shp = jax.ShapeDtypeStruct((L, L), BF16)
    spec = pl.BlockSpec((TF, L), lambda i: (i, 0))
    return pl.pallas_call(
        functools.partial(_dft_table_kernel, L=L, TF=TF), grid=(L // TF,),
        in_specs=[], out_specs=[spec, spec, spec], out_shape=[shp, shp, shp],
        compiler_params=_cp(1), name="dft_tables",
    )()


def _hy_filter_kernel(z_ref, w1_ref, b1_ref, w2_ref, b2_ref, w3_ref, b3_ref, fq_ref,
                      w4f_ref, w4b_ref, df_ref, db_ref, tn_ref, kp_ref, km_ref):
    fq = fq_ref[...]

    def lin(h, w_ref, b_ref):
        return jnp.dot(h, w_ref[...], precision=HIGHEST, preferred_element_type=F32) + b_ref[...]

    h = jnp.sin(fq * lin(z_ref[...], w1_ref, b1_ref))
    h = jnp.sin(fq * lin(h, w2_ref, b2_ref))
    h = jnp.sin(fq * lin(h, w3_ref, b3_ref))
    tn = tn_ref[...]
    hf = jnp.dot(h, w4f_ref[...], precision=HIGHEST, preferred_element_type=F32)
    hf = hf * jnp.exp(-tn * jnp.abs(df_ref[...]))
    hb = jnp.dot(h, w4b_ref[...], precision=HIGHEST, preferred_element_type=F32)
    hb = hb * jnp.exp(-tn * jnp.abs(db_ref[...]))
    row = lax.broadcasted_iota(jnp.int32, hb.shape, 0)
    hb = jnp.where(row == 0, 0.0, hb)
    nrm = lax.rsqrt(jnp.sum(hf * hf, axis=0, keepdims=True) + jnp.sum(hb * hb, axis=0, keepdims=True) + 1e-6)
    hf = hf * nrm
    hb = hb * nrm
    kp_ref[...] = (hf + hb).astype(BF16)
    km_ref[...] = (hf - hb).astype(BF16)


def _hy_spectrum_kernel(c_ref, s_ref, s0_ref, kp_ref, km_ref, ka_ref, kb_ref, kc_ref, *, L, TF):
    i = pl.program_id(0)
    inv_n = 1.0 / (2 * L)
    kr = jnp.dot(c_ref[...], kp_ref[...], preferred_element_type=F32)
    ks = jnp.dot(s_ref[...], km_ref[...], preferred_element_type=F32)
    nyq = jnp.dot(s0_ref[...], kp_ref[...], preferred_element_type=F32)[0:1]
    f = lax.broadcasted_iota(jnp.int32, kr.shape, 0) + i * TF
    dc = f == 0
    ka_ref[...] = jnp.where(dc, kr * inv_n, kr * (2.0 * inv_n))
    kb_ref[...] = jnp.where(dc, 0.0, ks * (-2.0 * inv_n))
    kc_ref[...] = jnp.where(dc, nyq * inv_n, kr * (2.0 * inv_n))


def _hy_filters(L, tabs, fw1, fb1, fw2, fb2, fw3, fb3, fw4, ffreq, fdecay, D):
    cm, sm, _ = tabs
    E = fw1.shape[0]
    Hd = fw1.shape[1]
    bands = (E - 1) // 2
    t = jnp.arange(L, dtype=F32)
    t_norm = t / max(L - 1, 1)
    fr = jnp.linspace(1e-4, bands - 1, bands, dtype=F32)
    ang = (2.0 * math.pi / L) * t[:, None] * fr[None, :]
    z = jnp.concatenate([t_norm[:, None], jnp.cos(ang), -jnp.sin(ang)], -1)
    Ep, Hp = -(-E // LANES) * LANES, -(-Hd // LANES) * LANES
    z = jnp.pad(z, ((0, 0), (0, Ep - E)))
    fw1 = jnp.pad(fw1, ((0, Ep - E), (0, Hp - Hd)))
    fw2 = jnp.pad(fw2, ((0, Hp - Hd), (0, Hp - Hd)))
    fw3 = jnp.pad(fw3, ((0, Hp - Hd), (0, Hp - Hd)))
    fw4 = jnp.pad(fw4, ((0, Hp - Hd), (0, 0)))
    fb1, fb2, fb3, ffreq = (jnp.pad(v, (0, Hp - Hd)) for v in (fb1, fb2, fb3, ffreq))
    E, Hd = Ep, Hp
    CT = min(2 * D, 512)
    nct = 2 * D // CT
    dec = fdecay.reshape(1, 4 * D)
    kp, km = pl.pallas_call(
        _hy_filter_kernel, grid=(nct,),
        in_specs=[_full_spec((L, E)), _full_spec((E, Hd)), _full_spec((1, Hd)), _full_spec((Hd, Hd)),
                  _full_spec((1, Hd)), _full_spec((Hd, Hd)), _full_spec((1, Hd)), _full_spec((1, Hd)),
                  pl.BlockSpec((Hd, CT), lambda j: (0, j)), pl.BlockSpec((Hd, CT), lambda j: (0, j + nct)),
                  pl.BlockSpec((1, CT), lambda j: (0, j)), pl.BlockSpec((1, CT), lambda j: (0, j + nct)),
                  _full_spec((L, 1))],
        out_specs=[pl.BlockSpec((L, CT), lambda j: (0, j))] * 2,
        out_shape=[jax.ShapeDtypeStruct((L, 2 * D), BF16)] * 2,
        compiler_params=_cp(1), name="hy_filter",
    )(z, fw1, fb1.reshape(1, Hd), fw2, fb2.reshape(1, Hd), fw3, fb3.reshape(1, Hd), ffreq.reshape(1, Hd),
      fw4, fw4, dec, dec, t_norm[:, None])
    TF = min(L, 256)
    spec_w = pl.BlockSpec((TF, L), lambda i, j: (i, 0))
    spec_k = pl.BlockSpec((L, CT), lambda i, j: (0, j))
    spec_o = pl.BlockSpec((TF, CT), lambda i, j: (i, j))
    shp = jax.ShapeDtypeStruct((L, 2 * D), F32)
    return pl.pallas_call(
        functools.partial(_hy_spectrum_kernel, L=L, TF=TF), grid=(L // TF, nct),
        in_specs=[spec_w, spec_w, pl.BlockSpec((SUBLANES, L), lambda i, j: (0, 0)), spec_k, spec_k],
        out_specs=[spec_o] * 3, out_shape=[shp] * 3,
        compiler_params=_cp(2), name="hy_spectrum",
    )(cm, sm, sm, kp, km)


def _hy_fwd_kernel(z_ref, c_ref, s_ref, ka_ref, kb_ref, kc_ref, p_ref):
    z = z_ref[0]
    zr = jnp.dot(c_ref[...], z, preferred_element_type=F32)
    zs = jnp.dot(s_ref[...], z, preferred_element_type=F32)
    kb = kb_ref[...]
    p_ref[0, 0] = (zr * ka_ref[...] + zs * kb).astype(BF16)
    p_ref[0, 1] = (zs * kc_ref[...] - zr * kb).astype(BF16)


def _hy_inv_kernel(p_ref, c_ref, st_ref, z_ref, x_ref, fb_ref, o_ref):
    y = jnp.dot(c_ref[...], p_ref[0, 0], preferred_element_type=F32)
    y = y + jnp.dot(st_ref[...], p_ref[0, 1], preferred_element_type=F32)
    o_ref[0] = (x_ref[0] * (y + z_ref[0].astype(F32) * fb_ref[...])).astype(o_ref.dtype)


def _hy_inv_kernel_alias(p_ref, c_ref, st_ref, z_ref, x_ref, fb_ref, prev_ref, o_ref):
    del prev_ref
    _hy_inv_kernel(p_ref, c_ref, st_ref, z_ref, x_ref, fb_ref, o_ref)


def _hy_conv(geo, L, off, tabs, z, z_col, kfilt, k_col, xmul, x_col, fbias, out):
    B, P, D = geo.B, geo.P, geo.D
    cm, sm, smt = tabs
    ka, kb, kc = kfilt
    rb = off // L
    TF = min(L, 256)
    spec_w = pl.BlockSpec((TF, L), lambda b, i: (i, 0))
    spec_k = pl.BlockSpec((TF, D), lambda b, i: (i, k_col))
    p = pl.pallas_call(
        _hy_fwd_kernel, grid=(B, L // TF),
        in_specs=[pl.BlockSpec((1, L, D), lambda b, i: (b, rb, z_col)), spec_w, spec_w,
                  spec_k, spec_k, spec_k],
        out_specs=pl.BlockSpec((1, 2, TF, D), lambda b, i: (b, 0, i, 0)),
        out_shape=jax.ShapeDtypeStruct((B, 2, L, D), BF16),
        compiler_params=_cp(2), name="hy_fwd",
    )(z, cm, sm, ka, kb, kc)
    CT = min(D, 512)
    nct = D // CT
    rt = off // TF
    spec_wi = pl.BlockSpec((TF, L), lambda b, j, i: (i, 0))
    return pl.pallas_call(
        _hy_inv_kernel_alias, grid=(B, nct, L // TF),
        in_specs=[pl.BlockSpec((1, 2, L, CT), lambda b, j, i: (b, 0, 0, j)), spec_wi, spec_wi,
                  pl.BlockSpec((1, TF, CT), lambda b, j, i: (b, rt + i, z_col * nct + j)),
                  pl.BlockSpec((1, TF, CT), lambda b, j, i: (b, rt + i, x_col * nct + j)),
                  pl.BlockSpec((1, CT), lambda b, j, i: (0, k_col * nct + j)),
                  pl.BlockSpec(memory_space=pl.ANY)],
        out_specs=pl.BlockSpec((1, TF, CT), lambda b, j, i: (b, rt + i, j)),
        out_shape=jax.ShapeDtypeStruct((B, P, D), BF16),
        input_output_aliases={6: 0},
        compiler_params=_cp(3), name="hy_inv",
    )(p, cm, smt, z, xmul, fbias, out)


def _hyena_mixer(geo, xs, modl, w_in, b_in, short_w, short_b, fw1, fb1, fw2, fb2, fw3, fb3, fw4,
                 ffreq, fdecay, fbias):
    B, P, D, S, C = geo.B, geo.P, geo.D, geo.S, geo.C
    u0 = pl.pallas_call(
        _mm_bias_kernel, grid=(B, geo.nt, 3),
        in_specs=[geo.row_spec(D), geo.mod_spec(0), geo.mod_spec(1),
                  pl.BlockSpec((D, D), lambda b, i, j: (0, j)), pl.BlockSpec((1, D), lambda b, i, j: (0, j))],
        out_specs=pl.BlockSpec((1, geo.TM, D), lambda b, i, j: (b, i, j)),
        out_shape=jax.ShapeDtypeStruct((B, P, 3 * D), F32),
        compiler_params=_cp(3), name="hy_in",
    )(xs, modl, modl, w_in.astype(BF16), b_in.reshape(1, 3 * D))
    CT = min(D, 256)
    Ks = short_w.shape[0]
    spec = pl.BlockSpec((1, P, CT), lambda b, j: (b, 0, j))
    u, ub = pl.pallas_call(
        functools.partial(_short_conv_kernel, S=S, C=C), grid=(B, 3 * D // CT),
        in_specs=[spec, pl.BlockSpec((Ks, CT), lambda b, j: (0, j)), pl.BlockSpec((1, CT), lambda b, j: (0, j))],
        out_specs=[spec, spec],
        out_shape=[jax.ShapeDtypeStruct((B, P, 3 * D), F32), jax.ShapeDtypeStruct((B, P, 3 * D), BF16)],
        compiler_params=_cp(2), name="hy_short",
    )(u0, short_w, short_b.reshape(1, 3 * D))
    fb = fbias.reshape(1, 2 * D)
    z1 = jnp.zeros((B, P, D), BF16)
    z2 = jnp.zeros((B, P, D), BF16)
    segs = [(S, 0), (C, S)]
    convs = []
    for L, off in segs:
        tabs = _dft_tables(L)
        kf = _hy_filters(L, tabs, fw1, fb1, fw2, fb2, fw3, fb3, fw4, ffreq, fdecay, D)
        convs.append((L, off, tabs, kf))
    for L, off, tabs, kf in convs:
        z1 = _hy_conv(geo, L, off, tabs, ub, 0, kf, 0, u, 1, fb, z1)
    for L, off, tabs, kf in convs:
        z2 = _hy_conv(geo, L, off, tabs, z1, 0, kf, 1, u, 2, fb, z2)
    return z2


def _rope_tables(S, D):
    t = jnp.arange(S)
    row = (t // GRID_W).astype(F32)
    col = (t % GRID_W).astype(F32)
    axis_dim = DA_HEAD_DIM // 2
    half = axis_dim // 2
    inv = ROPE_THETA ** (-jnp.arange(0, axis_dim, 2, dtype=F32) / axis_dim)
    lane = jnp.arange(D)
    within = lane % DA_HEAD_DIM
    pos = jnp.where((within // axis_dim)[None, :] == 0, row[:, None], col[:, None])
    ang = pos * inv[lane % half][None, :]
    sign = jnp.where((lane % axis_dim) < half, -1.0, 1.0)[None, :]
    return jnp.cos(ang), jnp.sin(ang) * sign


def _da_in_kernel(x_ref, sh_ref, sc_ref, w_ref, cos_ref, sin_ref, o_ref, *, n_lat):
    i = pl.program_id(1)
    j = pl.program_id(2)
    h = (x_ref[0] * (1.0 + sc_ref[...]) + sh_ref[...]).astype(BF16)
    acc = jnp.dot(h, w_ref[...], preferred_element_type=F32)
    rot = jnp.logical_and(i < n_lat, j < 2)

    @pl.when(rot)
    def _():
        Dn = acc.shape[-1]
        half = DA_HEAD_DIM // 4
        lane = lax.broadcasted_iota(jnp.int32, acc.shape, 1)
        up = pltpu.roll(acc, Dn - half, 1)
        dn = pltpu.roll(acc, half, 1)
        partner = jnp.where(jnp.bitwise_and(lane, 2 * half - 1) < half, up, dn)
        o_ref[0] = (acc * cos_ref[...] + partner * sin_ref[...]).astype(o_ref.dtype)

    @pl.when(jnp.logical_not(rot))
    def _():
        o_ref[0] = acc.astype(o_ref.dtype)


def _da_attn_kernel(q_ref, k_ref, v_ref, lam_ref, sub_ref, o_ref, *, S, C, n_lat, lam_init):
    i = pl.program_id(2)
    lp = lam_ref[...]
    lam = (jnp.exp(jnp.sum(lp[0:1] * lp[1:2], axis=1, keepdims=True))
           - jnp.exp(jnp.sum(lp[2:3] * lp[3:4], axis=1, keepdims=True)) + lam_init)
    scale = DA_HEAD_DIM ** -0.5

    def attend(k, v):
        q = q_ref[0]
        outs = []
        for c in range(2):
            qc = q[:, c * DA_HEAD_DIM:(c + 1) * DA_HEAD_DIM]
            kc = k[:, c * DA_HEAD_DIM:(c + 1) * DA_HEAD_DIM]
            s = lax.dot_general(qc, kc, (((1,), (1,)), ((), ())), preferred_element_type=F32) * scale
            m = jnp.max(s, axis=-1, keepdims=True)
            p = jnp.exp(s - m)
            l = jnp.sum(p, axis=-1, keepdims=True)
            outs.append(jnp.dot(p.astype(BF16), v, preferred_element_type=F32) / l)
        o = outs[0] - lam * outs[1]
        o = o * lax.rsqrt(jnp.mean(o * o, axis=-1, keepdims=True) + 1e-5) * sub_ref[...] * (1.0 - lam_init)
        o_ref[0] = o.astype(o_ref.dtype)

    @pl.when(i < n_lat)
    def _():
        attend(k_ref[0], v_ref[0])

    @pl.when(i >= n_lat)
    def _():
        attend(k_ref[0, S:S + C, :], v_ref[0, S:S + C, :])


def _diff_attention_mixer(geo, xs, modl, w_in, lam_p, subln_w, layer_idx):
    B, P, D, S, C = geo.B, geo.P, geo.D, geo.S, geo.C
    H = D // (2 * DA_HEAD_DIM)
    HW = 2 * DA_HEAD_DIM
    cos_t, sin_t = _rope_tables(S, D)
    n_lat = geo.n_lat
    tab_spec = pl.BlockSpec((geo.TM, D), lambda b, i, j: (jnp.minimum(i, n_lat - 1), 0))
    qkv = pl.pallas_call(
        functools.partial(_da_in_kernel, n_lat=n_lat), grid=(B, geo.nt, 3),
        in_specs=[geo.row_spec(D), geo.mod_spec(0), geo.mod_spec(1),
                  pl.BlockSpec((D, D), lambda b, i, j: (0, j)), tab_spec, tab_spec],
        out_specs=pl.BlockSpec((1, geo.TM, D), lambda b, i, j: (b, i, j)),
        out_shape=jax.ShapeDtypeStruct((B, P, 3 * D), BF16),
        compiler_params=_cp(3), name="da_in",
    )(xs, modl, modl, w_in.astype(BF16), cos_t, sin_t)
    lam_init = 0.8 - 0.6 * math.exp(-0.3 * layer_idx)
    return pl.pallas_call(
        functools.partial(_da_attn_kernel, S=S, C=C, n_lat=n_lat, lam_init=lam_init),
        grid=(B, H, geo.nt),
        in_specs=[pl.BlockSpec((1, geo.TM, HW), lambda b, h, i: (b, i, h)),
                  pl.BlockSpec((1, P, HW), lambda b, h, i: (b, 0, H + h)),
                  pl.BlockSpec((1, P, HW), lambda b, h, i: (b, 0, 2 * H + h)),
                  _full_spec((4, DA_HEAD_DIM)), _full_spec((1, HW))],
        out_specs=pl.BlockSpec((1, geo.TM, HW), lambda b, h, i: (b, i, h)),
        out_shape=jax.ShapeDtypeStruct((B, P, D), BF16),
        compiler_params=_cp(3), name="da_attn",
    )(qkv, qkv, qkv, lam_p, subln_w.reshape(1, HW))


def _s5_operators(a_re, a_im, log_step, b_re, b_im, c_re, c_im):
    T = S5_CHUNK
    G, Pst = a_re.shape[1], a_re.shape[2]
    Hg = S5_GROUP
    GL = LANES // Hg
    LB = G // GL
    lam = lax.complex(jnp.minimum(a_re.astype(F32), -1e-4), a_im.astype(F32))
    step = jnp.exp(log_step.astype(F32))[..., None]
    abar = jnp.exp(lam * step)
    bbar = ((abar - 1.0) / lam)[..., None] * lax.complex(b_re.astype(F32), b_im.astype(F32))
    cmat = lax.complex(c_re.astype(F32), c_im.astype(F32))
    pows = jnp.stack([abar ** l for l in range(T + 1)], axis=1)
    eye = jnp.eye(GL, dtype=F32)
    ar = jnp.arange(T)
    big_m, big_g, big_h, a_t = [], [], [], []
    for d in range(2):
        pw = pows[d]
        kl = jnp.einsum('gjp,lgp,gph->lgjh', cmat[d], pw[:T], bbar[d]).real
        lag = (ar[None, :] - ar[:, None]) if d == 0 else (ar[:, None] - ar[None, :])
        tz = jnp.where((lag >= 0)[:, :, None, None, None], kl[jnp.clip(lag, 0, T - 1)], 0.0)
        tz = tz.reshape(T, T, LB, GL, Hg, Hg)
        m = jnp.einsum('stbgjh,gk->bsghtkj', tz, eye).reshape(LB, T * LANES, T * LANES)
        e_in = (T - 1 - ar) if d == 0 else ar
        gc = pw[e_in][:, :, :, None] * bbar[d][None]
        gc = gc.reshape(T, LB, GL, Pst, Hg)
        g_re = jnp.einsum('sbgph,gk->bsghkp', gc.real, eye).reshape(LB, T * LANES, GL * Pst)
        g_im = jnp.einsum('sbgph,gk->bsghkp', gc.imag, eye).reshape(LB, T * LANES, GL * Pst)
        e_out = (ar + 1) if d == 0 else (T - ar)
        hc = cmat[d][None] * pw[e_out][:, :, None, :]
        hc = hc.reshape(T, LB, GL, Hg, Pst)
        h_re = jnp.einsum('tbgjp,gk->bgptkj', hc.real, eye).reshape(LB, GL * Pst, T * LANES)
        h_im = jnp.einsum('tbgjp,gk->bgptkj', -hc.imag, eye).reshape(LB, GL * Pst, T * LANES)
        big_m.append(m)
        big_g.append(jnp.concatenate([g_re, g_im], axis=2))
        big_h.append(jnp.concatenate([h_re, h_im], axis=1))
        at = pw[T].reshape(LB, 1, GL * Pst)
        a_t.append(jnp.concatenate([at.real, at.imag], axis=2))
    return (jnp.stack(big_m).astype(BF16), jnp.stack(big_g).astype(BF16),
            jnp.stack(big_h).astype(BF16), jnp.stack(a_t).astype(F32))


def _modulate_kernel(x_ref, sh_ref, sc_ref, o_ref):
    o_ref[0] = x_ref[0] * (1.0 + sc_ref[...]) + sh_ref[...]


def _s5_kernel(u_ref, m_ref, g_ref, h_ref, a_ref, o_ref, gx_scr, sp_scr, *, S, C):
    d = pl.program_id(1)
    T = S5_CHUNK
    P = S + C
    n = P // T
    n_lat = S // T
    x = jnp.concatenate([u_ref[0, pl.ds(s, n, stride=T), :] for s in range(T)], axis=1).astype(BF16)
    gx_scr[...] = jnp.dot(x, g_ref[...], preferred_element_type=F32)
    ns = a_ref.shape[-1] // 2
    a_r = a_ref[:, :ns]
    a_i = a_ref[:, ns:]

    def scan(lo, cnt, rev, carry):
        def body(k, st):
            s_r, s_i = st
            c = (lo + cnt - 1 - k) if rev else (lo + k)
            sp_scr[pl.ds(c, 1), :] = jnp.concatenate([s_r, s_i], axis=1)
            gx = gx_scr[pl.ds(c, 1), :]
            return (a_r * s_r - a_i * s_i + gx[:, :ns], a_r * s_i + a_i * s_r + gx[:, ns:])
        return lax.fori_loop(0, cnt, body, carry)

    zero = (jnp.zeros((1, ns), F32), jnp.zeros((1, ns), F32))

    @pl.when(d == 0)
    def _():
        scan(0, n_lat, False, scan(n_lat, n - n_lat, False, zero))

    @pl.when(d == 1)
    def _():
        scan(0, n_lat, True, scan(n_lat, n - n_lat, True, zero))

    y = jnp.dot(x, m_ref[...], preferred_element_type=F32)
    y = y + jnp.dot(sp_scr[...].astype(BF16), h_ref[...], preferred_element_type=F32)
    for s in range(T):
        o_ref[0, pl.ds(s, n, stride=T), :] = y[:, s * LANES:(s + 1) * LANES]


def _s5_mixer(geo, xs, modl, a_re, a_im, log_step, b_re, b_im, c_re, c_im):
    B, P, D, S, C = geo.B, geo.P, geo.D, geo.S, geo.C
    u = pl.pallas_call(
        _modulate_kernel, grid=(B, geo.nt),
        in_specs=[geo.row_spec(D), geo.mod_spec(0), geo.mod_spec(1)],
        out_specs=geo.row_spec(D), out_shape=jax.ShapeDtypeStruct((B, P, D), F32),
        compiler_params=_cp(2), name="s5_modulate",
    )(xs, modl, modl)
    big_m, big_g, big_h, a_t = _s5_operators(a_re, a_im, log_step, b_re, b_im, c_re, c_im)
    LB = D // LANES
    TL = S5_CHUNK * LANES
    NS = big_g.shape[-1]
    n = P // S5_CHUNK
    y = pl.pallas_call(
        functools.partial(_s5_kernel, S=S, C=C), grid=(LB, 2, B),
        in_specs=[pl.BlockSpec((1, P, LANES), lambda l, d, b: (b, 0, l)),
                  pl.BlockSpec((None, None, TL, TL), lambda l, d, b: (d, l, 0, 0)),
                  pl.BlockSpec((None, None, TL, NS), lambda l, d, b: (d, l, 0, 0)),
                  pl.BlockSpec((None, None, NS, TL), lambda l, d, b: (d, l, 0, 0)),
                  pl.BlockSpec((None, None, 1, NS), lambda l, d, b: (d, l, 0, 0))],
        out_specs=pl.BlockSpec((None, 1, P, LANES), lambda l, d, b: (d, b, 0, l)),
        out_shape=jax.ShapeDtypeStruct((2, B, P, D), F32),
        scratch_shapes=[pltpu.VMEM((n, NS), F32), pltpu.VMEM((n, NS), F32)],
        compiler_params=_cp(3), name="s5_scan",
    )(u, big_m, big_g, big_h, a_t)
    return u, y


def _s5_post_kernel(yf_ref, yb_ref, u_ref, d_ref, w_ref, b_ref, x_ref, gate_ref, lng_ref, lnb_ref,
                    sh_ref, sc_ref, wr_ref, xm_ref, f_ref, s_ref, *, alpha):
    g = jax.nn.gelu(yf_ref[0] + yb_ref[0] + d_ref[...] * u_ref[0])
    vg = jnp.dot(g.astype(BF16), w_ref[...], preferred_element_type=F32) + b_ref[...]
    Dn = vg.shape[-1] // 2
    y = vg[:, :Dn] * jax.nn.sigmoid(vg[:, Dn:])
    _finish_mixer(y, x_ref[0], gate_ref[...], lng_ref[...], lnb_ref[...], sh_ref[...], sc_ref[...],
                  wr_ref, xm_ref, f_ref, s_ref, alpha)


def _s5_post(geo, y2, u, d_skip, w_glu, b_glu, xs, modl, lng, lnb, w_router, alpha):
    D = geo.D
    E = w_router.shape[-1]
    out_specs, out_shape = _post_outs(geo, E)
    TM = geo.TM
    return pl.pallas_call(
        functools.partial(_s5_post_kernel, alpha=alpha), grid=(geo.B, geo.nt),
        in_specs=[pl.BlockSpec((None, 1, TM, D), lambda b, i: (0, b, i, 0)),
                  pl.BlockSpec((None, 1, TM, D), lambda b, i: (1, b, i, 0)),
                  geo.row_spec(D), _full_spec((1, D)), _full_spec((D, 2 * D)), _full_spec((1, 2 * D)),
                  geo.row_spec(D), geo.mod_spec(2), _full_spec((1, D)), _full_spec((1, D)),
                  geo.mod_spec(3), geo.mod_spec(4), _full_spec((D, E))],
        out_specs=out_specs, out_shape=out_shape,
        compiler_params=_cp(2), name="s5_post",
    )(y2, y2, u, d_skip.reshape(1, D), w_glu.astype(BF16), b_glu.reshape(1, 2 * D), xs, modl,
      lng.reshape(1, D), lnb.reshape(1, D), modl, modl, w_router)


def _route(scores, bias):
    T, E = scores.shape
    biased = scores + bias.astype(F32)
    grp = biased.reshape(T, N_EXPERT_GROUPS, E // N_EXPERT_GROUPS)
    grp_score = lax.top_k(grp, 2)[0].sum(-1)
    _, gidx = lax.top_k(grp_score, TOPK_GROUPS)
    gmask = (gidx[:, :, None] == jnp.arange(N_EXPERT_GROUPS)).any(1)
    emask = jnp.repeat(gmask, E // N_EXPERT_GROUPS, axis=1)
    _, idx = lax.top_k(jnp.where(emask, biased, -jnp.inf), TOP_K)
    w = jnp.take_along_axis(scores, idx, 1)
    w = w / (w.sum(-1, keepdims=True) + 1e-20) * ROUTED_SCALE
    return idx, w


def _dispatch(idx, E, blk):
    T, K = idx.shape
    n_assign = T * K
    n_blocks = -(-(n_assign + E * (blk - 1)) // blk)
    n_rows = n_blocks * blk
    flat_e = idx.reshape(-1)
    order = jnp.argsort(flat_e)
    se = flat_e[order]
    counts = jnp.bincount(flat_e, length=E)
    starts = jnp.cumsum(counts) - counts
    pcounts = (counts + blk - 1) // blk * blk
    pends = jnp.cumsum(pcounts)
    dest_sorted = (pends[se] - pcounts[se] + jnp.arange(n_assign) - starts[se]).astype(jnp.int32)
    row_tok = jnp.zeros((n_rows,), jnp.int32).at[dest_sorted].set((order // K).astype(jnp.int32))
    dest = jnp.zeros((n_assign,), jnp.int32).at[order].set(dest_sorted)
    blk_e = jnp.minimum(jnp.searchsorted(pends, jnp.arange(n_blocks) * blk, side='right'), E - 1)
    return row_tok, dest.reshape(T, K), blk_e.astype(jnp.int32)


def _expert_kernel(be_ref, x_ref, wgu_ref, wd_ref, o_ref):
    del be_ref
    h = jnp.dot(x_ref[...], wgu_ref[...].astype(BF16), preferred_element_type=F32)
    Fh = h.shape[-1] // 2
    a = _silu(h[:, :Fh]) * h[:, Fh:]
    o_ref[...] = jnp.dot(a.astype(BF16), wd_ref[...].astype(BF16),
                         preferred_element_type=F32).astype(o_ref.dtype)


def _expert_ffn(x_sorted, blk_e, w_gu, w_down, layer, blk):
    n_rows, D = x_sorted.shape
    F2 = w_gu.shape[-1]
    grid_spec = pltpu.PrefetchScalarGridSpec(
        num_scalar_prefetch=1, grid=(n_rows // blk,),
        in_specs=[pl.BlockSpec((blk, D), lambda i, be: (i, 0)),
                  pl.BlockSpec((None, None, D, F2), lambda i, be: (layer, be[i], 0, 0)),
                  pl.BlockSpec((None, None, F2 // 2, D), lambda i, be: (layer, be[i], 0, 0))],
        out_specs=pl.BlockSpec((blk, D), lambda i, be: (i, 0)))
    return pl.pallas_call(
        _expert_kernel, grid_spec=grid_spec,
        out_shape=jax.ShapeDtypeStruct((n_rows, D), BF16),
        compiler_params=_cp(1), name="moe_experts",
    )(blk_e, x_sorted, w_gu, w_down)


def _moe_final_kernel(xm_ref, f_ref, ga_ref, w_ref, shgu_ref, shd_ref, gate_ref, lng_ref, lnb_ref, o_ref,
                      *, alpha):
    D = xm_ref.shape[-1]
    h = jnp.dot(f_ref[0], shgu_ref[...], preferred_element_type=F32)
    Fh = h.shape[-1] // 2
    a = _silu(h[:, :Fh]) * h[:, Fh:]
    y = jnp.dot(a.astype(BF16), shd_ref[...], preferred_element_type=F32)
    w = w_ref[0]
    for k in range(w.shape[-1]):
        y = y + w[:, k:k + 1] * ga_ref[0, :, k * D:(k + 1) * D].astype(F32)
    o_ref[0] = _layer_norm(alpha * xm_ref[0] + gate_ref[...] * y, lng_ref[...], lnb_ref[...])


def _moe(geo, xm, f, scores, modl, bias, w_gu, w_down, sh_gu, sh_down, lng, lnb, layer, alpha, blk):
    B, P, D = geo.B, geo.P, geo.D
    E = scores.shape[-1]
    T = B * P
    idx, w = _route(scores.reshape(T, E), bias)
    row_tok, dest, blk_e = _dispatch(idx, E, blk)
    x_sorted = f.reshape(T, D)[row_tok]
    y_sorted = _expert_ffn(x_sorted, blk_e, w_gu, w_down, layer, blk)
    K = idx.shape[-1]
    gathered = y_sorted[dest.reshape(-1)].reshape(B, P, K * D)
    F2 = sh_gu.shape[-1]
    return pl.pallas_call(
        functools.partial(_moe_final_kernel, alpha=alpha), grid=(B, geo.nt),
        in_specs=[geo.row_spec(D), geo.row_spec(D), geo.row_spec(K * D), geo.row_spec(K),
                  _full_spec((D, F2)), _full_spec((F2 // 2, D)), geo.mod_spec(5),
                  _full_spec((1, D)), _full_spec((1, D))],
        out_specs=geo.row_spec(D), out_shape=jax.ShapeDtypeStruct((B, P, D), F32),
        compiler_params=_cp(2), name="moe_final",
    )(xm, f, gathered, w.reshape(B, P, K), sh_gu.astype(BF16), sh_down.astype(BF16), modl,
      lng.reshape(1, D), lnb.reshape(1, D))


def kernel(x, c, ctx, c_ctx, mod_w, mod_b, ln_g, ln_b, rg_w_in, rg_conv_w, rg_conv_b, rg_gate_w, rg_gate_b, rg_lam, rg_w_out, hy_w_in, hy_b_in, hy_short_w, hy_short_b, hy_f_w1, hy_f_b1, hy_f_w2, hy_f_b2, hy_f_w3, hy_f_b3, hy_f_w4, hy_f_freq, hy_f_decay, hy_f_bias, hy_w_out, hy_b_out, da_w_in, da_lam, da_subln, da_w_out, s5_a_re, s5_a_im, s5_log_step, s5_b_re, s5_b_im, s5_c_re, s5_c_im, s5_d, s5_w_glu, s5_b_glu, moe_w_router, moe_bias, moe_w_gu, moe_w_down, moe_sh_gu, moe_sh_down):
    B, S, D = x.shape
    C = ctx.shape[1]
    depth = mod_w.shape[0]
    alpha = (2 * depth) ** 0.25
    geo = _Geo(B, S, C, D)
    xs = jnp.concatenate([x, ctx], axis=1)
    R = -(-(B + 1) // SUBLANES) * SUBLANES
    cc = jnp.zeros((R, D), F32).at[:B].set(c).at[B].set(c_ctx)
    modt = _mod_table(cc, mod_w, mod_b).reshape(depth, 6, R, 1, D)
    blk = min(MOE_BLOCK, geo.TM)
    for i in range(depth):
        kind, j = i % N_MIXERS, i // N_MIXERS
        modl = modt[i]
        post = functools.partial(_post_mixer, geo, xs=xs, modl=modl, lng=ln_g[i, 0], lnb=ln_b[i, 0],
                                 w_router=moe_w_router[i], alpha=alpha)
        if kind == 0:
            y = _rglru_mixer(geo, xs, modl, rg_w_in[j], rg_conv_w[j], rg_conv_b[j], rg_gate_w[j],
                             rg_gate_b[j], rg_lam[j])
            xm, f, scores = post(y=y, w_out=rg_w_out[j], b_out=None)
        elif kind == 1:
            y = _hyena_mixer(geo, xs, modl, hy_w_in[j], hy_b_in[j], hy_short_w[j], hy_short_b[j],
                             hy_f_w1[j], hy_f_b1[j], hy_f_w2[j], hy_f_b2[j], hy_f_w3[j], hy_f_b3[j],
                             hy_f_w4[j], hy_f_freq[j], hy_f_decay[j], hy_f_bias[j])
            xm, f, scores = post(y=y, w_out=hy_w_out[j], b_out=hy_b_out[j])
        elif kind == 2:
            y = _diff_attention_mixer(geo, xs, modl, da_w_in[j], da_lam[j], da_subln[j], i)
            xm, f, scores = post(y=y, w_out=da_w_out[j], b_out=None)
        else:
            u, y2 = _s5_mixer(geo, xs, modl, s5_a_re[j], s5_a_im[j], s5_log_step[j], s5_b_re[j],
                              s5_b_im[j], s5_c_re[j], s5_c_im[j])
            xm, f, scores = _s5_post(geo, y2, u, s5_d[j], s5_w_glu[j], s5_b_glu[j], xs, modl,
                                     ln_g[i, 0], ln_b[i, 0], moe_w_router[i], alpha)
        xs = _moe(geo, xm, f, scores, modl, moe_bias[i], moe_w_gu, moe_w_down, moe_sh_gu[i],
                  moe_sh_down[i], ln_g[i, 1], ln_b[i, 1], i, alpha, blk)
    return xs[:, :S]
```

```python
import functools
import math

import jax
import jax.numpy as jnp
from jax import lax
from jax.experimental import pallas as pl
from jax.experimental.pallas import tpu as pltpu

F32 = jnp.float32
BF16 = jnp.bfloat16
HIGHEST = lax.Precision.HIGHEST

N_MIXERS = 4
LN_EPS = 1e-6
LRU_C = 8.0
RG_BLOCK = 128
GRID_W = 64
DA_HEAD_DIM = 64
ROPE_THETA = 10000.0
S5_GROUP = 16
S5_CHUNK = 16
TOP_K = 8
N_EXPERT_GROUPS = 8
TOPK_GROUPS = 4
ROUTED_SCALE = 2.5
MOE_BLOCK = 256

LANES = 128
SUBLANES = 8
VMEM_LIMIT = 56 * 1024 * 1024


def _cp(n_grid):
    return pltpu.CompilerParams(dimension_semantics=("arbitrary",) * n_grid,
                                vmem_limit_bytes=VMEM_LIMIT)


def _silu(x):
    return x * jax.nn.sigmoid(x)


def _mod_table_kernel(c_ref, w_ref, b_ref, o_ref):
    s = _silu(c_ref[...])
    o_ref[...] = jnp.dot(s, w_ref[...], precision=HIGHEST, preferred_element_type=F32) + b_ref[...]


def _mod_table(cc, mod_w, mod_b):
    depth, D, _ = mod_w.shape
    R = cc.shape[0]
    return pl.pallas_call(
        _mod_table_kernel,
        grid=(depth, 6),
        in_specs=[pl.BlockSpec((R, D), lambda i, k: (0, 0)),
                  pl.BlockSpec((None, D, D), lambda i, k: (i, 0, k)),
                  pl.BlockSpec((None, None, 1, D), lambda i, k: (i, k, 0, 0))],
        out_specs=pl.BlockSpec((None, None, R, D), lambda i, k: (i, k, 0, 0)),
        out_shape=jax.ShapeDtypeStruct((depth, 6, R, D), F32),
        compiler_params=_cp(2), name="mod_table",
    )(cc, mod_w, mod_b.reshape(depth, 6, 1, D))


class _Geo:
    def __init__(self, B, S, C, D):
        self.B, self.S, self.C, self.D = B, S, C, D
        self.P = S + C
        self.TM = math.gcd(S, C)
        while self.TM > 256:
            self.TM //= 2
        self.nt = self.P // self.TM
        self.n_lat = S // self.TM

    def mod_spec(self, k, width=None, col=None):
        D, B, n_lat = self.D, self.B, self.n_lat
        width = D if width is None else width
        if col is None:
            return pl.BlockSpec((None, None, 1, width),
                                lambda b, i, *_: (k, jnp.where(i < n_lat, b, B), 0, 0))
        return pl.BlockSpec((None, None, 1, width),
                            lambda b, i, j, *_: (k, jnp.where(i < n_lat, b, B), 0, col(j)))

    def row_spec(self, width, col=0):
        return pl.BlockSpec((1, self.TM, width), lambda b, i, *_: (b, i, col))

    def col_spec(self, height):
        return pl.BlockSpec((1, height, self.TM), lambda b, i, *_: (b, 0, i))


def _full_spec(shape):
    nd = len(shape)
    return pl.BlockSpec(shape, lambda *_: (0,) * nd)


def _layer_norm(z, g, b):
    mu = jnp.mean(z, axis=-1, keepdims=True)
    zc = z - mu
    var = jnp.mean(zc * zc, axis=-1, keepdims=True)
    return zc * lax.rsqrt(var + LN_EPS) * g + b


def _finish_mixer(y, x, gate, lng, lnb, sh, sc, wr_ref, xm_ref, f_ref, s_ref, alpha):
    xn = _layer_norm(alpha * x + gate * y, lng, lnb)
    xm_ref[0] = xn
    f = xn * (1.0 + sc) + sh
    f_ref[0] = f.astype(f_ref.dtype)
    logits = lax.dot_general(wr_ref[...], f, (((1,), (1,)), ((), ())), precision=HIGHEST,
                             preferred_element_type=F32)
    s_ref[0] = jax.nn.sigmoid(logits)


def _post_kernel(*refs, alpha, has_bias):
    if has_bias:
        (y_ref, w_ref, b_ref, x_ref, gate_ref, lng_ref, lnb_ref, sh_ref, sc_ref, wr_ref,
         xm_ref, f_ref, s_ref) = refs
    else:
        (y_ref, w_ref, x_ref, gate_ref, lng_ref, lnb_ref, sh_ref, sc_ref, wr_ref,
         xm_ref, f_ref, s_ref) = refs
    y = jnp.dot(y_ref[0].astype(BF16), w_ref[...], preferred_element_type=F32)
    if has_bias:
        y = y + b_ref[...]
    _finish_mixer(y, x_ref[0], gate_ref[...], lng_ref[...], lnb_ref[...], sh_ref[...], sc_ref[...],
                  wr_ref, xm_ref, f_ref, s_ref, alpha)


def _post_outs(geo, E):
    B, P, D = geo.B, geo.P, geo.D
    out_specs = [geo.row_spec(D), geo.row_spec(D), geo.col_spec(E)]
    out_shape = [jax.ShapeDtypeStruct((B, P, D), F32), jax.ShapeDtypeStruct((B, P, D), BF16),
                 jax.ShapeDtypeStruct((B, E, P), F32)]
    return out_specs, out_shape


def _post_mixer(geo, y, w_out, b_out, xs, modl, lng, lnb, w_router, alpha):
    D = geo.D
    Kd = y.shape[-1]
    E = w_router.shape[-1]
    has_bias = b_out is not None
    ins = [y, w_out.astype(BF16)]
    specs = [geo.row_spec(Kd), _full_spec((Kd, D))]
    if has_bias:
        ins.append(b_out.reshape(1, D))
        specs.append(_full_spec((1, D)))
    ins += [xs, modl, lng.reshape(1, D), lnb.reshape(1, D), modl, modl, w_router.T]
    specs += [geo.row_spec(D), geo.mod_spec(2), _full_spec((1, D)), _full_spec((1, D)),
              geo.mod_spec(3), geo.mod_spec(4), _full_spec((E, D))]
    out_specs, out_shape = _post_outs(geo, E)
    return pl.pallas_call(
        functools.partial(_post_kernel, alpha=alpha, has_bias=has_bias),
        grid=(geo.B, geo.nt), in_specs=specs, out_specs=out_specs, out_shape=out_shape,
        compiler_params=_cp(2), name="post_mixer",
    )(*ins)


def _dwconv_seg(r, cw, cb, S, C, lo):
    P = S + C
    row = lax.broadcasted_iota(jnp.int32, r.shape, 0)
    tl = jnp.where(row < S, row, row - S)
    sl = jnp.where(row < S, S, C)
    acc = jnp.zeros_like(r) + cb
    for k in range(cw.shape[0]):
        off = k - lo
        if off == 0:
            term = r
        else:
            shifted = pltpu.roll(r, (-off) % P, 0)
            valid = jnp.logical_and(tl + off >= 0, tl + off < sl)
            term = jnp.where(valid, shifted, 0.0)
        acc = acc + cw[k:k + 1, :] * term
    return acc


def _rg_in_kernel(x_ref, sh_ref, sc_ref, wg_ref, wr_ref, g_ref, r_ref):
    h = (x_ref[0] * (1.0 + sc_ref[...]) + sh_ref[...]).astype(BF16)
    g = jnp.dot(h, wg_ref[...], preferred_element_type=F32)
    g_ref[0] = jax.nn.gelu(g).astype(g_ref.dtype)
    r_ref[0] = jnp.dot(h, wr_ref[...], preferred_element_type=F32)


def _rg_scan_kernel(r_ref, g_ref, cw_ref, cb_ref, gw_ref, gb_ref, lam_ref, o_ref,
                    a_scr, b_scr, h_scr, *, S, C):
    P = S + C
    n = r_ref.shape[-1]
    r = r_ref[0]
    rc = _dwconv_seg(r, cw_ref[...], cb_ref[...], S, C, cw_ref.shape[0] // 2)
    rcb = rc.astype(BF16)
    row = lax.broadcasted_iota(jnp.int32, (P, n), 0)
    sub = jnp.bitwise_and(row, SUBLANES - 1)
    for d in range(2):
        rev = d == 1
        gr = jax.nn.sigmoid(jnp.dot(rcb, gw_ref[d, 0], preferred_element_type=F32) + gb_ref[d, 0])
        gi = jax.nn.sigmoid(jnp.dot(rcb, gw_ref[d, 1], preferred_element_type=F32) + gb_ref[d, 1])
        nl = -lam_ref[d]
        sp = jnp.maximum(nl, 0.0) + jnp.log1p(jnp.exp(-jnp.abs(nl)))
        a = jnp.exp(-LRU_C * gr * sp)
        bb = jnp.sqrt(1.0 - a * a) * gi * rc
        for s in (1, 2, 4):
            shift = (P - s) if rev else s
            a_sh = pltpu.roll(a, shift, 0)
            b_sh = pltpu.roll(bb, shift, 0)
            m = (sub < SUBLANES - s) if rev else (sub >= s)
            bb = jnp.where(m, a * b_sh + bb, bb)
            a = jnp.where(m, a * a_sh, a)
        a_scr[...] = a
        b_scr[...] = bb

        def chain(lo_tile, n_tiles, c0):
            def body(i, c):
                t = (lo_tile + n_tiles - 1 - i) if rev else (lo_tile + i)
                off = pl.multiple_of(t * SUBLANES, SUBLANES)
                h = b_scr[pl.ds(off, SUBLANES), :] + a_scr[pl.ds(off, SUBLANES), :] * c
                if d == 0:
                    h_scr[pl.ds(off, SUBLANES), :] = h
                else:
                    h_scr[pl.ds(off, SUBLANES), :] = h_scr[pl.ds(off, SUBLANES), :] + h
                edge = h[0:1] if rev else h[SUBLANES - 1:SUBLANES]
                return jnp.broadcast_to(edge, (SUBLANES, n))
            return lax.fori_loop(0, n_tiles, body, c0)

        c_ctx = chain(S // SUBLANES, C // SUBLANES, jnp.zeros((SUBLANES, n), F32))
        chain(0, S // SUBLANES, c_ctx)
    o_ref[0] = (g_ref[0].astype(F32) * h_scr[...]).astype(o_ref.dtype)


def _rglru_mixer(geo, xs, modl, w_in, conv_w, conv_b, gate_w, gate_b, lam):
    B, P, D, S, C = geo.B, geo.P, geo.D, geo.S, geo.C
    R = w_in.shape[1] // 2
    nb = R // RG_BLOCK
    w_in = w_in.astype(BF16)
    g, r = pl.pallas_call(
        _rg_in_kernel, grid=(B, geo.nt),
        in_specs=[geo.row_spec(D), geo.mod_spec(0), geo.mod_spec(1),
                  pl.BlockSpec((D, R), lambda b, i: (0, 0)), pl.BlockSpec((D, R), lambda b, i: (0, 1))],
        out_specs=[geo.row_spec(R), geo.row_spec(R)],
        out_shape=[jax.ShapeDtypeStruct((B, P, R), BF16), jax.ShapeDtypeStruct((B, P, R), F32)],
        compiler_params=_cp(2), name="rg_in",
    )(xs, modl, modl, w_in, w_in)
    K = conv_w.shape[0]
    seq_spec = pl.BlockSpec((1, P, RG_BLOCK), lambda b, n: (b, 0, n))
    y = pl.pallas_call(
        functools.partial(_rg_scan_kernel, S=S, C=C), grid=(B, nb),
        in_specs=[seq_spec, seq_spec,
                  pl.BlockSpec((K, RG_BLOCK), lambda b, n: (0, n)),
                  pl.BlockSpec((1, RG_BLOCK), lambda b, n: (0, n)),
                  pl.BlockSpec((2, 2, None, RG_BLOCK, RG_BLOCK), lambda b, n: (0, 0, n, 0, 0)),
                  pl.BlockSpec((2, 2, 1, RG_BLOCK), lambda b, n: (0, 0, 0, n)),
                  pl.BlockSpec((2, 1, RG_BLOCK), lambda b, n: (0, 0, n))],
        out_specs=seq_spec,
        out_shape=jax.ShapeDtypeStruct((B, P, R), BF16),
        scratch_shapes=[pltpu.VMEM((P, RG_BLOCK), F32)] * 3,
        compiler_params=_cp(2), name="rg_scan",
    )(r, g, conv_w, conv_b.reshape(1, R), gate_w.astype(BF16), gate_b.reshape(2, 2, 1, R),
      lam.reshape(2, 1, R))
    return y


def _mm_bias_kernel(x_ref, sh_ref, sc_ref, w_ref, b_ref, o_ref):
    h = (x_ref[0] * (1.0 + sc_ref[...]) + sh_ref[...]).astype(BF16)
    o_ref[0] = (jnp.dot(h, w_ref[...], preferred_element_type=F32) + b_ref[...]).astype(o_ref.dtype)


def _short_conv_kernel(u_ref, cw_ref, cb_ref, o_ref, ob_ref, *, S, C):
    y = _dwconv_seg(u_ref[0], cw_ref[...], cb_ref[...], S, C, (cw_ref.shape[0] - 1) // 2)
    o_ref[0] = y
    ob_ref[0] = y.astype(BF16)


def _dft_table_kernel(c_ref, s_ref, st_ref, *, L, TF):
    i = pl.program_id(0)
    N = 2 * L
    f = lax.broadcasted_iota(jnp.int32, (TF, L), 0) + i * TF
    t = lax.broadcasted_iota(jnp.int32, (TF, L), 1)
    ang = jnp.bitwise_and(f * t, N - 1).astype(F32) * (2.0 * math.pi / N)
    c_ref[...] = jnp.cos(ang).astype(BF16)
    nyq_t = (1 - 2 * jnp.bitwise_and(t, 1)).astype(F32)
    s_ref[...] = jnp.where(f == 0, nyq_t, jnp.sin(ang)).astype(BF16)
    nyq_f = (1 - 2 * jnp.bitwise_and(f, 1)).astype(F32)
    st_ref[...] = jnp.where(t == 0, nyq_f, jnp.sin(ang)).astype(BF16)


def _dft_tables(L):
    TF = min(L, 256)
    shp = jax.ShapeDtypeStruct((L, L), BF16)
    spec = pl.BlockSpec((TF, L), lambda i: (i, 0))
    return pl.pallas_call(
        functools.partial(_dft_table_kernel, L=L, TF=TF), grid=(L // TF,),
        in_specs=[], out_specs=[spec, spec, spec], out_shape=[shp, shp, shp],
        compiler_params=_cp(1), name="dft_tables",
    )()


def _hy_filter_kernel(z_ref, w1_ref, b1_ref, w2_ref, b2_ref, w3_ref, b3_ref, fq_ref,
                      w4f_ref, w4b_ref, df_ref, db_ref, tn_ref, kp_ref, km_ref):
    fq = fq_ref[...]

    def lin(h, w_ref, b_ref):
        return jnp.dot(h, w_ref[...], precision=HIGHEST, preferred_element_type=F32) + b_ref[...]

    h = jnp.sin(fq * lin(z_ref[...], w1_ref, b1_ref))
    h = jnp.sin(fq * lin(h, w2_ref, b2_ref))
    h = jnp.sin(fq * lin(h, w3_ref, b3_ref))
    tn = tn_ref[...]
    hf = jnp.dot(h, w4f_ref[...], precision=HIGHEST, preferred_element_type=F32)
    hf = hf * jnp.exp(-tn * jnp.abs(df_ref[...]))
    hb = jnp.dot(h, w4b_ref[...], precision=HIGHEST, preferred_element_type=F32)
    hb = hb * jnp.exp(-tn * jnp.abs(db_ref[...]))
    row = lax.broadcasted_iota(jnp.int32, hb.shape, 0)
    hb = jnp.where(row == 0, 0.0, hb)
    nrm = lax.rsqrt(jnp.sum(hf * hf, axis=0, keepdims=True) + jnp.sum(hb * hb, axis=0, keepdims=True) + 1e-6)
    hf = hf * nrm
    hb = hb * nrm
    kp_ref[...] = (hf + hb).astype(BF16)
    km_ref[...] = (hf - hb).astype(BF16)


def _hy_spectrum_kernel(c_ref, s_ref, s0_ref, kp_ref, km_ref, ka_ref, kb_ref, kc_ref, *, L, TF):
    i = pl.program_id(0)
    inv_n = 1.0 / (2 * L)
    kr = jnp.dot(c_ref[...], kp_ref[...], preferred_element_type=F32)
    ks = jnp.dot(s_ref[...], km_ref[...], preferred_element_type=F32)
    nyq = jnp.dot(s0_ref[...], kp_ref[...], preferred_element_type=F32)[0:1]
    f = lax.broadcasted_iota(jnp.int32, kr.shape, 0) + i * TF
    dc = f == 0
    ka_ref[...] = jnp.where(dc, kr * inv_n, kr * (2.0 * inv_n))
    kb_ref[...] = jnp.where(dc, 0.0, ks * (-2.0 * inv_n))
    kc_ref[...] = jnp.where(dc, nyq * inv_n, kr * (2.0 * inv_n))


def _hy_filters(L, tabs, fw1, fb1, fw2, fb2, fw3, fb3, fw4, ffreq, fdecay, D):
    cm, sm, _ = tabs
    E = fw1.shape[0]
    Hd = fw1.shape[1]
    bands = (E - 1) // 2
    t = jnp.arange(L, dtype=F32)
    t_norm = t / max(L - 1, 1)
    fr = jnp.linspace(1e-4, bands - 1, bands, dtype=F32)
    ang = (2.0 * math.pi / L) * t[:, None] * fr[None, :]
    z = jnp.concatenate([t_norm[:, None], jnp.cos(ang), -jnp.sin(ang)], -1)
    Ep, Hp = -(-E // LANES) * LANES, -(-Hd // LANES) * LANES
    z = jnp.pad(z, ((0, 0), (0, Ep - E)))
    fw1 = jnp.pad(fw1, ((0, Ep - E), (0, Hp - Hd)))
    fw2 = jnp.pad(fw2, ((0, Hp - Hd), (0, Hp - Hd)))
    fw3 = jnp.pad(fw3, ((0, Hp - Hd), (0, Hp - Hd)))
    fw4 = jnp.pad(fw4, ((0, Hp - Hd), (0, 0)))
    fb1, fb2, fb3, ffreq = (jnp.pad(v, (0, Hp - Hd)) for v in (fb1, fb2, fb3, ffreq))
    E, Hd = Ep, Hp
    CT = min(2 * D, 512)
    nct = 2 * D // CT
    dec = fdecay.reshape(1, 4 * D)
    kp, km = pl.pallas_call(
        _hy_filter_kernel, grid=(nct,),
        in_specs=[_full_spec((L, E)), _full_spec((E, Hd)), _full_spec((1, Hd)), _full_spec((Hd, Hd)),
                  _full_spec((1, Hd)), _full_spec((Hd, Hd)), _full_spec((1, Hd)), _full_spec((1, Hd)),
                  pl.BlockSpec((Hd, CT), lambda j: (0, j)), pl.BlockSpec((Hd, CT), lambda j: (0, j + nct)),
                  pl.BlockSpec((1, CT), lambda j: (0, j)), pl.BlockSpec((1, CT), lambda j: (0, j + nct)),
                  _full_spec((L, 1))],
        out_specs=[pl.BlockSpec((L, CT), lambda j: (0, j))] * 2,
        out_shape=[jax.ShapeDtypeStruct((L, 2 * D), BF16)] * 2,
        compiler_params=_cp(1), name="hy_filter",
    )(z, fw1, fb1.reshape(1, Hd), fw2, fb2.reshape(1, Hd), fw3, fb3.reshape(1, Hd), ffreq.reshape(1, Hd),
      fw4, fw4, dec, dec, t_norm[:, None])
    TF = min(L, 256)
    spec_w = pl.BlockSpec((TF, L), lambda i, j: (i, 0))
    spec_k = pl.BlockSpec((L, CT), lambda i, j: (0, j))
    spec_o = pl.BlockSpec((TF, CT), lambda i, j: (i, j))
    shp = jax.ShapeDtypeStruct((L, 2 * D), F32)
    return pl.pallas_call(
        functools.partial(_hy_spectrum_kernel, L=L, TF=TF), grid=(L // TF, nct),
        in_specs=[spec_w, spec_w, pl.BlockSpec((SUBLANES, L), lambda i, j: (0, 0)), spec_k, spec_k],
        out_specs=[spec_o] * 3, out_shape=[shp] * 3,
        compiler_params=_cp(2), name="hy_spectrum",
    )(cm, sm, sm, kp, km)


def _hy_fwd_kernel(z_ref, c_ref, s_ref, ka_ref, kb_ref, kc_ref, p_ref):
    z = z_ref[0]
    zr = jnp.dot(c_ref[...], z, preferred_element_type=F32)
    zs = jnp.dot(s_ref[...], z, preferred_element_type=F32)
    kb = kb_ref[...]
    p_ref[0, 0] = (zr * ka_ref[...] + zs * kb).astype(BF16)
    p_ref[0, 1] = (zs * kc_ref[...] - zr * kb).astype(BF16)


def _hy_inv_kernel(p_ref, c_ref, st_ref, z_ref, x_ref, fb_ref, o_ref):
    y = jnp.dot(c_ref[...], p_ref[0, 0], preferred_element_type=F32)
    y = y + jnp.dot(st_ref[...], p_ref[0, 1], preferred_element_type=F32)
    o_ref[0] = (x_ref[0] * (y + z_ref[0].astype(F32) * fb_ref[...])).astype(o_ref.dtype)


def _hy_inv_kernel_alias(p_ref, c_ref, st_ref, z_ref, x_ref, fb_ref, prev_ref, o_ref):
    del prev_ref
    _hy_inv_kernel(p_ref, c_ref, st_ref, z_ref, x_ref, fb_ref, o_ref)


def _hy_conv(geo, L, off, tabs, z, z_col, kfilt, k_col, xmul, x_col, fbias, out):
    B, P, D = geo.B, geo.P, geo.D
    cm, sm, smt = tabs
    ka, kb, kc = kfilt
    rb = off // L
    TF = min(L, 256)
    spec_w = pl.BlockSpec((TF, L), lambda b, i: (i, 0))
    spec_k = pl.BlockSpec((TF, D), lambda b, i: (i, k_col))
    p = pl.pallas_call(
        _hy_fwd_kernel, grid=(B, L // TF),
        in_specs=[pl.BlockSpec((1, L, D), lambda b, i: (b, rb, z_col)), spec_w, spec_w,
                  spec_k, spec_k, spec_k],
        out_specs=pl.BlockSpec((1, 2, TF, D), lambda b, i: (b, 0, i, 0)),
        out_shape=jax.ShapeDtypeStruct((B, 2, L, D), BF16),
        compiler_params=_cp(2), name="hy_fwd",
    )(z, cm, sm, ka, kb, kc)
    CT = min(D, 512)
    nct = D // CT
    rt = off // TF
    spec_wi = pl.BlockSpec((TF, L), lambda b, j, i: (i, 0))
    return pl.pallas_call(
        _hy_inv_kernel_alias, grid=(B, nct, L // TF),
        in_specs=[pl.BlockSpec((1, 2, L, CT), lambda b, j, i: (b, 0, 0, j)), spec_wi, spec_wi,
                  pl.BlockSpec((1, TF, CT), lambda b, j, i: (b, rt + i, z_col * nct + j)),
                  pl.BlockSpec((1, TF, CT), lambda b, j, i: (b, rt + i, x_col * nct + j)),
                  pl.BlockSpec((1, CT), lambda b, j, i: (0, k_col * nct + j)),
                  pl.BlockSpec(memory_space=pl.ANY)],
        out_specs=pl.BlockSpec((1, TF, CT), lambda b, j, i: (b, rt + i, j)),
        out_shape=jax.ShapeDtypeStruct((B, P, D), BF16),
        input_output_aliases={6: 0},
        compiler_params=_cp(3), name="hy_inv",
    )(p, cm, smt, z, xmul, fbias, out)


def _hyena_mixer(geo, xs, modl, w_in, b_in, short_w, short_b, fw1, fb1, fw2, fb2, fw3, fb3, fw4,
                 ffreq, fdecay, fbias):
    B, P, D, S, C = geo.B, geo.P, geo.D, geo.S, geo.C
    u0 = pl.pallas_call(
        _mm_bias_kernel, grid=(B, geo.nt, 3),
        in_specs=[geo.row_spec(D), geo.mod_spec(0), geo.mod_spec(1),
                  pl.BlockSpec((D, D), lambda b, i, j: (0, j)), pl.BlockSpec((1, D), lambda b, i, j: (0, j))],
        out_specs=pl.BlockSpec((1, geo.TM, D), lambda b, i, j: (b, i, j)),
        out_shape=jax.ShapeDtypeStruct((B, P, 3 * D), F32),
        compiler_params=_cp(3), name="hy_in",
    )(xs, modl, modl, w_in.astype(BF16), b_in.reshape(1, 3 * D))
    CT = min(D, 256)
    Ks = short_w.shape[0]
    spec = pl.BlockSpec((1, P, CT), lambda b, j: (b, 0, j))
    u, ub = pl.pallas_call(
        functools.partial(_short_conv_kernel, S=S, C=C), grid=(B, 3 * D // CT),
        in_specs=[spec, pl.BlockSpec((Ks, CT), lambda b, j: (0, j)), pl.BlockSpec((1, CT), lambda b, j: (0, j))],
        out_specs=[spec, spec],
        out_shape=[jax.ShapeDtypeStruct((B, P, 3 * D), F32), jax.ShapeDtypeStruct((B, P, 3 * D), BF16)],
        compiler_params=_cp(2), name="hy_short",
    )(u0, short_w, short_b.reshape(1, 3 * D))
    fb = fbias.reshape(1, 2 * D)
    z1 = jnp.zeros((B, P, D), BF16)
    z2 = jnp.zeros((B, P, D), BF16)
    segs = [(S, 0), (C, S)]
    convs = []
    for L, off in segs:
        tabs = _dft_tables(L)
        kf = _hy_filters(L, tabs, fw1, fb1, fw2, fb2, fw3, fb3, fw4, ffreq, fdecay, D)
        convs.append((L, off, tabs, kf))
    for L, off, tabs, kf in convs:
        z1 = _hy_conv(geo, L, off, tabs, ub, 0, kf, 0, u, 1, fb, z1)
    for L, off, tabs, kf in convs:
        z2 = _hy_conv(geo, L, off, tabs, z1, 0, kf, 1, u, 2, fb, z2)
    return z2


def _rope_tables(S, D):
    t = jnp.arange(S)
    row = (t // GRID_W).astype(F32)
    col = (t % GRID_W).astype(F32)
    axis_dim = DA_HEAD_DIM // 2
    half = axis_dim // 2
    inv = ROPE_THETA ** (-jnp.arange(0, axis_dim, 2, dtype=F32) / axis_dim)
    lane = jnp.arange(D)
    within = lane % DA_HEAD_DIM
    pos = jnp.where((within // axis_dim)[None, :] == 0, row[:, None], col[:, None])
    ang = pos * inv[lane % half][None, :]
    sign = jnp.where((lane % axis_dim) < half, -1.0, 1.0)[None, :]
    return jnp.cos(ang), jnp.sin(ang) * sign


def _da_in_kernel(x_ref, sh_ref, sc_ref, w_ref, cos_ref, sin_ref, o_ref, *, n_lat):
    i = pl.program_id(1)
    j = pl.program_id(2)
    h = (x_ref[0] * (1.0 + sc_ref[...]) + sh_ref[...]).astype(BF16)
    acc = jnp.dot(h, w_ref[...], preferred_element_type=F32)
    acc = acc * jnp.where(j == 0, DA_HEAD_DIM ** -0.5, 1.0)
    rot = jnp.logical_and(i < n_lat, j < 2)

    @pl.when(rot)
    def _():
        Dn = acc.shape[-1]
        half = DA_HEAD_DIM // 4
        lane = lax.broadcasted_iota(jnp.int32, acc.shape, 1)
        up = pltpu.roll(acc, Dn - half, 1)
        dn = pltpu.roll(acc, half, 1)
        partner = jnp.where(jnp.bitwise_and(lane, 2 * half - 1) < half, up, dn)
        o_ref[0] = (acc * cos_ref[...] + partner * sin_ref[...]).astype(o_ref.dtype)

    @pl.when(jnp.logical_not(rot))
    def _():
        o_ref[0] = acc.astype(o_ref.dtype)


def _da_attn_kernel(q_ref, k_ref, v_ref, lam_ref, sub_ref, o_ref, vx_scr, *, S, C, n_lat, lam_init):
    i = pl.program_id(2)
    HW = v_ref.shape[-1]
    lp = lam_ref[...]
    lam = (jnp.exp(jnp.sum(lp[0:1] * lp[1:2], axis=1, keepdims=True))
           - jnp.exp(jnp.sum(lp[2:3] * lp[3:4], axis=1, keepdims=True)) + lam_init)

    @pl.when(i == 0)
    def _():
        vx_scr[:, :HW] = v_ref[0]
        vx_scr[:, HW:] = jnp.ones((vx_scr.shape[0], HW), BF16)

    def attend(k, vx):
        q = q_ref[0]
        outs = []
        for c in range(2):
            qc = q[:, c * DA_HEAD_DIM:(c + 1) * DA_HEAD_DIM]
            kc = k[:, c * DA_HEAD_DIM:(c + 1) * DA_HEAD_DIM]
            s = lax.dot_general(qc, kc, (((1,), (1,)), ((), ())), preferred_element_type=F32)
            m = jnp.max(s, axis=-1, keepdims=True)
            p = jnp.exp((s - m).astype(BF16))
            ov = jnp.dot(p, vx, preferred_element_type=F32)
            outs.append(ov[:, :HW] / ov[:, HW:HW + 1])
        o = outs[0] - lam * outs[1]
        o = o * lax.rsqrt(jnp.mean(o * o, axis=-1, keepdims=True) + 1e-5) * sub_ref[...] * (1.0 - lam_init)
        o_ref[0] = o.astype(o_ref.dtype)

    @pl.when(i < n_lat)
    def _():
        attend(k_ref[0], vx_scr[...])

    @pl.when(i >= n_lat)
    def _():
        attend(k_ref[0, S:S + C, :], vx_scr[S:S + C, :])


def _diff_attention_mixer(geo, xs, modl, w_in, lam_p, subln_w, layer_idx):
    B, P, D, S, C = geo.B, geo.P, geo.D, geo.S, geo.C
    H = D // (2 * DA_HEAD_DIM)
    HW = 2 * DA_HEAD_DIM
    cos_t, sin_t = _rope_tables(S, D)
    n_lat = geo.n_lat
    tab_spec = pl.BlockSpec((geo.TM, D), lambda b, i, j: (jnp.minimum(i, n_lat - 1), 0))
    qkv = pl.pallas_call(
        functools.partial(_da_in_kernel, n_lat=n_lat), grid=(B, geo.nt, 3),
        in_specs=[geo.row_spec(D), geo.mod_spec(0), geo.mod_spec(1),
                  pl.BlockSpec((D, D), lambda b, i, j: (0, j)), tab_spec, tab_spec],
        out_specs=pl.BlockSpec((1, geo.TM, D), lambda b, i, j: (b, i, j)),
        out_shape=jax.ShapeDtypeStruct((B, P, 3 * D), BF16),
        compiler_params=_cp(3), name="da_in",
    )(xs, modl, modl, w_in.astype(BF16), cos_t, sin_t)
    lam_init = 0.8 - 0.6 * math.exp(-0.3 * layer_idx)
    return pl.pallas_call(
        functools.partial(_da_attn_kernel, S=S, C=C, n_lat=n_lat, lam_init=lam_init),
        grid=(B, H, geo.nt),
        in_specs=[pl.BlockSpec((1, geo.TM, HW), lambda b, h, i: (b, i, h)),
                  pl.BlockSpec((1, P, HW), lambda b, h, i: (b, 0, H + h)),
                  pl.BlockSpec((1, P, HW), lambda b, h, i: (b, 0, 2 * H + h)),
                  _full_spec((4, DA_HEAD_DIM)), _full_spec((1, HW))],
        out_specs=pl.BlockSpec((1, geo.TM, HW), lambda b, h, i: (b, i, h)),
        out_shape=jax.ShapeDtypeStruct((B, P, D), BF16),
        scratch_shapes=[pltpu.VMEM((P, 2 * HW), BF16)],
        compiler_params=_cp(3), name="da_attn",
    )(qkv, qkv, qkv, lam_p, subln_w.reshape(1, HW))


def _s5_operators(a_re, a_im, log_step, b_re, b_im, c_re, c_im):
    T = S5_CHUNK
    G, Pst = a_re.shape[1], a_re.shape[2]
    Hg = S5_GROUP
    GL = LANES // Hg
    LB = G // GL
    lam = lax.complex(jnp.minimum(a_re.astype(F32), -1e-4), a_im.astype(F32))
    step = jnp.exp(log_step.astype(F32))[..., None]
    abar = jnp.exp(lam * step)
    bbar = ((abar - 1.0) / lam)[..., None] * lax.complex(b_re.astype(F32), b_im.astype(F32))
    cmat = lax.complex(c_re.astype(F32), c_im.astype(F32))
    pows = jnp.stack([abar ** l for l in range(T + 1)], axis=1)
    eye = jnp.eye(GL, dtype=F32)
    ar = jnp.arange(T)
    big_m, big_g, big_h, a_t = [], [], [], []
    for d in range(2):
        pw = pows[d]
        kl = jnp.einsum('gjp,lgp,gph->lgjh', cmat[d], pw[:T], bbar[d]).real
        lag = (ar[None, :] - ar[:, None]) if d == 0 else (ar[:, None] - ar[None, :])
        tz = jnp.where((lag >= 0)[:, :, None, None, None], kl[jnp.clip(lag, 0, T - 1)], 0.0)
        tz = tz.reshape(T, T, LB, GL, Hg, Hg)
        m = jnp.einsum('stbgjh,gk->bsghtkj', tz, eye).reshape(LB, T * LANES, T * LANES)
        e_in = (T - 1 - ar) if d == 0 else ar
        gc = pw[e_in][:, :, :, None] * bbar[d][None]
        gc = gc.reshape(T, LB, GL, Pst, Hg)
        g_re = jnp.einsum('sbgph,gk->bsghkp', gc.real, eye).reshape(LB, T * LANES, GL * Pst)
        g_im = jnp.einsum('sbgph,gk->bsghkp', gc.imag, eye).reshape(LB, T * LANES, GL * Pst)
        e_out = (ar + 1) if d == 0 else (T - ar)
        hc = cmat[d][None] * pw[e_out][:, :, None, :]
        hc = hc.reshape(T, LB, GL, Hg, Pst)
        h_re = jnp.einsum('tbgjp,gk->bgptkj', hc.real, eye).reshape(LB, GL * Pst, T * LANES)
        h_im = jnp.einsum('tbgjp,gk->bgptkj', -hc.imag, eye).reshape(LB, GL * Pst, T * LANES)
        big_m.append(m)
        big_g.append(jnp.concatenate([g_re, g_im], axis=2))
        big_h.append(jnp.concatenate([h_re, h_im], axis=1))
        at = pw[T].reshape(LB, 1, GL * Pst)
        a_t.append(jnp.concatenate([at.real, at.imag], axis=2))
    return (jnp.stack(big_m).astype(BF16), jnp.stack(big_g).astype(BF16),
            jnp.stack(big_h).astype(BF16), jnp.stack(a_t).astype(F32))


def _modulate_kernel(x_ref, sh_ref, sc_ref, o_ref):
    o_ref[0] = x_ref[0] * (1.0 + sc_ref[...]) + sh_ref[...]


def _s5_kernel(u_ref, m_ref, g_ref, h_ref, a_ref, o_ref, gx_scr, sp_scr, *, S, C):
    d = pl.program_id(1)
    T = S5_CHUNK
    P = S + C
    n = P // T
    n_lat = S // T
    x = jnp.concatenate([u_ref[0, pl.ds(s, n, stride=T), :] for s in range(T)], axis=1).astype(BF16)
    gx_scr[...] = jnp.dot(x, g_ref[...], preferred_element_type=F32)
    ns = a_ref.shape[-1] // 2
    a_r = a_ref[:, :ns]
    a_i = a_ref[:, ns:]

    def scan(lo, cnt, rev, carry):
        def body(k, st):
            s_r, s_i = st
            c = (lo + cnt - 1 - k) if rev else (lo + k)
            sp_scr[pl.ds(c, 1), :] = jnp.concatenate([s_r, s_i], axis=1)
            gx = gx_scr[pl.ds(c, 1), :]
            return (a_r * s_r - a_i * s_i + gx[:, :ns], a_r * s_i + a_i * s_r + gx[:, ns:])
        return lax.fori_loop(0, cnt, body, carry)

    zero = (jnp.zeros((1, ns), F32), jnp.zeros((1, ns), F32))

    @pl.when(d == 0)
    def _():
        scan(0, n_lat, False, scan(n_lat, n - n_lat, False, zero))

    @pl.when(d == 1)
    def _():
        scan(0, n_lat, True, scan(n_lat, n - n_lat, True, zero))

    y = jnp.dot(x, m_ref[...], preferred_element_type=F32)
    y = y + jnp.dot(sp_scr[...].astype(BF16), h_ref[...], preferred_element_type=F32)
    for s in range(T):
        o_ref[0, pl.ds(s, n, stride=T), :] = y[:, s * LANES:(s + 1) * LANES]


def _s5_mixer(geo, xs, modl, a_re, a_im, log_step, b_re, b_im, c_re, c_im):
    B, P, D, S, C = geo.B, geo.P, geo.D, geo.S, geo.C
    u = pl.pallas_call(
        _modulate_kernel, grid=(B, geo.nt),
        in_specs=[geo.row_spec(D), geo.mod_spec(0), geo.mod_spec(1)],
        out_specs=geo.row_spec(D), out_shape=jax.ShapeDtypeStruct((B, P, D), F32),
        compiler_params=_cp(2), name="s5_modulate",
    )(xs, modl, modl)
    big_m, big_g, big_h, a_t = _s5_operators(a_re, a_im, log_step, b_re, b_im, c_re, c_im)
    LB = D // LANES
    TL = S5_CHUNK * LANES
    NS = big_g.shape[-1]
    n = P // S5_CHUNK
    y = pl.pallas_call(
        functools.partial(_s5_kernel, S=S, C=C), grid=(LB, 2, B),
        in_specs=[pl.BlockSpec((1, P, LANES), lambda l, d, b: (b, 0, l)),
                  pl.BlockSpec((None, None, TL, TL), lambda l, d, b: (d, l, 0, 0)),
                  pl.BlockSpec((None, None, TL, NS), lambda l, d, b: (d, l, 0, 0)),
                  pl.BlockSpec((None, None, NS, TL), lambda l, d, b: (d, l, 0, 0)),
                  pl.BlockSpec((None, None, 1, NS), lambda l, d, b: (d, l, 0, 0))],
        out_specs=pl.BlockSpec((None, 1, P, LANES), lambda l, d, b: (d, b, 0, l)),
        out_shape=jax.ShapeDtypeStruct((2, B, P, D), F32),
        scratch_shapes=[pltpu.VMEM((n, NS), F32), pltpu.VMEM((n, NS), F32)],
        compiler_params=_cp(3), name="s5_scan",
    )(u, big_m, big_g, big_h, a_t)
    return u, y


def _s5_post_kernel(yf_ref, yb_ref, u_ref, d_ref, w_ref, b_ref, x_ref, gate_ref, lng_ref, lnb_ref,
                    sh_ref, sc_ref, wr_ref, xm_ref, f_ref, s_ref, *, alpha):
    g = jax.nn.gelu(yf_ref[0] + yb_ref[0] + d_ref[...] * u_ref[0])
    vg = jnp.dot(g.astype(BF16), w_ref[...], preferred_element_type=F32) + b_ref[...]
    Dn = vg.shape[-1] // 2
    y = vg[:, :Dn] * jax.nn.sigmoid(vg[:, Dn:])
    _finish_mixer(y, x_ref[0], gate_ref[...], lng_ref[...], lnb_ref[...], sh_ref[...], sc_ref[...],
                  wr_ref, xm_ref, f_ref, s_ref, alpha)


def _s5_post(geo, y2, u, d_skip, w_glu, b_glu, xs, modl, lng, lnb, w_router, alpha):
    D = geo.D
    E = w_router.shape[-1]
    out_specs, out_shape = _post_outs(geo, E)
    TM = geo.TM
    return pl.pallas_call(
        functools.partial(_s5_post_kernel, alpha=alpha), grid=(geo.B, geo.nt),
        in_specs=[pl.BlockSpec((None, 1, TM, D), lambda b, i: (0, b, i, 0)),
                  pl.BlockSpec((None, 1, TM, D), lambda b, i: (1, b, i, 0)),
                  geo.row_spec(D), _full_spec((1, D)), _full_spec((D, 2 * D)), _full_spec((1, 2 * D)),
                  geo.row_spec(D), geo.mod_spec(2), _full_spec((1, D)), _full_spec((1, D)),
                  geo.mod_spec(3), geo.mod_spec(4), _full_spec((E, D))],
        out_specs=out_specs, out_shape=out_shape,
        compiler_params=_cp(2), name="s5_post",
    )(y2, y2, u, d_skip.reshape(1, D), w_glu.astype(BF16), b_glu.reshape(1, 2 * D), xs, modl,
      lng.reshape(1, D), lnb.reshape(1, D), modl, modl, w_router.T)


def _route_kernel(s_ref, b_ref, idx_ref, w_ref):
    sc = s_ref[0]
    E, TM = sc.shape
    biased = sc + b_ref[...]
    G = N_EXPERT_GROUPS
    per = E // G
    neg = -jnp.inf
    blocks, gs = [], []
    for g in range(G):
        blk = biased[g * per:(g + 1) * per]
        m1 = jnp.max(blk, axis=0, keepdims=True)
        is1 = blk == m1
        cnt = jnp.sum(is1.astype(F32), axis=0, keepdims=True)
        m2 = jnp.max(jnp.where(is1, neg, blk), axis=0, keepdims=True)
        blocks.append(blk)
        gs.append(m1 + jnp.where(cnt >= 2.0, m1, m2))
    masked = []
    for g in range(G):
        ahead = jnp.zeros((1, TM), F32)
        for h in range(G):
            if h < g:
                ahead = ahead + (gs[h] >= gs[g]).astype(F32)
            elif h > g:
                ahead = ahead + (gs[h] > gs[g]).astype(F32)
        masked.append(jnp.where(ahead < float(TOPK_GROUPS), blocks[g], neg))
    masked = jnp.concatenate(masked, axis=0)
    iota_e = lax.broadcasted_iota(jnp.int32, (E, TM), 0)
    idxs, ws = [], []
    for _ in range(TOP_K):
        m = jnp.max(masked, axis=0, keepdims=True)
        ik = jnp.min(jnp.where(masked == m, iota_e, E), axis=0, keepdims=True)
        sel = iota_e == ik
        ws.append(jnp.sum(jnp.where(sel, sc, 0.0), axis=0, keepdims=True))
        idxs.append(ik)
        masked = jnp.where(sel, neg, masked)
    tot = ws[0]
    for wk in ws[1:]:
        tot = tot + wk
    w = jnp.concatenate(ws, axis=0)
    idx_ref[0] = jnp.concatenate(idxs, axis=0)
    w_ref[0] = w / (tot + 1e-20) * ROUTED_SCALE


def _rank_kernel(idx_ref, rank_ref, cnt_ref, run_scr, *, E):
    first = jnp.logical_and(pl.program_id(0) == 0, pl.program_id(1) == 0)

    @pl.when(first)
    def _():
        run_scr[...] = jnp.zeros_like(run_scr)

    idx = idx_ref[0]
    K, TM = idx.shape
    iota_e = lax.broadcasted_iota(jnp.int32, (E, TM), 0)
    member = jnp.zeros((E, TM), F32)
    for k in range(K):
        member = member + (iota_e == idx[k:k + 1]).astype(F32)
    before = (lax.broadcasted_iota(jnp.int32, (TM, TM), 0)
              < lax.broadcasted_iota(jnp.int32, (TM, TM), 1)).astype(BF16)
    rank = jnp.dot(member.astype(BF16), before, preferred_element_type=F32) + run_scr[...]
    rows = [jnp.sum(jnp.where(iota_e == idx[k:k + 1], rank, 0.0), axis=0, keepdims=True) for k in range(K)]
    rank_ref[0] = jnp.concatenate(rows, axis=0).astype(jnp.int32)
    run_scr[...] = run_scr[...] + jnp.sum(member, axis=1, keepdims=True)
    cnt_ref[...] = run_scr[...]


def _dest_kernel(idx_ref, rank_ref, start_ref, dest_ref, *, E):
    idx = idx_ref[0]
    K, TM = idx.shape
    iota_e = lax.broadcasted_iota(jnp.int32, (E, TM), 0)
    start = start_ref[...]
    rows = [jnp.sum(jnp.where(iota_e == idx[k:k + 1], start, 0), axis=0, keepdims=True) for k in range(K)]
    dest_ref[0] = jnp.concatenate(rows, axis=0) + rank_ref[0]


def _route_dispatch(geo, scores, bias, blk):
    B, P = geo.B, geo.P
    E = scores.shape[1]
    K = TOP_K
    kspec = geo.col_spec(K)
    idx, w = pl.pallas_call(
        _route_kernel, grid=(B, geo.nt),
        in_specs=[geo.col_spec(E), _full_spec((E, 1))],
        out_specs=[kspec, kspec],
        out_shape=[jax.ShapeDtypeStruct((B, K, P), jnp.int32), jax.ShapeDtypeStruct((B, K, P), F32)],
        compiler_params=_cp(2), name="moe_route",
    )(scores, bias.astype(F32).reshape(E, 1))
    rank, cnt = pl.pallas_call(
        functools.partial(_rank_kernel, E=E), grid=(B, geo.nt),
        in_specs=[kspec], out_specs=[kspec, _full_spec((E, 1))],
        out_shape=[jax.ShapeDtypeStruct((B, K, P), jnp.int32), jax.ShapeDtypeStruct((E, 1), F32)],
        scratch_shapes=[pltpu.VMEM((E, 1), F32)],
        compiler_params=_cp(2), name="moe_rank",
    )(idx)
    n_assign = B * P * K
    n_blocks = -(-(n_assign + E * (blk - 1)) // blk)
    counts = cnt[:, 0].astype(jnp.int32)
    pcounts = (counts + blk - 1) // blk * blk
    pends = jnp.cumsum(pcounts)
    starts = (pends - pcounts).astype(jnp.int32)
    n_used = pends[-1] // blk
    blk_e = jnp.minimum(jnp.searchsorted(pends, jnp.arange(n_blocks) * blk, side='right'), E - 1)
    blk_e = jnp.where(jnp.arange(n_blocks) < n_used, blk_e, blk_e[jnp.maximum(n_used - 1, 0)])
    dest = pl.pallas_call(
        functools.partial(_dest_kernel, E=E), grid=(B, geo.nt),
        in_specs=[kspec, kspec, _full_spec((E, 1))], out_specs=kspec,
        out_shape=jax.ShapeDtypeStruct((B, K, P), jnp.int32),
        compiler_params=_cp(2), name="moe_dest",
    )(idx, rank, starts.reshape(E, 1))
    return w, dest, blk_e.astype(jnp.int32), n_used.astype(jnp.int32).reshape(1), n_blocks


def _expert_kernel(be_ref, nu_ref, x_ref, wgu_ref, wd_ref, o_ref):
    del be_ref
    i = pl.program_id(0)

    @pl.when(i < nu_ref[0])
    def _():
        h = jnp.dot(x_ref[...], wgu_ref[...].astype(BF16), preferred_element_type=F32)
        Fh = h.shape[-1] // 2
        a = _silu(h[:, :Fh]) * h[:, Fh:]
        o_ref[...] = jnp.dot(a.astype(BF16), wd_ref[...].astype(BF16),
                             preferred_element_type=F32).astype(o_ref.dtype)

    @pl.when(i >= nu_ref[0])
    def _():
        o_ref[...] = jnp.zeros_like(o_ref)


def _expert_ffn(x_sorted, blk_e, n_used, w_gu, w_down, layer, blk):
    n_rows, D = x_sorted.shape
    F2 = w_gu.shape[-1]
    grid_spec = pltpu.PrefetchScalarGridSpec(
        num_scalar_prefetch=2, grid=(n_rows // blk,),
        in_specs=[pl.BlockSpec((blk, D), lambda i, be, nu: (i, 0)),
                  pl.BlockSpec((None, None, D, F2), lambda i, be, nu: (layer, be[i], 0, 0)),
                  pl.BlockSpec((None, None, F2 // 2, D), lambda i, be, nu: (layer, be[i], 0, 0))],
        out_specs=pl.BlockSpec((blk, D), lambda i, be, nu: (i, 0)))
    return pl.pallas_call(
        _expert_kernel, grid_spec=grid_spec,
        out_shape=jax.ShapeDtypeStruct((n_rows, D), BF16),
        compiler_params=_cp(1), name="moe_experts",
    )(blk_e, n_used, x_sorted, w_gu, w_down)


def _moe_final_kernel(xm_ref, f_ref, ga_ref, w_ref, shgu_ref, shd_ref, gate_ref, lng_ref, lnb_ref, o_ref,
                      *, alpha):
    h = jnp.dot(f_ref[0], shgu_ref[...], preferred_element_type=F32)
    Fh = h.shape[-1] // 2
    a = _silu(h[:, :Fh]) * h[:, Fh:]
    y = jnp.dot(a.astype(BF16), shd_ref[...], preferred_element_type=F32)
    w = w_ref[0]
    for k in range(w.shape[-1]):
        y = y + w[:, k:k + 1] * ga_ref[k, 0].astype(F32)
    o_ref[0] = _layer_norm(alpha * xm_ref[0] + gate_ref[...] * y, lng_ref[...], lnb_ref[...])


def _moe(geo, xm, f, scores, modl, bias, w_gu, w_down, sh_gu, sh_down, lng, lnb, layer, alpha, blk):
    B, P, D = geo.B, geo.P, geo.D
    T = B * P
    K = TOP_K
    w, dest, blk_e, n_used, n_blocks = _route_dispatch(geo, scores, bias, blk)
    dest_flat = jnp.swapaxes(dest, 0, 1).reshape(K * T)
    tok = jnp.broadcast_to(jnp.arange(T, dtype=jnp.int32)[None], (K, T)).reshape(K * T)
    row_tok = jnp.zeros((n_blocks * blk,), jnp.int32).at[dest_flat].set(tok)
    x_sorted = f.reshape(T, D)[row_tok]
    y_sorted = _expert_ffn(x_sorted, blk_e, n_used, w_gu, w_down, layer, blk)
    gathered = y_sorted[dest_flat].reshape(K, B, P, D)
    F2 = sh_gu.shape[-1]
    TM = geo.TM
    return pl.pallas_call(
        functools.partial(_moe_final_kernel, alpha=alpha), grid=(B, geo.nt),
        in_specs=[geo.row_spec(D), geo.row_spec(D),
                  pl.BlockSpec((K, 1, TM, D), lambda b, i: (0, b, i, 0)), geo.row_spec(K),
                  _full_spec((D, F2)), _full_spec((F2 // 2, D)), geo.mod_spec(5),
                  _full_spec((1, D)), _full_spec((1, D))],
        out_specs=geo.row_spec(D), out_shape=jax.ShapeDtypeStruct((B, P, D), F32),
        compiler_params=_cp(2), name="moe_final",
    )(xm, f, gathered, jnp.swapaxes(w, 1, 2), sh_gu.astype(BF16), sh_down.astype(BF16), modl,
      lng.reshape(1, D), lnb.reshape(1, D))


def kernel(x, c, ctx, c_ctx, mod_w, mod_b, ln_g, ln_b, rg_w_in, rg_conv_w, rg_conv_b, rg_gate_w, rg_gate_b, rg_lam, rg_w_out, hy_w_in, hy_b_in, hy_short_w, hy_short_b, hy_f_w1, hy_f_b1, hy_f_w2, hy_f_b2, hy_f_w3, hy_f_b3, hy_f_w4, hy_f_freq, hy_f_decay, hy_f_bias, hy_w_out, hy_b_out, da_w_in, da_lam, da_subln, da_w_out, s5_a_re, s5_a_im, s5_log_step, s5_b_re, s5_b_im, s5_c_re, s5_c_im, s5_d, s5_w_glu, s5_b_glu, moe_w_router, moe_bias, moe_w_gu, moe_w_down, moe_sh_gu, moe_sh_down):
    B, S, D = x.shape
    C = ctx.shape[1]
    depth = mod_w.shape[0]
    alpha = (2 * depth) ** 0.25
    geo = _Geo(B, S, C, D)
    xs = jnp.concatenate([x, ctx], axis=1)
    R = -(-(B + 1) // SUBLANES) * SUBLANES
    cc = jnp.zeros((R, D), F32).at[:B].set(c).at[B].set(c_ctx)
    modt = _mod_table(cc, mod_w, mod_b).reshape(depth, 6, R, 1, D)
    blk = min(MOE_BLOCK, geo.TM)
    for i in range(depth):
        kind, j = i % N_MIXERS, i // N_MIXERS
        modl = modt[i]
        post = functools.partial(_post_mixer, geo, xs=xs, modl=modl, lng=ln_g[i, 0], lnb=ln_b[i, 0],
                                 w_router=moe_w_router[i], alpha=alpha)
        if kind == 0:
            y = _rglru_mixer(geo, xs, modl, rg_w_in[j], rg_conv_w[j], rg_conv_b[j], rg_gate_w[j],
                             rg_gate_b[j], rg_lam[j])
            xm, f, scores = post(y=y, w_out=rg_w_out[j], b_out=None)
        elif kind == 1:
            y = _hyena_mixer(geo, xs, modl, hy_w_in[j], hy_b_in[j], hy_short_w[j], hy_short_b[j],
                             hy_f_w1[j], hy_f_b1[j], hy_f_w2[j], hy_f_b2[j], hy_f_w3[j], hy_f_b3[j],
                             hy_f_w4[j], hy_f_freq[j], hy_f_decay[j], hy_f_bias[j])
            xm, f, scores = post(y=y, w_out=hy_w_out[j], b_out=hy_b_out[j])
        elif kind == 2:
            y = _diff_attention_mixer(geo, xs, modl, da_w_in[j], da_lam[j], da_subln[j], i)
            xm, f, scores = post(y=y, w_out=da_w_out[j], b_out=None)
        else:
            u, y2 = _s5_mixer(geo, xs, modl, s5_a_re[j], s5_a_im[j], s5_log_step[j], s5_b_re[j],
                              s5_b_im[j], s5_c_re[j], s5_c_im[j])
            xm, f, scores = _s5_post(geo, y2, u, s5_d[j], s5_w_glu[j], s5_b_glu[j], xs, modl,
                                     ln_g[i, 0], ln_b[i, 0], moe_w_router[i], alpha)
        xs = _moe(geo, xm, f, scores, modl, moe_bias[i], moe_w_gu, moe_w_down, moe_sh_gu[i],
                  moe_sh_down[i], ln_g[i, 1], ln_b[i, 1], i, alpha, blk)
    return xs[:, :S]
```

```python
import functools
import math

import jax
import jax.numpy as jnp
from jax import lax
from jax.experimental import pallas as pl
from jax.experimental.pallas import tpu as pltpu

F32 = jnp.float32
BF16 = jnp.bfloat16
HIGHEST = lax.Precision.HIGHEST

N_MIXERS = 4
LN_EPS = 1e-6
LRU_C = 8.0
RG_BLOCK = 128
GRID_W = 64
DA_HEAD_DIM = 64
ROPE_THETA = 10000.0
S5_GROUP = 16
S5_CHUNK = 16
TOP_K = 8
N_EXPERT_GROUPS = 8
TOPK_GROUPS = 4
ROUTED_SCALE = 2.5
MOE_BLOCK = 256
DA_Q_TILE = 1024

LANES = 128
SUBLANES = 8
VMEM_LIMIT = 56 * 1024 * 1024


def _cp(n_grid):
    return pltpu.CompilerParams(dimension_semantics=("arbitrary",) * n_grid,
                                vmem_limit_bytes=VMEM_LIMIT)


def _silu(x):
    return x * jax.nn.sigmoid(x)


def _mod_table_kernel(c_ref, w_ref, b_ref, o_ref):
    s = _silu(c_ref[...])
    o_ref[...] = jnp.dot(s, w_ref[...], precision=HIGHEST, preferred_element_type=F32) + b_ref[...]


def _mod_table(cc, mod_w, mod_b):
    depth, D, _ = mod_w.shape
    R = cc.shape[0]
    return pl.pallas_call(
        _mod_table_kernel,
        grid=(depth, 6),
        in_specs=[pl.BlockSpec((R, D), lambda i, k: (0, 0)),
                  pl.BlockSpec((None, D, D), lambda i, k: (i, 0, k)),
                  pl.BlockSpec((None, None, 1, D), lambda i, k: (i, k, 0, 0))],
        out_specs=pl.BlockSpec((None, None, R, D), lambda i, k: (i, k, 0, 0)),
        out_shape=jax.ShapeDtypeStruct((depth, 6, R, D), F32),
        compiler_params=_cp(2), name="mod_table",
    )(cc, mod_w, mod_b.reshape(depth, 6, 1, D))


class _Geo:
    def __init__(self, B, S, C, D):
        self.B, self.S, self.C, self.D = B, S, C, D
        self.P = S + C
        self.TM = math.gcd(S, C)
        while self.TM > 256:
            self.TM //= 2
        self.nt = self.P // self.TM
        self.n_lat = S // self.TM

    def mod_spec(self, k, width=None, col=None):
        D, B, n_lat = self.D, self.B, self.n_lat
        width = D if width is None else width
        if col is None:
            return pl.BlockSpec((None, None, 1, width),
                                lambda b, i, *_: (k, jnp.where(i < n_lat, b, B), 0, 0))
        return pl.BlockSpec((None, None, 1, width),
                            lambda b, i, j, *_: (k, jnp.where(i < n_lat, b, B), 0, col(j)))

    def row_spec(self, width, col=0):
        return pl.BlockSpec((1, self.TM, width), lambda b, i, *_: (b, i, col))

    def col_spec(self, height):
        return pl.BlockSpec((1, height, self.TM), lambda b, i, *_: (b, 0, i))


def _full_spec(shape):
    nd = len(shape)
    return pl.BlockSpec(shape, lambda *_: (0,) * nd)


def _layer_norm(z, g, b):
    mu = jnp.mean(z, axis=-1, keepdims=True)
    zc = z - mu
    var = jnp.mean(zc * zc, axis=-1, keepdims=True)
    return zc * lax.rsqrt(var + LN_EPS) * g + b


def _finish_mixer(y, x, gate, lng, lnb, sh, sc, wr_ref, xm_ref, f_ref, s_ref, alpha):
    xn = _layer_norm(alpha * x + gate * y, lng, lnb)
    xm_ref[0] = xn
    f = xn * (1.0 + sc) + sh
    f_ref[0] = f.astype(f_ref.dtype)
    logits = lax.dot_general(wr_ref[...], f, (((1,), (1,)), ((), ())), precision=HIGHEST,
                             preferred_element_type=F32)
    s_ref[0] = jax.nn.sigmoid(logits)


def _post_kernel(*refs, alpha, has_bias):
    if has_bias:
        (y_ref, w_ref, b_ref, x_ref, gate_ref, lng_ref, lnb_ref, sh_ref, sc_ref, wr_ref,
         xm_ref, f_ref, s_ref) = refs
    else:
        (y_ref, w_ref, x_ref, gate_ref, lng_ref, lnb_ref, sh_ref, sc_ref, wr_ref,
         xm_ref, f_ref, s_ref) = refs
    y = jnp.dot(y_ref[0].astype(BF16), w_ref[...], preferred_element_type=F32)
    if has_bias:
        y = y + b_ref[...]
    _finish_mixer(y, x_ref[0], gate_ref[...], lng_ref[...], lnb_ref[...], sh_ref[...], sc_ref[...],
                  wr_ref, xm_ref, f_ref, s_ref, alpha)


def _post_outs(geo, E):
    B, P, D = geo.B, geo.P, geo.D
    out_specs = [geo.row_spec(D), geo.row_spec(D), geo.col_spec(E)]
    out_shape = [jax.ShapeDtypeStruct((B, P, D), F32), jax.ShapeDtypeStruct((B, P, D), BF16),
                 jax.ShapeDtypeStruct((B, E, P), F32)]
    return out_specs, out_shape


def _post_mixer(geo, y, w_out, b_out, xs, modl, lng, lnb, w_router, alpha):
    D = geo.D
    Kd = y.shape[-1]
    E = w_router.shape[-1]
    has_bias = b_out is not None
    ins = [y, w_out.astype(BF16)]
    specs = [geo.row_spec(Kd), _full_spec((Kd, D))]
    if has_bias:
        ins.append(b_out.reshape(1, D))
        specs.append(_full_spec((1, D)))
    ins += [xs, modl, lng.reshape(1, D), lnb.reshape(1, D), modl, modl, w_router.T]
    specs += [geo.row_spec(D), geo.mod_spec(2), _full_spec((1, D)), _full_spec((1, D)),
              geo.mod_spec(3), geo.mod_spec(4), _full_spec((E, D))]
    out_specs, out_shape = _post_outs(geo, E)
    return pl.pallas_call(
        functools.partial(_post_kernel, alpha=alpha, has_bias=has_bias),
        grid=(geo.B, geo.nt), in_specs=specs, out_specs=out_specs, out_shape=out_shape,
        compiler_params=_cp(2), name="post_mixer",
    )(*ins)


def _dwconv_seg(r, cw, cb, S, C, lo):
    P = S + C
    row = lax.broadcasted_iota(jnp.int32, r.shape, 0)
    tl = jnp.where(row < S, row, row - S)
    sl = jnp.where(row < S, S, C)
    acc = jnp.zeros_like(r) + cb
    for k in range(cw.shape[0]):
        off = k - lo
        if off == 0:
            term = r
        else:
            shifted = pltpu.roll(r, (-off) % P, 0)
            valid = jnp.logical_and(tl + off >= 0, tl + off < sl)
            term = jnp.where(valid, shifted, 0.0)
        acc = acc + cw[k:k + 1, :] * term
    return acc


def _rg_in_kernel(x_ref, sh_ref, sc_ref, wg_ref, wr_ref, g_ref, r_ref):
    h = (x_ref[0] * (1.0 + sc_ref[...]) + sh_ref[...]).astype(BF16)
    g = jnp.dot(h, wg_ref[...], preferred_element_type=F32)
    g_ref[0] = jax.nn.gelu(g).astype(g_ref.dtype)
    r_ref[0] = jnp.dot(h, wr_ref[...], preferred_element_type=F32)


def _rg_scan_kernel(r_ref, g_ref, cw_ref, cb_ref, gw_ref, gb_ref, lam_ref, o_ref,
                    a_scr, b_scr, h_scr, *, S, C):
    P = S + C
    n = r_ref.shape[-1]
    r = r_ref[0]
    rc = _dwconv_seg(r, cw_ref[...], cb_ref[...], S, C, cw_ref.shape[0] // 2)
    rcb = rc.astype(BF16)
    row = lax.broadcasted_iota(jnp.int32, (P, n), 0)
    sub = jnp.bitwise_and(row, SUBLANES - 1)
    for d in range(2):
        rev = d == 1
        gr = jax.nn.sigmoid(jnp.dot(rcb, gw_ref[d, 0], preferred_element_type=F32) + gb_ref[d, 0])
        gi = jax.nn.sigmoid(jnp.dot(rcb, gw_ref[d, 1], preferred_element_type=F32) + gb_ref[d, 1])
        nl = -lam_ref[d]
        sp = jnp.maximum(nl, 0.0) + jnp.log1p(jnp.exp(-jnp.abs(nl)))
        a = jnp.exp(-LRU_C * gr * sp)
        bb = jnp.sqrt(1.0 - a * a) * gi * rc
        for s in (1, 2, 4):
            shift = (P - s) if rev else s
            a_sh = pltpu.roll(a, shift, 0)
            b_sh = pltpu.roll(bb, shift, 0)
            m = (sub < SUBLANES - s) if rev else (sub >= s)
            bb = jnp.where(m, a * b_sh + bb, bb)
            a = jnp.where(m, a * a_sh, a)
        a_scr[...] = a
        b_scr[...] = bb

        def chain(lo_tile, n_tiles, c0):
            def body(i, c):
                t = (lo_tile + n_tiles - 1 - i) if rev else (lo_tile + i)
                off = pl.multiple_of(t * SUBLANES, SUBLANES)
                h = b_scr[pl.ds(off, SUBLANES), :] + a_scr[pl.ds(off, SUBLANES), :] * c
                if d == 0:
                    h_scr[pl.ds(off, SUBLANES), :] = h
                else:
                    h_scr[pl.ds(off, SUBLANES), :] = h_scr[pl.ds(off, SUBLANES), :] + h
                edge = h[0:1] if rev else h[SUBLANES - 1:SUBLANES]
                return jnp.broadcast_to(edge, (SUBLANES, n))
            return lax.fori_loop(0, n_tiles, body, c0)

        c_ctx = chain(S // SUBLANES, C // SUBLANES, jnp.zeros((SUBLANES, n), F32))
        chain(0, S // SUBLANES, c_ctx)
    o_ref[0] = (g_ref[0].astype(F32) * h_scr[...]).astype(o_ref.dtype)


def _rglru_mixer(geo, xs, modl, w_in, conv_w, conv_b, gate_w, gate_b, lam):
    B, P, D, S, C = geo.B, geo.P, geo.D, geo.S, geo.C
    R = w_in.shape[1] // 2
    nb = R // RG_BLOCK
    w_in = w_in.astype(BF16)
    g, r = pl.pallas_call(
        _rg_in_kernel, grid=(B, geo.nt),
        in_specs=[geo.row_spec(D), geo.mod_spec(0), geo.mod_spec(1),
                  pl.BlockSpec((D, R), lambda b, i: (0, 0)), pl.BlockSpec((D, R), lambda b, i: (0, 1))],
        out_specs=[geo.row_spec(R), geo.row_spec(R)],
        out_shape=[jax.ShapeDtypeStruct((B, P, R), BF16), jax.ShapeDtypeStruct((B, P, R), F32)],
        compiler_params=_cp(2), name="rg_in",
    )(xs, modl, modl, w_in, w_in)
    K = conv_w.shape[0]
    seq_spec = pl.BlockSpec((1, P, RG_BLOCK), lambda b, n: (b, 0, n))
    y = pl.pallas_call(
        functools.partial(_rg_scan_kernel, S=S, C=C), grid=(B, nb),
        in_specs=[seq_spec, seq_spec,
                  pl.BlockSpec((K, RG_BLOCK), lambda b, n: (0, n)),
                  pl.BlockSpec((1, RG_BLOCK), lambda b, n: (0, n)),
                  pl.BlockSpec((2, 2, None, RG_BLOCK, RG_BLOCK), lambda b, n: (0, 0, n, 0, 0)),
                  pl.BlockSpec((2, 2, 1, RG_BLOCK), lambda b, n: (0, 0, 0, n)),
                  pl.BlockSpec((2, 1, RG_BLOCK), lambda b, n: (0, 0, n))],
        out_specs=seq_spec,
        out_shape=jax.ShapeDtypeStruct((B, P, R), BF16),
        scratch_shapes=[pltpu.VMEM((P, RG_BLOCK), F32)] * 3,
        compiler_params=_cp(2), name="rg_scan",
    )(r, g, conv_w, conv_b.reshape(1, R), gate_w.astype(BF16), gate_b.reshape(2, 2, 1, R),
      lam.reshape(2, 1, R))
    return y


def _mm_bias_kernel(x_ref, sh_ref, sc_ref, w_ref, b_ref, o_ref):
    h = (x_ref[0] * (1.0 + sc_ref[...]) + sh_ref[...]).astype(BF16)
    o_ref[0] = (jnp.dot(h, w_ref[...], preferred_element_type=F32) + b_ref[...]).astype(o_ref.dtype)


def _short_conv_kernel(u_ref, cw_ref, cb_ref, o_ref, ob_ref, *, S, C):
    y = _dwconv_seg(u_ref[0], cw_ref[...], cb_ref[...], S, C, (cw_ref.shape[0] - 1) // 2)
    o_ref[0] = y
    ob_ref[0] = y.astype(BF16)


def _dft_table_kernel(c_ref, s_ref, st_ref, *, L, TF):
    i = pl.program_id(0)
    N = 2 * L
    f = lax.broadcasted_iota(jnp.int32, (TF, L), 0) + i * TF
    t = lax.broadcasted_iota(jnp.int32, (TF, L), 1)
    ang = jnp.bitwise_and(f * t, N - 1).astype(F32) * (2.0 * math.pi / N)
    c_ref[...] = jnp.cos(ang).astype(BF16)
    nyq_t = (1 - 2 * jnp.bitwise_and(t, 1)).astype(F32)
    s_ref[...] = jnp.where(f == 0, nyq_t, jnp.sin(ang)).astype(BF16)
    nyq_f = (1 - 2 * jnp.bitwise_and(f, 1)).astype(F32)
    st_ref[...] = jnp.where(t == 0, nyq_f, jnp.sin(ang)).astype(BF16)


def _dft_tables(L):
    TF = min(L, 256)
    shp = jax.ShapeDtypeStruct((L, L), BF16)
    spec = pl.BlockSpec((TF, L), lambda i: (i, 0))
    return pl.pallas_call(
        functools.partial(_dft_table_kernel, L=L, TF=TF), grid=(L // TF,),
        in_specs=[], out_specs=[spec, spec, spec], out_shape=[shp, shp, shp],
        compiler_params=_cp(1), name="dft_tables",
    )()


def _hy_filter_kernel(z_ref, w1_ref, b1_ref, w2_ref, b2_ref, w3_ref, b3_ref, fq_ref,
                      w4f_ref, w4b_ref, df_ref, db_ref, tn_ref, kp_ref, km_ref):
    fq = fq_ref[...]

    def lin(h, w_ref, b_ref):
        return jnp.dot(h, w_ref[...], precision=HIGHEST, preferred_element_type=F32) + b_ref[...]

    h = jnp.sin(fq * lin(z_ref[...], w1_ref, b1_ref))
    h = jnp.sin(fq * lin(h, w2_ref, b2_ref))
    h = jnp.sin(fq * lin(h, w3_ref, b3_ref))
    tn = tn_ref[...]
    hf = jnp.dot(h, w4f_ref[...], precision=HIGHEST, preferred_element_type=F32)
    hf = hf * jnp.exp(-tn * jnp.abs(df_ref[...]))
    hb = jnp.dot(h, w4b_ref[...], precision=HIGHEST, preferred_element_type=F32)
    hb = hb * jnp.exp(-tn * jnp.abs(db_ref[...]))
    row = lax.broadcasted_iota(jnp.int32, hb.shape, 0)
    hb = jnp.where(row == 0, 0.0, hb)
    nrm = lax.rsqrt(jnp.sum(hf * hf, axis=0, keepdims=True) + jnp.sum(hb * hb, axis=0, keepdims=True) + 1e-6)
    hf = hf * nrm
    hb = hb * nrm
    kp_ref[...] = (hf + hb).astype(BF16)
    km_ref[...] = (hf - hb).astype(BF16)


def _hy_spectrum_kernel(c_ref, s_ref, s0_ref, kp_ref, km_ref, ka_ref, kb_ref, kc_ref, *, L, TF):
    i = pl.program_id(0)
    inv_n = 1.0 / (2 * L)
    kr = jnp.dot(c_ref[...], kp_ref[...], preferred_element_type=F32)
    ks = jnp.dot(s_ref[...], km_ref[...], preferred_element_type=F32)
    nyq = jnp.dot(s0_ref[...], kp_ref[...], preferred_element_type=F32)[0:1]
    f = lax.broadcasted_iota(jnp.int32, kr.shape, 0) + i * TF
    dc = f == 0
    ka_ref[...] = jnp.where(dc, kr * inv_n, kr * (2.0 * inv_n))
    kb_ref[...] = jnp.where(dc, 0.0, ks * (-2.0 * inv_n))
    kc_ref[...] = jnp.where(dc, nyq * inv_n, kr * (2.0 * inv_n))


def _hy_filters(L, tabs, fw1, fb1, fw2, fb2, fw3, fb3, fw4, ffreq, fdecay, D):
    cm, sm, _ = tabs
    E = fw1.shape[0]
    Hd = fw1.shape[1]
    bands = (E - 1) // 2
    t = jnp.arange(L, dtype=F32)
    t_norm = t / max(L - 1, 1)
    fr = jnp.linspace(1e-4, bands - 1, bands, dtype=F32)
    ang = (2.0 * math.pi / L) * t[:, None] * fr[None, :]
    z = jnp.concatenate([t_norm[:, None], jnp.cos(ang), -jnp.sin(ang)], -1)
    Ep, Hp = -(-E // LANES) * LANES, -(-Hd // LANES) * LANES
    z = jnp.pad(z, ((0, 0), (0, Ep - E)))
    fw1 = jnp.pad(fw1, ((0, Ep - E), (0, Hp - Hd)))
    fw2 = jnp.pad(fw2, ((0, Hp - Hd), (0, Hp - Hd)))
    fw3 = jnp.pad(fw3, ((0, Hp - Hd), (0, Hp - Hd)))
    fw4 = jnp.pad(fw4, ((0, Hp - Hd), (0, 0)))
    fb1, fb2, fb3, ffreq = (jnp.pad(v, (0, Hp - Hd)) for v in (fb1, fb2, fb3, ffreq))
    E, Hd = Ep, Hp
    CT = min(2 * D, 512)
    nct = 2 * D // CT
    dec = fdecay.reshape(1, 4 * D)
    kp, km = pl.pallas_call(
        _hy_filter_kernel, grid=(nct,),
        in_specs=[_full_spec((L, E)), _full_spec((E, Hd)), _full_spec((1, Hd)), _full_spec((Hd, Hd)),
                  _full_spec((1, Hd)), _full_spec((Hd, Hd)), _full_spec((1, Hd)), _full_spec((1, Hd)),
                  pl.BlockSpec((Hd, CT), lambda j: (0, j)), pl.BlockSpec((Hd, CT), lambda j: (0, j + nct)),
                  pl.BlockSpec((1, CT), lambda j: (0, j)), pl.BlockSpec((1, CT), lambda j: (0, j + nct)),
                  _full_spec((L, 1))],
        out_specs=[pl.BlockSpec((L, CT), lambda j: (0, j))] * 2,
        out_shape=[jax.ShapeDtypeStruct((L, 2 * D), BF16)] * 2,
        compiler_params=_cp(1), name="hy_filter",
    )(z, fw1, fb1.reshape(1, Hd), fw2, fb2.reshape(1, Hd), fw3, fb3.reshape(1, Hd), ffreq.reshape(1, Hd),
      fw4, fw4, dec, dec, t_norm[:, None])
    TF = min(L, 256)
    spec_w = pl.BlockSpec((TF, L), lambda i, j: (i, 0))
    spec_k = pl.BlockSpec((L, CT), lambda i, j: (0, j))
    spec_o = pl.BlockSpec((TF, CT), lambda i, j: (i, j))
    shp = jax.ShapeDtypeStruct((L, 2 * D), F32)
    return pl.pallas_call(
        functools.partial(_hy_spectrum_kernel, L=L, TF=TF), grid=(L // TF, nct),
        in_specs=[spec_w, spec_w, pl.BlockSpec((SUBLANES, L), lambda i, j: (0, 0)), spec_k, spec_k],
        out_specs=[spec_o] * 3, out_shape=[shp] * 3,
        compiler_params=_cp(2), name="hy_spectrum",
    )(cm, sm, sm, kp, km)


def _hy_fwd_kernel(z_ref, c_ref, s_ref, ka_ref, kb_ref, kc_ref, p_ref):
    z = z_ref[0]
    zr = jnp.dot(c_ref[...], z, preferred_element_type=F32)
    zs = jnp.dot(s_ref[...], z, preferred_element_type=F32)
    kb = kb_ref[...]
    p_ref[0, 0] = (zr * ka_ref[...] + zs * kb).astype(BF16)
    p_ref[0, 1] = (zs * kc_ref[...] - zr * kb).astype(BF16)


def _hy_inv_kernel(p_ref, c_ref, st_ref, z_ref, x_ref, fb_ref, o_ref):
    y = jnp.dot(c_ref[...], p_ref[0, 0], preferred_element_type=F32)
    y = y + jnp.dot(st_ref[...], p_ref[0, 1], preferred_element_type=F32)
    o_ref[0] = (x_ref[0] * (y + z_ref[0].astype(F32) * fb_ref[...])).astype(o_ref.dtype)


def _hy_inv_kernel_alias(p_ref, c_ref, st_ref, z_ref, x_ref, fb_ref, prev_ref, o_ref):
    del prev_ref
    _hy_inv_kernel(p_ref, c_ref, st_ref, z_ref, x_ref, fb_ref, o_ref)


def _hy_conv(geo, L, off, tabs, z, z_col, kfilt, k_col, xmul, x_col, fbias, out):
    B, P, D = geo.B, geo.P, geo.D
    cm, sm, smt = tabs
    ka, kb, kc = kfilt
    rb = off // L
    TF = min(L, 256)
    spec_w = pl.BlockSpec((TF, L), lambda b, i: (i, 0))
    spec_k = pl.BlockSpec((TF, D), lambda b, i: (i, k_col))
    p = pl.pallas_call(
        _hy_fwd_kernel, grid=(B, L // TF),
        in_specs=[pl.BlockSpec((1, L, D), lambda b, i: (b, rb, z_col)), spec_w, spec_w,
                  spec_k, spec_k, spec_k],
        out_specs=pl.BlockSpec((1, 2, TF, D), lambda b, i: (b, 0, i, 0)),
        out_shape=jax.ShapeDtypeStruct((B, 2, L, D), BF16),
        compiler_params=_cp(2), name="hy_fwd",
    )(z, cm, sm, ka, kb, kc)
    CT = min(D, 512)
    nct = D // CT
    rt = off // TF
    spec_wi = pl.BlockSpec((TF, L), lambda b, j, i: (i, 0))
    return pl.pallas_call(
        _hy_inv_kernel_alias, grid=(B, nct, L // TF),
        in_specs=[pl.BlockSpec((1, 2, L, CT), lambda b, j, i: (b, 0, 0, j)), spec_wi, spec_wi,
                  pl.BlockSpec((1, TF, CT), lambda b, j, i: (b, rt + i, z_col * nct + j)),
                  pl.BlockSpec((1, TF, CT), lambda b, j, i: (b, rt + i, x_col * nct + j)),
                  pl.BlockSpec((1, CT), lambda b, j, i: (0, k_col * nct + j)),
                  pl.BlockSpec(memory_space=pl.ANY)],
        out_specs=pl.BlockSpec((1, TF, CT), lambda b, j, i: (b, rt + i, j)),
        out_shape=jax.ShapeDtypeStruct((B, P, D), BF16),
        input_output_aliases={6: 0},
        compiler_params=_cp(3), name="hy_inv",
    )(p, cm, smt, z, xmul, fbias, out)


def _hyena_mixer(geo, xs, modl, w_in, b_in, short_w, short_b, fw1, fb1, fw2, fb2, fw3, fb3, fw4,
                 ffreq, fdecay, fbias):
    B, P, D, S, C = geo.B, geo.P, geo.D, geo.S, geo.C
    u0 = pl.pallas_call(
        _mm_bias_kernel, grid=(B, geo.nt, 3),
        in_specs=[geo.row_spec(D), geo.mod_spec(0), geo.mod_spec(1),
                  pl.BlockSpec((D, D), lambda b, i, j: (0, j)), pl.BlockSpec((1, D), lambda b, i, j: (0, j))],
        out_specs=pl.BlockSpec((1, geo.TM, D), lambda b, i, j: (b, i, j)),
        out_shape=jax.ShapeDtypeStruct((B, P, 3 * D), F32),
        compiler_params=_cp(3), name="hy_in",
    )(xs, modl, modl, w_in.astype(BF16), b_in.reshape(1, 3 * D))
    CT = min(D, 256)
    Ks = short_w.shape[0]
    spec = pl.BlockSpec((1, P, CT), lambda b, j: (b, 0, j))
    u, ub = pl.pallas_call(
        functools.partial(_short_conv_kernel, S=S, C=C), grid=(B, 3 * D // CT),
        in_specs=[spec, pl.BlockSpec((Ks, CT), lambda b, j: (0, j)), pl.BlockSpec((1, CT), lambda b, j: (0, j))],
        out_specs=[spec, spec],
        out_shape=[jax.ShapeDtypeStruct((B, P, 3 * D), F32), jax.ShapeDtypeStruct((B, P, 3 * D), BF16)],
        compiler_params=_cp(2), name="hy_short",
    )(u0, short_w, short_b.reshape(1, 3 * D))
    fb = fbias.reshape(1, 2 * D)
    z1 = jnp.zeros((B, P, D), BF16)
    z2 = jnp.zeros((B, P, D), BF16)
    segs = [(S, 0), (C, S)]
    convs = []
    for L, off in segs:
        tabs = _dft_tables(L)
        kf = _hy_filters(L, tabs, fw1, fb1, fw2, fb2, fw3, fb3, fw4, ffreq, fdecay, D)
        convs.append((L, off, tabs, kf))
    for L, off, tabs, kf in convs:
        z1 = _hy_conv(geo, L, off, tabs, ub, 0, kf, 0, u, 1, fb, z1)
    for L, off, tabs, kf in convs:
        z2 = _hy_conv(geo, L, off, tabs, z1, 0, kf, 1, u, 2, fb, z2)
    return z2


def _rope_tables(S, D):
    t = jnp.arange(S)
    row = (t // GRID_W).astype(F32)
    col = (t % GRID_W).astype(F32)
    axis_dim = DA_HEAD_DIM // 2
    half = axis_dim // 2
    inv = ROPE_THETA ** (-jnp.arange(0, axis_dim, 2, dtype=F32) / axis_dim)
    lane = jnp.arange(D)
    within = lane % DA_HEAD_DIM
    pos = jnp.where((within // axis_dim)[None, :] == 0, row[:, None], col[:, None])
    ang = pos * inv[lane % half][None, :]
    sign = jnp.where((lane % axis_dim) < half, -1.0, 1.0)[None, :]
    return jnp.cos(ang), jnp.sin(ang) * sign


def _da_in_kernel(x_ref, sh_ref, sc_ref, w_ref, cos_ref, sin_ref, o_ref, *, n_lat):
    i = pl.program_id(1)
    j = pl.program_id(2)
    h = (x_ref[0] * (1.0 + sc_ref[...]) + sh_ref[...]).astype(BF16)
    acc = jnp.dot(h, w_ref[...], preferred_element_type=F32)
    acc = acc * jnp.where(j == 0, DA_HEAD_DIM ** -0.5, 1.0)
    rot = jnp.logical_and(i < n_lat, j < 2)

    @pl.when(rot)
    def _():
        Dn = acc.shape[-1]
        half = DA_HEAD_DIM // 4
        lane = lax.broadcasted_iota(jnp.int32, acc.shape, 1)
        up = pltpu.roll(acc, Dn - half, 1)
        dn = pltpu.roll(acc, half, 1)
        partner = jnp.where(jnp.bitwise_and(lane, 2 * half - 1) < half, up, dn)
        o_ref[0] = (acc * cos_ref[...] + partner * sin_ref[...]).astype(o_ref.dtype)

    @pl.when(jnp.logical_not(rot))
    def _():
        o_ref[0] = acc.astype(o_ref.dtype)


def _da_attn_kernel(*refs, kv_lo, nk, lam_init, aliased):
    if aliased:
        q_ref, k_ref, v_ref, lam_ref, sub_ref, _, o_ref, vx_scr, s_scr = refs
    else:
        q_ref, k_ref, v_ref, lam_ref, sub_ref, o_ref, vx_scr, s_scr = refs
    i = pl.program_id(2)
    HW = v_ref.shape[-1]
    TQ = q_ref.shape[1]
    lp = lam_ref[...]
    lam = (jnp.exp(jnp.sum(lp[0:1] * lp[1:2], axis=1, keepdims=True))
           - jnp.exp(jnp.sum(lp[2:3] * lp[3:4], axis=1, keepdims=True)) + lam_init)

    @pl.when(i == 0)
    def _():
        vx_scr[:, :HW] = v_ref[0]
        vx_scr[:, HW:] = jnp.ones((vx_scr.shape[0], HW), BF16)

    lane = lax.broadcasted_iota(jnp.int32, (TQ, HW), 1)
    q = q_ref[0]
    k = k_ref[0, kv_lo:kv_lo + nk, :]
    outs = []
    for c in range(2):
        qc = jnp.where((lane // DA_HEAD_DIM) == c, q, jnp.zeros_like(q))
        s_scr[...] = lax.dot_general(qc, k, (((1,), (1,)), ((), ())), preferred_element_type=F32)
        m = jnp.max(s_scr[...], axis=-1, keepdims=True)
        p = jnp.exp((s_scr[...] - m).astype(BF16))
        ov = jnp.dot(p, vx_scr[kv_lo:kv_lo + nk, :], preferred_element_type=F32)
        outs.append(ov[:, :HW] / ov[:, HW:HW + 1])
    o = outs[0] - lam * outs[1]
    o = o * lax.rsqrt(jnp.mean(o * o, axis=-1, keepdims=True) + 1e-5) * sub_ref[...] * (1.0 - lam_init)
    o_ref[0] = o.astype(o_ref.dtype)


def _diff_attention_mixer(geo, xs, modl, w_in, lam_p, subln_w, layer_idx):
    B, P, D, S, C = geo.B, geo.P, geo.D, geo.S, geo.C
    H = D // (2 * DA_HEAD_DIM)
    HW = 2 * DA_HEAD_DIM
    cos_t, sin_t = _rope_tables(S, D)
    n_lat = geo.n_lat
    tab_spec = pl.BlockSpec((geo.TM, D), lambda b, i, j: (jnp.minimum(i, n_lat - 1), 0))
    qkv = pl.pallas_call(
        functools.partial(_da_in_kernel, n_lat=n_lat), grid=(B, geo.nt, 3),
        in_specs=[geo.row_spec(D), geo.mod_spec(0), geo.mod_spec(1),
                  pl.BlockSpec((D, D), lambda b, i, j: (0, j)), tab_spec, tab_spec],
        out_specs=pl.BlockSpec((1, geo.TM, D), lambda b, i, j: (b, i, j)),
        out_shape=jax.ShapeDtypeStruct((B, P, 3 * D), BF16),
        compiler_params=_cp(3), name="da_in",
    )(xs, modl, modl, w_in.astype(BF16), cos_t, sin_t)
    lam_init = 0.8 - 0.6 * math.exp(-0.3 * layer_idx)
    def attend(TQ, row0, n_tiles, kv_lo, nk, prev):
        rb = row0 // TQ
        ins = [qkv, qkv, qkv, lam_p, subln_w.reshape(1, HW)]
        specs = [pl.BlockSpec((1, TQ, HW), lambda b, h, i: (b, rb + i, h)),
                 pl.BlockSpec((1, P, HW), lambda b, h, i: (b, 0, H + h)),
                 pl.BlockSpec((1, P, HW), lambda b, h, i: (b, 0, 2 * H + h)),
                 _full_spec((4, DA_HEAD_DIM)), _full_spec((1, HW))]
        if prev is not None:
            ins.append(prev)
            specs.append(pl.BlockSpec(memory_space=pl.ANY))
        return pl.pallas_call(
            functools.partial(_da_attn_kernel, kv_lo=kv_lo, nk=nk, lam_init=lam_init, aliased=prev is not None),
            grid=(B, H, n_tiles), in_specs=specs,
            out_specs=pl.BlockSpec((1, TQ, HW), lambda b, h, i: (b, rb + i, h)),
            out_shape=jax.ShapeDtypeStruct((B, P, D), BF16),
            scratch_shapes=[pltpu.VMEM((P, 2 * HW), BF16), pltpu.VMEM((TQ, nk), F32)],
            input_output_aliases={} if prev is None else {5: 0},
            compiler_params=_cp(3), name="da_attn",
        )(*ins)

    TQ = geo.TM
    while TQ < DA_Q_TILE and S % (2 * TQ) == 0:
        TQ *= 2
    y = attend(TQ, 0, S // TQ, 0, P, None)
    return attend(geo.TM, S, C // geo.TM, S, C, y)


def _s5_operators(a_re, a_im, log_step, b_re, b_im, c_re, c_im):
    T = S5_CHUNK
    G, Pst = a_re.shape[1], a_re.shape[2]
    Hg = S5_GROUP
    GL = LANES // Hg
    LB = G // GL
    lam = lax.complex(jnp.minimum(a_re.astype(F32), -1e-4), a_im.astype(F32))
    step = jnp.exp(log_step.astype(F32))[..., None]
    abar = jnp.exp(lam * step)
    bbar = ((abar - 1.0) / lam)[..., None] * lax.complex(b_re.astype(F32), b_im.astype(F32))
    cmat = lax.complex(c_re.astype(F32), c_im.astype(F32))
    pows = jnp.stack([abar ** l for l in range(T + 1)], axis=1)
    eye = jnp.eye(GL, dtype=F32)
    ar = jnp.arange(T)
    big_m, big_g, big_h, a_t = [], [], [], []
    for d in range(2):
        pw = pows[d]
        kl = jnp.einsum('gjp,lgp,gph->lgjh', cmat[d], pw[:T], bbar[d]).real
        lag = (ar[None, :] - ar[:, None]) if d == 0 else (ar[:, None] - ar[None, :])
        tz = jnp.where((lag >= 0)[:, :, None, None, None], kl[jnp.clip(lag, 0, T - 1)], 0.0)
        tz = tz.reshape(T, T, LB, GL, Hg, Hg)
        m = jnp.einsum('stbgjh,gk->bsghtkj', tz, eye).reshape(LB, T * LANES, T * LANES)
        e_in = (T - 1 - ar) if d == 0 else ar
        gc = pw[e_in][:, :, :, None] * bbar[d][None]
        gc = gc.reshape(T, LB, GL, Pst, Hg)
        g_re = jnp.einsum('sbgph,gk->bsghkp', gc.real, eye).reshape(LB, T * LANES, GL * Pst)
        g_im = jnp.einsum('sbgph,gk->bsghkp', gc.imag, eye).reshape(LB, T * LANES, GL * Pst)
        e_out = (ar + 1) if d == 0 else (T - ar)
        hc = cmat[d][None] * pw[e_out][:, :, None, :]
        hc = hc.reshape(T, LB, GL, Hg, Pst)
        h_re = jnp.einsum('tbgjp,gk->bgptkj', hc.real, eye).reshape(LB, GL * Pst, T * LANES)
        h_im = jnp.einsum('tbgjp,gk->bgptkj', -hc.imag, eye).reshape(LB, GL * Pst, T * LANES)
        big_m.append(m)
        big_g.append(jnp.concatenate([g_re, g_im], axis=2))
        big_h.append(jnp.concatenate([h_re, h_im], axis=1))
        at = pw[T].reshape(LB, 1, GL * Pst)
        a_t.append(jnp.concatenate([at.real, at.imag], axis=2))
    return (jnp.stack(big_m).astype(BF16), jnp.stack(big_g).astype(BF16),
            jnp.stack(big_h).astype(BF16), jnp.stack(a_t).astype(F32))


def _modulate_kernel(x_ref, sh_ref, sc_ref, o_ref):
    o_ref[0] = x_ref[0] * (1.0 + sc_ref[...]) + sh_ref[...]


def _s5_kernel(u_ref, m_ref, g_ref, h_ref, a_ref, o_ref, gx_scr, sp_scr, *, S, C):
    d = pl.program_id(1)
    T = S5_CHUNK
    P = S + C
    n = P // T
    n_lat = S // T
    x = jnp.concatenate([u_ref[0, pl.ds(s, n, stride=T), :] for s in range(T)], axis=1).astype(BF16)
    gx_scr[...] = jnp.dot(x, g_ref[...], preferred_element_type=F32)
    ns = a_ref.shape[-1] // 2
    a_r = a_ref[:, :ns]
    a_i = a_ref[:, ns:]

    def scan(lo, cnt, rev, carry):
        def body(k, st):
            s_r, s_i = st
            c = (lo + cnt - 1 - k) if rev else (lo + k)
            sp_scr[pl.ds(c, 1), :] = jnp.concatenate([s_r, s_i], axis=1)
            gx = gx_scr[pl.ds(c, 1), :]
            return (a_r * s_r - a_i * s_i + gx[:, :ns], a_r * s_i + a_i * s_r + gx[:, ns:])
        return lax.fori_loop(0, cnt, body, carry)

    zero = (jnp.zeros((1, ns), F32), jnp.zeros((1, ns), F32))

    @pl.when(d == 0)
    def _():
        scan(0, n_lat, False, scan(n_lat, n - n_lat, False, zero))

    @pl.when(d == 1)
    def _():
        scan(0, n_lat, True, scan(n_lat, n - n_lat, True, zero))

    y = jnp.dot(x, m_ref[...], preferred_element_type=F32)
    y = y + jnp.dot(sp_scr[...].astype(BF16), h_ref[...], preferred_element_type=F32)
    for s in range(T):
        o_ref[0, pl.ds(s, n, stride=T), :] = y[:, s * LANES:(s + 1) * LANES]


def _s5_mixer(geo, xs, modl, a_re, a_im, log_step, b_re, b_im, c_re, c_im):
    B, P, D, S, C = geo.B, geo.P, geo.D, geo.S, geo.C
    u = pl.pallas_call(
        _modulate_kernel, grid=(B, geo.nt),
        in_specs=[geo.row_spec(D), geo.mod_spec(0), geo.mod_spec(1)],
        out_specs=geo.row_spec(D), out_shape=jax.ShapeDtypeStruct((B, P, D), F32),
        compiler_params=_cp(2), name="s5_modulate",
    )(xs, modl, modl)
    big_m, big_g, big_h, a_t = _s5_operators(a_re, a_im, log_step, b_re, b_im, c_re, c_im)
    LB = D // LANES
    TL = S5_CHUNK * LANES
    NS = big_g.shape[-1]
    n = P // S5_CHUNK
    y = pl.pallas_call(
        functools.partial(_s5_kernel, S=S, C=C), grid=(LB, 2, B),
        in_specs=[pl.BlockSpec((1, P, LANES), lambda l, d, b: (b, 0, l)),
                  pl.BlockSpec((None, None, TL, TL), lambda l, d, b: (d, l, 0, 0)),
                  pl.BlockSpec((None, None, TL, NS), lambda l, d, b: (d, l, 0, 0)),
                  pl.BlockSpec((None, None, NS, TL), lambda l, d, b: (d, l, 0, 0)),
                  pl.BlockSpec((None, None, 1, NS), lambda l, d, b: (d, l, 0, 0))],
        out_specs=pl.BlockSpec((None, 1, P, LANES), lambda l, d, b: (d, b, 0, l)),
        out_shape=jax.ShapeDtypeStruct((2, B, P, D), F32),
        scratch_shapes=[pltpu.VMEM((n, NS), F32), pltpu.VMEM((n, NS), F32)],
        compiler_params=_cp(3), name="s5_scan",
    )(u, big_m, big_g, big_h, a_t)
    return u, y


def _s5_post_kernel(yf_ref, yb_ref, u_ref, d_ref, w_ref, b_ref, x_ref, gate_ref, lng_ref, lnb_ref,
                    sh_ref, sc_ref, wr_ref, xm_ref, f_ref, s_ref, *, alpha):
    g = jax.nn.gelu(yf_ref[0] + yb_ref[0] + d_ref[...] * u_ref[0])
    vg = jnp.dot(g.astype(BF16), w_ref[...], preferred_element_type=F32) + b_ref[...]
    Dn = vg.shape[-1] // 2
    y = vg[:, :Dn] * jax.nn.sigmoid(vg[:, Dn:])
    _finish_mixer(y, x_ref[0], gate_ref[...], lng_ref[...], lnb_ref[...], sh_ref[...], sc_ref[...],
                  wr_ref, xm_ref, f_ref, s_ref, alpha)


def _s5_post(geo, y2, u, d_skip, w_glu, b_glu, xs, modl, lng, lnb, w_router, alpha):
    D = geo.D
    E = w_router.shape[-1]
    out_specs, out_shape = _post_outs(geo, E)
    TM = geo.TM
    return pl.pallas_call(
        functools.partial(_s5_post_kernel, alpha=alpha), grid=(geo.B, geo.nt),
        in_specs=[pl.BlockSpec((None, 1, TM, D), lambda b, i: (0, b, i, 0)),
                  pl.BlockSpec((None, 1, TM, D), lambda b, i: (1, b, i, 0)),
                  geo.row_spec(D), _full_spec((1, D)), _full_spec((D, 2 * D)), _full_spec((1, 2 * D)),
                  geo.row_spec(D), geo.mod_spec(2), _full_spec((1, D)), _full_spec((1, D)),
                  geo.mod_spec(3), geo.mod_spec(4), _full_spec((E, D))],
        out_specs=out_specs, out_shape=out_shape,
        compiler_params=_cp(2), name="s5_post",
    )(y2, y2, u, d_skip.reshape(1, D), w_glu.astype(BF16), b_glu.reshape(1, 2 * D), xs, modl,
      lng.reshape(1, D), lnb.reshape(1, D), modl, modl, w_router.T)


def _route_kernel(s_ref, b_ref, idx_ref, w_ref):
    sc = s_ref[0]
    E, TM = sc.shape
    biased = sc + b_ref[...]
    G = N_EXPERT_GROUPS
    per = E // G
    neg = -jnp.inf
    blocks, gs = [], []
    for g in range(G):
        blk = biased[g * per:(g + 1) * per]
        m1 = jnp.max(blk, axis=0, keepdims=True)
        is1 = blk == m1
        cnt = jnp.sum(is1.astype(F32), axis=0, keepdims=True)
        m2 = jnp.max(jnp.where(is1, neg, blk), axis=0, keepdims=True)
        blocks.append(blk)
        gs.append(m1 + jnp.where(cnt >= 2.0, m1, m2))
    masked = []
    for g in range(G):
        ahead = jnp.zeros((1, TM), F32)
        for h in range(G):
            if h < g:
                ahead = ahead + (gs[h] >= gs[g]).astype(F32)
            elif h > g:
                ahead = ahead + (gs[h] > gs[g]).astype(F32)
        masked.append(jnp.where(ahead < float(TOPK_GROUPS), blocks[g], neg))
    masked = jnp.concatenate(masked, axis=0)
    iota_e = lax.broadcasted_iota(jnp.int32, (E, TM), 0)
    idxs, ws = [], []
    for _ in range(TOP_K):
        m = jnp.max(masked, axis=0, keepdims=True)
        ik = jnp.min(jnp.where(masked == m, iota_e, E), axis=0, keepdims=True)
        sel = iota_e == ik
        ws.append(jnp.sum(jnp.where(sel, sc, 0.0), axis=0, keepdims=True))
        idxs.append(ik)
        masked = jnp.where(sel, neg, masked)
    tot = ws[0]
    for wk in ws[1:]:
        tot = tot + wk
    w = jnp.concatenate(ws, axis=0)
    idx_ref[0] = jnp.concatenate(idxs, axis=0)
    w_ref[0] = w / (tot + 1e-20) * ROUTED_SCALE


def _rank_kernel(idx_ref, rank_ref, cnt_ref, run_scr, *, E):
    first = jnp.logical_and(pl.program_id(0) == 0, pl.program_id(1) == 0)

    @pl.when(first)
    def _():
        run_scr[...] = jnp.zeros_like(run_scr)

    idx = idx_ref[0]
    K, TM = idx.shape
    iota_e = lax.broadcasted_iota(jnp.int32, (E, TM), 0)
    member = jnp.zeros((E, TM), F32)
    for k in range(K):
        member = member + (iota_e == idx[k:k + 1]).astype(F32)
    before = (lax.broadcasted_iota(jnp.int32, (TM, TM), 0)
              < lax.broadcasted_iota(jnp.int32, (TM, TM), 1)).astype(BF16)
    rank = jnp.dot(member.astype(BF16), before, preferred_element_type=F32) + run_scr[...]
    rows = [jnp.sum(jnp.where(iota_e == idx[k:k + 1], rank, 0.0), axis=0, keepdims=True) for k in range(K)]
    rank_ref[0] = jnp.concatenate(rows, axis=0).astype(jnp.int32)
    run_scr[...] = run_scr[...] + jnp.sum(member, axis=1, keepdims=True)
    cnt_ref[...] = run_scr[...]


def _dest_kernel(idx_ref, rank_ref, start_ref, dest_ref, *, E):
    idx = idx_ref[0]
    K, TM = idx.shape
    iota_e = lax.broadcasted_iota(jnp.int32, (E, TM), 0)
    start = start_ref[...]
    rows = [jnp.sum(jnp.where(iota_e == idx[k:k + 1], start, 0), axis=0, keepdims=True) for k in range(K)]
    dest_ref[0] = jnp.concatenate(rows, axis=0) + rank_ref[0]


def _route_dispatch(geo, scores, bias, blk):
    B, P = geo.B, geo.P
    E = scores.shape[1]
    K = TOP_K
    kspec = geo.col_spec(K)
    idx, w = pl.pallas_call(
        _route_kernel, grid=(B, geo.nt),
        in_specs=[geo.col_spec(E), _full_spec((E, 1))],
        out_specs=[kspec, kspec],
        out_shape=[jax.ShapeDtypeStruct((B, K, P), jnp.int32), jax.ShapeDtypeStruct((B, K, P), F32)],
        compiler_params=_cp(2), name="moe_route",
    )(scores, bias.astype(F32).reshape(E, 1))
    rank, cnt = pl.pallas_call(
        functools.partial(_rank_kernel, E=E), grid=(B, geo.nt),
        in_specs=[kspec], out_specs=[kspec, _full_spec((E, 1))],
        out_shape=[jax.ShapeDtypeStruct((B, K, P), jnp.int32), jax.ShapeDtypeStruct((E, 1), F32)],
        scratch_shapes=[pltpu.VMEM((E, 1), F32)],
        compiler_params=_cp(2), name="moe_rank",
    )(idx)
    n_assign = B * P * K
    n_blocks = -(-(n_assign + E * (blk - 1)) // blk)
    counts = cnt[:, 0].astype(jnp.int32)
    pcounts = (counts + blk - 1) // blk * blk
    pends = jnp.cumsum(pcounts)
    starts = (pends - pcounts).astype(jnp.int32)
    n_used = pends[-1] // blk
    blk_e = jnp.minimum(jnp.searchsorted(pends, jnp.arange(n_blocks) * blk, side='right'), E - 1)
    blk_e = jnp.where(jnp.arange(n_blocks) < n_used, blk_e, blk_e[jnp.maximum(n_used - 1, 0)])
    dest = pl.pallas_call(
        functools.partial(_dest_kernel, E=E), grid=(B, geo.nt),
        in_specs=[kspec, kspec, _full_spec((E, 1))], out_specs=kspec,
        out_shape=jax.ShapeDtypeStruct((B, K, P), jnp.int32),
        compiler_params=_cp(2), name="moe_dest",
    )(idx, rank, starts.reshape(E, 1))
    return w, dest, blk_e.astype(jnp.int32), n_used.astype(jnp.int32).reshape(1), n_blocks


def _expert_kernel(be_ref, nu_ref, x_ref, wgu_ref, wd_ref, o_ref):
    del be_ref
    i = pl.program_id(0)

    @pl.when(i < nu_ref[0])
    def _():
        h = jnp.dot(x_ref[...], wgu_ref[...].astype(BF16), preferred_element_type=F32)
        Fh = h.shape[-1] // 2
        a = _silu(h[:, :Fh]) * h[:, Fh:]
        o_ref[...] = jnp.dot(a.astype(BF16), wd_ref[...].astype(BF16),
                             preferred_element_type=F32).astype(o_ref.dtype)

    @pl.when(i >= nu_ref[0])
    def _():
        o_ref[...] = jnp.zeros_like(o_ref)


def _expert_ffn(x_sorted, blk_e, n_used, w_gu, w_down, layer, blk):
    n_rows, D = x_sorted.shape
    F2 = w_gu.shape[-1]
    grid_spec = pltpu.PrefetchScalarGridSpec(
        num_scalar_prefetch=2, grid=(n_rows // blk,),
        in_specs=[pl.BlockSpec((blk, D), lambda i, be, nu: (i, 0)),
                  pl.BlockSpec((None, None, D, F2), lambda i, be, nu: (layer, be[i], 0, 0)),
                  pl.BlockSpec((None, None, F2 // 2, D), lambda i, be, nu: (layer, be[i], 0, 0))],
        out_specs=pl.BlockSpec((blk, D), lambda i, be, nu: (i, 0)))
    return pl.pallas_call(
        _expert_kernel, grid_spec=grid_spec,
        out_shape=jax.ShapeDtypeStruct((n_rows, D), BF16),
        compiler_params=_cp(1), name="moe_experts",
    )(blk_e, n_used, x_sorted, w_gu, w_down)


def _moe_final_kernel(xm_ref, f_ref, ga_ref, w_ref, shgu_ref, shd_ref, gate_ref, lng_ref, lnb_ref, o_ref,
                      *, alpha):
    h = jnp.dot(f_ref[0], shgu_ref[...], preferred_element_type=F32)
    Fh = h.shape[-1] // 2
    a = _silu(h[:, :Fh]) * h[:, Fh:]
    y = jnp.dot(a.astype(BF16), shd_ref[...], preferred_element_type=F32)
    w = w_ref[0]
    for k in range(w.shape[-1]):
        y = y + w[:, k:k + 1] * ga_ref[k, 0].astype(F32)
    o_ref[0] = _layer_norm(alpha * xm_ref[0] + gate_ref[...] * y, lng_ref[...], lnb_ref[...])


def _moe(geo, xm, f, scores, modl, bias, w_gu, w_down, sh_gu, sh_down, lng, lnb, layer, alpha, blk):
    B, P, D = geo.B, geo.P, geo.D
    T = B * P
    K = TOP_K
    w, dest, blk_e, n_used, n_blocks = _route_dispatch(geo, scores, bias, blk)
    dest_flat = jnp.swapaxes(dest, 0, 1).reshape(K * T)
    tok = jnp.broadcast_to(jnp.arange(T, dtype=jnp.int32)[None], (K, T)).reshape(K * T)
    row_tok = (jnp.arange(n_blocks * blk, dtype=jnp.int32) % T).at[dest_flat].set(
        tok, unique_indices=True, mode='promise_in_bounds')
    x_sorted = f.reshape(T, D).at[row_tok].get(mode='promise_in_bounds')
    y_sorted = _expert_ffn(x_sorted, blk_e, n_used, w_gu, w_down, layer, blk)
    gathered = y_sorted.at[dest_flat].get(unique_indices=True, mode='promise_in_bounds').reshape(K, B, P, D)
    F2 = sh_gu.shape[-1]
    TM = geo.TM
    return pl.pallas_call(
        functools.partial(_moe_final_kernel, alpha=alpha), grid=(B, geo.nt),
        in_specs=[geo.row_spec(D), geo.row_spec(D),
                  pl.BlockSpec((K, 1, TM, D), lambda b, i: (0, b, i, 0)), geo.row_spec(K),
                  _full_spec((D, F2)), _full_spec((F2 // 2, D)), geo.mod_spec(5),
                  _full_spec((1, D)), _full_spec((1, D))],
        out_specs=geo.row_spec(D), out_shape=jax.ShapeDtypeStruct((B, P, D), F32),
        compiler_params=_cp(2), name="moe_final",
    )(xm, f, gathered, jnp.swapaxes(w, 1, 2), sh_gu.astype(BF16), sh_down.astype(BF16), modl,
      lng.reshape(1, D), lnb.reshape(1, D))


def kernel(x, c, ctx, c_ctx, mod_w, mod_b, ln_g, ln_b, rg_w_in, rg_conv_w, rg_conv_b, rg_gate_w, rg_gate_b, rg_lam, rg_w_out, hy_w_in, hy_b_in, hy_short_w, hy_short_b, hy_f_w1, hy_f_b1, hy_f_w2, hy_f_b2, hy_f_w3, hy_f_b3, hy_f_w4, hy_f_freq, hy_f_decay, hy_f_bias, hy_w_out, hy_b_out, da_w_in, da_lam, da_subln, da_w_out, s5_a_re, s5_a_im, s5_log_step, s5_b_re, s5_b_im, s5_c_re, s5_c_im, s5_d, s5_w_glu, s5_b_glu, moe_w_router, moe_bias, moe_w_gu, moe_w_down, moe_sh_gu, moe_sh_down):
    B, S, D = x.shape
    C = ctx.shape[1]
    depth = mod_w.shape[0]
    alpha = (2 * depth) ** 0.25
    geo = _Geo(B, S, C, D)
    xs = jnp.concatenate([x, ctx], axis=1)
    R = -(-(B + 1) // SUBLANES) * SUBLANES
    cc = jnp.zeros((R, D), F32).at[:B].set(c).at[B].set(c_ctx)
    modt = _mod_table(cc, mod_w, mod_b).reshape(depth, 6, R, 1, D)
    blk = min(MOE_BLOCK, geo.TM)
    for i in range(depth):
        kind, j = i % N_MIXERS, i // N_MIXERS
        modl = modt[i]
        post = functools.partial(_post_mixer, geo, xs=xs, modl=modl, lng=ln_g[i, 0], lnb=ln_b[i, 0],
                                 w_router=moe_w_router[i], alpha=alpha)
        if kind == 0:
            y = _rglru_mixer(geo, xs, modl, rg_w_in[j], rg_conv_w[j], rg_conv_b[j], rg_gate_w[j],
                             rg_gate_b[j], rg_lam[j])
            xm, f, scores = post(y=y, w_out=rg_w_out[j], b_out=None)
        elif kind == 1:
            y = _hyena_mixer(geo, xs, modl, hy_w_in[j], hy_b_in[j], hy_short_w[j], hy_short_b[j],
                             hy_f_w1[j], hy_f_b1[j], hy_f_w2[j], hy_f_b2[j], hy_f_w3[j], hy_f_b3[j],
                             hy_f_w4[j], hy_f_freq[j], hy_f_decay[j], hy_f_bias[j])
            xm, f, scores = post(y=y, w_out=hy_w_out[j], b_out=hy_b_out[j])
        elif kind == 2:
            y = _diff_attention_mixer(geo, xs, modl, da_w_in[j], da_lam[j], da_subln[j], i)
            xm, f, scores = post(y=y, w_out=da_w_out[j], b_out=None)
        else:
            u, y2 = _s5_mixer(geo, xs, modl, s5_a_re[j], s5_a_im[j], s5_log_step[j], s5_b_re[j],
                              s5_b_im[j], s5_c_re[j], s5_c_im[j])
            xm, f, scores = _s5_post(geo, y2, u, s5_d[j], s5_w_glu[j], s5_b_glu[j], xs, modl,
                                     ln_g[i, 0], ln_b[i, 0], moe_w_router[i], alpha)
        xs = _moe(geo, xm, f, scores, modl, moe_bias[i], moe_w_gu, moe_w_down, moe_sh_gu[i],
                  moe_sh_down[i], ln_g[i, 1], ln_b[i, 1], i, alpha, blk)
    return xs[:, :S]
```

```python
import functools
import math

import jax
import jax.numpy as jnp
from jax import lax
from jax.experimental import pallas as pl
from jax.experimental.pallas import tpu as pltpu

F32 = jnp.float32
BF16 = jnp.bfloat16
HIGHEST = lax.Precision.HIGHEST

N_MIXERS = 4
LN_EPS = 1e-6
LRU_C = 8.0
RG_BLOCK = 128
GRID_W = 64
DA_HEAD_DIM = 64
ROPE_THETA = 10000.0
S5_GROUP = 16
S5_CHUNK = 16
TOP_K = 8
N_EXPERT_GROUPS = 8
TOPK_GROUPS = 4
ROUTED_SCALE = 2.5
MOE_BLOCK = 512
MOE_GROUPS = 2
DA_Q_TILE = 1024

LANES = 128
SUBLANES = 8
VMEM_LIMIT = 56 * 1024 * 1024


def _cp(n_grid):
    return pltpu.CompilerParams(dimension_semantics=("arbitrary",) * n_grid,
                                vmem_limit_bytes=VMEM_LIMIT)


def _silu(x):
    return x * jax.nn.sigmoid(x)


def _mod_table_kernel(c_ref, w_ref, b_ref, o_ref):
    s = _silu(c_ref[...])
    o_ref[...] = jnp.dot(s, w_ref[...], precision=HIGHEST, preferred_element_type=F32) + b_ref[...]


def _mod_table(cc, mod_w, mod_b):
    depth, D, _ = mod_w.shape
    R = cc.shape[0]
    return pl.pallas_call(
        _mod_table_kernel,
        grid=(depth, 6),
        in_specs=[pl.BlockSpec((R, D), lambda i, k: (0, 0)),
                  pl.BlockSpec((None, D, D), lambda i, k: (i, 0, k)),
                  pl.BlockSpec((None, None, 1, D), lambda i, k: (i, k, 0, 0))],
        out_specs=pl.BlockSpec((None, None, R, D), lambda i, k: (i, k, 0, 0)),
        out_shape=jax.ShapeDtypeStruct((depth, 6, R, D), F32),
        compiler_params=_cp(2), name="mod_table",
    )(cc, mod_w, mod_b.reshape(depth, 6, 1, D))


class _Geo:
    def __init__(self, B, S, C, D, b0=0, B_all=None):
        self.B, self.S, self.C, self.D = B, S, C, D
        self.b0 = b0
        self.B_all = B if B_all is None else B_all
        self.P = S + C
        self.TM = math.gcd(S, C)
        while self.TM > 256:
            self.TM //= 2
        self.nt = self.P // self.TM
        self.n_lat = S // self.TM

    def group(self, b0, nb):
        return _Geo(nb, self.S, self.C, self.D, b0=b0, B_all=self.B_all)

    def mod_spec(self, k):
        D, b0, Ba, n_lat = self.D, self.b0, self.B_all, self.n_lat
        return pl.BlockSpec((None, None, 1, D),
                            lambda b, i, *_: (k, jnp.where(i < n_lat, b + b0, Ba), 0, 0))

    def row_spec(self, width, col=0, whole=False):
        b0 = self.b0 if whole else 0
        return pl.BlockSpec((1, self.TM, width), lambda b, i, *_: (b + b0, i, col))

    def col_spec(self, height, whole=False):
        b0 = self.b0 if whole else 0
        return pl.BlockSpec((1, height, self.TM), lambda b, i, *_: (b + b0, 0, i))


def _full_spec(shape):
    nd = len(shape)
    return pl.BlockSpec(shape, lambda *_: (0,) * nd)


def _layer_norm(z, g, b):
    mu = jnp.mean(z, axis=-1, keepdims=True)
    zc = z - mu
    var = jnp.mean(zc * zc, axis=-1, keepdims=True)
    return zc * lax.rsqrt(var + LN_EPS) * g + b


def _finish_mixer(y, x, gate, lng, lnb, sh, sc, wr_ref, xm_ref, f_ref, s_ref, alpha):
    xn = _layer_norm(alpha * x + gate * y, lng, lnb)
    xm_ref[0] = xn
    f = xn * (1.0 + sc) + sh
    f_ref[0] = f.astype(f_ref.dtype)
    logits = lax.dot_general(wr_ref[...], f, (((1,), (1,)), ((), ())), precision=HIGHEST,
                             preferred_element_type=F32)
    s_ref[0] = jax.nn.sigmoid(logits)


def _post_kernel(*refs, alpha, has_bias):
    if has_bias:
        (y_ref, w_ref, b_ref, x_ref, gate_ref, lng_ref, lnb_ref, sh_ref, sc_ref, wr_ref,
         xm_ref, f_ref, s_ref) = refs
    else:
        (y_ref, w_ref, x_ref, gate_ref, lng_ref, lnb_ref, sh_ref, sc_ref, wr_ref,
         xm_ref, f_ref, s_ref) = refs
    y = jnp.dot(y_ref[0].astype(BF16), w_ref[...], preferred_element_type=F32)
    if has_bias:
        y = y + b_ref[...]
    _finish_mixer(y, x_ref[0], gate_ref[...], lng_ref[...], lnb_ref[...], sh_ref[...], sc_ref[...],
                  wr_ref, xm_ref, f_ref, s_ref, alpha)


def _post_outs(geo, E):
    B, P, D = geo.B, geo.P, geo.D
    out_specs = [geo.row_spec(D), geo.row_spec(D), geo.col_spec(E)]
    out_shape = [jax.ShapeDtypeStruct((B, P, D), F32), jax.ShapeDtypeStruct((B, P, D), BF16),
                 jax.ShapeDtypeStruct((B, E, P), F32)]
    return out_specs, out_shape


def _post_mixer(geo, y, w_out, b_out, xs, modl, lng, lnb, w_router, alpha):
    D = geo.D
    Kd = y.shape[-1]
    E = w_router.shape[-1]
    has_bias = b_out is not None
    ins = [y, w_out.astype(BF16)]
    specs = [geo.row_spec(Kd), _full_spec((Kd, D))]
    if has_bias:
        ins.append(b_out.reshape(1, D))
        specs.append(_full_spec((1, D)))
    ins += [xs, modl, lng.reshape(1, D), lnb.reshape(1, D), modl, modl, w_router.T]
    specs += [geo.row_spec(D), geo.mod_spec(2), _full_spec((1, D)), _full_spec((1, D)),
              geo.mod_spec(3), geo.mod_spec(4), _full_spec((E, D))]
    out_specs, out_shape = _post_outs(geo, E)
    return pl.pallas_call(
        functools.partial(_post_kernel, alpha=alpha, has_bias=has_bias),
        grid=(geo.B, geo.nt), in_specs=specs, out_specs=out_specs, out_shape=out_shape,
        compiler_params=_cp(2), name="post_mixer",
    )(*ins)


def _dwconv_seg(r, cw, cb, S, C, lo):
    P = S + C
    row = lax.broadcasted_iota(jnp.int32, r.shape, 0)
    tl = jnp.where(row < S, row, row - S)
    sl = jnp.where(row < S, S, C)
    acc = jnp.zeros_like(r) + cb
    for k in range(cw.shape[0]):
        off = k - lo
        if off == 0:
            term = r
        else:
            shifted = pltpu.roll(r, (-off) % P, 0)
            valid = jnp.logical_and(tl + off >= 0, tl + off < sl)
            term = jnp.where(valid, shifted, 0.0)
        acc = acc + cw[k:k + 1, :] * term
    return acc


def _rg_in_kernel(x_ref, sh_ref, sc_ref, wg_ref, wr_ref, g_ref, r_ref):
    h = (x_ref[0] * (1.0 + sc_ref[...]) + sh_ref[...]).astype(BF16)
    g = jnp.dot(h, wg_ref[...], preferred_element_type=F32)
    g_ref[0] = jax.nn.gelu(g).astype(g_ref.dtype)
    r_ref[0] = jnp.dot(h, wr_ref[...], preferred_element_type=F32)


def _rg_scan_kernel(r_ref, g_ref, cw_ref, cb_ref, gw_ref, gb_ref, lam_ref, o_ref,
                    a_scr, b_scr, h_scr, *, S, C):
    P = S + C
    n = r_ref.shape[-1]
    r = r_ref[0]
    rc = _dwconv_seg(r, cw_ref[...], cb_ref[...], S, C, cw_ref.shape[0] // 2)
    rcb = rc.astype(BF16)
    row = lax.broadcasted_iota(jnp.int32, (P, n), 0)
    sub = jnp.bitwise_and(row, SUBLANES - 1)
    for d in range(2):
        rev = d == 1
        gr = jax.nn.sigmoid(jnp.dot(rcb, gw_ref[d, 0], preferred_element_type=F32) + gb_ref[d, 0])
        gi = jax.nn.sigmoid(jnp.dot(rcb, gw_ref[d, 1], preferred_element_type=F32) + gb_ref[d, 1])
        nl = -lam_ref[d]
        sp = jnp.maximum(nl, 0.0) + jnp.log1p(jnp.exp(-jnp.abs(nl)))
        a = jnp.exp(-LRU_C * gr * sp)
        bb = jnp.sqrt(1.0 - a * a) * gi * rc
        for s in (1, 2, 4):
            shift = (P - s) if rev else s
            a_sh = pltpu.roll(a, shift, 0)
            b_sh = pltpu.roll(bb, shift, 0)
            m = (sub < SUBLANES - s) if rev else (sub >= s)
            bb = jnp.where(m, a * b_sh + bb, bb)
            a = jnp.where(m, a * a_sh, a)
        a_scr[...] = a
        b_scr[...] = bb

        def chain(lo_tile, n_tiles, c0):
            def body(i, c):
                t = (lo_tile + n_tiles - 1 - i) if rev else (lo_tile + i)
                off = pl.multiple_of(t * SUBLANES, SUBLANES)
                h = b_scr[pl.ds(off, SUBLANES), :] + a_scr[pl.ds(off, SUBLANES), :] * c
                if d == 0:
                    h_scr[pl.ds(off, SUBLANES), :] = h
                else:
                    h_scr[pl.ds(off, SUBLANES), :] = h_scr[pl.ds(off, SUBLANES), :] + h
                edge = h[0:1] if rev else h[SUBLANES - 1:SUBLANES]
                return jnp.broadcast_to(edge, (SUBLANES, n))
            return lax.fori_loop(0, n_tiles, body, c0)

        c_ctx = chain(S // SUBLANES, C // SUBLANES, jnp.zeros((SUBLANES, n), F32))
        chain(0, S // SUBLANES, c_ctx)
    o_ref[0] = (g_ref[0].astype(F32) * h_scr[...]).astype(o_ref.dtype)


def _rglru_mixer(geo, xs, modl, w_in, conv_w, conv_b, gate_w, gate_b, lam):
    B, P, D, S, C = geo.B, geo.P, geo.D, geo.S, geo.C
    R = w_in.shape[1] // 2
    nb = R // RG_BLOCK
    w_in = w_in.astype(BF16)
    g, r = pl.pallas_call(
        _rg_in_kernel, grid=(B, geo.nt),
        in_specs=[geo.row_spec(D), geo.mod_spec(0), geo.mod_spec(1),
                  pl.BlockSpec((D, R), lambda b, i: (0, 0)), pl.BlockSpec((D, R), lambda b, i: (0, 1))],
        out_specs=[geo.row_spec(R), geo.row_spec(R)],
        out_shape=[jax.ShapeDtypeStruct((B, P, R), BF16), jax.ShapeDtypeStruct((B, P, R), F32)],
        compiler_params=_cp(2), name="rg_in",
    )(xs, modl, modl, w_in, w_in)
    K = conv_w.shape[0]
    seq_spec = pl.BlockSpec((1, P, RG_BLOCK), lambda b, n: (b, 0, n))
    y = pl.pallas_call(
        functools.partial(_rg_scan_kernel, S=S, C=C), grid=(B, nb),
        in_specs=[seq_spec, seq_spec,
                  pl.BlockSpec((K, RG_BLOCK), lambda b, n: (0, n)),
                  pl.BlockSpec((1, RG_BLOCK), lambda b, n: (0, n)),
                  pl.BlockSpec((2, 2, None, RG_BLOCK, RG_BLOCK), lambda b, n: (0, 0, n, 0, 0)),
                  pl.BlockSpec((2, 2, 1, RG_BLOCK), lambda b, n: (0, 0, 0, n)),
                  pl.BlockSpec((2, 1, RG_BLOCK), lambda b, n: (0, 0, n))],
        out_specs=seq_spec,
        out_shape=jax.ShapeDtypeStruct((B, P, R), BF16),
        scratch_shapes=[pltpu.VMEM((P, RG_BLOCK), F32)] * 3,
        compiler_params=_cp(2), name="rg_scan",
    )(r, g, conv_w, conv_b.reshape(1, R), gate_w.astype(BF16), gate_b.reshape(2, 2, 1, R),
      lam.reshape(2, 1, R))
    return y


def _mm_bias_kernel(x_ref, sh_ref, sc_ref, w_ref, b_ref, o_ref):
    h = (x_ref[0] * (1.0 + sc_ref[...]) + sh_ref[...]).astype(BF16)
    o_ref[0] = (jnp.dot(h, w_ref[...], preferred_element_type=F32) + b_ref[...]).astype(o_ref.dtype)


def _short_conv_kernel(u_ref, cw_ref, cb_ref, o_ref, ob_ref, *, S, C):
    y = _dwconv_seg(u_ref[0], cw_ref[...], cb_ref[...], S, C, (cw_ref.shape[0] - 1) // 2)
    o_ref[0] = y
    ob_ref[0] = y.astype(BF16)


def _dft_table_kernel(c_ref, s_ref, st_ref, *, L, TF):
    i = pl.program_id(0)
    N = 2 * L
    f = lax.broadcasted_iota(jnp.int32, (TF, L), 0) + i * TF
    t = lax.broadcasted_iota(jnp.int32, (TF, L), 1)
    ang = jnp.bitwise_and(f * t, N - 1).astype(F32) * (2.0 * math.pi / N)
    c_ref[...] = jnp.cos(ang).astype(BF16)
    nyq_t = (1 - 2 * jnp.bitwise_and(t, 1)).astype(F32)
    s_ref[...] = jnp.where(f == 0, nyq_t, jnp.sin(ang)).astype(BF16)
    nyq_f = (1 - 2 * jnp.bitwise_and(f, 1)).astype(F32)
    st_ref[...] = jnp.where(t == 0, nyq_f, jnp.sin(ang)).astype(BF16)


def _dft_tables(L):
    TF = min(L, 256)
    shp = jax.ShapeDtypeStruct((L, L), BF16)
    spec = pl.BlockSpec((TF, L), lambda i: (i, 0))
    return pl.pallas_call(
        functools.partial(_dft_table_kernel, L=L, TF=TF), grid=(L // TF,),
        in_specs=[], out_specs=[spec, spec, spec], out_shape=[shp, shp, shp],
        compiler_params=_cp(1), name="dft_tables",
    )()


def _hy_filter_kernel(z_ref, w1_ref, b1_ref, w2_ref, b2_ref, w3_ref, b3_ref, fq_ref,
                      w4f_ref, w4b_ref, df_ref, db_ref, tn_ref, kp_ref, km_ref):
    fq = fq_ref[...]

    def lin(h, w_ref, b_ref):
        return jnp.dot(h, w_ref[...], precision=HIGHEST, preferred_element_type=F32) + b_ref[...]

    h = jnp.sin(fq * lin(z_ref[...], w1_ref, b1_ref))
    h = jnp.sin(fq * lin(h, w2_ref, b2_ref))
    h = jnp.sin(fq * lin(h, w3_ref, b3_ref))
    tn = tn_ref[...]
    hf = jnp.dot(h, w4f_ref[...], precision=HIGHEST, preferred_element_type=F32)
    hf = hf * jnp.exp(-tn * jnp.abs(df_ref[...]))
    hb = jnp.dot(h, w4b_ref[...], precision=HIGHEST, preferred_element_type=F32)
    hb = hb * jnp.exp(-tn * jnp.abs(db_ref[...]))
    row = lax.broadcasted_iota(jnp.int32, hb.shape, 0)
    hb = jnp.where(row == 0, 0.0, hb)
    nrm = lax.rsqrt(jnp.sum(hf * hf, axis=0, keepdims=True) + jnp.sum(hb * hb, axis=0, keepdims=True) + 1e-6)
    hf = hf * nrm
    hb = hb * nrm
    kp_ref[...] = (hf + hb).astype(BF16)
    km_ref[...] = (hf - hb).astype(BF16)


def _hy_spectrum_kernel(c_ref, s_ref, s0_ref, kp_ref, km_ref, ka_ref, kb_ref, kc_ref, *, L, TF):
    i = pl.program_id(0)
    inv_n = 1.0 / (2 * L)
    kr = jnp.dot(c_ref[...], kp_ref[...], preferred_element_type=F32)
    ks = jnp.dot(s_ref[...], km_ref[...], preferred_element_type=F32)
    nyq = jnp.dot(s0_ref[...], kp_ref[...], preferred_element_type=F32)[0:1]
    f = lax.broadcasted_iota(jnp.int32, kr.shape, 0) + i * TF
    dc = f == 0
    ka_ref[...] = jnp.where(dc, kr * inv_n, kr * (2.0 * inv_n))
    kb_ref[...] = jnp.where(dc, 0.0, ks * (-2.0 * inv_n))
    kc_ref[...] = jnp.where(dc, nyq * inv_n, kr * (2.0 * inv_n))


def _hy_filters(L, tabs, fw1, fb1, fw2, fb2, fw3, fb3, fw4, ffreq, fdecay, D):
    cm, sm, _ = tabs
    E = fw1.shape[0]
    Hd = fw1.shape[1]
    bands = (E - 1) // 2
    t = jnp.arange(L, dtype=F32)
    t_norm = t / max(L - 1, 1)
    fr = jnp.linspace(1e-4, bands - 1, bands, dtype=F32)
    ang = (2.0 * math.pi / L) * t[:, None] * fr[None, :]
    z = jnp.concatenate([t_norm[:, None], jnp.cos(ang), -jnp.sin(ang)], -1)
    Ep, Hp = -(-E // LANES) * LANES, -(-Hd // LANES) * LANES
    z = jnp.pad(z, ((0, 0), (0, Ep - E)))
    fw1 = jnp.pad(fw1, ((0, Ep - E), (0, Hp - Hd)))
    fw2 = jnp.pad(fw2, ((0, Hp - Hd), (0, Hp - Hd)))
    fw3 = jnp.pad(fw3, ((0, Hp - Hd), (0, Hp - Hd)))
    fw4 = jnp.pad(fw4, ((0, Hp - Hd), (0, 0)))
    fb1, fb2, fb3, ffreq = (jnp.pad(v, (0, Hp - Hd)) for v in (fb1, fb2, fb3, ffreq))
    E, Hd = Ep, Hp
    CT = min(2 * D, 512)
    nct = 2 * D // CT
    dec = fdecay.reshape(1, 4 * D)
    kp, km = pl.pallas_call(
        _hy_filter_kernel, grid=(nct,),
        in_specs=[_full_spec((L, E)), _full_spec((E, Hd)), _full_spec((1, Hd)), _full_spec((Hd, Hd)),
                  _full_spec((1, Hd)), _full_spec((Hd, Hd)), _full_spec((1, Hd)), _full_spec((1, Hd)),
                  pl.BlockSpec((Hd, CT), lambda j: (0, j)), pl.BlockSpec((Hd, CT), lambda j: (0, j + nct)),
                  pl.BlockSpec((1, CT), lambda j: (0, j)), pl.BlockSpec((1, CT), lambda j: (0, j + nct)),
                  _full_spec((L, 1))],
        out_specs=[pl.BlockSpec((L, CT), lambda j: (0, j))] * 2,
        out_shape=[jax.ShapeDtypeStruct((L, 2 * D), BF16)] * 2,
        compiler_params=_cp(1), name="hy_filter",
    )(z, fw1, fb1.reshape(1, Hd), fw2, fb2.reshape(1, Hd), fw3, fb3.reshape(1, Hd), ffreq.reshape(1, Hd),
      fw4, fw4, dec, dec, t_norm[:, None])
    TF = min(L, 256)
    spec_w = pl.BlockSpec((TF, L), lambda i, j: (i, 0))
    spec_k = pl.BlockSpec((L, CT), lambda i, j: (0, j))
    spec_o = pl.BlockSpec((TF, CT), lambda i, j: (i, j))
    shp = jax.ShapeDtypeStruct((L, 2 * D), F32)
    return pl.pallas_call(
        functools.partial(_hy_spectrum_kernel, L=L, TF=TF), grid=(L // TF, nct),
        in_specs=[spec_w, spec_w, pl.BlockSpec((SUBLANES, L), lambda i, j: (0, 0)), spec_k, spec_k],
        out_specs=[spec_o] * 3, out_shape=[shp] * 3,
        compiler_params=_cp(2), name="hy_spectrum",
    )(cm, sm, sm, kp, km)


def _hy_fwd_kernel(z_ref, c_ref, s_ref, ka_ref, kb_ref, kc_ref, p_ref):
    z = z_ref[0]
    zr = jnp.dot(c_ref[...], z, preferred_element_type=F32)
    zs = jnp.dot(s_ref[...], z, preferred_element_type=F32)
    kb = kb_ref[...]
    p_ref[0, 0] = (zr * ka_ref[...] + zs * kb).astype(BF16)
    p_ref[0, 1] = (zs * kc_ref[...] - zr * kb).astype(BF16)


def _hy_inv_kernel(p_ref, c_ref, st_ref, z_ref, x_ref, fb_ref, o_ref):
    y = jnp.dot(c_ref[...], p_ref[0, 0], preferred_element_type=F32)
    y = y + jnp.dot(st_ref[...], p_ref[0, 1], preferred_element_type=F32)
    o_ref[0] = (x_ref[0] * (y + z_ref[0].astype(F32) * fb_ref[...])).astype(o_ref.dtype)


def _hy_inv_kernel_alias(p_ref, c_ref, st_ref, z_ref, x_ref, fb_ref, prev_ref, o_ref):
    del prev_ref
    _hy_inv_kernel(p_ref, c_ref, st_ref, z_ref, x_ref, fb_ref, o_ref)


def _hy_conv(geo, L, off, tabs, z, z_col, kfilt, k_col, xmul, x_col, fbias, out):
    B, P, D = geo.B, geo.P, geo.D
    cm, sm, smt = tabs
    ka, kb, kc = kfilt
    rb = off // L
    TF = min(L, 256)
    spec_w = pl.BlockSpec((TF, L), lambda b, i: (i, 0))
    spec_k = pl.BlockSpec((TF, D), lambda b, i: (i, k_col))
    p = pl.pallas_call(
        _hy_fwd_kernel, grid=(B, L // TF),
        in_specs=[pl.BlockSpec((1, L, D), lambda b, i: (b, rb, z_col)), spec_w, spec_w,
                  spec_k, spec_k, spec_k],
        out_specs=pl.BlockSpec((1, 2, TF, D), lambda b, i: (b, 0, i, 0)),
        out_shape=jax.ShapeDtypeStruct((B, 2, L, D), BF16),
        compiler_params=_cp(2), name="hy_fwd",
    )(z, cm, sm, ka, kb, kc)
    CT = min(D, 512)
    nct = D // CT
    rt = off // TF
    spec_wi = pl.BlockSpec((TF, L), lambda b, j, i: (i, 0))
    return pl.pallas_call(
        _hy_inv_kernel_alias, grid=(B, nct, L // TF),
        in_specs=[pl.BlockSpec((1, 2, L, CT), lambda b, j, i: (b, 0, 0, j)), spec_wi, spec_wi,
                  pl.BlockSpec((1, TF, CT), lambda b, j, i: (b, rt + i, z_col * nct + j)),
                  pl.BlockSpec((1, TF, CT), lambda b, j, i: (b, rt + i, x_col * nct + j)),
                  pl.BlockSpec((1, CT), lambda b, j, i: (0, k_col * nct + j)),
                  pl.BlockSpec(memory_space=pl.ANY)],
        out_specs=pl.BlockSpec((1, TF, CT), lambda b, j, i: (b, rt + i, j)),
        out_shape=jax.ShapeDtypeStruct((B, P, D), BF16),
        input_output_aliases={6: 0},
        compiler_params=_cp(3), name="hy_inv",
    )(p, cm, smt, z, xmul, fbias, out)


def _hyena_mixer(geo, xs, modl, w_in, b_in, short_w, short_b, fw1, fb1, fw2, fb2, fw3, fb3, fw4,
                 ffreq, fdecay, fbias):
    B, P, D, S, C = geo.B, geo.P, geo.D, geo.S, geo.C
    u0 = pl.pallas_call(
        _mm_bias_kernel, grid=(B, geo.nt, 3),
        in_specs=[geo.row_spec(D), geo.mod_spec(0), geo.mod_spec(1),
                  pl.BlockSpec((D, D), lambda b, i, j: (0, j)), pl.BlockSpec((1, D), lambda b, i, j: (0, j))],
        out_specs=pl.BlockSpec((1, geo.TM, D), lambda b, i, j: (b, i, j)),
        out_shape=jax.ShapeDtypeStruct((B, P, 3 * D), F32),
        compiler_params=_cp(3), name="hy_in",
    )(xs, modl, modl, w_in.astype(BF16), b_in.reshape(1, 3 * D))
    CT = min(D, 256)
    Ks = short_w.shape[0]
    spec = pl.BlockSpec((1, P, CT), lambda b, j: (b, 0, j))
    u, ub = pl.pallas_call(
        functools.partial(_short_conv_kernel, S=S, C=C), grid=(B, 3 * D // CT),
        in_specs=[spec, pl.BlockSpec((Ks, CT), lambda b, j: (0, j)), pl.BlockSpec((1, CT), lambda b, j: (0, j))],
        out_specs=[spec, spec],
        out_shape=[jax.ShapeDtypeStruct((B, P, 3 * D), F32), jax.ShapeDtypeStruct((B, P, 3 * D), BF16)],
        compiler_params=_cp(2), name="hy_short",
    )(u0, short_w, short_b.reshape(1, 3 * D))
    fb = fbias.reshape(1, 2 * D)
    z1 = jnp.zeros((B, P, D), BF16)
    z2 = jnp.zeros((B, P, D), BF16)
    segs = [(S, 0), (C, S)]
    convs = []
    for L, off in segs:
        tabs = _dft_tables(L)
        kf = _hy_filters(L, tabs, fw1, fb1, fw2, fb2, fw3, fb3, fw4, ffreq, fdecay, D)
        convs.append((L, off, tabs, kf))
    for L, off, tabs, kf in convs:
        z1 = _hy_conv(geo, L, off, tabs, ub, 0, kf, 0, u, 1, fb, z1)
    for L, off, tabs, kf in convs:
        z2 = _hy_conv(geo, L, off, tabs, z1, 0, kf, 1, u, 2, fb, z2)
    return z2


def _rope_tables(S, D):
    t = jnp.arange(S)
    row = (t // GRID_W).astype(F32)
    col = (t % GRID_W).astype(F32)
    axis_dim = DA_HEAD_DIM // 2
    half = axis_dim // 2
    inv = ROPE_THETA ** (-jnp.arange(0, axis_dim, 2, dtype=F32) / axis_dim)
    lane = jnp.arange(D)
    within = lane % DA_HEAD_DIM
    pos = jnp.where((within // axis_dim)[None, :] == 0, row[:, None], col[:, None])
    ang = pos * inv[lane % half][None, :]
    sign = jnp.where((lane % axis_dim) < half, -1.0, 1.0)[None, :]
    return jnp.cos(ang), jnp.sin(ang) * sign


def _da_in_kernel(x_ref, sh_ref, sc_ref, w_ref, cos_ref, sin_ref, o_ref, *, n_lat):
    i = pl.program_id(1)
    j = pl.program_id(2)
    h = (x_ref[0] * (1.0 + sc_ref[...]) + sh_ref[...]).astype(BF16)
    acc = jnp.dot(h, w_ref[...], preferred_element_type=F32)
    acc = acc * jnp.where(j == 0, DA_HEAD_DIM ** -0.5, 1.0)
    rot = jnp.logical_and(i < n_lat, j < 2)

    @pl.when(rot)
    def _():
        Dn = acc.shape[-1]
        half = DA_HEAD_DIM // 4
        lane = lax.broadcasted_iota(jnp.int32, acc.shape, 1)
        up = pltpu.roll(acc, Dn - half, 1)
        dn = pltpu.roll(acc, half, 1)
        partner = jnp.where(jnp.bitwise_and(lane, 2 * half - 1) < half, up, dn)
        o_ref[0] = (acc * cos_ref[...] + partner * sin_ref[...]).astype(o_ref.dtype)

    @pl.when(jnp.logical_not(rot))
    def _():
        o_ref[0] = acc.astype(o_ref.dtype)


def _da_attn_kernel(*refs, kv_lo, nk, lam_init, aliased):
    if aliased:
        q_ref, k_ref, v_ref, lam_ref, sub_ref, _, o_ref, vx_scr, s_scr = refs
    else:
        q_ref, k_ref, v_ref, lam_ref, sub_ref, o_ref, vx_scr, s_scr = refs
    i = pl.program_id(2)
    HW = v_ref.shape[-1]
    TQ = q_ref.shape[1]
    lp = lam_ref[...]
    lam = (jnp.exp(jnp.sum(lp[0:1] * lp[1:2], axis=1, keepdims=True))
           - jnp.exp(jnp.sum(lp[2:3] * lp[3:4], axis=1, keepdims=True)) + lam_init)

    @pl.when(i == 0)
    def _():
        vx_scr[:, :HW] = v_ref[0]
        vx_scr[:, HW:] = jnp.ones((vx_scr.shape[0], HW), BF16)

    lane = lax.broadcasted_iota(jnp.int32, (TQ, HW), 1)
    q = q_ref[0]
    k = k_ref[0, kv_lo:kv_lo + nk, :]
    outs = []
    for c in range(2):
        qc = jnp.where((lane // DA_HEAD_DIM) == c, q, jnp.zeros_like(q))
        s_scr[...] = lax.dot_general(qc, k, (((1,), (1,)), ((), ())), preferred_element_type=F32)
        m = jnp.max(s_scr[...], axis=-1, keepdims=True)
        p = jnp.exp((s_scr[...] - m).astype(BF16))
        ov = jnp.dot(p, vx_scr[kv_lo:kv_lo + nk, :], preferred_element_type=F32)
        outs.append(ov[:, :HW] / ov[:, HW:HW + 1])
    o = outs[0] - lam * outs[1]
    o = o * lax.rsqrt(jnp.mean(o * o, axis=-1, keepdims=True) + 1e-5) * sub_ref[...] * (1.0 - lam_init)
    o_ref[0] = o.astype(o_ref.dtype)


def _diff_attention_mixer(geo, xs, modl, w_in, lam_p, subln_w, layer_idx):
    B, P, D, S, C = geo.B, geo.P, geo.D, geo.S, geo.C
    H = D // (2 * DA_HEAD_DIM)
    HW = 2 * DA_HEAD_DIM
    cos_t, sin_t = _rope_tables(S, D)
    n_lat = geo.n_lat
    tab_spec = pl.BlockSpec((geo.TM, D), lambda b, i, j: (jnp.minimum(i, n_lat - 1), 0))
    qkv = pl.pallas_call(
        functools.partial(_da_in_kernel, n_lat=n_lat), grid=(B, geo.nt, 3),
        in_specs=[geo.row_spec(D), geo.mod_spec(0), geo.mod_spec(1),
                  pl.BlockSpec((D, D), lambda b, i, j: (0, j)), tab_spec, tab_spec],
        out_specs=pl.BlockSpec((1, geo.TM, D), lambda b, i, j: (b, i, j)),
        out_shape=jax.ShapeDtypeStruct((B, P, 3 * D), BF16),
        compiler_params=_cp(3), name="da_in",
    )(xs, modl, modl, w_in.astype(BF16), cos_t, sin_t)
    lam_init = 0.8 - 0.6 * math.exp(-0.3 * layer_idx)
    def attend(TQ, row0, n_tiles, kv_lo, nk, prev):
        rb = row0 // TQ
        ins = [qkv, qkv, qkv, lam_p, subln_w.reshape(1, HW)]
        specs = [pl.BlockSpec((1, TQ, HW), lambda b, h, i: (b, rb + i, h)),
                 pl.BlockSpec((1, P, HW), lambda b, h, i: (b, 0, H + h)),
                 pl.BlockSpec((1, P, HW), lambda b, h, i: (b, 0, 2 * H + h)),
                 _full_spec((4, DA_HEAD_DIM)), _full_spec((1, HW))]
        if prev is not None:
            ins.append(prev)
            specs.append(pl.BlockSpec(memory_space=pl.ANY))
        return pl.pallas_call(
            functools.partial(_da_attn_kernel, kv_lo=kv_lo, nk=nk, lam_init=lam_init, aliased=prev is not None),
            grid=(B, H, n_tiles), in_specs=specs,
            out_specs=pl.BlockSpec((1, TQ, HW), lambda b, h, i: (b, rb + i, h)),
            out_shape=jax.ShapeDtypeStruct((B, P, D), BF16),
            scratch_shapes=[pltpu.VMEM((P, 2 * HW), BF16), pltpu.VMEM((TQ, nk), F32)],
            input_output_aliases={} if prev is None else {5: 0},
            compiler_params=_cp(3), name="da_attn",
        )(*ins)

    TQ = geo.TM
    while TQ < DA_Q_TILE and S % (2 * TQ) == 0:
        TQ *= 2
    y = attend(TQ, 0, S // TQ, 0, P, None)
    return attend(geo.TM, S, C // geo.TM, S, C, y)


def _s5_operators(a_re, a_im, log_step, b_re, b_im, c_re, c_im):
    T = S5_CHUNK
    G, Pst = a_re.shape[1], a_re.shape[2]
    Hg = S5_GROUP
    GL = LANES // Hg
    LB = G // GL
    lam = lax.complex(jnp.minimum(a_re.astype(F32), -1e-4), a_im.astype(F32))
    step = jnp.exp(log_step.astype(F32))[..., None]
    abar = jnp.exp(lam * step)
    bbar = ((abar - 1.0) / lam)[..., None] * lax.complex(b_re.astype(F32), b_im.astype(F32))
    cmat = lax.complex(c_re.astype(F32), c_im.astype(F32))
    pows = jnp.stack([abar ** l for l in range(T + 1)], axis=1)
    eye = jnp.eye(GL, dtype=F32)
    ar = jnp.arange(T)
    big_m, big_g, big_h, a_t = [], [], [], []
    for d in range(2):
        pw = pows[d]
        kl = jnp.einsum('gjp,lgp,gph->lgjh', cmat[d], pw[:T], bbar[d]).real
        lag = (ar[None, :] - ar[:, None]) if d == 0 else (ar[:, None] - ar[None, :])
        tz = jnp.where((lag >= 0)[:, :, None, None, None], kl[jnp.clip(lag, 0, T - 1)], 0.0)
        tz = tz.reshape(T, T, LB, GL, Hg, Hg)
        m = jnp.einsum('stbgjh,gk->bsghtkj', tz, eye).reshape(LB, T * LANES, T * LANES)
        e_in = (T - 1 - ar) if d == 0 else ar
        gc = pw[e_in][:, :, :, None] * bbar[d][None]
        gc = gc.reshape(T, LB, GL, Pst, Hg)
        g_re = jnp.einsum('sbgph,gk->bsghkp', gc.real, eye).reshape(LB, T * LANES, GL * Pst)
        g_im = jnp.einsum('sbgph,gk->bsghkp', gc.imag, eye).reshape(LB, T * LANES, GL * Pst)
        e_out = (ar + 1) if d == 0 else (T - ar)
        hc = cmat[d][None] * pw[e_out][:, :, None, :]
        hc = hc.reshape(T, LB, GL, Hg, Pst)
        h_re = jnp.einsum('tbgjp,gk->bgptkj', hc.real, eye).reshape(LB, GL * Pst, T * LANES)
        h_im = jnp.einsum('tbgjp,gk->bgptkj', -hc.imag, eye).reshape(LB, GL * Pst, T * LANES)
        big_m.append(m)
        big_g.append(jnp.concatenate([g_re, g_im], axis=2))
        big_h.append(jnp.concatenate([h_re, h_im], axis=1))
        at = pw[T].reshape(LB, 1, GL * Pst)
        a_t.append(jnp.concatenate([at.real, at.imag], axis=2))
    return (jnp.stack(big_m).astype(BF16), jnp.stack(big_g).astype(BF16),
            jnp.stack(big_h).astype(BF16), jnp.stack(a_t).astype(F32))


def _modulate_kernel(x_ref, sh_ref, sc_ref, o_ref):
    o_ref[0] = x_ref[0] * (1.0 + sc_ref[...]) + sh_ref[...]


def _s5_kernel(u_ref, m_ref, g_ref, h_ref, a_ref, o_ref, gx_scr, sp_scr, *, S, C):
    d = pl.program_id(1)
    T = S5_CHUNK
    P = S + C
    n = P // T
    n_lat = S // T
    x = jnp.concatenate([u_ref[0, pl.ds(s, n, stride=T), :] for s in range(T)], axis=1).astype(BF16)
    gx_scr[...] = jnp.dot(x, g_ref[...], preferred_element_type=F32)
    ns = a_ref.shape[-1] // 2
    a_r = a_ref[:, :ns]
    a_i = a_ref[:, ns:]

    def scan(lo, cnt, rev, carry):
        def body(k, st):
            s_r, s_i = st
            c = (lo + cnt - 1 - k) if rev else (lo + k)
            sp_scr[pl.ds(c, 1), :] = jnp.concatenate([s_r, s_i], axis=1)
            gx = gx_scr[pl.ds(c, 1), :]
            return (a_r * s_r - a_i * s_i + gx[:, :ns], a_r * s_i + a_i * s_r + gx[:, ns:])
        return lax.fori_loop(0, cnt, body, carry)

    zero = (jnp.zeros((1, ns), F32), jnp.zeros((1, ns), F32))

    @pl.when(d == 0)
    def _():
        scan(0, n_lat, False, scan(n_lat, n - n_lat, False, zero))

    @pl.when(d == 1)
    def _():
        scan(0, n_lat, True, scan(n_lat, n - n_lat, True, zero))

    y = jnp.dot(x, m_ref[...], preferred_element_type=F32)
    y = y + jnp.dot(sp_scr[...].astype(BF16), h_ref[...], preferred_element_type=F32)
    for s in range(T):
        o_ref[0, pl.ds(s, n, stride=T), :] = y[:, s * LANES:(s + 1) * LANES]


def _s5_mixer(geo, xs, modl, a_re, a_im, log_step, b_re, b_im, c_re, c_im):
    B, P, D, S, C = geo.B, geo.P, geo.D, geo.S, geo.C
    u = pl.pallas_call(
        _modulate_kernel, grid=(B, geo.nt),
        in_specs=[geo.row_spec(D), geo.mod_spec(0), geo.mod_spec(1)],
        out_specs=geo.row_spec(D), out_shape=jax.ShapeDtypeStruct((B, P, D), F32),
        compiler_params=_cp(2), name="s5_modulate",
    )(xs, modl, modl)
    big_m, big_g, big_h, a_t = _s5_operators(a_re, a_im, log_step, b_re, b_im, c_re, c_im)
    LB = D // LANES
    TL = S5_CHUNK * LANES
    NS = big_g.shape[-1]
    n = P // S5_CHUNK
    y = pl.pallas_call(
        functools.partial(_s5_kernel, S=S, C=C), grid=(LB, 2, B),
        in_specs=[pl.BlockSpec((1, P, LANES), lambda l, d, b: (b, 0, l)),
                  pl.BlockSpec((None, None, TL, TL), lambda l, d, b: (d, l, 0, 0)),
                  pl.BlockSpec((None, None, TL, NS), lambda l, d, b: (d, l, 0, 0)),
                  pl.BlockSpec((None, None, NS, TL), lambda l, d, b: (d, l, 0, 0)),
                  pl.BlockSpec((None, None, 1, NS), lambda l, d, b: (d, l, 0, 0))],
        out_specs=pl.BlockSpec((None, 1, P, LANES), lambda l, d, b: (d, b, 0, l)),
        out_shape=jax.ShapeDtypeStruct((2, B, P, D), F32),
        scratch_shapes=[pltpu.VMEM((n, NS), F32), pltpu.VMEM((n, NS), F32)],
        compiler_params=_cp(3), name="s5_scan",
    )(u, big_m, big_g, big_h, a_t)
    return u, y


def _s5_post_kernel(yf_ref, yb_ref, u_ref, d_ref, w_ref, b_ref, x_ref, gate_ref, lng_ref, lnb_ref,
                    sh_ref, sc_ref, wr_ref, xm_ref, f_ref, s_ref, *, alpha):
    g = jax.nn.gelu(yf_ref[0] + yb_ref[0] + d_ref[...] * u_ref[0])
    vg = jnp.dot(g.astype(BF16), w_ref[...], preferred_element_type=F32) + b_ref[...]
    Dn = vg.shape[-1] // 2
    y = vg[:, :Dn] * jax.nn.sigmoid(vg[:, Dn:])
    _finish_mixer(y, x_ref[0], gate_ref[...], lng_ref[...], lnb_ref[...], sh_ref[...], sc_ref[...],
                  wr_ref, xm_ref, f_ref, s_ref, alpha)


def _s5_post(geo, y2, u, d_skip, w_glu, b_glu, xs, modl, lng, lnb, w_router, alpha):
    D = geo.D
    E = w_router.shape[-1]
    out_specs, out_shape = _post_outs(geo, E)
    TM = geo.TM
    return pl.pallas_call(
        functools.partial(_s5_post_kernel, alpha=alpha), grid=(geo.B, geo.nt),
        in_specs=[pl.BlockSpec((None, 1, TM, D), lambda b, i: (0, b, i, 0)),
                  pl.BlockSpec((None, 1, TM, D), lambda b, i: (1, b, i, 0)),
                  geo.row_spec(D), _full_spec((1, D)), _full_spec((D, 2 * D)), _full_spec((1, 2 * D)),
                  geo.row_spec(D), geo.mod_spec(2), _full_spec((1, D)), _full_spec((1, D)),
                  geo.mod_spec(3), geo.mod_spec(4), _full_spec((E, D))],
        out_specs=out_specs, out_shape=out_shape,
        compiler_params=_cp(2), name="s5_post",
    )(y2, y2, u, d_skip.reshape(1, D), w_glu.astype(BF16), b_glu.reshape(1, 2 * D), xs, modl,
      lng.reshape(1, D), lnb.reshape(1, D), modl, modl, w_router.T)


def _route_kernel(s_ref, b_ref, idx_ref, w_ref):
    sc = s_ref[0]
    E, TM = sc.shape
    biased = sc + b_ref[...]
    G = N_EXPERT_GROUPS
    per = E // G
    neg = -jnp.inf
    blocks, gs = [], []
    for g in range(G):
        blk = biased[g * per:(g + 1) * per]
        m1 = jnp.max(blk, axis=0, keepdims=True)
        is1 = blk == m1
        cnt = jnp.sum(is1.astype(F32), axis=0, keepdims=True)
        m2 = jnp.max(jnp.where(is1, neg, blk), axis=0, keepdims=True)
        blocks.append(blk)
        gs.append(m1 + jnp.where(cnt >= 2.0, m1, m2))
    masked = []
    for g in range(G):
        ahead = jnp.zeros((1, TM), F32)
        for h in range(G):
            if h < g:
                ahead = ahead + (gs[h] >= gs[g]).astype(F32)
            elif h > g:
                ahead = ahead + (gs[h] > gs[g]).astype(F32)
        masked.append(jnp.where(ahead < float(TOPK_GROUPS), blocks[g], neg))
    masked = jnp.concatenate(masked, axis=0)
    iota_e = lax.broadcasted_iota(jnp.int32, (E, TM), 0)
    idxs, ws = [], []
    for _ in range(TOP_K):
        m = jnp.max(masked, axis=0, keepdims=True)
        ik = jnp.min(jnp.where(masked == m, iota_e, E), axis=0, keepdims=True)
        sel = iota_e == ik
        ws.append(jnp.sum(jnp.where(sel, sc, 0.0), axis=0, keepdims=True))
        idxs.append(ik)
        masked = jnp.where(sel, neg, masked)
    tot = ws[0]
    for wk in ws[1:]:
        tot = tot + wk
    w = jnp.concatenate(ws, axis=0)
    idx_ref[0] = jnp.concatenate(idxs, axis=0)
    w_ref[0] = w / (tot + 1e-20) * ROUTED_SCALE


def _rank_kernel(idx_ref, rank_ref, cnt_ref, run_scr, *, E):
    first = jnp.logical_and(pl.program_id(0) == 0, pl.program_id(1) == 0)

    @pl.when(first)
    def _():
        run_scr[...] = jnp.zeros_like(run_scr)

    idx = idx_ref[0]
    K, TM = idx.shape
    iota_e = lax.broadcasted_iota(jnp.int32, (E, TM), 0)
    member = jnp.zeros((E, TM), F32)
    for k in range(K):
        member = member + (iota_e == idx[k:k + 1]).astype(F32)
    before = (lax.broadcasted_iota(jnp.int32, (TM, TM), 0)
              < lax.broadcasted_iota(jnp.int32, (TM, TM), 1)).astype(BF16)
    rank = jnp.dot(member.astype(BF16), before, preferred_element_type=F32) + run_scr[...]
    rows = [jnp.sum(jnp.where(iota_e == idx[k:k + 1], rank, 0.0), axis=0, keepdims=True) for k in range(K)]
    rank_ref[0] = jnp.concatenate(rows, axis=0).astype(jnp.int32)
    run_scr[...] = run_scr[...] + jnp.sum(member, axis=1, keepdims=True)
    cnt_ref[...] = run_scr[...]


def _dest_kernel(idx_ref, rank_ref, start_ref, dest_ref, *, E):
    idx = idx_ref[0]
    K, TM = idx.shape
    iota_e = lax.broadcasted_iota(jnp.int32, (E, TM), 0)
    start = start_ref[...]
    rows = [jnp.sum(jnp.where(iota_e == idx[k:k + 1], start, 0), axis=0, keepdims=True) for k in range(K)]
    dest_ref[0] = jnp.concatenate(rows, axis=0) + rank_ref[0]


def _route_dispatch(geo, scores, bias, blk):
    B, P = geo.B, geo.P
    E = scores.shape[1]
    K = TOP_K
    kspec = geo.col_spec(K)
    idx, w = pl.pallas_call(
        _route_kernel, grid=(B, geo.nt),
        in_specs=[geo.col_spec(E, whole=True), _full_spec((E, 1))],
        out_specs=[kspec, kspec],
        out_shape=[jax.ShapeDtypeStruct((B, K, P), jnp.int32), jax.ShapeDtypeStruct((B, K, P), F32)],
        compiler_params=_cp(2), name="moe_route",
    )(scores, bias.astype(F32).reshape(E, 1))
    rank, cnt = pl.pallas_call(
        functools.partial(_rank_kernel, E=E), grid=(B, geo.nt),
        in_specs=[kspec], out_specs=[kspec, _full_spec((E, 1))],
        out_shape=[jax.ShapeDtypeStruct((B, K, P), jnp.int32), jax.ShapeDtypeStruct((E, 1), F32)],
        scratch_shapes=[pltpu.VMEM((E, 1), F32)],
        compiler_params=_cp(2), name="moe_rank",
    )(idx)
    n_assign = B * P * K
    n_blocks = -(-(n_assign + E * (blk - 1)) // blk)
    counts = cnt[:, 0].astype(jnp.int32)
    pcounts = (counts + blk - 1) // blk * blk
    pends = jnp.cumsum(pcounts)
    starts = (pends - pcounts).astype(jnp.int32)
    n_used = pends[-1] // blk
    blk_e = jnp.minimum(jnp.searchsorted(pends, jnp.arange(n_blocks) * blk, side='right'), E - 1)
    blk_e = jnp.where(jnp.arange(n_blocks) < n_used, blk_e, blk_e[jnp.maximum(n_used - 1, 0)])
    dest = pl.pallas_call(
        functools.partial(_dest_kernel, E=E), grid=(B, geo.nt),
        in_specs=[kspec, kspec, _full_spec((E, 1))], out_specs=kspec,
        out_shape=jax.ShapeDtypeStruct((B, K, P), jnp.int32),
        compiler_params=_cp(2), name="moe_dest",
    )(idx, rank, starts.reshape(E, 1))
    return w, dest, blk_e.astype(jnp.int32), n_used.astype(jnp.int32).reshape(1), n_blocks


def _expert_kernel(be_ref, nu_ref, x_ref, wgu_ref, wd_ref, o_ref):
    del be_ref
    i = pl.program_id(0)

    @pl.when(i < nu_ref[0])
    def _():
        h = jnp.dot(x_ref[...], wgu_ref[...].astype(BF16), preferred_element_type=F32)
        Fh = h.shape[-1] // 2
        a = _silu(h[:, :Fh]) * h[:, Fh:]
        o_ref[...] = jnp.dot(a.astype(BF16), wd_ref[...].astype(BF16),
                             preferred_element_type=F32).astype(o_ref.dtype)

    @pl.when(i >= nu_ref[0])
    def _():
        o_ref[...] = jnp.zeros_like(o_ref)


def _expert_ffn(x_sorted, blk_e, n_used, w_gu, w_down, layer, blk):
    n_rows, D = x_sorted.shape
    F2 = w_gu.shape[-1]
    grid_spec = pltpu.PrefetchScalarGridSpec(
        num_scalar_prefetch=2, grid=(n_rows // blk,),
        in_specs=[pl.BlockSpec((blk, D), lambda i, be, nu: (i, 0)),
                  pl.BlockSpec((None, None, D, F2), lambda i, be, nu: (layer, be[i], 0, 0)),
                  pl.BlockSpec((None, None, F2 // 2, D), lambda i, be, nu: (layer, be[i], 0, 0))],
        out_specs=pl.BlockSpec((blk, D), lambda i, be, nu: (i, 0)))
    return pl.pallas_call(
        _expert_kernel, grid_spec=grid_spec,
        out_shape=jax.ShapeDtypeStruct((n_rows, D), BF16),
        compiler_params=_cp(1), name="moe_experts",
    )(blk_e, n_used, x_sorted, w_gu, w_down)


def _moe_final_kernel(*refs, alpha, aliased):
    if aliased:
        xm_ref, f_ref, ga_ref, w_ref, shgu_ref, shd_ref, gate_ref, lng_ref, lnb_ref, _, o_ref = refs
    else:
        xm_ref, f_ref, ga_ref, w_ref, shgu_ref, shd_ref, gate_ref, lng_ref, lnb_ref, o_ref = refs
    h = jnp.dot(f_ref[0], shgu_ref[...], preferred_element_type=F32)
    Fh = h.shape[-1] // 2
    a = _silu(h[:, :Fh]) * h[:, Fh:]
    y = jnp.dot(a.astype(BF16), shd_ref[...], preferred_element_type=F32)
    w = w_ref[0]
    for k in range(w.shape[-1]):
        y = y + w[:, k:k + 1] * ga_ref[k, 0].astype(F32)
    o_ref[0] = _layer_norm(alpha * xm_ref[0] + gate_ref[...] * y, lng_ref[...], lnb_ref[...])


def _moe_group(geo, xm, f, scores, modl, bias, w_gu, w_down, sh_gu, sh_down, lng, lnb, layer, alpha, blk, prev):
    B, P, D = geo.B, geo.P, geo.D
    T = B * P
    K = TOP_K
    w, dest, blk_e, n_used, n_blocks = _route_dispatch(geo, scores, bias, blk)
    dest_flat = jnp.swapaxes(dest, 0, 1).reshape(K * T)
    t0 = geo.b0 * P
    tok = jnp.broadcast_to(jnp.arange(t0, t0 + T, dtype=jnp.int32)[None], (K, T)).reshape(K * T)
    row_tok = (t0 + jnp.arange(n_blocks * blk, dtype=jnp.int32) % T).at[dest_flat].set(
        tok, unique_indices=True, mode='promise_in_bounds')
    x_sorted = f.reshape(geo.B_all * P, D).at[row_tok].get(mode='promise_in_bounds')
    y_sorted = _expert_ffn(x_sorted, blk_e, n_used, w_gu, w_down, layer, blk)
    gathered = y_sorted.at[dest_flat].get(unique_indices=True, mode='promise_in_bounds').reshape(K, B, P, D)
    F2 = sh_gu.shape[-1]
    TM = geo.TM
    ins = [xm, f, gathered, jnp.swapaxes(w, 1, 2), sh_gu.astype(BF16), sh_down.astype(BF16), modl,
           lng.reshape(1, D), lnb.reshape(1, D)]
    specs = [geo.row_spec(D, whole=True), geo.row_spec(D, whole=True),
             pl.BlockSpec((K, 1, TM, D), lambda b, i: (0, b, i, 0)), geo.row_spec(K),
             _full_spec((D, F2)), _full_spec((F2 // 2, D)), geo.mod_spec(5),
             _full_spec((1, D)), _full_spec((1, D))]
    if prev is not None:
        ins.append(prev)
        specs.append(pl.BlockSpec(memory_space=pl.ANY))
    return pl.pallas_call(
        functools.partial(_moe_final_kernel, alpha=alpha, aliased=prev is not None), grid=(B, geo.nt),
        in_specs=specs, out_specs=geo.row_spec(D, whole=True),
        out_shape=jax.ShapeDtypeStruct((geo.B_all, P, D), F32),
        input_output_aliases={} if prev is None else {9: 0},
        compiler_params=_cp(2), name="moe_final",
    )(*ins)


def _moe(geo, xm, f, scores, modl, bias, w_gu, w_down, sh_gu, sh_down, lng, lnb, layer, alpha, blk):
    n_groups = MOE_GROUPS if geo.B % MOE_GROUPS == 0 else 1
    nb = geo.B // n_groups
    out = None
    for g in range(n_groups):
        out = _moe_group(geo.group(g * nb, nb), xm, f, scores, modl, bias, w_gu, w_down, sh_gu, sh_down,
                         lng, lnb, layer, alpha, blk, out)
    return out


def kernel(x, c, ctx, c_ctx, mod_w, mod_b, ln_g, ln_b, rg_w_in, rg_conv_w, rg_conv_b, rg_gate_w, rg_gate_b, rg_lam, rg_w_out, hy_w_in, hy_b_in, hy_short_w, hy_short_b, hy_f_w1, hy_f_b1, hy_f_w2, hy_f_b2, hy_f_w3, hy_f_b3, hy_f_w4, hy_f_freq, hy_f_decay, hy_f_bias, hy_w_out, hy_b_out, da_w_in, da_lam, da_subln, da_w_out, s5_a_re, s5_a_im, s5_log_step, s5_b_re, s5_b_im, s5_c_re, s5_c_im, s5_d, s5_w_glu, s5_b_glu, moe_w_router, moe_bias, moe_w_gu, moe_w_down, moe_sh_gu, moe_sh_down):
    B, S, D = x.shape
    C = ctx.shape[1]
    depth = mod_w.shape[0]
    alpha = (2 * depth) ** 0.25
    geo = _Geo(B, S, C, D)
    xs = jnp.concatenate([x, ctx], axis=1)
    R = -(-(B + 1) // SUBLANES) * SUBLANES
    cc = jnp.zeros((R, D), F32).at[:B].set(c).at[B].set(c_ctx)
    modt = _mod_table(cc, mod_w, mod_b).reshape(depth, 6, R, 1, D)
    blk = MOE_BLOCK
    for i in range(depth):
        kind, j = i % N_MIXERS, i // N_MIXERS
        modl = modt[i]
        post = functools.partial(_post_mixer, geo, xs=xs, modl=modl, lng=ln_g[i, 0], lnb=ln_b[i, 0],
                                 w_router=moe_w_router[i], alpha=alpha)
        if kind == 0:
            y = _rglru_mixer(geo, xs, modl, rg_w_in[j], rg_conv_w[j], rg_conv_b[j], rg_gate_w[j],
                             rg_gate_b[j], rg_lam[j])
            xm, f, scores = post(y=y, w_out=rg_w_out[j], b_out=None)
        elif kind == 1:
            y = _hyena_mixer(geo, xs, modl, hy_w_in[j], hy_b_in[j], hy_short_w[j], hy_short_b[j],
                             hy_f_w1[j], hy_f_b1[j], hy_f_w2[j], hy_f_b2[j], hy_f_w3[j], hy_f_b3[j],
                             hy_f_w4[j], hy_f_freq[j], hy_f_decay[j], hy_f_bias[j])
            xm, f, scores = post(y=y, w_out=hy_w_out[j], b_out=hy_b_out[j])
        elif kind == 2:
            y = _diff_attention_mixer(geo, xs, modl, da_w_in[j], da_lam[j], da_subln[j], i)
            xm, f, scores = post(y=y, w_out=da_w_out[j], b_out=None)
        else:
            u, y2 = _s5_mixer(geo, xs, modl, s5_a_re[j], s5_a_im[j], s5_log_step[j], s5_b_re[j],
                              s5_b_im[j], s5_c_re[j], s5_c_im[j])
            xm, f, scores = _s5_post(geo, y2, u, s5_d[j], s5_w_glu[j], s5_b_glu[j], xs, modl,
                                     ln_g[i, 0], ln_b[i, 0], moe_w_router[i], alpha)
        xs = _moe(geo, xm, f, scores, modl, moe_bias[i], moe_w_gu, moe_w_down, moe_sh_gu[i],
                  moe_sh_down[i], ln_g[i, 1], ln_b[i, 1], i, alpha, blk)
    return xs[:, :S]
```

```python
import functools
import math

import jax
import jax.numpy as jnp
from jax import lax
from jax.experimental import pallas as pl
from jax.experimental.pallas import tpu as pltpu

F32 = jnp.float32
BF16 = jnp.bfloat16
HIGHEST = lax.Precision.HIGHEST

N_MIXERS = 4
LN_EPS = 1e-6
LRU_C = 8.0
RG_BLOCK = 128
GRID_W = 64
DA_HEAD_DIM = 64
ROPE_THETA = 10000.0
S5_GROUP = 16
S5_CHUNK = 16
TOP_K = 8
N_EXPERT_GROUPS = 8
TOPK_GROUPS = 4
ROUTED_SCALE = 2.5
MOE_BLOCK = 256
MOE_GROUPS = 2
DA_Q_TILE = 1024

LANES = 128
SUBLANES = 8
VMEM_LIMIT = 56 * 1024 * 1024


def _cp(n_grid):
    return pltpu.CompilerParams(dimension_semantics=("arbitrary",) * n_grid,
                                vmem_limit_bytes=VMEM_LIMIT)


def _silu(x):
    return x * jax.nn.sigmoid(x)


def _mod_table_kernel(c_ref, w_ref, b_ref, o_ref):
    s = _silu(c_ref[...])
    o_ref[...] = jnp.dot(s, w_ref[...], precision=HIGHEST, preferred_element_type=F32) + b_ref[...]


def _mod_table(cc, mod_w, mod_b):
    depth, D, _ = mod_w.shape
    R = cc.shape[0]
    return pl.pallas_call(
        _mod_table_kernel,
        grid=(depth, 6),
        in_specs=[pl.BlockSpec((R, D), lambda i, k: (0, 0)),
                  pl.BlockSpec((None, D, D), lambda i, k: (i, 0, k)),
                  pl.BlockSpec((None, None, 1, D), lambda i, k: (i, k, 0, 0))],
        out_specs=pl.BlockSpec((None, None, R, D), lambda i, k: (i, k, 0, 0)),
        out_shape=jax.ShapeDtypeStruct((depth, 6, R, D), F32),
        compiler_params=_cp(2), name="mod_table",
    )(cc, mod_w, mod_b.reshape(depth, 6, 1, D))


class _Geo:
    def __init__(self, B, S, C, D, b0=0, B_all=None):
        self.B, self.S, self.C, self.D = B, S, C, D
        self.b0 = b0
        self.B_all = B if B_all is None else B_all
        self.P = S + C
        self.TM = math.gcd(S, C)
        while self.TM > 256:
            self.TM //= 2
        self.nt = self.P // self.TM
        self.n_lat = S // self.TM

    def group(self, b0, nb):
        return _Geo(nb, self.S, self.C, self.D, b0=b0, B_all=self.B_all)

    def mod_spec(self, k):
        D, b0, Ba, n_lat = self.D, self.b0, self.B_all, self.n_lat
        return pl.BlockSpec((None, None, 1, D),
                            lambda b, i, *_: (k, jnp.where(i < n_lat, b + b0, Ba), 0, 0))

    def row_spec(self, width, col=0, whole=False):
        b0 = self.b0 if whole else 0
        return pl.BlockSpec((1, self.TM, width), lambda b, i, *_: (b + b0, i, col))

    def col_spec(self, height, whole=False):
        b0 = self.b0 if whole else 0
        return pl.BlockSpec((1, height, self.TM), lambda b, i, *_: (b + b0, 0, i))


def _full_spec(shape):
    nd = len(shape)
    return pl.BlockSpec(shape, lambda *_: (0,) * nd)


def _layer_norm(z, g, b):
    mu = jnp.mean(z, axis=-1, keepdims=True)
    zc = z - mu
    var = jnp.mean(zc * zc, axis=-1, keepdims=True)
    return zc * lax.rsqrt(var + LN_EPS) * g + b


def _finish_mixer(y, x, gate, lng, lnb, sh, sc, wr_ref, xm_ref, f_ref, s_ref, alpha):
    xn = _layer_norm(alpha * x + gate * y, lng, lnb)
    xm_ref[0] = xn
    f = xn * (1.0 + sc) + sh
    f_ref[0] = f.astype(f_ref.dtype)
    logits = lax.dot_general(wr_ref[...], f, (((1,), (1,)), ((), ())), precision=HIGHEST,
                             preferred_element_type=F32)
    s_ref[0] = jax.nn.sigmoid(logits)


def _post_kernel(*refs, alpha, has_bias):
    if has_bias:
        (y_ref, w_ref, b_ref, x_ref, gate_ref, lng_ref, lnb_ref, sh_ref, sc_ref, wr_ref,
         xm_ref, f_ref, s_ref) = refs
    else:
        (y_ref, w_ref, x_ref, gate_ref, lng_ref, lnb_ref, sh_ref, sc_ref, wr_ref,
         xm_ref, f_ref, s_ref) = refs
    y = jnp.dot(y_ref[0].astype(BF16), w_ref[...], preferred_element_type=F32)
    if has_bias:
        y = y + b_ref[...]
    _finish_mixer(y, x_ref[0], gate_ref[...], lng_ref[...], lnb_ref[...], sh_ref[...], sc_ref[...],
                  wr_ref, xm_ref, f_ref, s_ref, alpha)


def _post_outs(geo, E):
    B, P, D = geo.B, geo.P, geo.D
    out_specs = [geo.row_spec(D), geo.row_spec(D), geo.col_spec(E)]
    out_shape = [jax.ShapeDtypeStruct((B, P, D), F32), jax.ShapeDtypeStruct((B, P, D), BF16),
                 jax.ShapeDtypeStruct((B, E, P), F32)]
    return out_specs, out_shape


def _post_mixer(geo, y, w_out, b_out, xs, modl, lng, lnb, w_router, alpha):
    D = geo.D
    Kd = y.shape[-1]
    E = w_router.shape[-1]
    has_bias = b_out is not None
    ins = [y, w_out.astype(BF16)]
    specs = [geo.row_spec(Kd), _full_spec((Kd, D))]
    if has_bias:
        ins.append(b_out.reshape(1, D))
        specs.append(_full_spec((1, D)))
    ins += [xs, modl, lng.reshape(1, D), lnb.reshape(1, D), modl, modl, w_router.T]
    specs += [geo.row_spec(D), geo.mod_spec(2), _full_spec((1, D)), _full_spec((1, D)),
              geo.mod_spec(3), geo.mod_spec(4), _full_spec((E, D))]
    out_specs, out_shape = _post_outs(geo, E)
    return pl.pallas_call(
        functools.partial(_post_kernel, alpha=alpha, has_bias=has_bias),
        grid=(geo.B, geo.nt), in_specs=specs, out_specs=out_specs, out_shape=out_shape,
        compiler_params=_cp(2), name="post_mixer",
    )(*ins)


def _dwconv_seg(r, cw, cb, S, C, lo):
    P = S + C
    row = lax.broadcasted_iota(jnp.int32, r.shape, 0)
    tl = jnp.where(row < S, row, row - S)
    sl = jnp.where(row < S, S, C)
    acc = jnp.zeros_like(r) + cb
    for k in range(cw.shape[0]):
        off = k - lo
        if off == 0:
            term = r
        else:
            shifted = pltpu.roll(r, (-off) % P, 0)
            valid = jnp.logical_and(tl + off >= 0, tl + off < sl)
            term = jnp.where(valid, shifted, 0.0)
        acc = acc + cw[k:k + 1, :] * term
    return acc


def _rg_in_kernel(x_ref, sh_ref, sc_ref, wg_ref, wr_ref, g_ref, r_ref):
    h = (x_ref[0] * (1.0 + sc_ref[...]) + sh_ref[...]).astype(BF16)
    g = jnp.dot(h, wg_ref[...], preferred_element_type=F32)
    g_ref[0] = jax.nn.gelu(g).astype(g_ref.dtype)
    r_ref[0] = jnp.dot(h, wr_ref[...], preferred_element_type=F32)


def _rg_scan_kernel(r_ref, g_ref, cw_ref, cb_ref, gw_ref, gb_ref, lam_ref, o_ref,
                    a_scr, b_scr, h_scr, *, S, C):
    P = S + C
    n = r_ref.shape[-1]
    r = r_ref[0]
    rc = _dwconv_seg(r, cw_ref[...], cb_ref[...], S, C, cw_ref.shape[0] // 2)
    rcb = rc.astype(BF16)
    row = lax.broadcasted_iota(jnp.int32, (P, n), 0)
    sub = jnp.bitwise_and(row, SUBLANES - 1)
    for d in range(2):
        rev = d == 1
        gr = jax.nn.sigmoid(jnp.dot(rcb, gw_ref[d, 0], preferred_element_type=F32) + gb_ref[d, 0])
        gi = jax.nn.sigmoid(jnp.dot(rcb, gw_ref[d, 1], preferred_element_type=F32) + gb_ref[d, 1])
        nl = -lam_ref[d]
        sp = jnp.maximum(nl, 0.0) + jnp.log1p(jnp.exp(-jnp.abs(nl)))
        a = jnp.exp(-LRU_C * gr * sp)
        bb = jnp.sqrt(1.0 - a * a) * gi * rc
        for s in (1, 2, 4):
            shift = (P - s) if rev else s
            a_sh = pltpu.roll(a, shift, 0)
            b_sh = pltpu.roll(bb, shift, 0)
            m = (sub < SUBLANES - s) if rev else (sub >= s)
            bb = jnp.where(m, a * b_sh + bb, bb)
            a = jnp.where(m, a * a_sh, a)
        a_scr[...] = a
        b_scr[...] = bb

        def chain(lo_tile, n_tiles, c0):
            def body(i, c):
                t = (lo_tile + n_tiles - 1 - i) if rev else (lo_tile + i)
                off = pl.multiple_of(t * SUBLANES, SUBLANES)
                h = b_scr[pl.ds(off, SUBLANES), :] + a_scr[pl.ds(off, SUBLANES), :] * c
                if d == 0:
                    h_scr[pl.ds(off, SUBLANES), :] = h
                else:
                    h_scr[pl.ds(off, SUBLANES), :] = h_scr[pl.ds(off, SUBLANES), :] + h
                edge = h[0:1] if rev else h[SUBLANES - 1:SUBLANES]
                return jnp.broadcast_to(edge, (SUBLANES, n))
            return lax.fori_loop(0, n_tiles, body, c0)

        c_ctx = chain(S // SUBLANES, C // SUBLANES, jnp.zeros((SUBLANES, n), F32))
        chain(0, S // SUBLANES, c_ctx)
    o_ref[0] = (g_ref[0].astype(F32) * h_scr[...]).astype(o_ref.dtype)


def _rglru_mixer(geo, xs, modl, w_in, conv_w, conv_b, gate_w, gate_b, lam):
    B, P, D, S, C = geo.B, geo.P, geo.D, geo.S, geo.C
    R = w_in.shape[1] // 2
    nb = R // RG_BLOCK
    w_in = w_in.astype(BF16)
    g, r = pl.pallas_call(
        _rg_in_kernel, grid=(B, geo.nt),
        in_specs=[geo.row_spec(D), geo.mod_spec(0), geo.mod_spec(1),
                  pl.BlockSpec((D, R), lambda b, i: (0, 0)), pl.BlockSpec((D, R), lambda b, i: (0, 1))],
        out_specs=[geo.row_spec(R), geo.row_spec(R)],
        out_shape=[jax.ShapeDtypeStruct((B, P, R), BF16), jax.ShapeDtypeStruct((B, P, R), F32)],
        compiler_params=_cp(2), name="rg_in",
    )(xs, modl, modl, w_in, w_in)
    K = conv_w.shape[0]
    seq_spec = pl.BlockSpec((1, P, RG_BLOCK), lambda b, n: (b, 0, n))
    y = pl.pallas_call(
        functools.partial(_rg_scan_kernel, S=S, C=C), grid=(B, nb),
        in_specs=[seq_spec, seq_spec,
                  pl.BlockSpec((K, RG_BLOCK), lambda b, n: (0, n)),
                  pl.BlockSpec((1, RG_BLOCK), lambda b, n: (0, n)),
                  pl.BlockSpec((2, 2, None, RG_BLOCK, RG_BLOCK), lambda b, n: (0, 0, n, 0, 0)),
                  pl.BlockSpec((2, 2, 1, RG_BLOCK), lambda b, n: (0, 0, 0, n)),
                  pl.BlockSpec((2, 1, RG_BLOCK), lambda b, n: (0, 0, n))],
        out_specs=seq_spec,
        out_shape=jax.ShapeDtypeStruct((B, P, R), BF16),
        scratch_shapes=[pltpu.VMEM((P, RG_BLOCK), F32)] * 3,
        compiler_params=_cp(2), name="rg_scan",
    )(r, g, conv_w, conv_b.reshape(1, R), gate_w.astype(BF16), gate_b.reshape(2, 2, 1, R),
      lam.reshape(2, 1, R))
    return y


def _mm_bias_kernel(x_ref, sh_ref, sc_ref, w_ref, b_ref, o_ref):
    h = (x_ref[0] * (1.0 + sc_ref[...]) + sh_ref[...]).astype(BF16)
    o_ref[0] = (jnp.dot(h, w_ref[...], preferred_element_type=F32) + b_ref[...]).astype(o_ref.dtype)


def _short_conv_kernel(u_ref, cw_ref, cb_ref, o_ref, ob_ref, *, S, C):
    y = _dwconv_seg(u_ref[0], cw_ref[...], cb_ref[...], S, C, (cw_ref.shape[0] - 1) // 2)
    o_ref[0] = y
    ob_ref[0] = y.astype(BF16)


def _dft_table_kernel(c_ref, s_ref, st_ref, *, L, TF):
    i = pl.program_id(0)
    N = 2 * L
    f = lax.broadcasted_iota(jnp.int32, (TF, L), 0) + i * TF
    t = lax.broadcasted_iota(jnp.int32, (TF, L), 1)
    ang = jnp.bitwise_and(f * t, N - 1).astype(F32) * (2.0 * math.pi / N)
    c_ref[...] = jnp.cos(ang).astype(BF16)
    nyq_t = (1 - 2 * jnp.bitwise_and(t, 1)).astype(F32)
    s_ref[...] = jnp.where(f == 0, nyq_t, jnp.sin(ang)).astype(BF16)
    nyq_f = (1 - 2 * jnp.bitwise_and(f, 1)).astype(F32)
    st_ref[...] = jnp.where(t == 0, nyq_f, jnp.sin(ang)).astype(BF16)


def _dft_tables(L):
    TF = min(L, 256)
    shp = jax.ShapeDtypeStruct((L, L), BF16)
    spec = pl.BlockSpec((TF, L), lambda i: (i, 0))
    return pl.pallas_call(
        functools.partial(_dft_table_kernel, L=L, TF=TF), grid=(L // TF,),
        in_specs=[], out_specs=[spec, spec, spec], out_shape=[shp, shp, shp],
        compiler_params=_cp(1), name="dft_tables",
    )()


def _hy_filter_kernel(z_ref, w1_ref, b1_ref, w2_ref, b2_ref, w3_ref, b3_ref, fq_ref,
                      w4f_ref, w4b_ref, df_ref, db_ref, tn_ref, kp_ref, km_ref):
    fq = fq_ref[...]

    def lin(h, w_ref, b_ref):
        return jnp.dot(h, w_ref[...], precision=HIGHEST, preferred_element_type=F32) + b_ref[...]

    h = jnp.sin(fq * lin(z_ref[...], w1_ref, b1_ref))
    h = jnp.sin(fq * lin(h, w2_ref, b2_ref))
    h = jnp.sin(fq * lin(h, w3_ref, b3_ref))
    tn = tn_ref[...]
    hf = jnp.dot(h, w4f_ref[...], precision=HIGHEST, preferred_element_type=F32)
    hf = hf * jnp.exp(-tn * jnp.abs(df_ref[...]))
    hb = jnp.dot(h, w4b_ref[...], precision=HIGHEST, preferred_element_type=F32)
    hb = hb * jnp.exp(-tn * jnp.abs(db_ref[...]))
    row = lax.broadcasted_iota(jnp.int32, hb.shape, 0)
    hb = jnp.where(row == 0, 0.0, hb)
    nrm = lax.rsqrt(jnp.sum(hf * hf, axis=0, keepdims=True) + jnp.sum(hb * hb, axis=0, keepdims=True) + 1e-6)
    hf = hf * nrm
    hb = hb * nrm
    kp_ref[...] = (hf + hb).astype(BF16)
    km_ref[...] = (hf - hb).astype(BF16)


def _hy_spectrum_kernel(c_ref, s_ref, s0_ref, kp_ref, km_ref, ka_ref, kb_ref, kc_ref, *, L, TF):
    i = pl.program_id(0)
    inv_n = 1.0 / (2 * L)
    kr = jnp.dot(c_ref[...], kp_ref[...], preferred_element_type=F32)
    ks = jnp.dot(s_ref[...], km_ref[...], preferred_element_type=F32)
    nyq = jnp.dot(s0_ref[...], kp_ref[...], preferred_element_type=F32)[0:1]
    f = lax.broadcasted_iota(jnp.int32, kr.shape, 0) + i * TF
    dc = f == 0
    ka_ref[...] = jnp.where(dc, kr * inv_n, kr * (2.0 * inv_n))
    kb_ref[...] = jnp.where(dc, 0.0, ks * (-2.0 * inv_n))
    kc_ref[...] = jnp.where(dc, nyq * inv_n, kr * (2.0 * inv_n))


def _hy_filters(L, tabs, fw1, fb1, fw2, fb2, fw3, fb3, fw4, ffreq, fdecay, D):
    cm, sm, _ = tabs
    E = fw1.shape[0]
    Hd = fw1.shape[1]
    bands = (E - 1) // 2
    t = jnp.arange(L, dtype=F32)
    t_norm = t / max(L - 1, 1)
    fr = jnp.linspace(1e-4, bands - 1, bands, dtype=F32)
    ang = (2.0 * math.pi / L) * t[:, None] * fr[None, :]
    z = jnp.concatenate([t_norm[:, None], jnp.cos(ang), -jnp.sin(ang)], -1)
    Ep, Hp = -(-E // LANES) * LANES, -(-Hd // LANES) * LANES
    z = jnp.pad(z, ((0, 0), (0, Ep - E)))
    fw1 = jnp.pad(fw1, ((0, Ep - E), (0, Hp - Hd)))
    fw2 = jnp.pad(fw2, ((0, Hp - Hd), (0, Hp - Hd)))
    fw3 = jnp.pad(fw3, ((0, Hp - Hd), (0, Hp - Hd)))
    fw4 = jnp.pad(fw4, ((0, Hp - Hd), (0, 0)))
    fb1, fb2, fb3, ffreq = (jnp.pad(v, (0, Hp - Hd)) for v in (fb1, fb2, fb3, ffreq))
    E, Hd = Ep, Hp
    CT = min(2 * D, 512)
    nct = 2 * D // CT
    dec = fdecay.reshape(1, 4 * D)
    kp, km = pl.pallas_call(
        _hy_filter_kernel, grid=(nct,),
        in_specs=[_full_spec((L, E)), _full_spec((E, Hd)), _full_spec((1, Hd)), _full_spec((Hd, Hd)),
                  _full_spec((1, Hd)), _full_spec((Hd, Hd)), _full_spec((1, Hd)), _full_spec((1, Hd)),
                  pl.BlockSpec((Hd, CT), lambda j: (0, j)), pl.BlockSpec((Hd, CT), lambda j: (0, j + nct)),
                  pl.BlockSpec((1, CT), lambda j: (0, j)), pl.BlockSpec((1, CT), lambda j: (0, j + nct)),
                  _full_spec((L, 1))],
        out_specs=[pl.BlockSpec((L, CT), lambda j: (0, j))] * 2,
        out_shape=[jax.ShapeDtypeStruct((L, 2 * D), BF16)] * 2,
        compiler_params=_cp(1), name="hy_filter",
    )(z, fw1, fb1.reshape(1, Hd), fw2, fb2.reshape(1, Hd), fw3, fb3.reshape(1, Hd), ffreq.reshape(1, Hd),
      fw4, fw4, dec, dec, t_norm[:, None])
    TF = min(L, 256)
    spec_w = pl.BlockSpec((TF, L), lambda i, j: (i, 0))
    spec_k = pl.BlockSpec((L, CT), lambda i, j: (0, j))
    spec_o = pl.BlockSpec((TF, CT), lambda i, j: (i, j))
    shp = jax.ShapeDtypeStruct((L, 2 * D), F32)
    return pl.pallas_call(
        functools.partial(_hy_spectrum_kernel, L=L, TF=TF), grid=(L // TF, nct),
        in_specs=[spec_w, spec_w, pl.BlockSpec((SUBLANES, L), lambda i, j: (0, 0)), spec_k, spec_k],
        out_specs=[spec_o] * 3, out_shape=[shp] * 3,
        compiler_params=_cp(2), name="hy_spectrum",
    )(cm, sm, sm, kp, km)


def _hy_fwd_kernel(z_ref, c_ref, s_ref, ka_ref, kb_ref, kc_ref, p_ref):
    z = z_ref[0]
    zr = jnp.dot(c_ref[...], z, preferred_element_type=F32)
    zs = jnp.dot(s_ref[...], z, preferred_element_type=F32)
    kb = kb_ref[...]
    p_ref[0, 0] = (zr * ka_ref[...] + zs * kb).astype(BF16)
    p_ref[0, 1] = (zs * kc_ref[...] - zr * kb).astype(BF16)


def _hy_inv_kernel(p_ref, c_ref, st_ref, z_ref, x_ref, fb_ref, o_ref):
    y = jnp.dot(c_ref[...], p_ref[0, 0], preferred_element_type=F32)
    y = y + jnp.dot(st_ref[...], p_ref[0, 1], preferred_element_type=F32)
    o_ref[0] = (x_ref[0] * (y + z_ref[0].astype(F32) * fb_ref[...])).astype(o_ref.dtype)


def _hy_inv_kernel_alias(p_ref, c_ref, st_ref, z_ref, x_ref, fb_ref, prev_ref, o_ref):
    del prev_ref
    _hy_inv_kernel(p_ref, c_ref, st_ref, z_ref, x_ref, fb_ref, o_ref)


def _hy_conv(geo, L, off, tabs, z, z_col, kfilt, k_col, xmul, x_col, fbias, out):
    B, P, D = geo.B, geo.P, geo.D
    cm, sm, smt = tabs
    ka, kb, kc = kfilt
    rb = off // L
    TF = min(L, 256)
    spec_w = pl.BlockSpec((TF, L), lambda b, i: (i, 0))
    spec_k = pl.BlockSpec((TF, D), lambda b, i: (i, k_col))
    p = pl.pallas_call(
        _hy_fwd_kernel, grid=(B, L // TF),
        in_specs=[pl.BlockSpec((1, L, D), lambda b, i: (b, rb, z_col)), spec_w, spec_w,
                  spec_k, spec_k, spec_k],
        out_specs=pl.BlockSpec((1, 2, TF, D), lambda b, i: (b, 0, i, 0)),
        out_shape=jax.ShapeDtypeStruct((B, 2, L, D), BF16),
        compiler_params=_cp(2), name="hy_fwd",
    )(z, cm, sm, ka, kb, kc)
    CT = min(D, 512)
    nct = D // CT
    rt = off // TF
    spec_wi = pl.BlockSpec((TF, L), lambda b, j, i: (i, 0))
    return pl.pallas_call(
        _hy_inv_kernel_alias, grid=(B, nct, L // TF),
        in_specs=[pl.BlockSpec((1, 2, L, CT), lambda b, j, i: (b, 0, 0, j)), spec_wi, spec_wi,
                  pl.BlockSpec((1, TF, CT), lambda b, j, i: (b, rt + i, z_col * nct + j)),
                  pl.BlockSpec((1, TF, CT), lambda b, j, i: (b, rt + i, x_col * nct + j)),
                  pl.BlockSpec((1, CT), lambda b, j, i: (0, k_col * nct + j)),
                  pl.BlockSpec(memory_space=pl.ANY)],
        out_specs=pl.BlockSpec((1, TF, CT), lambda b, j, i: (b, rt + i, j)),
        out_shape=jax.ShapeDtypeStruct((B, P, D), BF16),
        input_output_aliases={6: 0},
        compiler_params=_cp(3), name="hy_inv",
    )(p, cm, smt, z, xmul, fbias, out)


def _hyena_mixer(geo, xs, modl, w_in, b_in, short_w, short_b, fw1, fb1, fw2, fb2, fw3, fb3, fw4,
                 ffreq, fdecay, fbias):
    B, P, D, S, C = geo.B, geo.P, geo.D, geo.S, geo.C
    u0 = pl.pallas_call(
        _mm_bias_kernel, grid=(B, geo.nt, 3),
        in_specs=[geo.row_spec(D), geo.mod_spec(0), geo.mod_spec(1),
                  pl.BlockSpec((D, D), lambda b, i, j: (0, j)), pl.BlockSpec((1, D), lambda b, i, j: (0, j))],
        out_specs=pl.BlockSpec((1, geo.TM, D), lambda b, i, j: (b, i, j)),
        out_shape=jax.ShapeDtypeStruct((B, P, 3 * D), F32),
        compiler_params=_cp(3), name="hy_in",
    )(xs, modl, modl, w_in.astype(BF16), b_in.reshape(1, 3 * D))
    CT = min(D, 256)
    Ks = short_w.shape[0]
    spec = pl.BlockSpec((1, P, CT), lambda b, j: (b, 0, j))
    u, ub = pl.pallas_call(
        functools.partial(_short_conv_kernel, S=S, C=C), grid=(B, 3 * D // CT),
        in_specs=[spec, pl.BlockSpec((Ks, CT), lambda b, j: (0, j)), pl.BlockSpec((1, CT), lambda b, j: (0, j))],
        out_specs=[spec, spec],
        out_shape=[jax.ShapeDtypeStruct((B, P, 3 * D), F32), jax.ShapeDtypeStruct((B, P, 3 * D), BF16)],
        compiler_params=_cp(2), name="hy_short",
    )(u0, short_w, short_b.reshape(1, 3 * D))
    fb = fbias.reshape(1, 2 * D)
    z1 = jnp.zeros((B, P, D), BF16)
    z2 = jnp.zeros((B, P, D), BF16)
    segs = [(S, 0), (C, S)]
    convs = []
    for L, off in segs:
        tabs = _dft_tables(L)
        kf = _hy_filters(L, tabs, fw1, fb1, fw2, fb2, fw3, fb3, fw4, ffreq, fdecay, D)
        convs.append((L, off, tabs, kf))
    for L, off, tabs, kf in convs:
        z1 = _hy_conv(geo, L, off, tabs, ub, 0, kf, 0, u, 1, fb, z1)
    for L, off, tabs, kf in convs:
        z2 = _hy_conv(geo, L, off, tabs, z1, 0, kf, 1, u, 2, fb, z2)
    return z2


def _rope_tables(S, D):
    t = jnp.arange(S)
    row = (t // GRID_W).astype(F32)
    col = (t % GRID_W).astype(F32)
    axis_dim = DA_HEAD_DIM // 2
    half = axis_dim // 2
    inv = ROPE_THETA ** (-jnp.arange(0, axis_dim, 2, dtype=F32) / axis_dim)
    lane = jnp.arange(D)
    within = lane % DA_HEAD_DIM
    pos = jnp.where((within // axis_dim)[None, :] == 0, row[:, None], col[:, None])
    ang = pos * inv[lane % half][None, :]
    sign = jnp.where((lane % axis_dim) < half, -1.0, 1.0)[None, :]
    return jnp.cos(ang), jnp.sin(ang) * sign


def _da_in_kernel(x_ref, sh_ref, sc_ref, w_ref, cos_ref, sin_ref, o_ref, *, n_lat):
    i = pl.program_id(1)
    j = pl.program_id(2)
    h = (x_ref[0] * (1.0 + sc_ref[...]) + sh_ref[...]).astype(BF16)
    acc = jnp.dot(h, w_ref[...], preferred_element_type=F32)
    acc = acc * jnp.where(j == 0, DA_HEAD_DIM ** -0.5, 1.0)
    rot = jnp.logical_and(i < n_lat, j < 2)

    @pl.when(rot)
    def _():
        Dn = acc.shape[-1]
        half = DA_HEAD_DIM // 4
        lane = lax.broadcasted_iota(jnp.int32, acc.shape, 1)
        up = pltpu.roll(acc, Dn - half, 1)
        dn = pltpu.roll(acc, half, 1)
        partner = jnp.where(jnp.bitwise_and(lane, 2 * half - 1) < half, up, dn)
        o_ref[0] = (acc * cos_ref[...] + partner * sin_ref[...]).astype(o_ref.dtype)

    @pl.when(jnp.logical_not(rot))
    def _():
        o_ref[0] = acc.astype(o_ref.dtype)


def _da_attn_kernel(*refs, kv_lo, nk, lam_init, aliased):
    if aliased:
        q_ref, k_ref, v_ref, lam_ref, sub_ref, _, o_ref, vx_scr, s_scr = refs
    else:
        q_ref, k_ref, v_ref, lam_ref, sub_ref, o_ref, vx_scr, s_scr = refs
    i = pl.program_id(2)
    HW = v_ref.shape[-1]
    TQ = q_ref.shape[1]
    lp = lam_ref[...]
    lam = (jnp.exp(jnp.sum(lp[0:1] * lp[1:2], axis=1, keepdims=True))
           - jnp.exp(jnp.sum(lp[2:3] * lp[3:4], axis=1, keepdims=True)) + lam_init)

    @pl.when(i == 0)
    def _():
        vx_scr[:, :HW] = v_ref[0]
        vx_scr[:, HW:] = jnp.ones((vx_scr.shape[0], HW), BF16)

    lane = lax.broadcasted_iota(jnp.int32, (TQ, HW), 1)
    q = q_ref[0]
    k = k_ref[0, kv_lo:kv_lo + nk, :]
    outs = []
    for c in range(2):
        qc = jnp.where((lane // DA_HEAD_DIM) == c, q, jnp.zeros_like(q))
        s_scr[...] = lax.dot_general(qc, k, (((1,), (1,)), ((), ())), preferred_element_type=F32)
        m = jnp.max(s_scr[...], axis=-1, keepdims=True)
        p = jnp.exp((s_scr[...] - m).astype(BF16))
        ov = jnp.dot(p, vx_scr[kv_lo:kv_lo + nk, :], preferred_element_type=F32)
        outs.append(ov[:, :HW] / ov[:, HW:HW + 1])
    o = outs[0] - lam * outs[1]
    o = o * lax.rsqrt(jnp.mean(o * o, axis=-1, keepdims=True) + 1e-5) * sub_ref[...] * (1.0 - lam_init)
    o_ref[0] = o.astype(o_ref.dtype)


def _diff_attention_mixer(geo, xs, modl, w_in, lam_p, subln_w, layer_idx):
    B, P, D, S, C = geo.B, geo.P, geo.D, geo.S, geo.C
    H = D // (2 * DA_HEAD_DIM)
    HW = 2 * DA_HEAD_DIM
    cos_t, sin_t = _rope_tables(S, D)
    n_lat = geo.n_lat
    tab_spec = pl.BlockSpec((geo.TM, D), lambda b, i, j: (jnp.minimum(i, n_lat - 1), 0))
    qkv = pl.pallas_call(
        functools.partial(_da_in_kernel, n_lat=n_lat), grid=(B, geo.nt, 3),
        in_specs=[geo.row_spec(D), geo.mod_spec(0), geo.mod_spec(1),
                  pl.BlockSpec((D, D), lambda b, i, j: (0, j)), tab_spec, tab_spec],
        out_specs=pl.BlockSpec((1, geo.TM, D), lambda b, i, j: (b, i, j)),
        out_shape=jax.ShapeDtypeStruct((B, P, 3 * D), BF16),
        compiler_params=_cp(3), name="da_in",
    )(xs, modl, modl, w_in.astype(BF16), cos_t, sin_t)
    lam_init = 0.8 - 0.6 * math.exp(-0.3 * layer_idx)
    def attend(TQ, row0, n_tiles, kv_lo, nk, prev):
        rb = row0 // TQ
        ins = [qkv, qkv, qkv, lam_p, subln_w.reshape(1, HW)]
        specs = [pl.BlockSpec((1, TQ, HW), lambda b, h, i: (b, rb + i, h)),
                 pl.BlockSpec((1, P, HW), lambda b, h, i: (b, 0, H + h)),
                 pl.BlockSpec((1, P, HW), lambda b, h, i: (b, 0, 2 * H + h)),
                 _full_spec((4, DA_HEAD_DIM)), _full_spec((1, HW))]
        if prev is not None:
            ins.append(prev)
            specs.append(pl.BlockSpec(memory_space=pl.ANY))
        return pl.pallas_call(
            functools.partial(_da_attn_kernel, kv_lo=kv_lo, nk=nk, lam_init=lam_init, aliased=prev is not None),
            grid=(B, H, n_tiles), in_specs=specs,
            out_specs=pl.BlockSpec((1, TQ, HW), lambda b, h, i: (b, rb + i, h)),
            out_shape=jax.ShapeDtypeStruct((B, P, D), BF16),
            scratch_shapes=[pltpu.VMEM((P, 2 * HW), BF16), pltpu.VMEM((TQ, nk), F32)],
            input_output_aliases={} if prev is None else {5: 0},
            compiler_params=_cp(3), name="da_attn",
        )(*ins)

    TQ = geo.TM
    while TQ < DA_Q_TILE and S % (2 * TQ) == 0:
        TQ *= 2
    y = attend(TQ, 0, S // TQ, 0, P, None)
    return attend(geo.TM, S, C // geo.TM, S, C, y)


def _s5_operators(a_re, a_im, log_step, b_re, b_im, c_re, c_im):
    T = S5_CHUNK
    G, Pst = a_re.shape[1], a_re.shape[2]
    Hg = S5_GROUP
    GL = LANES // Hg
    LB = G // GL
    lam = lax.complex(jnp.minimum(a_re.astype(F32), -1e-4), a_im.astype(F32))
    step = jnp.exp(log_step.astype(F32))[..., None]
    abar = jnp.exp(lam * step)
    bbar = ((abar - 1.0) / lam)[..., None] * lax.complex(b_re.astype(F32), b_im.astype(F32))
    cmat = lax.complex(c_re.astype(F32), c_im.astype(F32))
    pows = jnp.stack([abar ** l for l in range(T + 1)], axis=1)
    eye = jnp.eye(GL, dtype=F32)
    ar = jnp.arange(T)
    big_m, big_g, big_h, a_t = [], [], [], []
    for d in range(2):
        pw = pows[d]
        kl = jnp.einsum('gjp,lgp,gph->lgjh', cmat[d], pw[:T], bbar[d]).real
        lag = (ar[None, :] - ar[:, None]) if d == 0 else (ar[:, None] - ar[None, :])
        tz = jnp.where((lag >= 0)[:, :, None, None, None], kl[jnp.clip(lag, 0, T - 1)], 0.0)
        tz = tz.reshape(T, T, LB, GL, Hg, Hg).astype(BF16)
        m = jnp.einsum('stbgjh,gk->bsghtkj', tz, eye.astype(BF16)).reshape(LB, T * LANES, T * LANES)
        e_in = (T - 1 - ar) if d == 0 else ar
        gc = pw[e_in][:, :, :, None] * bbar[d][None]
        gc = gc.reshape(T, LB, GL, Pst, Hg)
        g_re = jnp.einsum('sbgph,gk->bsghkp', gc.real, eye).reshape(LB, T * LANES, GL * Pst)
        g_im = jnp.einsum('sbgph,gk->bsghkp', gc.imag, eye).reshape(LB, T * LANES, GL * Pst)
        e_out = (ar + 1) if d == 0 else (T - ar)
        hc = cmat[d][None] * pw[e_out][:, :, None, :]
        hc = hc.reshape(T, LB, GL, Hg, Pst)
        h_re = jnp.einsum('tbgjp,gk->bgptkj', hc.real, eye).reshape(LB, GL * Pst, T * LANES)
        h_im = jnp.einsum('tbgjp,gk->bgptkj', -hc.imag, eye).reshape(LB, GL * Pst, T * LANES)
        big_m.append(m)
        big_g.append(jnp.concatenate([g_re, g_im], axis=2))
        big_h.append(jnp.concatenate([h_re, h_im], axis=1))
        at = pw[T].reshape(LB, 1, GL * Pst)
        a_t.append(jnp.concatenate([at.real, at.imag], axis=2))
    return (jnp.stack(big_m).astype(BF16), jnp.stack(big_g).astype(BF16),
            jnp.stack(big_h).astype(BF16), jnp.stack(a_t).astype(F32))


def _modulate_kernel(x_ref, sh_ref, sc_ref, o_ref):
    o_ref[0] = x_ref[0] * (1.0 + sc_ref[...]) + sh_ref[...]


def _s5_kernel(u_ref, m_ref, g_ref, h_ref, a_ref, o_ref, gx_scr, sp_scr, *, S, C):
    d = pl.program_id(1)
    T = S5_CHUNK
    P = S + C
    n = P // T
    n_lat = S // T
    x = jnp.concatenate([u_ref[0, pl.ds(s, n, stride=T), :] for s in range(T)], axis=1).astype(BF16)
    gx_scr[...] = jnp.dot(x, g_ref[...], preferred_element_type=F32)
    ns = a_ref.shape[-1] // 2
    a_r = a_ref[:, :ns]
    a_i = a_ref[:, ns:]

    def scan(lo, cnt, rev, carry):
        def body(k, st):
            s_r, s_i = st
            c = (lo + cnt - 1 - k) if rev else (lo + k)
            sp_scr[pl.ds(c, 1), :] = jnp.concatenate([s_r, s_i], axis=1)
            gx = gx_scr[pl.ds(c, 1), :]
            return (a_r * s_r - a_i * s_i + gx[:, :ns], a_r * s_i + a_i * s_r + gx[:, ns:])
        return lax.fori_loop(0, cnt, body, carry)

    zero = (jnp.zeros((1, ns), F32), jnp.zeros((1, ns), F32))

    @pl.when(d == 0)
    def _():
        scan(0, n_lat, False, scan(n_lat, n - n_lat, False, zero))

    @pl.when(d == 1)
    def _():
        scan(0, n_lat, True, scan(n_lat, n - n_lat, True, zero))

    y = jnp.dot(x, m_ref[...], preferred_element_type=F32)
    y = y + jnp.dot(sp_scr[...].astype(BF16), h_ref[...], preferred_element_type=F32)
    for s in range(T):
        o_ref[0, pl.ds(s, n, stride=T), :] = y[:, s * LANES:(s + 1) * LANES]


def _s5_mixer(geo, xs, modl, a_re, a_im, log_step, b_re, b_im, c_re, c_im):
    B, P, D, S, C = geo.B, geo.P, geo.D, geo.S, geo.C
    u = pl.pallas_call(
        _modulate_kernel, grid=(B, geo.nt),
        in_specs=[geo.row_spec(D), geo.mod_spec(0), geo.mod_spec(1)],
        out_specs=geo.row_spec(D), out_shape=jax.ShapeDtypeStruct((B, P, D), F32),
        compiler_params=_cp(2), name="s5_modulate",
    )(xs, modl, modl)
    big_m, big_g, big_h, a_t = _s5_operators(a_re, a_im, log_step, b_re, b_im, c_re, c_im)
    LB = D // LANES
    TL = S5_CHUNK * LANES
    NS = big_g.shape[-1]
    n = P // S5_CHUNK
    y = pl.pallas_call(
        functools.partial(_s5_kernel, S=S, C=C), grid=(LB, 2, B),
        in_specs=[pl.BlockSpec((1, P, LANES), lambda l, d, b: (b, 0, l)),
                  pl.BlockSpec((None, None, TL, TL), lambda l, d, b: (d, l, 0, 0)),
                  pl.BlockSpec((None, None, TL, NS), lambda l, d, b: (d, l, 0, 0)),
                  pl.BlockSpec((None, None, NS, TL), lambda l, d, b: (d, l, 0, 0)),
                  pl.BlockSpec((None, None, 1, NS), lambda l, d, b: (d, l, 0, 0))],
        out_specs=pl.BlockSpec((None, 1, P, LANES), lambda l, d, b: (d, b, 0, l)),
        out_shape=jax.ShapeDtypeStruct((2, B, P, D), F32),
        scratch_shapes=[pltpu.VMEM((n, NS), F32), pltpu.VMEM((n, NS), F32)],
        compiler_params=_cp(3), name="s5_scan",
    )(u, big_m, big_g, big_h, a_t)
    return u, y


def _s5_post_kernel(yf_ref, yb_ref, u_ref, d_ref, w_ref, b_ref, x_ref, gate_ref, lng_ref, lnb_ref,
                    sh_ref, sc_ref, wr_ref, xm_ref, f_ref, s_ref, *, alpha):
    g = jax.nn.gelu(yf_ref[0] + yb_ref[0] + d_ref[...] * u_ref[0])
    vg = jnp.dot(g.astype(BF16), w_ref[...], preferred_element_type=F32) + b_ref[...]
    Dn = vg.shape[-1] // 2
    y = vg[:, :Dn] * jax.nn.sigmoid(vg[:, Dn:])
    _finish_mixer(y, x_ref[0], gate_ref[...], lng_ref[...], lnb_ref[...], sh_ref[...], sc_ref[...],
                  wr_ref, xm_ref, f_ref, s_ref, alpha)


def _s5_post(geo, y2, u, d_skip, w_glu, b_glu, xs, modl, lng, lnb, w_router, alpha):
    D = geo.D
    E = w_router.shape[-1]
    out_specs, out_shape = _post_outs(geo, E)
    TM = geo.TM
    return pl.pallas_call(
        functools.partial(_s5_post_kernel, alpha=alpha), grid=(geo.B, geo.nt),
        in_specs=[pl.BlockSpec((None, 1, TM, D), lambda b, i: (0, b, i, 0)),
                  pl.BlockSpec((None, 1, TM, D), lambda b, i: (1, b, i, 0)),
                  geo.row_spec(D), _full_spec((1, D)), _full_spec((D, 2 * D)), _full_spec((1, 2 * D)),
                  geo.row_spec(D), geo.mod_spec(2), _full_spec((1, D)), _full_spec((1, D)),
                  geo.mod_spec(3), geo.mod_spec(4), _full_spec((E, D))],
        out_specs=out_specs, out_shape=out_shape,
        compiler_params=_cp(2), name="s5_post",
    )(y2, y2, u, d_skip.reshape(1, D), w_glu.astype(BF16), b_glu.reshape(1, 2 * D), xs, modl,
      lng.reshape(1, D), lnb.reshape(1, D), modl, modl, w_router.T)


def _route_kernel(s_ref, b_ref, idx_ref, w_ref):
    sc = s_ref[0]
    E, TM = sc.shape
    biased = sc + b_ref[...]
    G = N_EXPERT_GROUPS
    per = E // G
    neg = -jnp.inf
    blocks, gs = [], []
    for g in range(G):
        blk = biased[g * per:(g + 1) * per]
        m1 = jnp.max(blk, axis=0, keepdims=True)
        is1 = blk == m1
        cnt = jnp.sum(is1.astype(F32), axis=0, keepdims=True)
        m2 = jnp.max(jnp.where(is1, neg, blk), axis=0, keepdims=True)
        blocks.append(blk)
        gs.append(m1 + jnp.where(cnt >= 2.0, m1, m2))
    masked = []
    for g in range(G):
        ahead = jnp.zeros((1, TM), F32)
        for h in range(G):
            if h < g:
                ahead = ahead + (gs[h] >= gs[g]).astype(F32)
            elif h > g:
                ahead = ahead + (gs[h] > gs[g]).astype(F32)
        masked.append(jnp.where(ahead < float(TOPK_GROUPS), blocks[g], neg))
    masked = jnp.concatenate(masked, axis=0)
    iota_e = lax.broadcasted_iota(jnp.int32, (E, TM), 0)
    idxs, ws = [], []
    for _ in range(TOP_K):
        m = jnp.max(masked, axis=0, keepdims=True)
        ik = jnp.min(jnp.where(masked == m, iota_e, E), axis=0, keepdims=True)
        sel = iota_e == ik
        ws.append(jnp.sum(jnp.where(sel, sc, 0.0), axis=0, keepdims=True))
        idxs.append(ik)
        masked = jnp.where(sel, neg, masked)
    tot = ws[0]
    for wk in ws[1:]:
        tot = tot + wk
    w = jnp.concatenate(ws, axis=0)
    idx_ref[0] = jnp.concatenate(idxs, axis=0)
    w_ref[0] = w / (tot + 1e-20) * ROUTED_SCALE


def _rank_kernel(idx_ref, rank_ref, cnt_ref, run_scr, *, E):
    first = jnp.logical_and(pl.program_id(0) == 0, pl.program_id(1) == 0)

    @pl.when(first)
    def _():
        run_scr[...] = jnp.zeros_like(run_scr)

    idx = idx_ref[0]
    K, TM = idx.shape
    iota_e = lax.broadcasted_iota(jnp.int32, (E, TM), 0)
    member = jnp.zeros((E, TM), F32)
    for k in range(K):
        member = member + (iota_e == idx[k:k + 1]).astype(F32)
    before = (lax.broadcasted_iota(jnp.int32, (TM, TM), 0)
              < lax.broadcasted_iota(jnp.int32, (TM, TM), 1)).astype(BF16)
    rank = jnp.dot(member.astype(BF16), before, preferred_element_type=F32) + run_scr[...]
    rows = [jnp.sum(jnp.where(iota_e == idx[k:k + 1], rank, 0.0), axis=0, keepdims=True) for k in range(K)]
    rank_ref[0] = jnp.concatenate(rows, axis=0).astype(jnp.int32)
    run_scr[...] = run_scr[...] + jnp.sum(member, axis=1, keepdims=True)
    cnt_ref[...] = run_scr[...]


def _dest_kernel(idx_ref, rank_ref, start_ref, dest_ref, *, E):
    idx = idx_ref[0]
    K, TM = idx.shape
    iota_e = lax.broadcasted_iota(jnp.int32, (E, TM), 0)
    start = start_ref[...]
    rows = [jnp.sum(jnp.where(iota_e == idx[k:k + 1], start, 0), axis=0, keepdims=True) for k in range(K)]
    dest_ref[0] = jnp.concatenate(rows, axis=0) + rank_ref[0]


def _route_dispatch(geo, scores, bias, blk):
    B, P = geo.B, geo.P
    E = scores.shape[1]
    K = TOP_K
    kspec = geo.col_spec(K)
    idx, w = pl.pallas_call(
        _route_kernel, grid=(B, geo.nt),
        in_specs=[geo.col_spec(E, whole=True), _full_spec((E, 1))],
        out_specs=[kspec, kspec],
        out_shape=[jax.ShapeDtypeStruct((B, K, P), jnp.int32), jax.ShapeDtypeStruct((B, K, P), F32)],
        compiler_params=_cp(2), name="moe_route",
    )(scores, bias.astype(F32).reshape(E, 1))
    rank, cnt = pl.pallas_call(
        functools.partial(_rank_kernel, E=E), grid=(B, geo.nt),
        in_specs=[kspec], out_specs=[kspec, _full_spec((E, 1))],
        out_shape=[jax.ShapeDtypeStruct((B, K, P), jnp.int32), jax.ShapeDtypeStruct((E, 1), F32)],
        scratch_shapes=[pltpu.VMEM((E, 1), F32)],
        compiler_params=_cp(2), name="moe_rank",
    )(idx)
    n_assign = B * P * K
    n_blocks = -(-(n_assign + E * (blk - 1)) // blk)
    counts = cnt[:, 0].astype(jnp.int32)
    pcounts = (counts + blk - 1) // blk * blk
    pends = jnp.cumsum(pcounts)
    starts = (pends - pcounts).astype(jnp.int32)
    n_used = pends[-1] // blk
    blk_e = jnp.minimum(jnp.searchsorted(pends, jnp.arange(n_blocks) * blk, side='right'), E - 1)
    blk_e = jnp.where(jnp.arange(n_blocks) < n_used, blk_e, blk_e[jnp.maximum(n_used - 1, 0)])
    dest = pl.pallas_call(
        functools.partial(_dest_kernel, E=E), grid=(B, geo.nt),
        in_specs=[kspec, kspec, _full_spec((E, 1))], out_specs=kspec,
        out_shape=jax.ShapeDtypeStruct((B, K, P), jnp.int32),
        compiler_params=_cp(2), name="moe_dest",
    )(idx, rank, starts.reshape(E, 1))
    return w, dest, blk_e.astype(jnp.int32), n_used.astype(jnp.int32).reshape(1), n_blocks


def _expert_kernel(be_ref, nu_ref, x_ref, wgu_ref, wd_ref, o_ref):
    del be_ref
    i = pl.program_id(0)

    @pl.when(i < nu_ref[0])
    def _():
        h = jnp.dot(x_ref[...], wgu_ref[...].astype(BF16), preferred_element_type=F32)
        Fh = h.shape[-1] // 2
        a = _silu(h[:, :Fh]) * h[:, Fh:]
        o_ref[...] = jnp.dot(a.astype(BF16), wd_ref[...].astype(BF16),
                             preferred_element_type=F32).astype(o_ref.dtype)

    @pl.when(i >= nu_ref[0])
    def _():
        o_ref[...] = jnp.zeros_like(o_ref)


def _expert_ffn(x_sorted, blk_e, n_used, w_gu, w_down, layer, blk):
    n_rows, D = x_sorted.shape
    F2 = w_gu.shape[-1]
    grid_spec = pltpu.PrefetchScalarGridSpec(
        num_scalar_prefetch=2, grid=(n_rows // blk,),
        in_specs=[pl.BlockSpec((blk, D), lambda i, be, nu: (i, 0)),
                  pl.BlockSpec((None, None, D, F2), lambda i, be, nu: (layer, be[i], 0, 0)),
                  pl.BlockSpec((None, None, F2 // 2, D), lambda i, be, nu: (layer, be[i], 0, 0))],
        out_specs=pl.BlockSpec((blk, D), lambda i, be, nu: (i, 0)))
    return pl.pallas_call(
        _expert_kernel, grid_spec=grid_spec,
        out_shape=jax.ShapeDtypeStruct((n_rows, D), BF16),
        compiler_params=_cp(1), name="moe_experts",
    )(blk_e, n_used, x_sorted, w_gu, w_down)


def _moe_final_kernel(*refs, alpha, aliased):
    if aliased:
        xm_ref, f_ref, ga_ref, w_ref, shgu_ref, shd_ref, gate_ref, lng_ref, lnb_ref, _, o_ref = refs
    else:
        xm_ref, f_ref, ga_ref, w_ref, shgu_ref, shd_ref, gate_ref, lng_ref, lnb_ref, o_ref = refs
    h = jnp.dot(f_ref[0], shgu_ref[...], preferred_element_type=F32)
    Fh = h.shape[-1] // 2
    a = _silu(h[:, :Fh]) * h[:, Fh:]
    y = jnp.dot(a.astype(BF16), shd_ref[...], preferred_element_type=F32)
    w = w_ref[0]
    for k in range(w.shape[-1]):
        y = y + w[:, k:k + 1] * ga_ref[k, 0].astype(F32)
    o_ref[0] = _layer_norm(alpha * xm_ref[0] + gate_ref[...] * y, lng_ref[...], lnb_ref[...])


def _moe_group(geo, xm, f, scores, modl, bias, w_gu, w_down, sh_gu, sh_down, lng, lnb, layer, alpha, blk, prev):
    B, P, D = geo.B, geo.P, geo.D
    T = B * P
    K = TOP_K
    w, dest, blk_e, n_used, n_blocks = _route_dispatch(geo, scores, bias, blk)
    dest_flat = jnp.swapaxes(dest, 0, 1).reshape(K * T)
    t0 = geo.b0 * P
    tok = jnp.broadcast_to(jnp.arange(t0, t0 + T, dtype=jnp.int32)[None], (K, T)).reshape(K * T)
    hit = jnp.zeros((n_blocks * blk,), jnp.int32).at[dest_flat].add(
        tok + 1, unique_indices=True, mode='promise_in_bounds')
    row_tok = jnp.where(hit > 0, hit - 1, t0 + jnp.arange(n_blocks * blk, dtype=jnp.int32) % T)
    x_sorted = f.reshape(geo.B_all * P, D).at[row_tok].get(mode='promise_in_bounds')
    y_sorted = _expert_ffn(x_sorted, blk_e, n_used, w_gu, w_down, layer, blk)
    gathered = y_sorted.at[dest_flat].get(unique_indices=True, mode='promise_in_bounds').reshape(K, B, P, D)
    F2 = sh_gu.shape[-1]
    TM = geo.TM
    ins = [xm, f, gathered, jnp.swapaxes(w, 1, 2), sh_gu.astype(BF16), sh_down.astype(BF16), modl,
           lng.reshape(1, D), lnb.reshape(1, D)]
    specs = [geo.row_spec(D, whole=True), geo.row_spec(D, whole=True),
             pl.BlockSpec((K, 1, TM, D), lambda b, i: (0, b, i, 0)), geo.row_spec(K),
             _full_spec((D, F2)), _full_spec((F2 // 2, D)), geo.mod_spec(5),
             _full_spec((1, D)), _full_spec((1, D))]
    if prev is not None:
        ins.append(prev)
        specs.append(pl.BlockSpec(memory_space=pl.ANY))
    return pl.pallas_call(
        functools.partial(_moe_final_kernel, alpha=alpha, aliased=prev is not None), grid=(B, geo.nt),
        in_specs=specs, out_specs=geo.row_spec(D, whole=True),
        out_shape=jax.ShapeDtypeStruct((geo.B_all, P, D), F32),
        input_output_aliases={} if prev is None else {9: 0},
        compiler_params=_cp(2), name="moe_final",
    )(*ins)


def _moe(geo, xm, f, scores, modl, bias, w_gu, w_down, sh_gu, sh_down, lng, lnb, layer, alpha, blk):
    n_groups = MOE_GROUPS if geo.B % MOE_GROUPS == 0 else 1
    nb = geo.B // n_groups
    out = None
    for g in range(n_groups):
        out = _moe_group(geo.group(g * nb, nb), xm, f, scores, modl, bias, w_gu, w_down, sh_gu, sh_down,
                         lng, lnb, layer, alpha, blk, out)
    return out


def kernel(x, c, ctx, c_ctx, mod_w, mod_b, ln_g, ln_b, rg_w_in, rg_conv_w, rg_conv_b, rg_gate_w, rg_gate_b, rg_lam, rg_w_out, hy_w_in, hy_b_in, hy_short_w, hy_short_b, hy_f_w1, hy_f_b1, hy_f_w2, hy_f_b2, hy_f_w3, hy_f_b3, hy_f_w4, hy_f_freq, hy_f_decay, hy_f_bias, hy_w_out, hy_b_out, da_w_in, da_lam, da_subln, da_w_out, s5_a_re, s5_a_im, s5_log_step, s5_b_re, s5_b_im, s5_c_re, s5_c_im, s5_d, s5_w_glu, s5_b_glu, moe_w_router, moe_bias, moe_w_gu, moe_w_down, moe_sh_gu, moe_sh_down):
    B, S, D = x.shape
    C = ctx.shape[1]
    depth = mod_w.shape[0]
    alpha = (2 * depth) ** 0.25
    geo = _Geo(B, S, C, D)
    xs = jnp.concatenate([x, ctx], axis=1)
    R = -(-(B + 1) // SUBLANES) * SUBLANES
    cc = jnp.zeros((R, D), F32).at[:B].set(c).at[B].set(c_ctx)
    modt = _mod_table(cc, mod_w, mod_b).reshape(depth, 6, R, 1, D)
    blk = MOE_BLOCK
    for i in range(depth):
        kind, j = i % N_MIXERS, i // N_MIXERS
        modl = modt[i]
        post = functools.partial(_post_mixer, geo, xs=xs, modl=modl, lng=ln_g[i, 0], lnb=ln_b[i, 0],
                                 w_router=moe_w_router[i], alpha=alpha)
        if kind == 0:
            y = _rglru_mixer(geo, xs, modl, rg_w_in[j], rg_conv_w[j], rg_conv_b[j], rg_gate_w[j],
                             rg_gate_b[j], rg_lam[j])
            xm, f, scores = post(y=y, w_out=rg_w_out[j], b_out=None)
        elif kind == 1:
            y = _hyena_mixer(geo, xs, modl, hy_w_in[j], hy_b_in[j], hy_short_w[j], hy_short_b[j],
                             hy_f_w1[j], hy_f_b1[j], hy_f_w2[j], hy_f_b2[j], hy_f_w3[j], hy_f_b3[j],
                             hy_f_w4[j], hy_f_freq[j], hy_f_decay[j], hy_f_bias[j])
            xm, f, scores = post(y=y, w_out=hy_w_out[j], b_out=hy_b_out[j])
        elif kind == 2:
            y = _diff_attention_mixer(geo, xs, modl, da_w_in[j], da_lam[j], da_subln[j], i)
            xm, f, scores = post(y=y, w_out=da_w_out[j], b_out=None)
        else:
            u, y2 = _s5_mixer(geo, xs, modl, s5_a_re[j], s5_a_im[j], s5_log_step[j], s5_b_re[j],
                              s5_b_im[j], s5_c_re[j], s5_c_im[j])
            xm, f, scores = _s5_post(geo, y2, u, s5_d[j], s5_w_glu[j], s5_b_glu[j], xs, modl,
                                     ln_g[i, 0], ln_b[i, 0], moe_w_router[i], alpha)
        xs = _moe(geo, xm, f, scores, modl, moe_bias[i], moe_w_gu, moe_w_down, moe_sh_gu[i],
                  moe_sh_down[i], ln_g[i, 1], ln_b[i, 1], i, alpha, blk)
    return xs[:, :S]
```

```python
import functools
import math

import jax
import jax.numpy as jnp
from jax import lax
from jax.experimental import pallas as pl
from jax.experimental.pallas import tpu as pltpu

F32 = jnp.float32
BF16 = jnp.bfloat16
HIGHEST = lax.Precision.HIGHEST

N_MIXERS = 4
LN_EPS = 1e-6
LRU_C = 8.0
RG_BLOCK = 128
GRID_W = 64
DA_HEAD_DIM = 64
ROPE_THETA = 10000.0
S5_GROUP = 16
S5_CHUNK = 16
TOP_K = 8
N_EXPERT_GROUPS = 8
TOPK_GROUPS = 4
ROUTED_SCALE = 2.5
MOE_BLOCK = 512
MOE_GROUPS = 1
MOE_ROW_PARTS = 2
DA_Q_TILE = 1024

LANES = 128
SUBLANES = 8
VMEM_LIMIT = 56 * 1024 * 1024


def _cp(n_grid):
    return pltpu.CompilerParams(dimension_semantics=("arbitrary",) * n_grid,
                                vmem_limit_bytes=VMEM_LIMIT)


def _silu(x):
    return x * jax.nn.sigmoid(x)


def _mod_table_kernel(c_ref, w_ref, b_ref, o_ref):
    s = _silu(c_ref[...])
    o_ref[...] = jnp.dot(s, w_ref[...], precision=HIGHEST, preferred_element_type=F32) + b_ref[...]


def _mod_table(cc, mod_w, mod_b):
    depth, D, _ = mod_w.shape
    R = cc.shape[0]
    return pl.pallas_call(
        _mod_table_kernel,
        grid=(depth, 6),
        in_specs=[pl.BlockSpec((R, D), lambda i, k: (0, 0)),
                  pl.BlockSpec((None, D, D), lambda i, k: (i, 0, k)),
                  pl.BlockSpec((None, None, 1, D), lambda i, k: (i, k, 0, 0))],
        out_specs=pl.BlockSpec((None, None, R, D), lambda i, k: (i, k, 0, 0)),
        out_shape=jax.ShapeDtypeStruct((depth, 6, R, D), F32),
        compiler_params=_cp(2), name="mod_table",
    )(cc, mod_w, mod_b.reshape(depth, 6, 1, D))


class _Geo:
    def __init__(self, B, S, C, D, b0=0, B_all=None):
        self.B, self.S, self.C, self.D = B, S, C, D
        self.b0 = b0
        self.B_all = B if B_all is None else B_all
        self.P = S + C
        self.TM = math.gcd(S, C)
        while self.TM > 256:
            self.TM //= 2
        self.nt = self.P // self.TM
        self.n_lat = S // self.TM

    def group(self, b0, nb):
        return _Geo(nb, self.S, self.C, self.D, b0=b0, B_all=self.B_all)

    def mod_spec(self, k):
        D, b0, Ba, n_lat = self.D, self.b0, self.B_all, self.n_lat
        return pl.BlockSpec((None, None, 1, D),
                            lambda b, i, *_: (k, jnp.where(i < n_lat, b + b0, Ba), 0, 0))

    def row_spec(self, width, col=0, whole=False):
        b0 = self.b0 if whole else 0
        return pl.BlockSpec((1, self.TM, width), lambda b, i, *_: (b + b0, i, col))

    def col_spec(self, height, whole=False):
        b0 = self.b0 if whole else 0
        return pl.BlockSpec((1, height, self.TM), lambda b, i, *_: (b + b0, 0, i))


def _full_spec(shape):
    nd = len(shape)
    return pl.BlockSpec(shape, lambda *_: (0,) * nd)


def _layer_norm(z, g, b):
    mu = jnp.mean(z, axis=-1, keepdims=True)
    zc = z - mu
    var = jnp.mean(zc * zc, axis=-1, keepdims=True)
    return zc * lax.rsqrt(var + LN_EPS) * g + b


def _finish_mixer(y, x, gate, lng, lnb, sh, sc, wr_ref, xm_ref, f_ref, s_ref, alpha):
    xn = _layer_norm(alpha * x + gate * y, lng, lnb)
    xm_ref[0] = xn
    f = xn * (1.0 + sc) + sh
    f_ref[0] = f.astype(f_ref.dtype)
    logits = lax.dot_general(wr_ref[...], f, (((1,), (1,)), ((), ())), precision=HIGHEST,
                             preferred_element_type=F32)
    s_ref[0] = jax.nn.sigmoid(logits)


def _post_kernel(*refs, alpha, has_bias):
    if has_bias:
        (y_ref, w_ref, b_ref, x_ref, gate_ref, lng_ref, lnb_ref, sh_ref, sc_ref, wr_ref,
         xm_ref, f_ref, s_ref) = refs
    else:
        (y_ref, w_ref, x_ref, gate_ref, lng_ref, lnb_ref, sh_ref, sc_ref, wr_ref,
         xm_ref, f_ref, s_ref) = refs
    y = jnp.dot(y_ref[0].astype(BF16), w_ref[...], preferred_element_type=F32)
    if has_bias:
        y = y + b_ref[...]
    _finish_mixer(y, x_ref[0], gate_ref[...], lng_ref[...], lnb_ref[...], sh_ref[...], sc_ref[...],
                  wr_ref, xm_ref, f_ref, s_ref, alpha)


def _post_outs(geo, E):
    B, P, D = geo.B, geo.P, geo.D
    out_specs = [geo.row_spec(D), geo.row_spec(D), geo.col_spec(E)]
    out_shape = [jax.ShapeDtypeStruct((B, P, D), F32), jax.ShapeDtypeStruct((B, P, D), BF16),
                 jax.ShapeDtypeStruct((B, E, P), F32)]
    return out_specs, out_shape


def _post_mixer(geo, y, w_out, b_out, xs, modl, lng, lnb, w_router, alpha):
    D = geo.D
    Kd = y.shape[-1]
    E = w_router.shape[-1]
    has_bias = b_out is not None
    ins = [y, w_out.astype(BF16)]
    specs = [geo.row_spec(Kd), _full_spec((Kd, D))]
    if has_bias:
        ins.append(b_out.reshape(1, D))
        specs.append(_full_spec((1, D)))
    ins += [xs, modl, lng.reshape(1, D), lnb.reshape(1, D), modl, modl, w_router.T]
    specs += [geo.row_spec(D), geo.mod_spec(2), _full_spec((1, D)), _full_spec((1, D)),
              geo.mod_spec(3), geo.mod_spec(4), _full_spec((E, D))]
    out_specs, out_shape = _post_outs(geo, E)
    return pl.pallas_call(
        functools.partial(_post_kernel, alpha=alpha, has_bias=has_bias),
        grid=(geo.B, geo.nt), in_specs=specs, out_specs=out_specs, out_shape=out_shape,
        compiler_params=_cp(2), name="post_mixer",
    )(*ins)


def _dwconv_seg(r, pad_scr, cw, cb, S, C, lo):
    P = S + C
    n = r.shape[-1]
    pad_scr[0:SUBLANES, :] = jnp.zeros((SUBLANES, n), F32)
    pad_scr[SUBLANES + P:, :] = jnp.zeros((SUBLANES, n), F32)
    pad_scr[SUBLANES:SUBLANES + P, :] = r
    row = lax.broadcasted_iota(jnp.int32, r.shape, 0)
    tl = jnp.where(row < S, row, row - S)
    sl = jnp.where(row < S, S, C)
    acc = jnp.zeros_like(r) + cb
    for k in range(cw.shape[0]):
        off = k - lo
        if off == 0:
            term = r
        else:
            shifted = pad_scr[SUBLANES + off:SUBLANES + off + P, :]
            valid = jnp.logical_and(tl + off >= 0, tl + off < sl)
            term = jnp.where(valid, shifted, 0.0)
        acc = acc + cw[k:k + 1, :] * term
    return acc


def _rg_in_kernel(x_ref, sh_ref, sc_ref, wg_ref, wr_ref, g_ref, r_ref):
    h = (x_ref[0] * (1.0 + sc_ref[...]) + sh_ref[...]).astype(BF16)
    g = jnp.dot(h, wg_ref[...], preferred_element_type=F32)
    g_ref[0] = jax.nn.gelu(g).astype(g_ref.dtype)
    r_ref[0] = jnp.dot(h, wr_ref[...], preferred_element_type=F32)


def _rg_scan_kernel(r_ref, g_ref, cw_ref, cb_ref, gw_ref, gb_ref, lam_ref, o_ref,
                    a_scr, b_scr, h_scr, pad_scr, *, S, C):
    P = S + C
    n = r_ref.shape[-1]
    r = r_ref[0]
    rc = _dwconv_seg(r, pad_scr, cw_ref[...], cb_ref[...], S, C, cw_ref.shape[0] // 2)
    rcb = rc.astype(BF16)
    row = lax.broadcasted_iota(jnp.int32, (P, n), 0)
    sub = jnp.bitwise_and(row, SUBLANES - 1)
    for scr in (a_scr, b_scr):
        scr[0:SUBLANES, :] = jnp.zeros((SUBLANES, n), F32)
        scr[SUBLANES + P:, :] = jnp.zeros((SUBLANES, n), F32)
    for d in range(2):
        rev = d == 1
        gr = jax.nn.sigmoid(jnp.dot(rcb, gw_ref[d, 0], preferred_element_type=F32) + gb_ref[d, 0])
        gi = jax.nn.sigmoid(jnp.dot(rcb, gw_ref[d, 1], preferred_element_type=F32) + gb_ref[d, 1])
        nl = -lam_ref[d]
        sp = jnp.maximum(nl, 0.0) + jnp.log1p(jnp.exp(-jnp.abs(nl)))
        a = jnp.exp(-LRU_C * gr * sp)
        om = 1.0 - a * a
        bb = om * lax.rsqrt(jnp.maximum(om, 1e-30)) * gi * rc
        for s in (1, 2, 4):
            a_scr[SUBLANES:SUBLANES + P, :] = a
            b_scr[SUBLANES:SUBLANES + P, :] = bb
            lo = SUBLANES + (s if rev else -s)
            a_sh = a_scr[lo:lo + P, :]
            b_sh = b_scr[lo:lo + P, :]
            m = (sub < SUBLANES - s) if rev else (sub >= s)
            bb = jnp.where(m, a * b_sh + bb, bb)
            a = jnp.where(m, a * a_sh, a)
        a_scr[SUBLANES:SUBLANES + P, :] = a
        b_scr[SUBLANES:SUBLANES + P, :] = bb

        def chain(lo_tile, n_tiles, c0):
            def body(i, c):
                t = (lo_tile + n_tiles - 1 - i) if rev else (lo_tile + i)
                off = pl.multiple_of((t + 1) * SUBLANES, SUBLANES)
                h = b_scr[pl.ds(off, SUBLANES), :] + a_scr[pl.ds(off, SUBLANES), :] * c
                if d == 0:
                    h_scr[pl.ds(off, SUBLANES), :] = h
                else:
                    h_scr[pl.ds(off, SUBLANES), :] = h_scr[pl.ds(off, SUBLANES), :] + h
                edge = h[0:1] if rev else h[SUBLANES - 1:SUBLANES]
                return jnp.broadcast_to(edge, (SUBLANES, n))
            return lax.fori_loop(0, n_tiles, body, c0)

        c_ctx = chain(S // SUBLANES, C // SUBLANES, jnp.zeros((SUBLANES, n), F32))
        chain(0, S // SUBLANES, c_ctx)
    o_ref[0] = (g_ref[0].astype(F32) * h_scr[SUBLANES:SUBLANES + P, :]).astype(o_ref.dtype)


def _rglru_mixer(geo, xs, modl, w_in, conv_w, conv_b, gate_w, gate_b, lam):
    B, P, D, S, C = geo.B, geo.P, geo.D, geo.S, geo.C
    R = w_in.shape[1] // 2
    nb = R // RG_BLOCK
    w_in = w_in.astype(BF16)
    g, r = pl.pallas_call(
        _rg_in_kernel, grid=(B, geo.nt),
        in_specs=[geo.row_spec(D), geo.mod_spec(0), geo.mod_spec(1),
                  pl.BlockSpec((D, R), lambda b, i: (0, 0)), pl.BlockSpec((D, R), lambda b, i: (0, 1))],
        out_specs=[geo.row_spec(R), geo.row_spec(R)],
        out_shape=[jax.ShapeDtypeStruct((B, P, R), BF16), jax.ShapeDtypeStruct((B, P, R), F32)],
        compiler_params=_cp(2), name="rg_in",
    )(xs, modl, modl, w_in, w_in)
    K = conv_w.shape[0]
    seq_spec = pl.BlockSpec((1, P, RG_BLOCK), lambda b, n: (b, 0, n))
    y = pl.pallas_call(
        functools.partial(_rg_scan_kernel, S=S, C=C), grid=(B, nb),
        in_specs=[seq_spec, seq_spec,
                  pl.BlockSpec((K, RG_BLOCK), lambda b, n: (0, n)),
                  pl.BlockSpec((1, RG_BLOCK), lambda b, n: (0, n)),
                  pl.BlockSpec((2, 2, None, RG_BLOCK, RG_BLOCK), lambda b, n: (0, 0, n, 0, 0)),
                  pl.BlockSpec((2, 2, 1, RG_BLOCK), lambda b, n: (0, 0, 0, n)),
                  pl.BlockSpec((2, 1, RG_BLOCK), lambda b, n: (0, 0, n))],
        out_specs=seq_spec,
        out_shape=jax.ShapeDtypeStruct((B, P, R), BF16),
        scratch_shapes=[pltpu.VMEM((P + 2 * SUBLANES, RG_BLOCK), F32)] * 4,
        compiler_params=_cp(2), name="rg_scan",
    )(r, g, conv_w, conv_b.reshape(1, R), gate_w.astype(BF16), gate_b.reshape(2, 2, 1, R),
      lam.reshape(2, 1, R))
    return y


def _mm_bias_kernel(x_ref, sh_ref, sc_ref, w_ref, b_ref, o_ref):
    h = (x_ref[0] * (1.0 + sc_ref[...]) + sh_ref[...]).astype(BF16)
    o_ref[0] = (jnp.dot(h, w_ref[...], preferred_element_type=F32) + b_ref[...]).astype(o_ref.dtype)


def _short_conv_kernel(u_ref, cw_ref, cb_ref, o_ref, ob_ref, pad_scr, *, S, C):
    y = _dwconv_seg(u_ref[0], pad_scr, cw_ref[...], cb_ref[...], S, C, (cw_ref.shape[0] - 1) // 2)
    o_ref[0] = y
    ob_ref[0] = y.astype(BF16)


def _dft_table_kernel(c_ref, s_ref, st_ref, *, L, TF):
    i = pl.program_id(0)
    N = 2 * L
    f = lax.broadcasted_iota(jnp.int32, (TF, L), 0) + i * TF
    t = lax.broadcasted_iota(jnp.int32, (TF, L), 1)
    ang = jnp.bitwise_and(f * t, N - 1).astype(F32) * (2.0 * math.pi / N)
    c_ref[...] = jnp.cos(ang).astype(BF16)
    nyq_t = (1 - 2 * jnp.bitwise_and(t, 1)).astype(F32)
    s_ref[...] = jnp.where(f == 0, nyq_t, jnp.sin(ang)).astype(BF16)
    nyq_f = (1 - 2 * jnp.bitwise_and(f, 1)).astype(F32)
    st_ref[...] = jnp.where(t == 0, nyq_f, jnp.sin(ang)).astype(BF16)


def _dft_tables(L):
    TF = min(L, 256)
    shp = jax.ShapeDtypeStruct((L, L), BF16)
    spec = pl.BlockSpec((TF, L), lambda i: (i, 0))
    return pl.pallas_call(
        functools.partial(_dft_table_kernel, L=L, TF=TF), grid=(L // TF,),
        in_specs=[], out_specs=[spec, spec, spec], out_shape=[shp, shp, shp],
        compiler_params=_cp(1), name="dft_tables",
    )()


def _hy_filter_kernel(z_ref, w1_ref, b1_ref, w2_ref, b2_ref, w3_ref, b3_ref, fq_ref,
                      w4f_ref, w4b_ref, df_ref, db_ref, tn_ref, kp_ref, km_ref):
    fq = fq_ref[...]

    def lin(h, w_ref, b_ref):
        return jnp.dot(h, w_ref[...], precision=HIGHEST, preferred_element_type=F32) + b_ref[...]

    h = jnp.sin(fq * lin(z_ref[...], w1_ref, b1_ref))
    h = jnp.sin(fq * lin(h, w2_ref, b2_ref))
    h = jnp.sin(fq * lin(h, w3_ref, b3_ref))
    tn = tn_ref[...]
    hf = jnp.dot(h, w4f_ref[...], precision=HIGHEST, preferred_element_type=F32)
    hf = hf * jnp.exp(-tn * jnp.abs(df_ref[...]))
    hb = jnp.dot(h, w4b_ref[...], precision=HIGHEST, preferred_element_type=F32)
    hb = hb * jnp.exp(-tn * jnp.abs(db_ref[...]))
    row = lax.broadcasted_iota(jnp.int32, hb.shape, 0)
    hb = jnp.where(row == 0, 0.0, hb)
    nrm = lax.rsqrt(jnp.sum(hf * hf, axis=0, keepdims=True) + jnp.sum(hb * hb, axis=0, keepdims=True) + 1e-6)
    hf = hf * nrm
    hb = hb * nrm
    kp_ref[...] = (hf + hb).astype(BF16)
    km_ref[...] = (hf - hb).astype(BF16)


def _hy_spectrum_kernel(c_ref, s_ref, s0_ref, kp_ref, km_ref, ka_ref, kb_ref, kc_ref, *, L, TF):
    i = pl.program_id(0)
    inv_n = 1.0 / (2 * L)
    kr = jnp.dot(c_ref[...], kp_ref[...], preferred_element_type=F32)
    ks = jnp.dot(s_ref[...], km_ref[...], preferred_element_type=F32)
    nyq = jnp.dot(s0_ref[...], kp_ref[...], preferred_element_type=F32)[0:1]
    f = lax.broadcasted_iota(jnp.int32, kr.shape, 0) + i * TF
    dc = f == 0
    ka_ref[...] = jnp.where(dc, kr * inv_n, kr * (2.0 * inv_n))
    kb_ref[...] = jnp.where(dc, 0.0, ks * (-2.0 * inv_n))
    kc_ref[...] = jnp.where(dc, nyq * inv_n, kr * (2.0 * inv_n))


def _hy_filters(L, tabs, fw1, fb1, fw2, fb2, fw3, fb3, fw4, ffreq, fdecay, D):
    cm, sm, _ = tabs
    E = fw1.shape[0]
    Hd = fw1.shape[1]
    bands = (E - 1) // 2
    t = jnp.arange(L, dtype=F32)
    t_norm = t / max(L - 1, 1)
    fr = jnp.linspace(1e-4, bands - 1, bands, dtype=F32)
    ang = (2.0 * math.pi / L) * t[:, None] * fr[None, :]
    z = jnp.concatenate([t_norm[:, None], jnp.cos(ang), -jnp.sin(ang)], -1)
    Ep, Hp = -(-E // LANES) * LANES, -(-Hd // LANES) * LANES
    z = jnp.pad(z, ((0, 0), (0, Ep - E)))
    fw1 = jnp.pad(fw1, ((0, Ep - E), (0, Hp - Hd)))
    fw2 = jnp.pad(fw2, ((0, Hp - Hd), (0, Hp - Hd)))
    fw3 = jnp.pad(fw3, ((0, Hp - Hd), (0, Hp - Hd)))
    fw4 = jnp.pad(fw4, ((0, Hp - Hd), (0, 0)))
    fb1, fb2, fb3, ffreq = (jnp.pad(v, (0, Hp - Hd)) for v in (fb1, fb2, fb3, ffreq))
    E, Hd = Ep, Hp
    CT = min(2 * D, 512)
    nct = 2 * D // CT
    dec = fdecay.reshape(1, 4 * D)
    kp, km = pl.pallas_call(
        _hy_filter_kernel, grid=(nct,),
        in_specs=[_full_spec((L, E)), _full_spec((E, Hd)), _full_spec((1, Hd)), _full_spec((Hd, Hd)),
                  _full_spec((1, Hd)), _full_spec((Hd, Hd)), _full_spec((1, Hd)), _full_spec((1, Hd)),
                  pl.BlockSpec((Hd, CT), lambda j: (0, j)), pl.BlockSpec((Hd, CT), lambda j: (0, j + nct)),
                  pl.BlockSpec((1, CT), lambda j: (0, j)), pl.BlockSpec((1, CT), lambda j: (0, j + nct)),
                  _full_spec((L, 1))],
        out_specs=[pl.BlockSpec((L, CT), lambda j: (0, j))] * 2,
        out_shape=[jax.ShapeDtypeStruct((L, 2 * D), BF16)] * 2,
        compiler_params=_cp(1), name="hy_filter",
    )(z, fw1, fb1.reshape(1, Hd), fw2, fb2.reshape(1, Hd), fw3, fb3.reshape(1, Hd), ffreq.reshape(1, Hd),
      fw4, fw4, dec, dec, t_norm[:, None])
    TF = min(L, 256)
    spec_w = pl.BlockSpec((TF, L), lambda i, j: (i, 0))
    spec_k = pl.BlockSpec((L, CT), lambda i, j: (0, j))
    spec_o = pl.BlockSpec((TF, CT), lambda i, j: (i, j))
    shp = jax.ShapeDtypeStruct((L, 2 * D), F32)
    return pl.pallas_call(
        functools.partial(_hy_spectrum_kernel, L=L, TF=TF), grid=(L // TF, nct),
        in_specs=[spec_w, spec_w, pl.BlockSpec((SUBLANES, L), lambda i, j: (0, 0)), spec_k, spec_k],
        out_specs=[spec_o] * 3, out_shape=[shp] * 3,
        compiler_params=_cp(2), name="hy_spectrum",
    )(cm, sm, sm, kp, km)


def _hy_fwd_kernel(z_ref, c_ref, s_ref, ka_ref, kb_ref, kc_ref, p_ref):
    z = z_ref[0]
    zr = jnp.dot(c_ref[...], z, preferred_element_type=F32)
    zs = jnp.dot(s_ref[...], z, preferred_element_type=F32)
    kb = kb_ref[...]
    p_ref[0, 0] = (zr * ka_ref[...] + zs * kb).astype(BF16)
    p_ref[0, 1] = (zs * kc_ref[...] - zr * kb).astype(BF16)


def _hy_inv_kernel(p_ref, c_ref, st_ref, z_ref, x_ref, fb_ref, o_ref):
    y = jnp.dot(c_ref[...], p_ref[0, 0], preferred_element_type=F32)
    y = y + jnp.dot(st_ref[...], p_ref[0, 1], preferred_element_type=F32)
    o_ref[0] = (x_ref[0] * (y + z_ref[0].astype(F32) * fb_ref[...])).astype(o_ref.dtype)


def _hy_inv_kernel_alias(p_ref, c_ref, st_ref, z_ref, x_ref, fb_ref, prev_ref, o_ref):
    del prev_ref
    _hy_inv_kernel(p_ref, c_ref, st_ref, z_ref, x_ref, fb_ref, o_ref)


def _hy_conv(geo, L, off, tabs, z, z_col, kfilt, k_col, xmul, x_col, fbias, out):
    B, P, D = geo.B, geo.P, geo.D
    cm, sm, smt = tabs
    ka, kb, kc = kfilt
    rb = off // L
    TF = min(L, 256)
    spec_w = pl.BlockSpec((TF, L), lambda b, i: (i, 0))
    spec_k = pl.BlockSpec((TF, D), lambda b, i: (i, k_col))
    p = pl.pallas_call(
        _hy_fwd_kernel, grid=(B, L // TF),
        in_specs=[pl.BlockSpec((1, L, D), lambda b, i: (b, rb, z_col)), spec_w, spec_w,
                  spec_k, spec_k, spec_k],
        out_specs=pl.BlockSpec((1, 2, TF, D), lambda b, i: (b, 0, i, 0)),
        out_shape=jax.ShapeDtypeStruct((B, 2, L, D), BF16),
        compiler_params=_cp(2), name="hy_fwd",
    )(z, cm, sm, ka, kb, kc)
    CT = min(D, 512)
    nct = D // CT
    rt = off // TF
    spec_wi = pl.BlockSpec((TF, L), lambda b, j, i: (i, 0))
    return pl.pallas_call(
        _hy_inv_kernel_alias, grid=(B, nct, L // TF),
        in_specs=[pl.BlockSpec((1, 2, L, CT), lambda b, j, i: (b, 0, 0, j)), spec_wi, spec_wi,
                  pl.BlockSpec((1, TF, CT), lambda b, j, i: (b, rt + i, z_col * nct + j)),
                  pl.BlockSpec((1, TF, CT), lambda b, j, i: (b, rt + i, x_col * nct + j)),
                  pl.BlockSpec((1, CT), lambda b, j, i: (0, k_col * nct + j)),
                  pl.BlockSpec(memory_space=pl.ANY)],
        out_specs=pl.BlockSpec((1, TF, CT), lambda b, j, i: (b, rt + i, j)),
        out_shape=jax.ShapeDtypeStruct((B, P, D), BF16),
        input_output_aliases={6: 0},
        compiler_params=_cp(3), name="hy_inv",
    )(p, cm, smt, z, xmul, fbias, out)


def _hyena_mixer(geo, xs, modl, w_in, b_in, short_w, short_b, fw1, fb1, fw2, fb2, fw3, fb3, fw4,
                 ffreq, fdecay, fbias):
    B, P, D, S, C = geo.B, geo.P, geo.D, geo.S, geo.C
    u0 = pl.pallas_call(
        _mm_bias_kernel, grid=(B, geo.nt, 3),
        in_specs=[geo.row_spec(D), geo.mod_spec(0), geo.mod_spec(1),
                  pl.BlockSpec((D, D), lambda b, i, j: (0, j)), pl.BlockSpec((1, D), lambda b, i, j: (0, j))],
        out_specs=pl.BlockSpec((1, geo.TM, D), lambda b, i, j: (b, i, j)),
        out_shape=jax.ShapeDtypeStruct((B, P, 3 * D), F32),
        compiler_params=_cp(3), name="hy_in",
    )(xs, modl, modl, w_in.astype(BF16), b_in.reshape(1, 3 * D))
    CT = min(D, 256)
    Ks = short_w.shape[0]
    spec = pl.BlockSpec((1, P, CT), lambda b, j: (b, 0, j))
    u, ub = pl.pallas_call(
        functools.partial(_short_conv_kernel, S=S, C=C), grid=(B, 3 * D // CT),
        in_specs=[spec, pl.BlockSpec((Ks, CT), lambda b, j: (0, j)), pl.BlockSpec((1, CT), lambda b, j: (0, j))],
        out_specs=[spec, spec],
        out_shape=[jax.ShapeDtypeStruct((B, P, 3 * D), F32), jax.ShapeDtypeStruct((B, P, 3 * D), BF16)],
        scratch_shapes=[pltpu.VMEM((P + 2 * SUBLANES, CT), F32)],
        compiler_params=_cp(2), name="hy_short",
    )(u0, short_w, short_b.reshape(1, 3 * D))
    fb = fbias.reshape(1, 2 * D)
    z1 = jnp.zeros((B, P, D), BF16)
    z2 = jnp.zeros((B, P, D), BF16)
    segs = [(S, 0), (C, S)]
    convs = []
    for L, off in segs:
        tabs = _dft_tables(L)
        kf = _hy_filters(L, tabs, fw1, fb1, fw2, fb2, fw3, fb3, fw4, ffreq, fdecay, D)
        convs.append((L, off, tabs, kf))
    for L, off, tabs, kf in convs:
        z1 = _hy_conv(geo, L, off, tabs, ub, 0, kf, 0, u, 1, fb, z1)
    for L, off, tabs, kf in convs:
        z2 = _hy_conv(geo, L, off, tabs, z1, 0, kf, 1, u, 2, fb, z2)
    return z2


def _rope_tables(S, D):
    t = jnp.arange(S)
    row = (t // GRID_W).astype(F32)
    col = (t % GRID_W).astype(F32)
    axis_dim = DA_HEAD_DIM // 2
    half = axis_dim // 2
    inv = ROPE_THETA ** (-jnp.arange(0, axis_dim, 2, dtype=F32) / axis_dim)
    lane = jnp.arange(D)
    within = lane % DA_HEAD_DIM
    pos = jnp.where((within // axis_dim)[None, :] == 0, row[:, None], col[:, None])
    ang = pos * inv[lane % half][None, :]
    sign = jnp.where((lane % axis_dim) < half, -1.0, 1.0)[None, :]
    return jnp.cos(ang), jnp.sin(ang) * sign


def _da_in_kernel(x_ref, sh_ref, sc_ref, w_ref, cos_ref, sin_ref, o_ref, *, n_lat):
    i = pl.program_id(1)
    j = pl.program_id(2)
    h = (x_ref[0] * (1.0 + sc_ref[...]) + sh_ref[...]).astype(BF16)
    acc = jnp.dot(h, w_ref[...], preferred_element_type=F32)
    acc = acc * jnp.where(j == 0, DA_HEAD_DIM ** -0.5, 1.0)
    rot = jnp.logical_and(i < n_lat, j < 2)

    @pl.when(rot)
    def _():
        Dn = acc.shape[-1]
        half = DA_HEAD_DIM // 4
        lane = lax.broadcasted_iota(jnp.int32, acc.shape, 1)
        up = pltpu.roll(acc, Dn - half, 1)
        dn = pltpu.roll(acc, half, 1)
        partner = jnp.where(jnp.bitwise_and(lane, 2 * half - 1) < half, up, dn)
        o_ref[0] = (acc * cos_ref[...] + partner * sin_ref[...]).astype(o_ref.dtype)

    @pl.when(jnp.logical_not(rot))
    def _():
        o_ref[0] = acc.astype(o_ref.dtype)


def _da_attn_kernel(*refs, kv_lo, nk, lam_init, aliased):
    if aliased:
        q_ref, k_ref, v_ref, lam_ref, sub_ref, _, o_ref, vx_scr, s_scr = refs
    else:
        q_ref, k_ref, v_ref, lam_ref, sub_ref, o_ref, vx_scr, s_scr = refs
    i = pl.program_id(2)
    HW = v_ref.shape[-1]
    TQ = q_ref.shape[1]
    lp = lam_ref[...]
    lam = (jnp.exp(jnp.sum(lp[0:1] * lp[1:2], axis=1, keepdims=True))
           - jnp.exp(jnp.sum(lp[2:3] * lp[3:4], axis=1, keepdims=True)) + lam_init)

    @pl.when(i == 0)
    def _():
        vx_scr[:, :HW] = v_ref[0]
        vx_scr[:, HW:] = jnp.ones((vx_scr.shape[0], HW), BF16)

    lane = lax.broadcasted_iota(jnp.int32, (TQ, HW), 1)
    q = q_ref[0]
    k = k_ref[0, kv_lo:kv_lo + nk, :]
    outs = []
    for c in range(2):
        qc = jnp.where((lane // DA_HEAD_DIM) == c, q, jnp.zeros_like(q))
        s_scr[...] = lax.dot_general(qc, k, (((1,), (1,)), ((), ())), preferred_element_type=F32)
        m = jnp.max(s_scr[...], axis=-1, keepdims=True)
        p = jnp.exp((s_scr[...] - m).astype(BF16))
        ov = jnp.dot(p, vx_scr[kv_lo:kv_lo + nk, :], preferred_element_type=F32)
        outs.append(ov[:, :HW] / ov[:, HW:HW + 1])
    o = outs[0] - lam * outs[1]
    o = o * lax.rsqrt(jnp.mean(o * o, axis=-1, keepdims=True) + 1e-5) * sub_ref[...] * (1.0 - lam_init)
    o_ref[0] = o.astype(o_ref.dtype)


def _diff_attention_mixer(geo, xs, modl, w_in, lam_p, subln_w, layer_idx):
    B, P, D, S, C = geo.B, geo.P, geo.D, geo.S, geo.C
    H = D // (2 * DA_HEAD_DIM)
    HW = 2 * DA_HEAD_DIM
    cos_t, sin_t = _rope_tables(S, D)
    n_lat = geo.n_lat
    tab_spec = pl.BlockSpec((geo.TM, D), lambda b, i, j: (jnp.minimum(i, n_lat - 1), 0))
    qkv = pl.pallas_call(
        functools.partial(_da_in_kernel, n_lat=n_lat), grid=(B, geo.nt, 3),
        in_specs=[geo.row_spec(D), geo.mod_spec(0), geo.mod_spec(1),
                  pl.BlockSpec((D, D), lambda b, i, j: (0, j)), tab_spec, tab_spec],
        out_specs=pl.BlockSpec((1, geo.TM, D), lambda b, i, j: (b, i, j)),
        out_shape=jax.ShapeDtypeStruct((B, P, 3 * D), BF16),
        compiler_params=_cp(3), name="da_in",
    )(xs, modl, modl, w_in.astype(BF16), cos_t, sin_t)
    lam_init = 0.8 - 0.6 * math.exp(-0.3 * layer_idx)
    def attend(TQ, row0, n_tiles, kv_lo, nk, prev):
        rb = row0 // TQ
        ins = [qkv, qkv, qkv, lam_p, subln_w.reshape(1, HW)]
        specs = [pl.BlockSpec((1, TQ, HW), lambda b, h, i: (b, rb + i, h)),
                 pl.BlockSpec((1, P, HW), lambda b, h, i: (b, 0, H + h)),
                 pl.BlockSpec((1, P, HW), lambda b, h, i: (b, 0, 2 * H + h)),
                 _full_spec((4, DA_HEAD_DIM)), _full_spec((1, HW))]
        if prev is not None:
            ins.append(prev)
            specs.append(pl.BlockSpec(memory_space=pl.ANY))
        return pl.pallas_call(
            functools.partial(_da_attn_kernel, kv_lo=kv_lo, nk=nk, lam_init=lam_init, aliased=prev is not None),
            grid=(B, H, n_tiles), in_specs=specs,
            out_specs=pl.BlockSpec((1, TQ, HW), lambda b, h, i: (b, rb + i, h)),
            out_shape=jax.ShapeDtypeStruct((B, P, D), BF16),
            scratch_shapes=[pltpu.VMEM((P, 2 * HW), BF16), pltpu.VMEM((TQ, nk), F32)],
            input_output_aliases={} if prev is None else {5: 0},
            compiler_params=_cp(3), name="da_attn",
        )(*ins)

    TQ = geo.TM
    while TQ < DA_Q_TILE and S % (2 * TQ) == 0:
        TQ *= 2
    y = attend(TQ, 0, S // TQ, 0, P, None)
    return attend(geo.TM, S, C // geo.TM, S, C, y)


def _s5_operators(a_re, a_im, log_step, b_re, b_im, c_re, c_im):
    T = S5_CHUNK
    G, Pst = a_re.shape[1], a_re.shape[2]
    Hg = S5_GROUP
    GL = LANES // Hg
    LB = G // GL
    lam = lax.complex(jnp.minimum(a_re.astype(F32), -1e-4), a_im.astype(F32))
    step = jnp.exp(log_step.astype(F32))[..., None]
    abar = jnp.exp(lam * step)
    bbar = ((abar - 1.0) / lam)[..., None] * lax.complex(b_re.astype(F32), b_im.astype(F32))
    cmat = lax.complex(c_re.astype(F32), c_im.astype(F32))
    pows = jnp.stack([abar ** l for l in range(T + 1)], axis=1)
    eye = jnp.eye(GL, dtype=F32)
    ar = jnp.arange(T)
    big_m, big_g, big_h, a_t = [], [], [], []
    for d in range(2):
        pw = pows[d]
        kl = jnp.einsum('gjp,lgp,gph->lgjh', cmat[d], pw[:T], bbar[d]).real
        lag = (ar[None, :] - ar[:, None]) if d == 0 else (ar[:, None] - ar[None, :])
        tz = jnp.where((lag >= 0)[:, :, None, None, None], kl[jnp.clip(lag, 0, T - 1)], 0.0)
        tz = tz.reshape(T, T, LB, GL, Hg, Hg).astype(BF16)
        m = jnp.einsum('stbgjh,gk->bsghtkj', tz, eye.astype(BF16)).reshape(LB, T * LANES, T * LANES)
        e_in = (T - 1 - ar) if d == 0 else ar
        gc = pw[e_in][:, :, :, None] * bbar[d][None]
        gc = gc.reshape(T, LB, GL, Pst, Hg)
        g_re = jnp.einsum('sbgph,gk->bsghkp', gc.real, eye).reshape(LB, T * LANES, GL * Pst)
        g_im = jnp.einsum('sbgph,gk->bsghkp', gc.imag, eye).reshape(LB, T * LANES, GL * Pst)
        e_out = (ar + 1) if d == 0 else (T - ar)
        hc = cmat[d][None] * pw[e_out][:, :, None, :]
        hc = hc.reshape(T, LB, GL, Hg, Pst)
        h_re = jnp.einsum('tbgjp,gk->bgptkj', hc.real, eye).reshape(LB, GL * Pst, T * LANES)
        h_im = jnp.einsum('tbgjp,gk->bgptkj', -hc.imag, eye).reshape(LB, GL * Pst, T * LANES)
        big_m.append(m)
        big_g.append(jnp.concatenate([g_re, g_im], axis=2))
        big_h.append(jnp.concatenate([h_re, h_im], axis=1))
        at = pw[T].reshape(LB, 1, GL * Pst)
        a_t.append(jnp.concatenate([at.real, at.imag], axis=2))
    return (jnp.stack(big_m).astype(BF16), jnp.stack(big_g).astype(BF16),
            jnp.stack(big_h).astype(BF16), jnp.stack(a_t).astype(F32))


def _modulate_kernel(x_ref, sh_ref, sc_ref, o_ref):
    o_ref[0] = x_ref[0] * (1.0 + sc_ref[...]) + sh_ref[...]


def _s5_kernel(u_ref, m_ref, g_ref, h_ref, a_ref, o_ref, gx_scr, sp_scr, *, S, C):
    d = pl.program_id(1)
    T = S5_CHUNK
    P = S + C
    n = P // T
    n_lat = S // T
    x = jnp.concatenate([u_ref[0, pl.ds(s, n, stride=T), :] for s in range(T)], axis=1).astype(BF16)
    gx_scr[...] = jnp.dot(x, g_ref[...], preferred_element_type=F32)
    ns = a_ref.shape[-1] // 2
    a_r = a_ref[:, :ns]
    a_i = a_ref[:, ns:]

    def scan(lo, cnt, rev, carry):
        def body(k, st):
            s_r, s_i = st
            c = (lo + cnt - 1 - k) if rev else (lo + k)
            sp_scr[pl.ds(c, 1), :] = jnp.concatenate([s_r, s_i], axis=1)
            gx = gx_scr[pl.ds(c, 1), :]
            return (a_r * s_r - a_i * s_i + gx[:, :ns], a_r * s_i + a_i * s_r + gx[:, ns:])
        return lax.fori_loop(0, cnt, body, carry)

    zero = (jnp.zeros((1, ns), F32), jnp.zeros((1, ns), F32))

    @pl.when(d == 0)
    def _():
        scan(0, n_lat, False, scan(n_lat, n - n_lat, False, zero))

    @pl.when(d == 1)
    def _():
        scan(0, n_lat, True, scan(n_lat, n - n_lat, True, zero))

    y = jnp.dot(x, m_ref[...], preferred_element_type=F32)
    y = y + jnp.dot(sp_scr[...].astype(BF16), h_ref[...], preferred_element_type=F32)
    for s in range(T):
        o_ref[0, pl.ds(s, n, stride=T), :] = y[:, s * LANES:(s + 1) * LANES]


def _s5_mixer(geo, xs, modl, a_re, a_im, log_step, b_re, b_im, c_re, c_im):
    B, P, D, S, C = geo.B, geo.P, geo.D, geo.S, geo.C
    u = pl.pallas_call(
        _modulate_kernel, grid=(B, geo.nt),
        in_specs=[geo.row_spec(D), geo.mod_spec(0), geo.mod_spec(1)],
        out_specs=geo.row_spec(D), out_shape=jax.ShapeDtypeStruct((B, P, D), F32),
        compiler_params=_cp(2), name="s5_modulate",
    )(xs, modl, modl)
    big_m, big_g, big_h, a_t = _s5_operators(a_re, a_im, log_step, b_re, b_im, c_re, c_im)
    LB = D // LANES
    TL = S5_CHUNK * LANES
    NS = big_g.shape[-1]
    n = P // S5_CHUNK
    y = pl.pallas_call(
        functools.partial(_s5_kernel, S=S, C=C), grid=(LB, 2, B),
        in_specs=[pl.BlockSpec((1, P, LANES), lambda l, d, b: (b, 0, l)),
                  pl.BlockSpec((None, None, TL, TL), lambda l, d, b: (d, l, 0, 0)),
                  pl.BlockSpec((None, None, TL, NS), lambda l, d, b: (d, l, 0, 0)),
                  pl.BlockSpec((None, None, NS, TL), lambda l, d, b: (d, l, 0, 0)),
                  pl.BlockSpec((None, None, 1, NS), lambda l, d, b: (d, l, 0, 0))],
        out_specs=pl.BlockSpec((None, 1, P, LANES), lambda l, d, b: (d, b, 0, l)),
        out_shape=jax.ShapeDtypeStruct((2, B, P, D), F32),
        scratch_shapes=[pltpu.VMEM((n, NS), F32), pltpu.VMEM((n, NS), F32)],
        compiler_params=_cp(3), name="s5_scan",
    )(u, big_m, big_g, big_h, a_t)
    return u, y


def _s5_post_kernel(yf_ref, yb_ref, u_ref, d_ref, w_ref, b_ref, x_ref, gate_ref, lng_ref, lnb_ref,
                    sh_ref, sc_ref, wr_ref, xm_ref, f_ref, s_ref, *, alpha):
    g = jax.nn.gelu(yf_ref[0] + yb_ref[0] + d_ref[...] * u_ref[0])
    vg = jnp.dot(g.astype(BF16), w_ref[...], preferred_element_type=F32) + b_ref[...]
    Dn = vg.shape[-1] // 2
    y = vg[:, :Dn] * jax.nn.sigmoid(vg[:, Dn:])
    _finish_mixer(y, x_ref[0], gate_ref[...], lng_ref[...], lnb_ref[...], sh_ref[...], sc_ref[...],
                  wr_ref, xm_ref, f_ref, s_ref, alpha)


def _s5_post(geo, y2, u, d_skip, w_glu, b_glu, xs, modl, lng, lnb, w_router, alpha):
    D = geo.D
    E = w_router.shape[-1]
    out_specs, out_shape = _post_outs(geo, E)
    TM = geo.TM
    return pl.pallas_call(
        functools.partial(_s5_post_kernel, alpha=alpha), grid=(geo.B, geo.nt),
        in_specs=[pl.BlockSpec((None, 1, TM, D), lambda b, i: (0, b, i, 0)),
                  pl.BlockSpec((None, 1, TM, D), lambda b, i: (1, b, i, 0)),
                  geo.row_spec(D), _full_spec((1, D)), _full_spec((D, 2 * D)), _full_spec((1, 2 * D)),
                  geo.row_spec(D), geo.mod_spec(2), _full_spec((1, D)), _full_spec((1, D)),
                  geo.mod_spec(3), geo.mod_spec(4), _full_spec((E, D))],
        out_specs=out_specs, out_shape=out_shape,
        compiler_params=_cp(2), name="s5_post",
    )(y2, y2, u, d_skip.reshape(1, D), w_glu.astype(BF16), b_glu.reshape(1, 2 * D), xs, modl,
      lng.reshape(1, D), lnb.reshape(1, D), modl, modl, w_router.T)


def _route_kernel(s_ref, b_ref, idx_ref, w_ref):
    sc = s_ref[0]
    E, TM = sc.shape
    biased = sc + b_ref[...]
    G = N_EXPERT_GROUPS
    per = E // G
    neg = -jnp.inf
    blocks, gs = [], []
    for g in range(G):
        blk = biased[g * per:(g + 1) * per]
        m1 = jnp.max(blk, axis=0, keepdims=True)
        is1 = blk == m1
        cnt = jnp.sum(is1.astype(F32), axis=0, keepdims=True)
        m2 = jnp.max(jnp.where(is1, neg, blk), axis=0, keepdims=True)
        blocks.append(blk)
        gs.append(m1 + jnp.where(cnt >= 2.0, m1, m2))
    masked = []
    for g in range(G):
        ahead = jnp.zeros((1, TM), F32)
        for h in range(G):
            if h < g:
                ahead = ahead + (gs[h] >= gs[g]).astype(F32)
            elif h > g:
                ahead = ahead + (gs[h] > gs[g]).astype(F32)
        masked.append(jnp.where(ahead < float(TOPK_GROUPS), blocks[g], neg))
    masked = jnp.concatenate(masked, axis=0)
    iota_e = lax.broadcasted_iota(jnp.int32, (E, TM), 0)
    idxs, ws = [], []
    for _ in range(TOP_K):
        m = jnp.max(masked, axis=0, keepdims=True)
        ik = jnp.min(jnp.where(masked == m, iota_e, E), axis=0, keepdims=True)
        sel = iota_e == ik
        ws.append(jnp.sum(jnp.where(sel, sc, 0.0), axis=0, keepdims=True))
        idxs.append(ik)
        masked = jnp.where(sel, neg, masked)
    tot = ws[0]
    for wk in ws[1:]:
        tot = tot + wk
    w = jnp.concatenate(ws, axis=0)
    idx_ref[0] = jnp.concatenate(idxs, axis=0)
    w_ref[0] = w / (tot + 1e-20) * ROUTED_SCALE


def _rank_kernel(idx_ref, rank_ref, cnt_ref, run_scr, *, E):
    first = jnp.logical_and(pl.program_id(0) == 0, pl.program_id(1) == 0)

    @pl.when(first)
    def _():
        run_scr[...] = jnp.zeros_like(run_scr)

    idx = idx_ref[0]
    K, TM = idx.shape
    iota_e = lax.broadcasted_iota(jnp.int32, (E, TM), 0)
    member = jnp.zeros((E, TM), F32)
    for k in range(K):
        member = member + (iota_e == idx[k:k + 1]).astype(F32)
    before = (lax.broadcasted_iota(jnp.int32, (TM, TM), 0)
              < lax.broadcasted_iota(jnp.int32, (TM, TM), 1)).astype(BF16)
    rank = jnp.dot(member.astype(BF16), before, preferred_element_type=F32) + run_scr[...]
    rows = [jnp.sum(jnp.where(iota_e == idx[k:k + 1], rank, 0.0), axis=0, keepdims=True) for k in range(K)]
    rank_ref[0] = jnp.concatenate(rows, axis=0).astype(jnp.int32)
    run_scr[...] = run_scr[...] + jnp.sum(member, axis=1, keepdims=True)
    cnt_ref[...] = run_scr[...]


def _dest_kernel(idx_ref, rank_ref, start_ref, dest_ref, *, E):
    idx = idx_ref[0]
    K, TM = idx.shape
    iota_e = lax.broadcasted_iota(jnp.int32, (E, TM), 0)
    start = start_ref[...]
    rows = [jnp.sum(jnp.where(iota_e == idx[k:k + 1], start, 0), axis=0, keepdims=True) for k in range(K)]
    dest_ref[0] = jnp.concatenate(rows, axis=0) + rank_ref[0]


def _route_dispatch(geo, scores, bias, blk):
    B, P = geo.B, geo.P
    E = scores.shape[1]
    K = TOP_K
    kspec = geo.col_spec(K)
    idx, w = pl.pallas_call(
        _route_kernel, grid=(B, geo.nt),
        in_specs=[geo.col_spec(E, whole=True), _full_spec((E, 1))],
        out_specs=[kspec, kspec],
        out_shape=[jax.ShapeDtypeStruct((B, K, P), jnp.int32), jax.ShapeDtypeStruct((B, K, P), F32)],
        compiler_params=_cp(2), name="moe_route",
    )(scores, bias.astype(F32).reshape(E, 1))
    rank, cnt = pl.pallas_call(
        functools.partial(_rank_kernel, E=E), grid=(B, geo.nt),
        in_specs=[kspec], out_specs=[kspec, _full_spec((E, 1))],
        out_shape=[jax.ShapeDtypeStruct((B, K, P), jnp.int32), jax.ShapeDtypeStruct((E, 1), F32)],
        scratch_shapes=[pltpu.VMEM((E, 1), F32)],
        compiler_params=_cp(2), name="moe_rank",
    )(idx)
    n_assign = B * P * K
    n_blocks = -(-(n_assign + E * (blk - 1)) // blk)
    counts = cnt[:, 0].astype(jnp.int32)
    pcounts = (counts + blk - 1) // blk * blk
    pends = jnp.cumsum(pcounts)
    starts = (pends - pcounts).astype(jnp.int32)
    n_used = pends[-1] // blk
    blk_e = jnp.minimum(jnp.searchsorted(pends, jnp.arange(n_blocks) * blk, side='right'), E - 1)
    blk_e = jnp.where(jnp.arange(n_blocks) < n_used, blk_e, blk_e[jnp.maximum(n_used - 1, 0)])
    dest = pl.pallas_call(
        functools.partial(_dest_kernel, E=E), grid=(B, geo.nt),
        in_specs=[kspec, kspec, _full_spec((E, 1))], out_specs=kspec,
        out_shape=jax.ShapeDtypeStruct((B, K, P), jnp.int32),
        compiler_params=_cp(2), name="moe_dest",
    )(idx, rank, starts.reshape(E, 1))
    return w, dest, blk_e.astype(jnp.int32), n_used.astype(jnp.int32).reshape(1), n_blocks


def _expert_kernel(be_ref, nu_ref, x_ref, wgu_ref, wd_ref, *rest):
    del be_ref
    o_ref = rest[-1]
    i = pl.program_id(0)

    @pl.when(i < nu_ref[0])
    def _():
        h = jnp.dot(x_ref[...], wgu_ref[...].astype(BF16), preferred_element_type=F32)
        Fh = h.shape[-1] // 2
        a = _silu(h[:, :Fh]) * h[:, Fh:]
        o_ref[...] = jnp.dot(a.astype(BF16), wd_ref[...].astype(BF16),
                             preferred_element_type=F32).astype(o_ref.dtype)

    @pl.when(i >= nu_ref[0])
    def _():
        o_ref[...] = jnp.zeros_like(o_ref)


def _expert_ffn(x_part, blk_e, n_used, w_gu, w_down, layer, blk, n_rows_all, blk0, prev):
    n_rows, D = x_part.shape
    F2 = w_gu.shape[-1]
    ins = [blk_e, n_used, x_part, w_gu, w_down]
    specs = [pl.BlockSpec((blk, D), lambda i, be, nu: (i, 0)),
             pl.BlockSpec((None, None, D, F2), lambda i, be, nu: (layer, be[i], 0, 0)),
             pl.BlockSpec((None, None, F2 // 2, D), lambda i, be, nu: (layer, be[i], 0, 0))]
    if prev is not None:
        ins.append(prev)
        specs.append(pl.BlockSpec(memory_space=pl.ANY))
    grid_spec = pltpu.PrefetchScalarGridSpec(
        num_scalar_prefetch=2, grid=(n_rows // blk,), in_specs=specs,
        out_specs=pl.BlockSpec((blk, D), lambda i, be, nu: (blk0 + i, 0)))
    return pl.pallas_call(
        _expert_kernel, grid_spec=grid_spec,
        out_shape=jax.ShapeDtypeStruct((n_rows_all, D), BF16),
        input_output_aliases={} if prev is None else {5: 0},
        compiler_params=_cp(1), name="moe_experts",
    )(*ins)


def _moe_final_kernel(*refs, alpha, aliased):
    if aliased:
        xm_ref, f_ref, ga_ref, w_ref, shgu_ref, shd_ref, gate_ref, lng_ref, lnb_ref, _, o_ref = refs
    else:
        xm_ref, f_ref, ga_ref, w_ref, shgu_ref, shd_ref, gate_ref, lng_ref, lnb_ref, o_ref = refs
    h = jnp.dot(f_ref[0], shgu_ref[...], preferred_element_type=F32)
    Fh = h.shape[-1] // 2
    a = _silu(h[:, :Fh]) * h[:, Fh:]
    y = jnp.dot(a.astype(BF16), shd_ref[...], preferred_element_type=F32)
    w = w_ref[0]
    for k in range(w.shape[-1]):
        y = y + w[:, k:k + 1] * ga_ref[k, 0].astype(F32)
    o_ref[0] = _layer_norm(alpha * xm_ref[0] + gate_ref[...] * y, lng_ref[...], lnb_ref[...])


def _moe_group(geo, xm, f, scores, modl, bias, w_gu, w_down, sh_gu, sh_down, lng, lnb, layer, alpha, blk, prev):
    B, P, D = geo.B, geo.P, geo.D
    T = B * P
    K = TOP_K
    w, dest, blk_e, n_used, n_blocks = _route_dispatch(geo, scores, bias, blk)
    dest_flat = jnp.swapaxes(dest, 0, 1).reshape(K * T)
    t0 = geo.b0 * P
    tok = jnp.broadcast_to(jnp.arange(t0, t0 + T, dtype=jnp.int32)[None], (K, T)).reshape(K * T)
    hit = jnp.zeros((n_blocks * blk,), jnp.int32).at[dest_flat].add(
        tok + 1, unique_indices=True, mode='promise_in_bounds')
    row_tok = jnp.where(hit > 0, hit - 1, t0 + jnp.arange(n_blocks * blk, dtype=jnp.int32) % T)
    f2d = f.reshape(geo.B_all * P, D)
    y_sorted = None
    for part in range(MOE_ROW_PARTS):
        lo = n_blocks * part // MOE_ROW_PARTS
        hi = n_blocks * (part + 1) // MOE_ROW_PARTS
        x_part = f2d.at[row_tok[lo * blk:hi * blk]].get(mode='promise_in_bounds')
        used = jnp.clip(n_used - lo, 0, hi - lo)
        y_sorted = _expert_ffn(x_part, blk_e[lo:hi], used, w_gu, w_down, layer, blk, n_blocks * blk, lo, y_sorted)
    gathered = y_sorted.at[dest_flat].get(unique_indices=True, mode='promise_in_bounds').reshape(K, B, P, D)
    F2 = sh_gu.shape[-1]
    TM = geo.TM
    ins = [xm, f, gathered, jnp.swapaxes(w, 1, 2), sh_gu.astype(BF16), sh_down.astype(BF16), modl,
           lng.reshape(1, D), lnb.reshape(1, D)]
    specs = [geo.row_spec(D, whole=True), geo.row_spec(D, whole=True),
             pl.BlockSpec((K, 1, TM, D), lambda b, i: (0, b, i, 0)), geo.row_spec(K),
             _full_spec((D, F2)), _full_spec((F2 // 2, D)), geo.mod_spec(5),
             _full_spec((1, D)), _full_spec((1, D))]
    if prev is not None:
        ins.append(prev)
        specs.append(pl.BlockSpec(memory_space=pl.ANY))
    return pl.pallas_call(
        functools.partial(_moe_final_kernel, alpha=alpha, aliased=prev is not None), grid=(B, geo.nt),
        in_specs=specs, out_specs=geo.row_spec(D, whole=True),
        out_shape=jax.ShapeDtypeStruct((geo.B_all, P, D), F32),
        input_output_aliases={} if prev is None else {9: 0},
        compiler_params=_cp(2), name="moe_final",
    )(*ins)


def _moe(geo, xm, f, scores, modl, bias, w_gu, w_down, sh_gu, sh_down, lng, lnb, layer, alpha, blk):
    n_groups = MOE_GROUPS if geo.B % MOE_GROUPS == 0 else 1
    nb = geo.B // n_groups
    out = None
    for g in range(n_groups):
        out = _moe_group(geo.group(g * nb, nb), xm, f, scores, modl, bias, w_gu, w_down, sh_gu, sh_down,
                         lng, lnb, layer, alpha, blk, out)
    return out


def kernel(x, c, ctx, c_ctx, mod_w, mod_b, ln_g, ln_b, rg_w_in, rg_conv_w, rg_conv_b, rg_gate_w, rg_gate_b, rg_lam, rg_w_out, hy_w_in, hy_b_in, hy_short_w, hy_short_b, hy_f_w1, hy_f_b1, hy_f_w2, hy_f_b2, hy_f_w3, hy_f_b3, hy_f_w4, hy_f_freq, hy_f_decay, hy_f_bias, hy_w_out, hy_b_out, da_w_in, da_lam, da_subln, da_w_out, s5_a_re, s5_a_im, s5_log_step, s5_b_re, s5_b_im, s5_c_re, s5_c_im, s5_d, s5_w_glu, s5_b_glu, moe_w_router, moe_bias, moe_w_gu, moe_w_down, moe_sh_gu, moe_sh_down):
    B, S, D = x.shape
    C = ctx.shape[1]
    depth = mod_w.shape[0]
    alpha = (2 * depth) ** 0.25
    geo = _Geo(B, S, C, D)
    xs = jnp.concatenate([x, ctx], axis=1)
    R = -(-(B + 1) // SUBLANES) * SUBLANES
    cc = jnp.zeros((R, D), F32).at[:B].set(c).at[B].set(c_ctx)
    modt = _mod_table(cc, mod_w, mod_b).reshape(depth, 6, R, 1, D)
    blk = MOE_BLOCK
    for i in range(depth):
        kind, j = i % N_MIXERS, i // N_MIXERS
        modl = modt[i]
        post = functools.partial(_post_mixer, geo, xs=xs, modl=modl, lng=ln_g[i, 0], lnb=ln_b[i, 0],
                                 w_router=moe_w_router[i], alpha=alpha)
        if kind == 0:
            y = _rglru_mixer(geo, xs, modl, rg_w_in[j], rg_conv_w[j], rg_conv_b[j], rg_gate_w[j],
                             rg_gate_b[j], rg_lam[j])
            xm, f, scores = post(y=y, w_out=rg_w_out[j], b_out=None)
        elif kind == 1:
            y = _hyena_mixer(geo, xs, modl, hy_w_in[j], hy_b_in[j], hy_short_w[j], hy_short_b[j],
                             hy_f_w1[j], hy_f_b1[j], hy_f_w2[j], hy_f_b2[j], hy_f_w3[j], hy_f_b3[j],
                             hy_f_w4[j], hy_f_freq[j], hy_f_decay[j], hy_f_bias[j])
            xm, f, scores = post(y=y, w_out=hy_w_out[j], b_out=hy_b_out[j])
        elif kind == 2:
            y = _diff_attention_mixer(geo, xs, modl, da_w_in[j], da_lam[j], da_subln[j], i)
            xm, f, scores = post(y=y, w_out=da_w_out[j], b_out=None)
        else:
            u, y2 = _s5_mixer(geo, xs, modl, s5_a_re[j], s5_a_im[j], s5_log_step[j], s5_b_re[j],
                              s5_b_im[j], s5_c_re[j], s5_c_im[j])
            xm, f, scores = _s5_post(geo, y2, u, s5_d[j], s5_w_glu[j], s5_b_glu[j], xs, modl,
                                     ln_g[i, 0], ln_b[i, 0], moe_w_router[i], alpha)
        xs = _moe(geo, xm, f, scores, modl, moe_bias[i], moe_w_gu, moe_w_down, moe_sh_gu[i],
                  moe_sh_down[i], ln_g[i, 1], ln_b[i, 1], i, alpha, blk)
    return xs[:, :S]
```

```python
import functools
import math

import jax
import jax.numpy as jnp
from jax import lax
from jax.experimental import pallas as pl
from jax.experimental.pallas import tpu as pltpu

F32 = jnp.float32
BF16 = jnp.bfloat16
HIGHEST = lax.Precision.HIGHEST

N_MIXERS = 4
LN_EPS = 1e-6
LRU_C = 8.0
RG_BLOCK = 128
GRID_W = 64
DA_HEAD_DIM = 64
ROPE_THETA = 10000.0
S5_GROUP = 16
S5_CHUNK = 16
TOP_K = 8
N_EXPERT_GROUPS = 8
TOPK_GROUPS = 4
ROUTED_SCALE = 2.5
MOE_BLOCK = 512
MOE_GROUPS = 1
MOE_ROW_PARTS = 4
DA_Q_TILE = 1024

LANES = 128
SUBLANES = 8
VMEM_LIMIT = 56 * 1024 * 1024


def _cp(n_grid):
    return pltpu.CompilerParams(dimension_semantics=("arbitrary",) * n_grid,
                                vmem_limit_bytes=VMEM_LIMIT)


def _silu(x):
    return x * jax.nn.sigmoid(x)


def _mod_table_kernel(c_ref, w_ref, b_ref, o_ref):
    s = _silu(c_ref[...])
    o_ref[...] = jnp.dot(s, w_ref[...], precision=HIGHEST, preferred_element_type=F32) + b_ref[...]


def _mod_table(cc, mod_w, mod_b):
    depth, D, _ = mod_w.shape
    R = cc.shape[0]
    return pl.pallas_call(
        _mod_table_kernel,
        grid=(depth, 6),
        in_specs=[pl.BlockSpec((R, D), lambda i, k: (0, 0)),
                  pl.BlockSpec((None, D, D), lambda i, k: (i, 0, k)),
                  pl.BlockSpec((None, None, 1, D), lambda i, k: (i, k, 0, 0))],
        out_specs=pl.BlockSpec((None, None, R, D), lambda i, k: (i, k, 0, 0)),
        out_shape=jax.ShapeDtypeStruct((depth, 6, R, D), F32),
        compiler_params=_cp(2), name="mod_table",
    )(cc, mod_w, mod_b.reshape(depth, 6, 1, D))


class _Geo:
    def __init__(self, B, S, C, D, b0=0, B_all=None):
        self.B, self.S, self.C, self.D = B, S, C, D
        self.b0 = b0
        self.B_all = B if B_all is None else B_all
        self.P = S + C
        self.TM = math.gcd(S, C)
        while self.TM > 256:
            self.TM //= 2
        self.nt = self.P // self.TM
        self.n_lat = S // self.TM

    def group(self, b0, nb):
        return _Geo(nb, self.S, self.C, self.D, b0=b0, B_all=self.B_all)

    def mod_spec(self, k):
        D, b0, Ba, n_lat = self.D, self.b0, self.B_all, self.n_lat
        return pl.BlockSpec((None, None, 1, D),
                            lambda b, i, *_: (k, jnp.where(i < n_lat, b + b0, Ba), 0, 0))

    def row_spec(self, width, col=0, whole=False):
        b0 = self.b0 if whole else 0
        return pl.BlockSpec((1, self.TM, width), lambda b, i, *_: (b + b0, i, col))

    def col_spec(self, height, whole=False):
        b0 = self.b0 if whole else 0
        return pl.BlockSpec((1, height, self.TM), lambda b, i, *_: (b + b0, 0, i))


def _full_spec(shape):
    nd = len(shape)
    return pl.BlockSpec(shape, lambda *_: (0,) * nd)


def _layer_norm(z, g, b):
    mu = jnp.mean(z, axis=-1, keepdims=True)
    zc = z - mu
    var = jnp.mean(zc * zc, axis=-1, keepdims=True)
    return zc * lax.rsqrt(var + LN_EPS) * g + b


def _finish_mixer(y, x, gate, lng, lnb, sh, sc, wr_ref, xm_ref, f_ref, s_ref, alpha):
    xn = _layer_norm(alpha * x + gate * y, lng, lnb)
    xm_ref[0] = xn
    f = xn * (1.0 + sc) + sh
    f_ref[0] = f.astype(f_ref.dtype)
    logits = lax.dot_general(wr_ref[...], f, (((1,), (1,)), ((), ())), precision=HIGHEST,
                             preferred_element_type=F32)
    s_ref[0] = jax.nn.sigmoid(logits)


def _post_kernel(*refs, alpha, has_bias):
    if has_bias:
        (y_ref, w_ref, b_ref, x_ref, gate_ref, lng_ref, lnb_ref, sh_ref, sc_ref, wr_ref,
         xm_ref, f_ref, s_ref) = refs
    else:
        (y_ref, w_ref, x_ref, gate_ref, lng_ref, lnb_ref, sh_ref, sc_ref, wr_ref,
         xm_ref, f_ref, s_ref) = refs
    y = jnp.dot(y_ref[0].astype(BF16), w_ref[...], preferred_element_type=F32)
    if has_bias:
        y = y + b_ref[...]
    _finish_mixer(y, x_ref[0], gate_ref[...], lng_ref[...], lnb_ref[...], sh_ref[...], sc_ref[...],
                  wr_ref, xm_ref, f_ref, s_ref, alpha)


def _post_outs(geo, E):
    B, P, D = geo.B, geo.P, geo.D
    out_specs = [geo.row_spec(D), geo.row_spec(D), geo.col_spec(E)]
    out_shape = [jax.ShapeDtypeStruct((B, P, D), F32), jax.ShapeDtypeStruct((B, P, D), BF16),
                 jax.ShapeDtypeStruct((B, E, P), F32)]
    return out_specs, out_shape


def _post_mixer(geo, y, w_out, b_out, xs, modl, lng, lnb, w_router, alpha):
    D = geo.D
    Kd = y.shape[-1]
    E = w_router.shape[-1]
    has_bias = b_out is not None
    ins = [y, w_out.astype(BF16)]
    specs = [geo.row_spec(Kd), _full_spec((Kd, D))]
    if has_bias:
        ins.append(b_out.reshape(1, D))
        specs.append(_full_spec((1, D)))
    ins += [xs, modl, lng.reshape(1, D), lnb.reshape(1, D), modl, modl, w_router.T]
    specs += [geo.row_spec(D), geo.mod_spec(2), _full_spec((1, D)), _full_spec((1, D)),
              geo.mod_spec(3), geo.mod_spec(4), _full_spec((E, D))]
    out_specs, out_shape = _post_outs(geo, E)
    return pl.pallas_call(
        functools.partial(_post_kernel, alpha=alpha, has_bias=has_bias),
        grid=(geo.B, geo.nt), in_specs=specs, out_specs=out_specs, out_shape=out_shape,
        compiler_params=_cp(2), name="post_mixer",
    )(*ins)


def _dwconv_seg(r, pad_scr, cw, cb, S, C, lo):
    P = S + C
    n = r.shape[-1]
    pad_scr[0:SUBLANES, :] = jnp.zeros((SUBLANES, n), F32)
    pad_scr[SUBLANES + P:, :] = jnp.zeros((SUBLANES, n), F32)
    pad_scr[SUBLANES:SUBLANES + P, :] = r
    row = lax.broadcasted_iota(jnp.int32, r.shape, 0)
    tl = jnp.where(row < S, row, row - S)
    sl = jnp.where(row < S, S, C)
    acc = jnp.zeros_like(r) + cb
    for k in range(cw.shape[0]):
        off = k - lo
        if off == 0:
            term = r
        else:
            shifted = pad_scr[SUBLANES + off:SUBLANES + off + P, :]
            valid = jnp.logical_and(tl + off >= 0, tl + off < sl)
            term = jnp.where(valid, shifted, 0.0)
        acc = acc + cw[k:k + 1, :] * term
    return acc


def _rg_in_kernel(x_ref, sh_ref, sc_ref, wg_ref, wr_ref, g_ref, r_ref):
    h = (x_ref[0] * (1.0 + sc_ref[...]) + sh_ref[...]).astype(BF16)
    g = jnp.dot(h, wg_ref[...], preferred_element_type=F32)
    g_ref[0] = jax.nn.gelu(g).astype(g_ref.dtype)
    r_ref[0] = jnp.dot(h, wr_ref[...], preferred_element_type=F32)


def _rg_scan_kernel(r_ref, g_ref, cw_ref, cb_ref, gw_ref, gb_ref, lam_ref, o_ref,
                    a_scr, b_scr, pad_scr, al_scr, bl_scr, cin_scr, *, S, C):
    P = S + C
    NT = P // SUBLANES
    hsum = None
    n = r_ref.shape[-1]
    r = r_ref[0]
    rc = _dwconv_seg(r, pad_scr, cw_ref[...], cb_ref[...], S, C, cw_ref.shape[0] // 2)
    rcb = rc.astype(BF16)
    row = lax.broadcasted_iota(jnp.int32, (P, n), 0)
    sub = jnp.bitwise_and(row, SUBLANES - 1)
    for scr in (a_scr, b_scr):
        scr[0:SUBLANES, :] = jnp.zeros((SUBLANES, n), F32)
        scr[SUBLANES + P:, :] = jnp.zeros((SUBLANES, n), F32)
    for d in range(2):
        rev = d == 1
        gr = jax.nn.sigmoid(jnp.dot(rcb, gw_ref[d, 0], preferred_element_type=F32) + gb_ref[d, 0])
        gi = jax.nn.sigmoid(jnp.dot(rcb, gw_ref[d, 1], preferred_element_type=F32) + gb_ref[d, 1])
        nl = -lam_ref[d]
        sp = jnp.maximum(nl, 0.0) + jnp.log1p(jnp.exp(-jnp.abs(nl)))
        a = jnp.exp(-LRU_C * gr * sp)
        om = 1.0 - a * a
        bb = om * lax.rsqrt(jnp.maximum(om, 1e-30)) * gi * rc
        for s in (1, 2, 4):
            a_scr[SUBLANES:SUBLANES + P, :] = a
            b_scr[SUBLANES:SUBLANES + P, :] = bb
            lo = SUBLANES + (s if rev else -s)
            a_sh = a_scr[lo:lo + P, :]
            b_sh = b_scr[lo:lo + P, :]
            m = (sub < SUBLANES - s) if rev else (sub >= s)
            bb = jnp.where(m, a * b_sh + bb, bb)
            a = jnp.where(m, a * a_sh, a)
        a_scr[SUBLANES:SUBLANES + P, :] = a
        b_scr[SUBLANES:SUBLANES + P, :] = bb
        edge = SUBLANES + (0 if rev else SUBLANES - 1)
        al_scr[...] = a_scr[pl.ds(edge, NT, stride=SUBLANES), :]
        bl_scr[...] = b_scr[pl.ds(edge, NT, stride=SUBLANES), :]

        def chain(lo_tile, n_tiles, c0):
            def body(i, c):
                t = (lo_tile + n_tiles - 1 - i) if rev else (lo_tile + i)
                cin_scr[pl.ds(t, 1), :] = c
                return bl_scr[pl.ds(t, 1), :] + al_scr[pl.ds(t, 1), :] * c
            return lax.fori_loop(0, n_tiles, body, c0, unroll=4)

        c_ctx = chain(S // SUBLANES, C // SUBLANES, jnp.zeros((1, n), F32))
        chain(0, S // SUBLANES, c_ctx)
        cin = cin_scr[...]
        for j in range(SUBLANES):
            pad_scr[pl.ds(j, NT, stride=SUBLANES), :] = cin
        h = bb + a * pad_scr[0:P, :]
        hsum = h if d == 0 else hsum + h
    o_ref[0] = (g_ref[0].astype(F32) * hsum).astype(o_ref.dtype)


def _rglru_mixer(geo, xs, modl, w_in, conv_w, conv_b, gate_w, gate_b, lam):
    B, P, D, S, C = geo.B, geo.P, geo.D, geo.S, geo.C
    R = w_in.shape[1] // 2
    nb = R // RG_BLOCK
    w_in = w_in.astype(BF16)
    g, r = pl.pallas_call(
        _rg_in_kernel, grid=(B, geo.nt),
        in_specs=[geo.row_spec(D), geo.mod_spec(0), geo.mod_spec(1),
                  pl.BlockSpec((D, R), lambda b, i: (0, 0)), pl.BlockSpec((D, R), lambda b, i: (0, 1))],
        out_specs=[geo.row_spec(R), geo.row_spec(R)],
        out_shape=[jax.ShapeDtypeStruct((B, P, R), BF16), jax.ShapeDtypeStruct((B, P, R), F32)],
        compiler_params=_cp(2), name="rg_in",
    )(xs, modl, modl, w_in, w_in)
    K = conv_w.shape[0]
    seq_spec = pl.BlockSpec((1, P, RG_BLOCK), lambda b, n: (b, 0, n))
    y = pl.pallas_call(
        functools.partial(_rg_scan_kernel, S=S, C=C), grid=(B, nb),
        in_specs=[seq_spec, seq_spec,
                  pl.BlockSpec((K, RG_BLOCK), lambda b, n: (0, n)),
                  pl.BlockSpec((1, RG_BLOCK), lambda b, n: (0, n)),
                  pl.BlockSpec((2, 2, None, RG_BLOCK, RG_BLOCK), lambda b, n: (0, 0, n, 0, 0)),
                  pl.BlockSpec((2, 2, 1, RG_BLOCK), lambda b, n: (0, 0, 0, n)),
                  pl.BlockSpec((2, 1, RG_BLOCK), lambda b, n: (0, 0, n))],
        out_specs=seq_spec,
        out_shape=jax.ShapeDtypeStruct((B, P, R), BF16),
        scratch_shapes=([pltpu.VMEM((P + 2 * SUBLANES, RG_BLOCK), F32)] * 3
                        + [pltpu.VMEM((P // SUBLANES, RG_BLOCK), F32)] * 3),
        compiler_params=_cp(2), name="rg_scan",
    )(r, g, conv_w, conv_b.reshape(1, R), gate_w.astype(BF16), gate_b.reshape(2, 2, 1, R),
      lam.reshape(2, 1, R))
    return y


def _mm_bias_kernel(x_ref, sh_ref, sc_ref, w_ref, b_ref, o_ref):
    h = (x_ref[0] * (1.0 + sc_ref[...]) + sh_ref[...]).astype(BF16)
    o_ref[0] = (jnp.dot(h, w_ref[...], preferred_element_type=F32) + b_ref[...]).astype(o_ref.dtype)


def _short_conv_kernel(u_ref, cw_ref, cb_ref, o_ref, ob_ref, pad_scr, *, S, C):
    y = _dwconv_seg(u_ref[0], pad_scr, cw_ref[...], cb_ref[...], S, C, (cw_ref.shape[0] - 1) // 2)
    o_ref[0] = y
    ob_ref[0] = y.astype(BF16)


def _dft_table_kernel(c_ref, s_ref, st_ref, *, L, TF):
    i = pl.program_id(0)
    N = 2 * L
    f = lax.broadcasted_iota(jnp.int32, (TF, L), 0) + i * TF
    t = lax.broadcasted_iota(jnp.int32, (TF, L), 1)
    ang = jnp.bitwise_and(f * t, N - 1).astype(F32) * (2.0 * math.pi / N)
    c_ref[...] = jnp.cos(ang).astype(BF16)
    nyq_t = (1 - 2 * jnp.bitwise_and(t, 1)).astype(F32)
    s_ref[...] = jnp.where(f == 0, nyq_t, jnp.sin(ang)).astype(BF16)
    nyq_f = (1 - 2 * jnp.bitwise_and(f, 1)).astype(F32)
    st_ref[...] = jnp.where(t == 0, nyq_f, jnp.sin(ang)).astype(BF16)


def _dft_tables(L):
    TF = min(L, 256)
    shp = jax.ShapeDtypeStruct((L, L), BF16)
    spec = pl.BlockSpec((TF, L), lambda i: (i, 0))
    return pl.pallas_call(
        functools.partial(_dft_table_kernel, L=L, TF=TF), grid=(L // TF,),
        in_specs=[], out_specs=[spec, spec, spec], out_shape=[shp, shp, shp],
        compiler_params=_cp(1), name="dft_tables",
    )()


def _hy_filter_kernel(z_ref, w1_ref, b1_ref, w2_ref, b2_ref, w3_ref, b3_ref, fq_ref,
                      w4f_ref, w4b_ref, df_ref, db_ref, tn_ref, kp_ref, km_ref):
    fq = fq_ref[...]

    def lin(h, w_ref, b_ref):
        return jnp.dot(h, w_ref[...], precision=HIGHEST, preferred_element_type=F32) + b_ref[...]

    h = jnp.sin(fq * lin(z_ref[...], w1_ref, b1_ref))
    h = jnp.sin(fq * lin(h, w2_ref, b2_ref))
    h = jnp.sin(fq * lin(h, w3_ref, b3_ref))
    tn = tn_ref[...]
    hf = jnp.dot(h, w4f_ref[...], precision=HIGHEST, preferred_element_type=F32)
    hf = hf * jnp.exp(-tn * jnp.abs(df_ref[...]))
    hb = jnp.dot(h, w4b_ref[...], precision=HIGHEST, preferred_element_type=F32)
    hb = hb * jnp.exp(-tn * jnp.abs(db_ref[...]))
    row = lax.broadcasted_iota(jnp.int32, hb.shape, 0)
    hb = jnp.where(row == 0, 0.0, hb)
    nrm = lax.rsqrt(jnp.sum(hf * hf, axis=0, keepdims=True) + jnp.sum(hb * hb, axis=0, keepdims=True) + 1e-6)
    hf = hf * nrm
    hb = hb * nrm
    kp_ref[...] = (hf + hb).astype(BF16)
    km_ref[...] = (hf - hb).astype(BF16)


def _hy_spectrum_kernel(c_ref, s_ref, s0_ref, kp_ref, km_ref, ka_ref, kb_ref, kc_ref, *, L, TF):
    i = pl.program_id(0)
    inv_n = 1.0 / (2 * L)
    kr = jnp.dot(c_ref[...], kp_ref[...], preferred_element_type=F32)
    ks = jnp.dot(s_ref[...], km_ref[...], preferred_element_type=F32)
    nyq = jnp.dot(s0_ref[...], kp_ref[...], preferred_element_type=F32)[0:1]
    f = lax.broadcasted_iota(jnp.int32, kr.shape, 0) + i * TF
    dc = f == 0
    ka_ref[...] = jnp.where(dc, kr * inv_n, kr * (2.0 * inv_n))
    kb_ref[...] = jnp.where(dc, 0.0, ks * (-2.0 * inv_n))
    kc_ref[...] = jnp.where(dc, nyq * inv_n, kr * (2.0 * inv_n))


def _hy_filters(L, tabs, fw1, fb1, fw2, fb2, fw3, fb3, fw4, ffreq, fdecay, D):
    cm, sm, _ = tabs
    E = fw1.shape[0]
    Hd = fw1.shape[1]
    bands = (E - 1) // 2
    t = jnp.arange(L, dtype=F32)
    t_norm = t / max(L - 1, 1)
    fr = jnp.linspace(1e-4, bands - 1, bands, dtype=F32)
    ang = (2.0 * math.pi / L) * t[:, None] * fr[None, :]
    z = jnp.concatenate([t_norm[:, None], jnp.cos(ang), -jnp.sin(ang)], -1)
    Ep, Hp = -(-E // LANES) * LANES, -(-Hd // LANES) * LANES
    z = jnp.pad(z, ((0, 0), (0, Ep - E)))
    fw1 = jnp.pad(fw1, ((0, Ep - E), (0, Hp - Hd)))
    fw2 = jnp.pad(fw2, ((0, Hp - Hd), (0, Hp - Hd)))
    fw3 = jnp.pad(fw3, ((0, Hp - Hd), (0, Hp - Hd)))
    fw4 = jnp.pad(fw4, ((0, Hp - Hd), (0, 0)))
    fb1, fb2, fb3, ffreq = (jnp.pad(v, (0, Hp - Hd)) for v in (fb1, fb2, fb3, ffreq))
    E, Hd = Ep, Hp
    CT = min(2 * D, 512)
    nct = 2 * D // CT
    dec = fdecay.reshape(1, 4 * D)
    kp, km = pl.pallas_call(
        _hy_filter_kernel, grid=(nct,),
        in_specs=[_full_spec((L, E)), _full_spec((E, Hd)), _full_spec((1, Hd)), _full_spec((Hd, Hd)),
                  _full_spec((1, Hd)), _full_spec((Hd, Hd)), _full_spec((1, Hd)), _full_spec((1, Hd)),
                  pl.BlockSpec((Hd, CT), lambda j: (0, j)), pl.BlockSpec((Hd, CT), lambda j: (0, j + nct)),
                  pl.BlockSpec((1, CT), lambda j: (0, j)), pl.BlockSpec((1, CT), lambda j: (0, j + nct)),
                  _full_spec((L, 1))],
        out_specs=[pl.BlockSpec((L, CT), lambda j: (0, j))] * 2,
        out_shape=[jax.ShapeDtypeStruct((L, 2 * D), BF16)] * 2,
        compiler_params=_cp(1), name="hy_filter",
    )(z, fw1, fb1.reshape(1, Hd), fw2, fb2.reshape(1, Hd), fw3, fb3.reshape(1, Hd), ffreq.reshape(1, Hd),
      fw4, fw4, dec, dec, t_norm[:, None])
    TF = min(L, 256)
    spec_w = pl.BlockSpec((TF, L), lambda i, j: (i, 0))
    spec_k = pl.BlockSpec((L, CT), lambda i, j: (0, j))
    spec_o = pl.BlockSpec((TF, CT), lambda i, j: (i, j))
    shp = jax.ShapeDtypeStruct((L, 2 * D), F32)
    return pl.pallas_call(
        functools.partial(_hy_spectrum_kernel, L=L, TF=TF), grid=(L // TF, nct),
        in_specs=[spec_w, spec_w, pl.BlockSpec((SUBLANES, L), lambda i, j: (0, 0)), spec_k, spec_k],
        out_specs=[spec_o] * 3, out_shape=[shp] * 3,
        compiler_params=_cp(2), name="hy_spectrum",
    )(cm, sm, sm, kp, km)


def _hy_fwd_kernel(z_ref, c_ref, s_ref, ka_ref, kb_ref, kc_ref, p_ref):
    z = z_ref[0]
    zr = jnp.dot(c_ref[...], z, preferred_element_type=F32)
    zs = jnp.dot(s_ref[...], z, preferred_element_type=F32)
    kb = kb_ref[...]
    p_ref[0, 0] = (zr * ka_ref[...] + zs * kb).astype(BF16)
    p_ref[0, 1] = (zs * kc_ref[...] - zr * kb).astype(BF16)


def _hy_inv_kernel(p_ref, c_ref, st_ref, z_ref, x_ref, fb_ref, o_ref):
    y = jnp.dot(c_ref[...], p_ref[0, 0], preferred_element_type=F32)
    y = y + jnp.dot(st_ref[...], p_ref[0, 1], preferred_element_type=F32)
    o_ref[0] = (x_ref[0] * (y + z_ref[0].astype(F32) * fb_ref[...])).astype(o_ref.dtype)


def _hy_inv_kernel_alias(p_ref, c_ref, st_ref, z_ref, x_ref, fb_ref, prev_ref, o_ref):
    del prev_ref
    _hy_inv_kernel(p_ref, c_ref, st_ref, z_ref, x_ref, fb_ref, o_ref)


def _hy_conv(geo, L, off, tabs, z, z_col, kfilt, k_col, xmul, x_col, fbias, out):
    B, P, D = geo.B, geo.P, geo.D
    cm, sm, smt = tabs
    ka, kb, kc = kfilt
    rb = off // L
    TF = min(L, 256)
    spec_w = pl.BlockSpec((TF, L), lambda b, i: (i, 0))
    spec_k = pl.BlockSpec((TF, D), lambda b, i: (i, k_col))
    p = pl.pallas_call(
        _hy_fwd_kernel, grid=(B, L // TF),
        in_specs=[pl.BlockSpec((1, L, D), lambda b, i: (b, rb, z_col)), spec_w, spec_w,
                  spec_k, spec_k, spec_k],
        out_specs=pl.BlockSpec((1, 2, TF, D), lambda b, i: (b, 0, i, 0)),
        out_shape=jax.ShapeDtypeStruct((B, 2, L, D), BF16),
        compiler_params=_cp(2), name="hy_fwd",
    )(z, cm, sm, ka, kb, kc)
    CT = min(D, 512)
    nct = D // CT
    rt = off // TF
    spec_wi = pl.BlockSpec((TF, L), lambda b, j, i: (i, 0))
    return pl.pallas_call(
        _hy_inv_kernel_alias, grid=(B, nct, L // TF),
        in_specs=[pl.BlockSpec((1, 2, L, CT), lambda b, j, i: (b, 0, 0, j)), spec_wi, spec_wi,
                  pl.BlockSpec((1, TF, CT), lambda b, j, i: (b, rt + i, z_col * nct + j)),
                  pl.BlockSpec((1, TF, CT), lambda b, j, i: (b, rt + i, x_col * nct + j)),
                  pl.BlockSpec((1, CT), lambda b, j, i: (0, k_col * nct + j)),
                  pl.BlockSpec(memory_space=pl.ANY)],
        out_specs=pl.BlockSpec((1, TF, CT), lambda b, j, i: (b, rt + i, j)),
        out_shape=jax.ShapeDtypeStruct((B, P, D), BF16),
        input_output_aliases={6: 0},
        compiler_params=_cp(3), name="hy_inv",
    )(p, cm, smt, z, xmul, fbias, out)


def _hyena_mixer(geo, xs, modl, w_in, b_in, short_w, short_b, fw1, fb1, fw2, fb2, fw3, fb3, fw4,
                 ffreq, fdecay, fbias):
    B, P, D, S, C = geo.B, geo.P, geo.D, geo.S, geo.C
    u0 = pl.pallas_call(
        _mm_bias_kernel, grid=(B, geo.nt, 3),
        in_specs=[geo.row_spec(D), geo.mod_spec(0), geo.mod_spec(1),
                  pl.BlockSpec((D, D), lambda b, i, j: (0, j)), pl.BlockSpec((1, D), lambda b, i, j: (0, j))],
        out_specs=pl.BlockSpec((1, geo.TM, D), lambda b, i, j: (b, i, j)),
        out_shape=jax.ShapeDtypeStruct((B, P, 3 * D), F32),
        compiler_params=_cp(3), name="hy_in",
    )(xs, modl, modl, w_in.astype(BF16), b_in.reshape(1, 3 * D))
    CT = min(D, 256)
    Ks = short_w.shape[0]
    spec = pl.BlockSpec((1, P, CT), lambda b, j: (b, 0, j))
    u, ub = pl.pallas_call(
        functools.partial(_short_conv_kernel, S=S, C=C), grid=(B, 3 * D // CT),
        in_specs=[spec, pl.BlockSpec((Ks, CT), lambda b, j: (0, j)), pl.BlockSpec((1, CT), lambda b, j: (0, j))],
        out_specs=[spec, spec],
        out_shape=[jax.ShapeDtypeStruct((B, P, 3 * D), F32), jax.ShapeDtypeStruct((B, P, 3 * D), BF16)],
        scratch_shapes=[pltpu.VMEM((P + 2 * SUBLANES, CT), F32)],
        compiler_params=_cp(2), name="hy_short",
    )(u0, short_w, short_b.reshape(1, 3 * D))
    fb = fbias.reshape(1, 2 * D)
    z1 = jnp.zeros((B, P, D), BF16)
    z2 = jnp.zeros((B, P, D), BF16)
    segs = [(S, 0), (C, S)]
    convs = []
    for L, off in segs:
        tabs = _dft_tables(L)
        kf = _hy_filters(L, tabs, fw1, fb1, fw2, fb2, fw3, fb3, fw4, ffreq, fdecay, D)
        convs.append((L, off, tabs, kf))
    for L, off, tabs, kf in convs:
        z1 = _hy_conv(geo, L, off, tabs, ub, 0, kf, 0, u, 1, fb, z1)
    for L, off, tabs, kf in convs:
        z2 = _hy_conv(geo, L, off, tabs, z1, 0, kf, 1, u, 2, fb, z2)
    return z2


def _rope_tables(S, D):
    t = jnp.arange(S)
    row = (t // GRID_W).astype(F32)
    col = (t % GRID_W).astype(F32)
    axis_dim = DA_HEAD_DIM // 2
    half = axis_dim // 2
    inv = ROPE_THETA ** (-jnp.arange(0, axis_dim, 2, dtype=F32) / axis_dim)
    lane = jnp.arange(D)
    within = lane % DA_HEAD_DIM
    pos = jnp.where((within // axis_dim)[None, :] == 0, row[:, None], col[:, None])
    ang = pos * inv[lane % half][None, :]
    sign = jnp.where((lane % axis_dim) < half, -1.0, 1.0)[None, :]
    return jnp.cos(ang), jnp.sin(ang) * sign


def _da_in_kernel(x_ref, sh_ref, sc_ref, w_ref, cos_ref, sin_ref, o_ref, *, n_lat):
    i = pl.program_id(1)
    j = pl.program_id(2)
    h = (x_ref[0] * (1.0 + sc_ref[...]) + sh_ref[...]).astype(BF16)
    acc = jnp.dot(h, w_ref[...], preferred_element_type=F32)
    acc = acc * jnp.where(j == 0, DA_HEAD_DIM ** -0.5, 1.0)
    rot = jnp.logical_and(i < n_lat, j < 2)

    @pl.when(rot)
    def _():
        Dn = acc.shape[-1]
        half = DA_HEAD_DIM // 4
        lane = lax.broadcasted_iota(jnp.int32, acc.shape, 1)
        up = pltpu.roll(acc, Dn - half, 1)
        dn = pltpu.roll(acc, half, 1)
        partner = jnp.where(jnp.bitwise_and(lane, 2 * half - 1) < half, up, dn)
        o_ref[0] = (acc * cos_ref[...] + partner * sin_ref[...]).astype(o_ref.dtype)

    @pl.when(jnp.logical_not(rot))
    def _():
        o_ref[0] = acc.astype(o_ref.dtype)


def _da_attn_kernel(*refs, kv_lo, nk, lam_init, aliased):
    if aliased:
        q_ref, k_ref, v_ref, lam_ref, sub_ref, _, o_ref, vx_scr, s_scr = refs
    else:
        q_ref, k_ref, v_ref, lam_ref, sub_ref, o_ref, vx_scr, s_scr = refs
    i = pl.program_id(2)
    HW = v_ref.shape[-1]
    TQ = q_ref.shape[1]
    lp = lam_ref[...]
    lam = (jnp.exp(jnp.sum(lp[0:1] * lp[1:2], axis=1, keepdims=True))
           - jnp.exp(jnp.sum(lp[2:3] * lp[3:4], axis=1, keepdims=True)) + lam_init)

    @pl.when(i == 0)
    def _():
        vx_scr[:, :HW] = v_ref[0]
        vx_scr[:, HW:] = jnp.ones((vx_scr.shape[0], HW), BF16)

    lane = lax.broadcasted_iota(jnp.int32, (TQ, HW), 1)
    q = q_ref[0]
    k = k_ref[0, kv_lo:kv_lo + nk, :]
    outs = []
    for c in range(2):
        qc = jnp.where((lane // DA_HEAD_DIM) == c, q, jnp.zeros_like(q))
        s_scr[...] = lax.dot_general(qc, k, (((1,), (1,)), ((), ())), preferred_element_type=F32)
        m = jnp.max(s_scr[...], axis=-1, keepdims=True)
        p = jnp.exp((s_scr[...] - m).astype(BF16))
        ov = jnp.dot(p, vx_scr[kv_lo:kv_lo + nk, :], preferred_element_type=F32)
        outs.append(ov[:, :HW] / ov[:, HW:HW + 1])
    o = outs[0] - lam * outs[1]
    o = o * lax.rsqrt(jnp.mean(o * o, axis=-1, keepdims=True) + 1e-5) * sub_ref[...] * (1.0 - lam_init)
    o_ref[0] = o.astype(o_ref.dtype)


def _diff_attention_mixer(geo, xs, modl, w_in, lam_p, subln_w, layer_idx):
    B, P, D, S, C = geo.B, geo.P, geo.D, geo.S, geo.C
    H = D // (2 * DA_HEAD_DIM)
    HW = 2 * DA_HEAD_DIM
    cos_t, sin_t = _rope_tables(S, D)
    n_lat = geo.n_lat
    tab_spec = pl.BlockSpec((geo.TM, D), lambda b, i, j: (jnp.minimum(i, n_lat - 1), 0))
    qkv = pl.pallas_call(
        functools.partial(_da_in_kernel, n_lat=n_lat), grid=(B, geo.nt, 3),
        in_specs=[geo.row_spec(D), geo.mod_spec(0), geo.mod_spec(1),
                  pl.BlockSpec((D, D), lambda b, i, j: (0, j)), tab_spec, tab_spec],
        out_specs=pl.BlockSpec((1, geo.TM, D), lambda b, i, j: (b, i, j)),
        out_shape=jax.ShapeDtypeStruct((B, P, 3 * D), BF16),
        compiler_params=_cp(3), name="da_in",
    )(xs, modl, modl, w_in.astype(BF16), cos_t, sin_t)
    lam_init = 0.8 - 0.6 * math.exp(-0.3 * layer_idx)
    def attend(TQ, row0, n_tiles, kv_lo, nk, prev):
        rb = row0 // TQ
        ins = [qkv, qkv, qkv, lam_p, subln_w.reshape(1, HW)]
        specs = [pl.BlockSpec((1, TQ, HW), lambda b, h, i: (b, rb + i, h)),
                 pl.BlockSpec((1, P, HW), lambda b, h, i: (b, 0, H + h)),
                 pl.BlockSpec((1, P, HW), lambda b, h, i: (b, 0, 2 * H + h)),
                 _full_spec((4, DA_HEAD_DIM)), _full_spec((1, HW))]
        if prev is not None:
            ins.append(prev)
            specs.append(pl.BlockSpec(memory_space=pl.ANY))
        return pl.pallas_call(
            functools.partial(_da_attn_kernel, kv_lo=kv_lo, nk=nk, lam_init=lam_init, aliased=prev is not None),
            grid=(B, H, n_tiles), in_specs=specs,
            out_specs=pl.BlockSpec((1, TQ, HW), lambda b, h, i: (b, rb + i, h)),
            out_shape=jax.ShapeDtypeStruct((B, P, D), BF16),
            scratch_shapes=[pltpu.VMEM((P, 2 * HW), BF16), pltpu.VMEM((TQ, nk), F32)],
            input_output_aliases={} if prev is None else {5: 0},
            compiler_params=_cp(3), name="da_attn",
        )(*ins)

    TQ = geo.TM
    while TQ < DA_Q_TILE and S % (2 * TQ) == 0:
        TQ *= 2
    y = attend(TQ, 0, S // TQ, 0, P, None)
    return attend(geo.TM, S, C // geo.TM, S, C, y)


def _s5_operators(a_re, a_im, log_step, b_re, b_im, c_re, c_im):
    T = S5_CHUNK
    G, Pst = a_re.shape[1], a_re.shape[2]
    Hg = S5_GROUP
    GL = LANES // Hg
    LB = G // GL
    lam = lax.complex(jnp.minimum(a_re.astype(F32), -1e-4), a_im.astype(F32))
    step = jnp.exp(log_step.astype(F32))[..., None]
    abar = jnp.exp(lam * step)
    bbar = ((abar - 1.0) / lam)[..., None] * lax.complex(b_re.astype(F32), b_im.astype(F32))
    cmat = lax.complex(c_re.astype(F32), c_im.astype(F32))
    pows = jnp.stack([abar ** l for l in range(T + 1)], axis=1)
    eye = jnp.eye(GL, dtype=F32)
    ar = jnp.arange(T)
    big_m, big_g, big_h, a_t = [], [], [], []
    for d in range(2):
        pw = pows[d]
        kl = jnp.einsum('gjp,lgp,gph->lgjh', cmat[d], pw[:T], bbar[d]).real
        lag = (ar[None, :] - ar[:, None]) if d == 0 else (ar[:, None] - ar[None, :])
        tz = jnp.where((lag >= 0)[:, :, None, None, None], kl[jnp.clip(lag, 0, T - 1)], 0.0)
        tz = tz.reshape(T, T, LB, GL, Hg, Hg).astype(BF16)
        m = jnp.einsum('stbgjh,gk->bsghtkj', tz, eye.astype(BF16)).reshape(LB, T * LANES, T * LANES)
        e_in = (T - 1 - ar) if d == 0 else ar
        gc = pw[e_in][:, :, :, None] * bbar[d][None]
        gc = gc.reshape(T, LB, GL, Pst, Hg)
        g_re = jnp.einsum('sbgph,gk->bsghkp', gc.real, eye).reshape(LB, T * LANES, GL * Pst)
        g_im = jnp.einsum('sbgph,gk->bsghkp', gc.imag, eye).reshape(LB, T * LANES, GL * Pst)
        e_out = (ar + 1) if d == 0 else (T - ar)
        hc = cmat[d][None] * pw[e_out][:, :, None, :]
        hc = hc.reshape(T, LB, GL, Hg, Pst)
        h_re = jnp.einsum('tbgjp,gk->bgptkj', hc.real, eye).reshape(LB, GL * Pst, T * LANES)
        h_im = jnp.einsum('tbgjp,gk->bgptkj', -hc.imag, eye).reshape(LB, GL * Pst, T * LANES)
        big_m.append(m)
        big_g.append(jnp.concatenate([g_re, g_im], axis=2))
        big_h.append(jnp.concatenate([h_re, h_im], axis=1))
        at = pw[T].reshape(LB, 1, GL * Pst)
        a_t.append(jnp.concatenate([at.real, at.imag], axis=2))
    return (jnp.stack(big_m).astype(BF16), jnp.stack(big_g).astype(BF16),
            jnp.stack(big_h).astype(BF16), jnp.stack(a_t).astype(F32))


def _modulate_kernel(x_ref, sh_ref, sc_ref, o_ref):
    o_ref[0] = x_ref[0] * (1.0 + sc_ref[...]) + sh_ref[...]


def _s5_kernel(u_ref, m_ref, g_ref, h_ref, a_ref, o_ref, gx_scr, sp_scr, *, S, C):
    d = pl.program_id(1)
    T = S5_CHUNK
    P = S + C
    n = P // T
    n_lat = S // T
    NB = u_ref.shape[0]
    x = jnp.concatenate(
        [jnp.concatenate([u_ref[bi, pl.ds(s, n, stride=T), :] for s in range(T)], axis=1) for bi in range(NB)],
        axis=0).astype(BF16)
    gx_scr[...] = jnp.dot(x, g_ref[...], preferred_element_type=F32)
    ns = a_ref.shape[-1] // 2
    a_r = a_ref[:, :ns]
    a_i = a_ref[:, ns:]

    def scan(lo, cnt, rev, carry):
        def body(k, st):
            c = (lo + cnt - 1 - k) if rev else (lo + k)
            new = []
            for bi in range(NB):
                s_r, s_i = st[2 * bi], st[2 * bi + 1]
                sp_scr[pl.ds(bi * n + c, 1), :] = jnp.concatenate([s_r, s_i], axis=1)
                gx = gx_scr[pl.ds(bi * n + c, 1), :]
                new += [a_r * s_r - a_i * s_i + gx[:, :ns], a_r * s_i + a_i * s_r + gx[:, ns:]]
            return tuple(new)
        return lax.fori_loop(0, cnt, body, carry)

    zero = tuple(jnp.zeros((1, ns), F32) for _ in range(2 * NB))

    @pl.when(d == 0)
    def _():
        scan(0, n_lat, False, scan(n_lat, n - n_lat, False, zero))

    @pl.when(d == 1)
    def _():
        scan(0, n_lat, True, scan(n_lat, n - n_lat, True, zero))

    y = jnp.dot(x, m_ref[...], preferred_element_type=F32)
    y = y + jnp.dot(sp_scr[...].astype(BF16), h_ref[...], preferred_element_type=F32)
    for bi in range(NB):
        for s in range(T):
            o_ref[bi, pl.ds(s, n, stride=T), :] = y[bi * n:(bi + 1) * n, s * LANES:(s + 1) * LANES]


def _s5_mixer(geo, xs, modl, a_re, a_im, log_step, b_re, b_im, c_re, c_im):
    B, P, D, S, C = geo.B, geo.P, geo.D, geo.S, geo.C
    u = pl.pallas_call(
        _modulate_kernel, grid=(B, geo.nt),
        in_specs=[geo.row_spec(D), geo.mod_spec(0), geo.mod_spec(1)],
        out_specs=geo.row_spec(D), out_shape=jax.ShapeDtypeStruct((B, P, D), F32),
        compiler_params=_cp(2), name="s5_modulate",
    )(xs, modl, modl)
    big_m, big_g, big_h, a_t = _s5_operators(a_re, a_im, log_step, b_re, b_im, c_re, c_im)
    LB = D // LANES
    TL = S5_CHUNK * LANES
    NS = big_g.shape[-1]
    n = P // S5_CHUNK
    NB = 2 if B % 2 == 0 else 1
    once = pl.Buffered(1)
    y = pl.pallas_call(
        functools.partial(_s5_kernel, S=S, C=C), grid=(LB, 2, B // NB),
        in_specs=[pl.BlockSpec((NB, P, LANES), lambda l, d, b: (b, 0, l)),
                  pl.BlockSpec((None, None, TL, TL), lambda l, d, b: (d, l, 0, 0), pipeline_mode=once),
                  pl.BlockSpec((None, None, TL, NS), lambda l, d, b: (d, l, 0, 0), pipeline_mode=once),
                  pl.BlockSpec((None, None, NS, TL), lambda l, d, b: (d, l, 0, 0), pipeline_mode=once),
                  pl.BlockSpec((None, None, 1, NS), lambda l, d, b: (d, l, 0, 0))],
        out_specs=pl.BlockSpec((None, NB, P, LANES), lambda l, d, b: (d, b, 0, l)),
        out_shape=jax.ShapeDtypeStruct((2, B, P, D), F32),
        scratch_shapes=[pltpu.VMEM((NB * n, NS), F32), pltpu.VMEM((NB * n, NS), F32)],
        compiler_params=_cp(3), name="s5_scan",
    )(u, big_m, big_g, big_h, a_t)
    return u, y


def _s5_post_kernel(yf_ref, yb_ref, u_ref, d_ref, w_ref, b_ref, x_ref, gate_ref, lng_ref, lnb_ref,
                    sh_ref, sc_ref, wr_ref, xm_ref, f_ref, s_ref, *, alpha):
    g = jax.nn.gelu(yf_ref[0] + yb_ref[0] + d_ref[...] * u_ref[0])
    vg = jnp.dot(g.astype(BF16), w_ref[...], preferred_element_type=F32) + b_ref[...]
    Dn = vg.shape[-1] // 2
    y = vg[:, :Dn] * jax.nn.sigmoid(vg[:, Dn:])
    _finish_mixer(y, x_ref[0], gate_ref[...], lng_ref[...], lnb_ref[...], sh_ref[...], sc_ref[...],
                  wr_ref, xm_ref, f_ref, s_ref, alpha)


def _s5_post(geo, y2, u, d_skip, w_glu, b_glu, xs, modl, lng, lnb, w_router, alpha):
    D = geo.D
    E = w_router.shape[-1]
    out_specs, out_shape = _post_outs(geo, E)
    TM = geo.TM
    return pl.pallas_call(
        functools.partial(_s5_post_kernel, alpha=alpha), grid=(geo.B, geo.nt),
        in_specs=[pl.BlockSpec((None, 1, TM, D), lambda b, i: (0, b, i, 0)),
                  pl.BlockSpec((None, 1, TM, D), lambda b, i: (1, b, i, 0)),
                  geo.row_spec(D), _full_spec((1, D)), _full_spec((D, 2 * D)), _full_spec((1, 2 * D)),
                  geo.row_spec(D), geo.mod_spec(2), _full_spec((1, D)), _full_spec((1, D)),
                  geo.mod_spec(3), geo.mod_spec(4), _full_spec((E, D))],
        out_specs=out_specs, out_shape=out_shape,
        compiler_params=_cp(2), name="s5_post",
    )(y2, y2, u, d_skip.reshape(1, D), w_glu.astype(BF16), b_glu.reshape(1, 2 * D), xs, modl,
      lng.reshape(1, D), lnb.reshape(1, D), modl, modl, w_router.T)


def _route_kernel(s_ref, b_ref, idx_ref, w_ref):
    sc = s_ref[0]
    E, TM = sc.shape
    biased = sc + b_ref[...]
    G = N_EXPERT_GROUPS
    per = E // G
    neg = -jnp.inf
    blocks, gs = [], []
    for g in range(G):
        blk = biased[g * per:(g + 1) * per]
        m1 = jnp.max(blk, axis=0, keepdims=True)
        is1 = blk == m1
        cnt = jnp.sum(is1.astype(F32), axis=0, keepdims=True)
        m2 = jnp.max(jnp.where(is1, neg, blk), axis=0, keepdims=True)
        blocks.append(blk)
        gs.append(m1 + jnp.where(cnt >= 2.0, m1, m2))
    masked = []
    for g in range(G):
        ahead = jnp.zeros((1, TM), F32)
        for h in range(G):
            if h < g:
                ahead = ahead + (gs[h] >= gs[g]).astype(F32)
            elif h > g:
                ahead = ahead + (gs[h] > gs[g]).astype(F32)
        masked.append(jnp.where(ahead < float(TOPK_GROUPS), blocks[g], neg))
    masked = jnp.concatenate(masked, axis=0)
    iota_e = lax.broadcasted_iota(jnp.int32, (E, TM), 0)
    idxs, ws = [], []
    for _ in range(TOP_K):
        m = jnp.max(masked, axis=0, keepdims=True)
        ik = jnp.min(jnp.where(masked == m, iota_e, E), axis=0, keepdims=True)
        sel = iota_e == ik
        ws.append(jnp.sum(jnp.where(sel, sc, 0.0), axis=0, keepdims=True))
        idxs.append(ik)
        masked = jnp.where(sel, neg, masked)
    tot = ws[0]
    for wk in ws[1:]:
        tot = tot + wk
    w = jnp.concatenate(ws, axis=0)
    idx_ref[0] = jnp.concatenate(idxs, axis=0)
    w_ref[0] = w / (tot + 1e-20) * ROUTED_SCALE


def _rank_kernel(idx_ref, rank_ref, cnt_ref, run_scr, *, E):
    first = jnp.logical_and(pl.program_id(0) == 0, pl.program_id(1) == 0)

    @pl.when(first)
    def _():
        run_scr[...] = jnp.zeros_like(run_scr)

    idx = idx_ref[0]
    K, TM = idx.shape
    iota_e = lax.broadcasted_iota(jnp.int32, (E, TM), 0)
    member = jnp.zeros((E, TM), F32)
    for k in range(K):
        member = member + (iota_e == idx[k:k + 1]).astype(F32)
    before = (lax.broadcasted_iota(jnp.int32, (TM, TM), 0)
              < lax.broadcasted_iota(jnp.int32, (TM, TM), 1)).astype(BF16)
    rank = jnp.dot(member.astype(BF16), before, preferred_element_type=F32) + run_scr[...]
    rows = [jnp.sum(jnp.where(iota_e == idx[k:k + 1], rank, 0.0), axis=0, keepdims=True) for k in range(K)]
    rank_ref[0] = jnp.concatenate(rows, axis=0).astype(jnp.int32)
    run_scr[...] = run_scr[...] + jnp.sum(member, axis=1, keepdims=True)
    cnt_ref[...] = run_scr[...]


def _dest_kernel(idx_ref, rank_ref, start_ref, dest_ref, *, E):
    idx = idx_ref[0]
    K, TM = idx.shape
    iota_e = lax.broadcasted_iota(jnp.int32, (E, TM), 0)
    start = start_ref[...]
    rows = [jnp.sum(jnp.where(iota_e == idx[k:k + 1], start, 0), axis=0, keepdims=True) for k in range(K)]
    dest_ref[0] = jnp.concatenate(rows, axis=0) + rank_ref[0]


def _route_dispatch(geo, scores, bias, blk):
    B, P = geo.B, geo.P
    E = scores.shape[1]
    K = TOP_K
    kspec = geo.col_spec(K)
    idx, w = pl.pallas_call(
        _route_kernel, grid=(B, geo.nt),
        in_specs=[geo.col_spec(E, whole=True), _full_spec((E, 1))],
        out_specs=[kspec, kspec],
        out_shape=[jax.ShapeDtypeStruct((B, K, P), jnp.int32), jax.ShapeDtypeStruct((B, K, P), F32)],
        compiler_params=_cp(2), name="moe_route",
    )(scores, bias.astype(F32).reshape(E, 1))
    rank, cnt = pl.pallas_call(
        functools.partial(_rank_kernel, E=E), grid=(B, geo.nt),
        in_specs=[kspec], out_specs=[kspec, _full_spec((E, 1))],
        out_shape=[jax.ShapeDtypeStruct((B, K, P), jnp.int32), jax.ShapeDtypeStruct((E, 1), F32)],
        scratch_shapes=[pltpu.VMEM((E, 1), F32)],
        compiler_params=_cp(2), name="moe_rank",
    )(idx)
    n_assign = B * P * K
    n_blocks = -(-(n_assign + E * (blk - 1)) // blk)
    counts = cnt[:, 0].astype(jnp.int32)
    pcounts = (counts + blk - 1) // blk * blk
    pends = jnp.cumsum(pcounts)
    starts = (pends - pcounts).astype(jnp.int32)
    n_used = pends[-1] // blk
    blk_e = jnp.minimum(jnp.searchsorted(pends, jnp.arange(n_blocks) * blk, side='right'), E - 1)
    blk_e = jnp.where(jnp.arange(n_blocks) < n_used, blk_e, blk_e[jnp.maximum(n_used - 1, 0)])
    dest = pl.pallas_call(
        functools.partial(_dest_kernel, E=E), grid=(B, geo.nt),
        in_specs=[kspec, kspec, _full_spec((E, 1))], out_specs=kspec,
        out_shape=jax.ShapeDtypeStruct((B, K, P), jnp.int32),
        compiler_params=_cp(2), name="moe_dest",
    )(idx, rank, starts.reshape(E, 1))
    return w, dest, blk_e.astype(jnp.int32), n_used.astype(jnp.int32).reshape(1), n_blocks


def _expert_kernel(be_ref, nu_ref, x_ref, wgu_ref, wd_ref, *rest):
    del be_ref
    o_ref = rest[-1]
    i = pl.program_id(0)

    @pl.when(i < nu_ref[0])
    def _():
        h = jnp.dot(x_ref[...], wgu_ref[...].astype(BF16), preferred_element_type=F32)
        Fh = h.shape[-1] // 2
        a = _silu(h[:, :Fh]) * h[:, Fh:]
        o_ref[...] = jnp.dot(a.astype(BF16), wd_ref[...].astype(BF16),
                             preferred_element_type=F32).astype(o_ref.dtype)

    @pl.when(i >= nu_ref[0])
    def _():
        o_ref[...] = jnp.zeros_like(o_ref)


def _expert_ffn(x_part, blk_e, n_used, w_gu, w_down, layer, blk, n_rows_all, blk0, prev):
    n_rows, D = x_part.shape
    F2 = w_gu.shape[-1]
    ins = [blk_e, n_used, x_part, w_gu, w_down]
    specs = [pl.BlockSpec((blk, D), lambda i, be, nu: (i, 0)),
             pl.BlockSpec((None, None, D, F2), lambda i, be, nu: (layer, be[i], 0, 0)),
             pl.BlockSpec((None, None, F2 // 2, D), lambda i, be, nu: (layer, be[i], 0, 0))]
    if prev is not None:
        ins.append(prev)
        specs.append(pl.BlockSpec(memory_space=pl.ANY))
    grid_spec = pltpu.PrefetchScalarGridSpec(
        num_scalar_prefetch=2, grid=(n_rows // blk,), in_specs=specs,
        out_specs=pl.BlockSpec((blk, D), lambda i, be, nu: (blk0 + i, 0)))
    return pl.pallas_call(
        _expert_kernel, grid_spec=grid_spec,
        out_shape=jax.ShapeDtypeStruct((n_rows_all, D), BF16),
        input_output_aliases={} if prev is None else {5: 0},
        compiler_params=_cp(1), name="moe_experts",
    )(*ins)


def _moe_final_kernel(*refs, alpha, aliased):
    if aliased:
        xm_ref, f_ref, ga_ref, w_ref, shgu_ref, shd_ref, gate_ref, lng_ref, lnb_ref, _, o_ref = refs
    else:
        xm_ref, f_ref, ga_ref, w_ref, shgu_ref, shd_ref, gate_ref, lng_ref, lnb_ref, o_ref = refs
    h = jnp.dot(f_ref[0], shgu_ref[...], preferred_element_type=F32)
    Fh = h.shape[-1] // 2
    a = _silu(h[:, :Fh]) * h[:, Fh:]
    y = jnp.dot(a.astype(BF16), shd_ref[...], preferred_element_type=F32)
    w = w_ref[0]
    for k in range(w.shape[-1]):
        y = y + w[:, k:k + 1] * ga_ref[k, 0].astype(F32)
    o_ref[0] = _layer_norm(alpha * xm_ref[0] + gate_ref[...] * y, lng_ref[...], lnb_ref[...])


def _moe_group(geo, xm, f, scores, modl, bias, w_gu, w_down, sh_gu, sh_down, lng, lnb, layer, alpha, blk, prev):
    B, P, D = geo.B, geo.P, geo.D
    T = B * P
    K = TOP_K
    w, dest, blk_e, n_used, n_blocks = _route_dispatch(geo, scores, bias, blk)
    dest_flat = jnp.swapaxes(dest, 0, 1).reshape(K * T)
    t0 = geo.b0 * P
    tok = jnp.broadcast_to(jnp.arange(t0, t0 + T, dtype=jnp.int32)[None], (K, T)).reshape(K * T)
    hit = jnp.zeros((n_blocks * blk,), jnp.int32).at[dest_flat].add(
        tok + 1, unique_indices=True, mode='promise_in_bounds')
    row_tok = jnp.where(hit > 0, hit - 1, t0 + jnp.arange(n_blocks * blk, dtype=jnp.int32) % T)
    f2d = f.reshape(geo.B_all * P, D)
    y_sorted = None
    for part in range(MOE_ROW_PARTS):
        lo = n_blocks * part // MOE_ROW_PARTS
        hi = n_blocks * (part + 1) // MOE_ROW_PARTS
        x_part = f2d.at[row_tok[lo * blk:hi * blk]].get(mode='promise_in_bounds')
        used = jnp.clip(n_used - lo, 0, hi - lo)
        y_sorted = _expert_ffn(x_part, blk_e[lo:hi], used, w_gu, w_down, layer, blk, n_blocks * blk, lo, y_sorted)
    gathered = y_sorted.at[dest_flat].get(unique_indices=True, mode='promise_in_bounds').reshape(K, B, P, D)
    F2 = sh_gu.shape[-1]
    TM = geo.TM
    ins = [xm, f, gathered, jnp.swapaxes(w, 1, 2), sh_gu.astype(BF16), sh_down.astype(BF16), modl,
           lng.reshape(1, D), lnb.reshape(1, D)]
    specs = [geo.row_spec(D, whole=True), geo.row_spec(D, whole=True),
             pl.BlockSpec((K, 1, TM, D), lambda b, i: (0, b, i, 0)), geo.row_spec(K),
             _full_spec((D, F2)), _full_spec((F2 // 2, D)), geo.mod_spec(5),
             _full_spec((1, D)), _full_spec((1, D))]
    if prev is not None:
        ins.append(prev)
        specs.append(pl.BlockSpec(memory_space=pl.ANY))
    return pl.pallas_call(
        functools.partial(_moe_final_kernel, alpha=alpha, aliased=prev is not None), grid=(B, geo.nt),
        in_specs=specs, out_specs=geo.row_spec(D, whole=True),
        out_shape=jax.ShapeDtypeStruct((geo.B_all, P, D), F32),
        input_output_aliases={} if prev is None else {9: 0},
        compiler_params=_cp(2), name="moe_final",
    )(*ins)


def _moe(geo, xm, f, scores, modl, bias, w_gu, w_down, sh_gu, sh_down, lng, lnb, layer, alpha, blk):
    n_groups = MOE_GROUPS if geo.B % MOE_GROUPS == 0 else 1
    nb = geo.B // n_groups
    out = None
    for g in range(n_groups):
        out = _moe_group(geo.group(g * nb, nb), xm, f, scores, modl, bias, w_gu, w_down, sh_gu, sh_down,
                         lng, lnb, layer, alpha, blk, out)
    return out


def kernel(x, c, ctx, c_ctx, mod_w, mod_b, ln_g, ln_b, rg_w_in, rg_conv_w, rg_conv_b, rg_gate_w, rg_gate_b, rg_lam, rg_w_out, hy_w_in, hy_b_in, hy_short_w, hy_short_b, hy_f_w1, hy_f_b1, hy_f_w2, hy_f_b2, hy_f_w3, hy_f_b3, hy_f_w4, hy_f_freq, hy_f_decay, hy_f_bias, hy_w_out, hy_b_out, da_w_in, da_lam, da_subln, da_w_out, s5_a_re, s5_a_im, s5_log_step, s5_b_re, s5_b_im, s5_c_re, s5_c_im, s5_d, s5_w_glu, s5_b_glu, moe_w_router, moe_bias, moe_w_gu, moe_w_down, moe_sh_gu, moe_sh_down):
    B, S, D = x.shape
    C = ctx.shape[1]
    depth = mod_w.shape[0]
    alpha = (2 * depth) ** 0.25
    geo = _Geo(B, S, C, D)
    xs = jnp.concatenate([x, ctx], axis=1)
    R = -(-(B + 1) // SUBLANES) * SUBLANES
    cc = jnp.zeros((R, D), F32).at[:B].set(c).at[B].set(c_ctx)
    modt = _mod_table(cc, mod_w, mod_b).reshape(depth, 6, R, 1, D)
    blk = MOE_BLOCK
    for i in range(depth):
        kind, j = i % N_MIXERS, i // N_MIXERS
        modl = modt[i]
        post = functools.partial(_post_mixer, geo, xs=xs, modl=modl, lng=ln_g[i, 0], lnb=ln_b[i, 0],
                                 w_router=moe_w_router[i], alpha=alpha)
        if kind == 0:
            y = _rglru_mixer(geo, xs, modl, rg_w_in[j], rg_conv_w[j], rg_conv_b[j], rg_gate_w[j],
                             rg_gate_b[j], rg_lam[j])
            xm, f, scores = post(y=y, w_out=rg_w_out[j], b_out=None)
        elif kind == 1:
            y = _hyena_mixer(geo, xs, modl, hy_w_in[j], hy_b_in[j], hy_short_w[j], hy_short_b[j],
                             hy_f_w1[j], hy_f_b1[j], hy_f_w2[j], hy_f_b2[j], hy_f_w3[j], hy_f_b3[j],
                             hy_f_w4[j], hy_f_freq[j], hy_f_decay[j], hy_f_bias[j])
            xm, f, scores = post(y=y, w_out=hy_w_out[j], b_out=hy_b_out[j])
        elif kind == 2:
            y = _diff_attention_mixer(geo, xs, modl, da_w_in[j], da_lam[j], da_subln[j], i)
            xm, f, scores = post(y=y, w_out=da_w_out[j], b_out=None)
        else:
            u, y2 = _s5_mixer(geo, xs, modl, s5_a_re[j], s5_a_im[j], s5_log_step[j], s5_b_re[j],
                              s5_b_im[j], s5_c_re[j], s5_c_im[j])
            xm, f, scores = _s5_post(geo, y2, u, s5_d[j], s5_w_glu[j], s5_b_glu[j], xs, modl,
                                     ln_g[i, 0], ln_b[i, 0], moe_w_router[i], alpha)
        xs = _moe(geo, xm, f, scores, modl, moe_bias[i], moe_w_gu, moe_w_down, moe_sh_gu[i],
                  moe_sh_down[i], ln_g[i, 1], ln_b[i, 1], i, alpha, blk)
    return xs[:, :S]
```

```python
import functools
import math

import jax
import jax.numpy as jnp
from jax import lax
from jax.experimental import pallas as pl
from jax.experimental.pallas import tpu as pltpu

F32 = jnp.float32
BF16 = jnp.bfloat16
HIGHEST = lax.Precision.HIGHEST

N_MIXERS = 4
LN_EPS = 1e-6
LRU_C = 8.0
RG_BLOCK = 128
GRID_W = 64
DA_HEAD_DIM = 64
ROPE_THETA = 10000.0
S5_GROUP = 16
S5_CHUNK = 16
TOP_K = 8
N_EXPERT_GROUPS = 8
TOPK_GROUPS = 4
ROUTED_SCALE = 2.5
MOE_BLOCK = 512
MOE_GROUPS = 1
MOE_ROW_PARTS = 4
DA_Q_TILE = 1024

LANES = 128
SUBLANES = 8
VMEM_LIMIT = 56 * 1024 * 1024


def _cp(n_grid):
    return pltpu.CompilerParams(dimension_semantics=("arbitrary",) * n_grid,
                                vmem_limit_bytes=VMEM_LIMIT)


def _silu(x):
    return x * jax.nn.sigmoid(x)


def _mod_table_kernel(c_ref, w_ref, b_ref, o_ref):
    s = _silu(c_ref[...])
    o_ref[...] = jnp.dot(s, w_ref[...], precision=HIGHEST, preferred_element_type=F32) + b_ref[...]


def _mod_table(cc, mod_w, mod_b):
    depth, D, _ = mod_w.shape
    R = cc.shape[0]
    return pl.pallas_call(
        _mod_table_kernel,
        grid=(depth, 6),
        in_specs=[pl.BlockSpec((R, D), lambda i, k: (0, 0)),
                  pl.BlockSpec((None, D, D), lambda i, k: (i, 0, k)),
                  pl.BlockSpec((None, None, 1, D), lambda i, k: (i, k, 0, 0))],
        out_specs=pl.BlockSpec((None, None, R, D), lambda i, k: (i, k, 0, 0)),
        out_shape=jax.ShapeDtypeStruct((depth, 6, R, D), F32),
        compiler_params=_cp(2), name="mod_table",
    )(cc, mod_w, mod_b.reshape(depth, 6, 1, D))


class _Geo:
    def __init__(self, B, S, C, D, b0=0, B_all=None):
        self.B, self.S, self.C, self.D = B, S, C, D
        self.b0 = b0
        self.B_all = B if B_all is None else B_all
        self.P = S + C
        self.TM = math.gcd(S, C)
        while self.TM > 256:
            self.TM //= 2
        self.nt = self.P // self.TM
        self.n_lat = S // self.TM

    def group(self, b0, nb):
        return _Geo(nb, self.S, self.C, self.D, b0=b0, B_all=self.B_all)

    def mod_spec(self, k):
        D, b0, Ba, n_lat = self.D, self.b0, self.B_all, self.n_lat
        return pl.BlockSpec((None, None, 1, D),
                            lambda b, i, *_: (k, jnp.where(i < n_lat, b + b0, Ba), 0, 0))

    def row_spec(self, width, col=0, whole=False):
        b0 = self.b0 if whole else 0
        return pl.BlockSpec((1, self.TM, width), lambda b, i, *_: (b + b0, i, col))

    def col_spec(self, height, whole=False):
        b0 = self.b0 if whole else 0
        return pl.BlockSpec((1, height, self.TM), lambda b, i, *_: (b + b0, 0, i))


def _full_spec(shape):
    nd = len(shape)
    return pl.BlockSpec(shape, lambda *_: (0,) * nd)


def _layer_norm(z, g, b):
    mu = jnp.mean(z, axis=-1, keepdims=True)
    zc = z - mu
    var = jnp.mean(zc * zc, axis=-1, keepdims=True)
    return zc * lax.rsqrt(var + LN_EPS) * g + b


def _finish_mixer(y, x, gate, lng, lnb, sh, sc, wr_ref, xm_ref, f_ref, s_ref, alpha):
    xn = _layer_norm(alpha * x + gate * y, lng, lnb)
    xm_ref[0] = xn
    f = xn * (1.0 + sc) + sh
    f_ref[0] = f.astype(f_ref.dtype)
    def nt(a, b):
        return lax.dot_general(a, b, (((1,), (1,)), ((), ())), preferred_element_type=F32)

    w = wr_ref[...]
    w_hi = w.astype(BF16)
    w_lo = (w - w_hi.astype(F32)).astype(BF16)
    f_hi = f.astype(BF16)
    f_lo = (f - f_hi.astype(F32)).astype(BF16)
    logits = nt(w_hi, f_hi) + (nt(w_hi, f_lo) + nt(w_lo, f_hi))
    s_ref[0] = jax.nn.sigmoid(logits)


def _post_kernel(*refs, alpha, has_bias):
    if has_bias:
        (y_ref, w_ref, b_ref, x_ref, gate_ref, lng_ref, lnb_ref, sh_ref, sc_ref, wr_ref,
         xm_ref, f_ref, s_ref) = refs
    else:
        (y_ref, w_ref, x_ref, gate_ref, lng_ref, lnb_ref, sh_ref, sc_ref, wr_ref,
         xm_ref, f_ref, s_ref) = refs
    y = jnp.dot(y_ref[0].astype(BF16), w_ref[...], preferred_element_type=F32)
    if has_bias:
        y = y + b_ref[...]
    _finish_mixer(y, x_ref[0], gate_ref[...], lng_ref[...], lnb_ref[...], sh_ref[...], sc_ref[...],
                  wr_ref, xm_ref, f_ref, s_ref, alpha)


def _post_outs(geo, E):
    B, P, D = geo.B, geo.P, geo.D
    out_specs = [geo.row_spec(D), geo.row_spec(D), geo.col_spec(E)]
    out_shape = [jax.ShapeDtypeStruct((B, P, D), F32), jax.ShapeDtypeStruct((B, P, D), BF16),
                 jax.ShapeDtypeStruct((B, E, P), F32)]
    return out_specs, out_shape


def _post_mixer(geo, y, w_out, b_out, xs, modl, lng, lnb, w_router, alpha):
    D = geo.D
    Kd = y.shape[-1]
    E = w_router.shape[-1]
    has_bias = b_out is not None
    ins = [y, w_out.astype(BF16)]
    specs = [geo.row_spec(Kd), _full_spec((Kd, D))]
    if has_bias:
        ins.append(b_out.reshape(1, D))
        specs.append(_full_spec((1, D)))
    ins += [xs, modl, lng.reshape(1, D), lnb.reshape(1, D), modl, modl, w_router.T]
    specs += [geo.row_spec(D), geo.mod_spec(2), _full_spec((1, D)), _full_spec((1, D)),
              geo.mod_spec(3), geo.mod_spec(4), _full_spec((E, D))]
    out_specs, out_shape = _post_outs(geo, E)
    return pl.pallas_call(
        functools.partial(_post_kernel, alpha=alpha, has_bias=has_bias),
        grid=(geo.B, geo.nt), in_specs=specs, out_specs=out_specs, out_shape=out_shape,
        compiler_params=_cp(2), name="post_mixer",
    )(*ins)


def _dwconv_seg(r, pad_scr, cw, cb, S, C, lo):
    P = S + C
    n = r.shape[-1]
    pad_scr[0:SUBLANES, :] = jnp.zeros((SUBLANES, n), F32)
    pad_scr[SUBLANES + P:, :] = jnp.zeros((SUBLANES, n), F32)
    pad_scr[SUBLANES:SUBLANES + P, :] = r
    row = lax.broadcasted_iota(jnp.int32, r.shape, 0)
    tl = jnp.where(row < S, row, row - S)
    sl = jnp.where(row < S, S, C)
    acc = jnp.zeros_like(r) + cb
    for k in range(cw.shape[0]):
        off = k - lo
        if off == 0:
            term = r
        else:
            shifted = pad_scr[SUBLANES + off:SUBLANES + off + P, :]
            valid = jnp.logical_and(tl + off >= 0, tl + off < sl)
            term = jnp.where(valid, shifted, 0.0)
        acc = acc + cw[k:k + 1, :] * term
    return acc


def _rg_in_kernel(x_ref, sh_ref, sc_ref, wg_ref, wr_ref, g_ref, r_ref):
    h = (x_ref[0] * (1.0 + sc_ref[...]) + sh_ref[...]).astype(BF16)
    g = jnp.dot(h, wg_ref[...], preferred_element_type=F32)
    g_ref[0] = jax.nn.gelu(g).astype(g_ref.dtype)
    r_ref[0] = jnp.dot(h, wr_ref[...], preferred_element_type=F32)


def _rg_scan_kernel(r_ref, g_ref, cw_ref, cb_ref, gw_ref, gb_ref, lam_ref, o_ref,
                    a_scr, b_scr, pad_scr, al_scr, bl_scr, cin_scr, *, S, C):
    P = S + C
    NT = P // SUBLANES
    hsum = None
    n = r_ref.shape[-1]
    r = r_ref[0]
    rc = _dwconv_seg(r, pad_scr, cw_ref[...], cb_ref[...], S, C, cw_ref.shape[0] // 2)
    rcb = rc.astype(BF16)
    row = lax.broadcasted_iota(jnp.int32, (P, n), 0)
    sub = jnp.bitwise_and(row, SUBLANES - 1)
    for scr in (a_scr, b_scr):
        scr[0:SUBLANES, :] = jnp.zeros((SUBLANES, n), F32)
        scr[SUBLANES + P:, :] = jnp.zeros((SUBLANES, n), F32)
    for d in range(2):
        rev = d == 1
        gr = jax.nn.sigmoid(jnp.dot(rcb, gw_ref[d, 0], preferred_element_type=F32) + gb_ref[d, 0])
        gi = jax.nn.sigmoid(jnp.dot(rcb, gw_ref[d, 1], preferred_element_type=F32) + gb_ref[d, 1])
        nl = -lam_ref[d]
        sp = jnp.maximum(nl, 0.0) + jnp.log1p(jnp.exp(-jnp.abs(nl)))
        a = jnp.exp(-LRU_C * gr * sp)
        om = 1.0 - a * a
        bb = om * lax.rsqrt(jnp.maximum(om, 1e-30)) * gi * rc
        for s in (1, 2, 4):
            a_scr[SUBLANES:SUBLANES + P, :] = a
            b_scr[SUBLANES:SUBLANES + P, :] = bb
            lo = SUBLANES + (s if rev else -s)
            a_sh = a_scr[lo:lo + P, :]
            b_sh = b_scr[lo:lo + P, :]
            m = (sub < SUBLANES - s) if rev else (sub >= s)
            bb = jnp.where(m, a * b_sh + bb, bb)
            a = jnp.where(m, a * a_sh, a)
        a_scr[SUBLANES:SUBLANES + P, :] = a
        b_scr[SUBLANES:SUBLANES + P, :] = bb
        edge = SUBLANES + (0 if rev else SUBLANES - 1)
        al_scr[...] = a_scr[pl.ds(edge, NT, stride=SUBLANES), :]
        bl_scr[...] = b_scr[pl.ds(edge, NT, stride=SUBLANES), :]

        def chain(lo_tile, n_tiles, c0):
            def body(i, c):
                t = (lo_tile + n_tiles - 1 - i) if rev else (lo_tile + i)
                cin_scr[pl.ds(t, 1), :] = c
                return bl_scr[pl.ds(t, 1), :] + al_scr[pl.ds(t, 1), :] * c
            return lax.fori_loop(0, n_tiles, body, c0, unroll=4)

        c_ctx = chain(S // SUBLANES, C // SUBLANES, jnp.zeros((1, n), F32))
        chain(0, S // SUBLANES, c_ctx)
        cin = cin_scr[...]
        for j in range(SUBLANES):
            pad_scr[pl.ds(j, NT, stride=SUBLANES), :] = cin
        h = bb + a * pad_scr[0:P, :]
        hsum = h if d == 0 else hsum + h
    o_ref[0] = (g_ref[0].astype(F32) * hsum).astype(o_ref.dtype)


def _rglru_mixer(geo, xs, modl, w_in, conv_w, conv_b, gate_w, gate_b, lam):
    B, P, D, S, C = geo.B, geo.P, geo.D, geo.S, geo.C
    R = w_in.shape[1] // 2
    nb = R // RG_BLOCK
    w_in = w_in.astype(BF16)
    g, r = pl.pallas_call(
        _rg_in_kernel, grid=(B, geo.nt),
        in_specs=[geo.row_spec(D), geo.mod_spec(0), geo.mod_spec(1),
                  pl.BlockSpec((D, R), lambda b, i: (0, 0)), pl.BlockSpec((D, R), lambda b, i: (0, 1))],
        out_specs=[geo.row_spec(R), geo.row_spec(R)],
        out_shape=[jax.ShapeDtypeStruct((B, P, R), BF16), jax.ShapeDtypeStruct((B, P, R), F32)],
        compiler_params=_cp(2), name="rg_in",
    )(xs, modl, modl, w_in, w_in)
    K = conv_w.shape[0]
    seq_spec = pl.BlockSpec((1, P, RG_BLOCK), lambda b, n: (b, 0, n))
    y = pl.pallas_call(
        functools.partial(_rg_scan_kernel, S=S, C=C), grid=(B, nb),
        in_specs=[seq_spec, seq_spec,
                  pl.BlockSpec((K, RG_BLOCK), lambda b, n: (0, n)),
                  pl.BlockSpec((1, RG_BLOCK), lambda b, n: (0, n)),
                  pl.BlockSpec((2, 2, None, RG_BLOCK, RG_BLOCK), lambda b, n: (0, 0, n, 0, 0)),
                  pl.BlockSpec((2, 2, 1, RG_BLOCK), lambda b, n: (0, 0, 0, n)),
                  pl.BlockSpec((2, 1, RG_BLOCK), lambda b, n: (0, 0, n))],
        out_specs=seq_spec,
        out_shape=jax.ShapeDtypeStruct((B, P, R), BF16),
        scratch_shapes=([pltpu.VMEM((P + 2 * SUBLANES, RG_BLOCK), F32)] * 3
                        + [pltpu.VMEM((P // SUBLANES, RG_BLOCK), F32)] * 3),
        compiler_params=_cp(2), name="rg_scan",
    )(r, g, conv_w, conv_b.reshape(1, R), gate_w.astype(BF16), gate_b.reshape(2, 2, 1, R),
      lam.reshape(2, 1, R))
    return y


def _mm_bias_kernel(x_ref, sh_ref, sc_ref, w_ref, b_ref, o_ref):
    h = (x_ref[0] * (1.0 + sc_ref[...]) + sh_ref[...]).astype(BF16)
    o_ref[0] = (jnp.dot(h, w_ref[...], preferred_element_type=F32) + b_ref[...]).astype(o_ref.dtype)


def _short_conv_kernel(u_ref, cw_ref, cb_ref, o_ref, ob_ref, pad_scr, *, S, C):
    y = _dwconv_seg(u_ref[0], pad_scr, cw_ref[...], cb_ref[...], S, C, (cw_ref.shape[0] - 1) // 2)
    o_ref[0] = y
    ob_ref[0] = y.astype(BF16)


def _dft_table_kernel(c_ref, s_ref, st_ref, *, L, TF):
    i = pl.program_id(0)
    N = 2 * L
    f = lax.broadcasted_iota(jnp.int32, (TF, L), 0) + i * TF
    t = lax.broadcasted_iota(jnp.int32, (TF, L), 1)
    ang = jnp.bitwise_and(f * t, N - 1).astype(F32) * (2.0 * math.pi / N)
    c_ref[...] = jnp.cos(ang).astype(BF16)
    nyq_t = (1 - 2 * jnp.bitwise_and(t, 1)).astype(F32)
    s_ref[...] = jnp.where(f == 0, nyq_t, jnp.sin(ang)).astype(BF16)
    nyq_f = (1 - 2 * jnp.bitwise_and(f, 1)).astype(F32)
    st_ref[...] = jnp.where(t == 0, nyq_f, jnp.sin(ang)).astype(BF16)


def _dft_tables(L):
    TF = min(L, 256)
    shp = jax.ShapeDtypeStruct((L, L), BF16)
    spec = pl.BlockSpec((TF, L), lambda i: (i, 0))
    return pl.pallas_call(
        functools.partial(_dft_table_kernel, L=L, TF=TF), grid=(L // TF,),
        in_specs=[], out_specs=[spec, spec, spec], out_shape=[shp, shp, shp],
        compiler_params=_cp(1), name="dft_tables",
    )()


def _hy_filter_kernel(z_ref, w1_ref, b1_ref, w2_ref, b2_ref, w3_ref, b3_ref, fq_ref,
                      w4f_ref, w4b_ref, df_ref, db_ref, tn_ref, kp_ref, km_ref):
    fq = fq_ref[...]

    def lin(h, w_ref, b_ref):
        return jnp.dot(h, w_ref[...], precision=HIGHEST, preferred_element_type=F32) + b_ref[...]

    h = jnp.sin(fq * lin(z_ref[...], w1_ref, b1_ref))
    h = jnp.sin(fq * lin(h, w2_ref, b2_ref))
    h = jnp.sin(fq * lin(h, w3_ref, b3_ref))
    tn = tn_ref[...]
    hf = jnp.dot(h, w4f_ref[...], precision=HIGHEST, preferred_element_type=F32)
    hf = hf * jnp.exp(-tn * jnp.abs(df_ref[...]))
    hb = jnp.dot(h, w4b_ref[...], precision=HIGHEST, preferred_element_type=F32)
    hb = hb * jnp.exp(-tn * jnp.abs(db_ref[...]))
    row = lax.broadcasted_iota(jnp.int32, hb.shape, 0)
    hb = jnp.where(row == 0, 0.0, hb)
    nrm = lax.rsqrt(jnp.sum(hf * hf, axis=0, keepdims=True) + jnp.sum(hb * hb, axis=0, keepdims=True) + 1e-6)
    hf = hf * nrm
    hb = hb * nrm
    kp_ref[...] = (hf + hb).astype(BF16)
    km_ref[...] = (hf - hb).astype(BF16)


def _hy_spectrum_kernel(c_ref, s_ref, s0_ref, kp_ref, km_ref, ka_ref, kb_ref, kc_ref, *, L, TF):
    i = pl.program_id(0)
    inv_n = 1.0 / (2 * L)
    kr = jnp.dot(c_ref[...], kp_ref[...], preferred_element_type=F32)
    ks = jnp.dot(s_ref[...], km_ref[...], preferred_element_type=F32)
    nyq = jnp.dot(s0_ref[...], kp_ref[...], preferred_element_type=F32)[0:1]
    f = lax.broadcasted_iota(jnp.int32, kr.shape, 0) + i * TF
    dc = f == 0
    ka_ref[...] = jnp.where(dc, kr * inv_n, kr * (2.0 * inv_n))
    kb_ref[...] = jnp.where(dc, 0.0, ks * (-2.0 * inv_n))
    kc_ref[...] = jnp.where(dc, nyq * inv_n, kr * (2.0 * inv_n))


def _hy_filters(L, tabs, fw1, fb1, fw2, fb2, fw3, fb3, fw4, ffreq, fdecay, D):
    cm, sm, _ = tabs
    E = fw1.shape[0]
    Hd = fw1.shape[1]
    bands = (E - 1) // 2
    t = jnp.arange(L, dtype=F32)
    t_norm = t / max(L - 1, 1)
    fr = jnp.linspace(1e-4, bands - 1, bands, dtype=F32)
    ang = (2.0 * math.pi / L) * t[:, None] * fr[None, :]
    z = jnp.concatenate([t_norm[:, None], jnp.cos(ang), -jnp.sin(ang)], -1)
    Ep, Hp = -(-E // LANES) * LANES, -(-Hd // LANES) * LANES
    z = jnp.pad(z, ((0, 0), (0, Ep - E)))
    fw1 = jnp.pad(fw1, ((0, Ep - E), (0, Hp - Hd)))
    fw2 = jnp.pad(fw2, ((0, Hp - Hd), (0, Hp - Hd)))
    fw3 = jnp.pad(fw3, ((0, Hp - Hd), (0, Hp - Hd)))
    fw4 = jnp.pad(fw4, ((0, Hp - Hd), (0, 0)))
    fb1, fb2, fb3, ffreq = (jnp.pad(v, (0, Hp - Hd)) for v in (fb1, fb2, fb3, ffreq))
    E, Hd = Ep, Hp
    CT = min(2 * D, 512)
    nct = 2 * D // CT
    dec = fdecay.reshape(1, 4 * D)
    kp, km = pl.pallas_call(
        _hy_filter_kernel, grid=(nct,),
        in_specs=[_full_spec((L, E)), _full_spec((E, Hd)), _full_spec((1, Hd)), _full_spec((Hd, Hd)),
                  _full_spec((1, Hd)), _full_spec((Hd, Hd)), _full_spec((1, Hd)), _full_spec((1, Hd)),
                  pl.BlockSpec((Hd, CT), lambda j: (0, j)), pl.BlockSpec((Hd, CT), lambda j: (0, j + nct)),
                  pl.BlockSpec((1, CT), lambda j: (0, j)), pl.BlockSpec((1, CT), lambda j: (0, j + nct)),
                  _full_spec((L, 1))],
        out_specs=[pl.BlockSpec((L, CT), lambda j: (0, j))] * 2,
        out_shape=[jax.ShapeDtypeStruct((L, 2 * D), BF16)] * 2,
        compiler_params=_cp(1), name="hy_filter",
    )(z, fw1, fb1.reshape(1, Hd), fw2, fb2.reshape(1, Hd), fw3, fb3.reshape(1, Hd), ffreq.reshape(1, Hd),
      fw4, fw4, dec, dec, t_norm[:, None])
    TF = min(L, 256)
    spec_w = pl.BlockSpec((TF, L), lambda i, j: (i, 0))
    spec_k = pl.BlockSpec((L, CT), lambda i, j: (0, j))
    spec_o = pl.BlockSpec((TF, CT), lambda i, j: (i, j))
    shp = jax.ShapeDtypeStruct((L, 2 * D), F32)
    return pl.pallas_call(
        functools.partial(_hy_spectrum_kernel, L=L, TF=TF), grid=(L // TF, nct),
        in_specs=[spec_w, spec_w, pl.BlockSpec((SUBLANES, L), lambda i, j: (0, 0)), spec_k, spec_k],
        out_specs=[spec_o] * 3, out_shape=[shp] * 3,
        compiler_params=_cp(2), name="hy_spectrum",
    )(cm, sm, sm, kp, km)


def _hy_fwd_kernel(z_ref, c_ref, s_ref, ka_ref, kb_ref, kc_ref, p_ref):
    z = z_ref[0]
    zr = jnp.dot(c_ref[...], z, preferred_element_type=F32)
    zs = jnp.dot(s_ref[...], z, preferred_element_type=F32)
    kb = kb_ref[...]
    p_ref[0, 0] = (zr * ka_ref[...] + zs * kb).astype(BF16)
    p_ref[0, 1] = (zs * kc_ref[...] - zr * kb).astype(BF16)


def _hy_inv_kernel(p_ref, c_ref, st_ref, z_ref, x_ref, fb_ref, o_ref):
    y = jnp.dot(c_ref[...], p_ref[0, 0], preferred_element_type=F32)
    y = y + jnp.dot(st_ref[...], p_ref[0, 1], preferred_element_type=F32)
    o_ref[0] = (x_ref[0] * (y + z_ref[0].astype(F32) * fb_ref[...])).astype(o_ref.dtype)


def _hy_inv_kernel_alias(p_ref, c_ref, st_ref, z_ref, x_ref, fb_ref, prev_ref, o_ref):
    del prev_ref
    _hy_inv_kernel(p_ref, c_ref, st_ref, z_ref, x_ref, fb_ref, o_ref)


def _hy_conv(geo, L, off, tabs, z, z_col, kfilt, k_col, xmul, x_col, fbias, out):
    B, P, D = geo.B, geo.P, geo.D
    cm, sm, smt = tabs
    ka, kb, kc = kfilt
    rb = off // L
    TF = min(L, 256)
    spec_w = pl.BlockSpec((TF, L), lambda b, i: (i, 0))
    spec_k = pl.BlockSpec((TF, D), lambda b, i: (i, k_col))
    p = pl.pallas_call(
        _hy_fwd_kernel, grid=(B, L // TF),
        in_specs=[pl.BlockSpec((1, L, D), lambda b, i: (b, rb, z_col)), spec_w, spec_w,
                  spec_k, spec_k, spec_k],
        out_specs=pl.BlockSpec((1, 2, TF, D), lambda b, i: (b, 0, i, 0)),
        out_shape=jax.ShapeDtypeStruct((B, 2, L, D), BF16),
        compiler_params=_cp(2), name="hy_fwd",
    )(z, cm, sm, ka, kb, kc)
    CT = min(D, 512)
    nct = D // CT
    rt = off // TF
    spec_wi = pl.BlockSpec((TF, L), lambda b, j, i: (i, 0))
    return pl.pallas_call(
        _hy_inv_kernel_alias, grid=(B, nct, L // TF),
        in_specs=[pl.BlockSpec((1, 2, L, CT), lambda b, j, i: (b, 0, 0, j)), spec_wi, spec_wi,
                  pl.BlockSpec((1, TF, CT), lambda b, j, i: (b, rt + i, z_col * nct + j)),
                  pl.BlockSpec((1, TF, CT), lambda b, j, i: (b, rt + i, x_col * nct + j)),
                  pl.BlockSpec((1, CT), lambda b, j, i: (0, k_col * nct + j)),
                  pl.BlockSpec(memory_space=pl.ANY)],
        out_specs=pl.BlockSpec((1, TF, CT), lambda b, j, i: (b, rt + i, j)),
        out_shape=jax.ShapeDtypeStruct((B, P, D), BF16),
        input_output_aliases={6: 0},
        compiler_params=_cp(3), name="hy_inv",
    )(p, cm, smt, z, xmul, fbias, out)


def _hyena_mixer(geo, xs, modl, w_in, b_in, short_w, short_b, fw1, fb1, fw2, fb2, fw3, fb3, fw4,
                 ffreq, fdecay, fbias):
    B, P, D, S, C = geo.B, geo.P, geo.D, geo.S, geo.C
    u0 = pl.pallas_call(
        _mm_bias_kernel, grid=(B, geo.nt, 3),
        in_specs=[geo.row_spec(D), geo.mod_spec(0), geo.mod_spec(1),
                  pl.BlockSpec((D, D), lambda b, i, j: (0, j)), pl.BlockSpec((1, D), lambda b, i, j: (0, j))],
        out_specs=pl.BlockSpec((1, geo.TM, D), lambda b, i, j: (b, i, j)),
        out_shape=jax.ShapeDtypeStruct((B, P, 3 * D), F32),
        compiler_params=_cp(3), name="hy_in",
    )(xs, modl, modl, w_in.astype(BF16), b_in.reshape(1, 3 * D))
    CT = min(D, 256)
    Ks = short_w.shape[0]
    spec = pl.BlockSpec((1, P, CT), lambda b, j: (b, 0, j))
    u, ub = pl.pallas_call(
        functools.partial(_short_conv_kernel, S=S, C=C), grid=(B, 3 * D // CT),
        in_specs=[spec, pl.BlockSpec((Ks, CT), lambda b, j: (0, j)), pl.BlockSpec((1, CT), lambda b, j: (0, j))],
        out_specs=[spec, spec],
        out_shape=[jax.ShapeDtypeStruct((B, P, 3 * D), F32), jax.ShapeDtypeStruct((B, P, 3 * D), BF16)],
        scratch_shapes=[pltpu.VMEM((P + 2 * SUBLANES, CT), F32)],
        compiler_params=_cp(2), name="hy_short",
    )(u0, short_w, short_b.reshape(1, 3 * D))
    fb = fbias.reshape(1, 2 * D)
    z1 = jnp.zeros((B, P, D), BF16)
    z2 = jnp.zeros((B, P, D), BF16)
    segs = [(S, 0), (C, S)]
    convs = []
    for L, off in segs:
        tabs = _dft_tables(L)
        kf = _hy_filters(L, tabs, fw1, fb1, fw2, fb2, fw3, fb3, fw4, ffreq, fdecay, D)
        convs.append((L, off, tabs, kf))
    for L, off, tabs, kf in convs:
        z1 = _hy_conv(geo, L, off, tabs, ub, 0, kf, 0, u, 1, fb, z1)
    for L, off, tabs, kf in convs:
        z2 = _hy_conv(geo, L, off, tabs, z1, 0, kf, 1, u, 2, fb, z2)
    return z2


def _rope_tables(S, D):
    t = jnp.arange(S)
    row = (t // GRID_W).astype(F32)
    col = (t % GRID_W).astype(F32)
    axis_dim = DA_HEAD_DIM // 2
    half = axis_dim // 2
    inv = ROPE_THETA ** (-jnp.arange(0, axis_dim, 2, dtype=F32) / axis_dim)
    lane = jnp.arange(D)
    within = lane % DA_HEAD_DIM
    pos = jnp.where((within // axis_dim)[None, :] == 0, row[:, None], col[:, None])
    ang = pos * inv[lane % half][None, :]
    sign = jnp.where((lane % axis_dim) < half, -1.0, 1.0)[None, :]
    return jnp.cos(ang), jnp.sin(ang) * sign


def _da_in_kernel(x_ref, sh_ref, sc_ref, w_ref, cos_ref, sin_ref, o_ref, *, n_lat):
    i = pl.program_id(1)
    j = pl.program_id(2)
    h = (x_ref[0] * (1.0 + sc_ref[...]) + sh_ref[...]).astype(BF16)
    acc = jnp.dot(h, w_ref[...], preferred_element_type=F32)
    acc = acc * jnp.where(j == 0, DA_HEAD_DIM ** -0.5, 1.0)
    rot = jnp.logical_and(i < n_lat, j < 2)

    @pl.when(rot)
    def _():
        Dn = acc.shape[-1]
        half = DA_HEAD_DIM // 4
        lane = lax.broadcasted_iota(jnp.int32, acc.shape, 1)
        up = pltpu.roll(acc, Dn - half, 1)
        dn = pltpu.roll(acc, half, 1)
        partner = jnp.where(jnp.bitwise_and(lane, 2 * half - 1) < half, up, dn)
        o_ref[0] = (acc * cos_ref[...] + partner * sin_ref[...]).astype(o_ref.dtype)

    @pl.when(jnp.logical_not(rot))
    def _():
        o_ref[0] = acc.astype(o_ref.dtype)


def _da_attn_kernel(*refs, kv_lo, nk, lam_init, aliased):
    if aliased:
        q_ref, k_ref, v_ref, lam_ref, sub_ref, _, o_ref, vx_scr, s_scr = refs
    else:
        q_ref, k_ref, v_ref, lam_ref, sub_ref, o_ref, vx_scr, s_scr = refs
    i = pl.program_id(2)
    HW = v_ref.shape[-1]
    TQ = q_ref.shape[1]
    lp = lam_ref[...]
    lam = (jnp.exp(jnp.sum(lp[0:1] * lp[1:2], axis=1, keepdims=True))
           - jnp.exp(jnp.sum(lp[2:3] * lp[3:4], axis=1, keepdims=True)) + lam_init)

    @pl.when(i == 0)
    def _():
        vx_scr[:, :HW] = v_ref[0]
        vx_scr[:, HW:] = jnp.ones((vx_scr.shape[0], HW), BF16)

    lane = lax.broadcasted_iota(jnp.int32, (TQ, HW), 1)
    q = q_ref[0]
    k = k_ref[0, kv_lo:kv_lo + nk, :]
    outs = []
    for c in range(2):
        qc = jnp.where((lane // DA_HEAD_DIM) == c, q, jnp.zeros_like(q))
        s_scr[...] = lax.dot_general(qc, k, (((1,), (1,)), ((), ())), preferred_element_type=F32)
        m = jnp.max(s_scr[...], axis=-1, keepdims=True)
        p = jnp.exp((s_scr[...] - m).astype(BF16))
        ov = jnp.dot(p, vx_scr[kv_lo:kv_lo + nk, :], preferred_element_type=F32)
        outs.append(ov[:, :HW] / ov[:, HW:HW + 1])
    o = outs[0] - lam * outs[1]
    o = o * lax.rsqrt(jnp.mean(o * o, axis=-1, keepdims=True) + 1e-5) * sub_ref[...] * (1.0 - lam_init)
    o_ref[0] = o.astype(o_ref.dtype)


def _diff_attention_mixer(geo, xs, modl, w_in, lam_p, subln_w, layer_idx):
    B, P, D, S, C = geo.B, geo.P, geo.D, geo.S, geo.C
    H = D // (2 * DA_HEAD_DIM)
    HW = 2 * DA_HEAD_DIM
    cos_t, sin_t = _rope_tables(S, D)
    n_lat = geo.n_lat
    tab_spec = pl.BlockSpec((geo.TM, D), lambda b, i, j: (jnp.minimum(i, n_lat - 1), 0))
    qkv = pl.pallas_call(
        functools.partial(_da_in_kernel, n_lat=n_lat), grid=(B, geo.nt, 3),
        in_specs=[geo.row_spec(D), geo.mod_spec(0), geo.mod_spec(1),
                  pl.BlockSpec((D, D), lambda b, i, j: (0, j)), tab_spec, tab_spec],
        out_specs=pl.BlockSpec((1, geo.TM, D), lambda b, i, j: (b, i, j)),
        out_shape=jax.ShapeDtypeStruct((B, P, 3 * D), BF16),
        compiler_params=_cp(3), name="da_in",
    )(xs, modl, modl, w_in.astype(BF16), cos_t, sin_t)
    lam_init = 0.8 - 0.6 * math.exp(-0.3 * layer_idx)
    def attend(TQ, row0, n_tiles, kv_lo, nk, prev):
        rb = row0 // TQ
        ins = [qkv, qkv, qkv, lam_p, subln_w.reshape(1, HW)]
        specs = [pl.BlockSpec((1, TQ, HW), lambda b, h, i: (b, rb + i, h)),
                 pl.BlockSpec((1, P, HW), lambda b, h, i: (b, 0, H + h)),
                 pl.BlockSpec((1, P, HW), lambda b, h, i: (b, 0, 2 * H + h)),
                 _full_spec((4, DA_HEAD_DIM)), _full_spec((1, HW))]
        if prev is not None:
            ins.append(prev)
            specs.append(pl.BlockSpec(memory_space=pl.ANY))
        return pl.pallas_call(
            functools.partial(_da_attn_kernel, kv_lo=kv_lo, nk=nk, lam_init=lam_init, aliased=prev is not None),
            grid=(B, H, n_tiles), in_specs=specs,
            out_specs=pl.BlockSpec((1, TQ, HW), lambda b, h, i: (b, rb + i, h)),
            out_shape=jax.ShapeDtypeStruct((B, P, D), BF16),
            scratch_shapes=[pltpu.VMEM((P, 2 * HW), BF16), pltpu.VMEM((TQ, nk), F32)],
            input_output_aliases={} if prev is None else {5: 0},
            compiler_params=_cp(3), name="da_attn",
        )(*ins)

    TQ = geo.TM
    while TQ < DA_Q_TILE and S % (2 * TQ) == 0:
        TQ *= 2
    y = attend(TQ, 0, S // TQ, 0, P, None)
    return attend(geo.TM, S, C // geo.TM, S, C, y)


def _s5_operators(a_re, a_im, log_step, b_re, b_im, c_re, c_im):
    T = S5_CHUNK
    G, Pst = a_re.shape[1], a_re.shape[2]
    Hg = S5_GROUP
    GL = LANES // Hg
    LB = G // GL
    lam = lax.complex(jnp.minimum(a_re.astype(F32), -1e-4), a_im.astype(F32))
    step = jnp.exp(log_step.astype(F32))[..., None]
    abar = jnp.exp(lam * step)
    bbar = ((abar - 1.0) / lam)[..., None] * lax.complex(b_re.astype(F32), b_im.astype(F32))
    cmat = lax.complex(c_re.astype(F32), c_im.astype(F32))
    pows = jnp.stack([abar ** l for l in range(T + 1)], axis=1)
    eye = jnp.eye(GL, dtype=F32)
    ar = jnp.arange(T)
    ops = []
    for d in range(2):
        pw = pows[d]
        kl = jnp.einsum('gjp,lgp,gph->lgjh', cmat[d], pw[:T], bbar[d]).real
        lag = (ar[None, :] - ar[:, None]) if d == 0 else (ar[:, None] - ar[None, :])
        tz = jnp.where((lag >= 0)[:, :, None, None, None], kl[jnp.clip(lag, 0, T - 1)], 0.0)
        tz = tz.reshape(T, T, LB, GL, Hg, Hg).astype(BF16)
        m = jnp.einsum('stbgjh,gk->bsghtkj', tz, eye.astype(BF16)).reshape(LB, T * LANES, T * LANES)
        e_in = (T - 1 - ar) if d == 0 else ar
        gc = pw[e_in][:, :, :, None] * bbar[d][None]
        gc = gc.reshape(T, LB, GL, Pst, Hg)
        eyeb = eye.astype(BF16)
        g_re = jnp.einsum('sbgph,gk->bsghkp', gc.real.astype(BF16), eyeb).reshape(LB, T * LANES, GL * Pst)
        g_im = jnp.einsum('sbgph,gk->bsghkp', gc.imag.astype(BF16), eyeb).reshape(LB, T * LANES, GL * Pst)
        e_out = (ar + 1) if d == 0 else (T - ar)
        hc = cmat[d][None] * pw[e_out][:, :, None, :]
        hc = hc.reshape(T, LB, GL, Hg, Pst)
        h_re = jnp.einsum('tbgjp,gk->bgptkj', hc.real.astype(BF16), eyeb).reshape(LB, GL * Pst, T * LANES)
        h_im = jnp.einsum('tbgjp,gk->bgptkj', (-hc.imag).astype(BF16), eyeb).reshape(LB, GL * Pst, T * LANES)
        at = pw[T].reshape(LB, 1, GL * Pst)
        ops.append((m, jnp.concatenate([g_re, g_im], axis=2), jnp.concatenate([h_re, h_im], axis=1),
                    jnp.concatenate([at.real, at.imag], axis=2).astype(F32)))
    return ops


def _modulate_kernel(x_ref, sh_ref, sc_ref, o_ref):
    o_ref[0] = x_ref[0] * (1.0 + sc_ref[...]) + sh_ref[...]


def _s5_kernel(u_ref, m_ref, g_ref, h_ref, a_ref, o_ref, gx_scr, sp_scr, *, S, C, reverse):
    T = S5_CHUNK
    P = S + C
    n = P // T
    n_lat = S // T
    NB = u_ref.shape[0]
    x = jnp.concatenate(
        [jnp.concatenate([u_ref[bi, pl.ds(s, n, stride=T), :] for s in range(T)], axis=1) for bi in range(NB)],
        axis=0).astype(BF16)
    gx_scr[...] = jnp.dot(x, g_ref[...], preferred_element_type=F32)
    ns = a_ref.shape[-1] // 2
    a_r = a_ref[:, :ns]
    a_i = a_ref[:, ns:]

    def scan(lo, cnt, rev, carry):
        def body(k, st):
            c = (lo + cnt - 1 - k) if rev else (lo + k)
            new = []
            for bi in range(NB):
                s_r, s_i = st[2 * bi], st[2 * bi + 1]
                sp_scr[pl.ds(bi * n + c, 1), :] = jnp.concatenate([s_r, s_i], axis=1)
                gx = gx_scr[pl.ds(bi * n + c, 1), :]
                new += [a_r * s_r - a_i * s_i + gx[:, :ns], a_r * s_i + a_i * s_r + gx[:, ns:]]
            return tuple(new)
        return lax.fori_loop(0, cnt, body, carry)

    zero = tuple(jnp.zeros((1, ns), F32) for _ in range(2 * NB))
    scan(0, n_lat, reverse, scan(n_lat, n - n_lat, reverse, zero))

    y = jnp.dot(x, m_ref[...], preferred_element_type=F32)
    y = y + jnp.dot(sp_scr[...].astype(BF16), h_ref[...], preferred_element_type=F32)
    for bi in range(NB):
        for s in range(T):
            o_ref[bi, pl.ds(s, n, stride=T), :] = y[bi * n:(bi + 1) * n, s * LANES:(s + 1) * LANES]


def _s5_mixer(geo, xs, modl, ops):
    B, P, D, S, C = geo.B, geo.P, geo.D, geo.S, geo.C
    u = pl.pallas_call(
        _modulate_kernel, grid=(B, geo.nt),
        in_specs=[geo.row_spec(D), geo.mod_spec(0), geo.mod_spec(1)],
        out_specs=geo.row_spec(D), out_shape=jax.ShapeDtypeStruct((B, P, D), F32),
        compiler_params=_cp(2), name="s5_modulate",
    )(xs, modl, modl)
    LB = D // LANES
    TL = S5_CHUNK * LANES
    NS = ops[0][1].shape[-1]
    n = P // S5_CHUNK
    NB = 2 if B % 2 == 0 else 1
    once = pl.Buffered(1)
    ys = []
    for d, (big_m, big_g, big_h, a_t) in enumerate(ops):
        ys.append(pl.pallas_call(
            functools.partial(_s5_kernel, S=S, C=C, reverse=d == 1), grid=(LB, B // NB),
            in_specs=[pl.BlockSpec((NB, P, LANES), lambda l, b: (b, 0, l)),
                      pl.BlockSpec((None, TL, TL), lambda l, b: (l, 0, 0), pipeline_mode=once),
                      pl.BlockSpec((None, TL, NS), lambda l, b: (l, 0, 0), pipeline_mode=once),
                      pl.BlockSpec((None, NS, TL), lambda l, b: (l, 0, 0), pipeline_mode=once),
                      pl.BlockSpec((None, 1, NS), lambda l, b: (l, 0, 0))],
            out_specs=pl.BlockSpec((NB, P, LANES), lambda l, b: (b, 0, l)),
            out_shape=jax.ShapeDtypeStruct((B, P, D), F32),
            scratch_shapes=[pltpu.VMEM((NB * n, NS), F32), pltpu.VMEM((NB * n, NS), F32)],
            compiler_params=_cp(2), name="s5_scan",
        )(u, big_m, big_g, big_h, a_t))
    return u, ys


def _s5_post_kernel(yf_ref, yb_ref, u_ref, d_ref, w_ref, b_ref, x_ref, gate_ref, lng_ref, lnb_ref,
                    sh_ref, sc_ref, wr_ref, xm_ref, f_ref, s_ref, *, alpha):
    g = jax.nn.gelu(yf_ref[0] + yb_ref[0] + d_ref[...] * u_ref[0])
    vg = jnp.dot(g.astype(BF16), w_ref[...], preferred_element_type=F32) + b_ref[...]
    Dn = vg.shape[-1] // 2
    y = vg[:, :Dn] * jax.nn.sigmoid(vg[:, Dn:])
    _finish_mixer(y, x_ref[0], gate_ref[...], lng_ref[...], lnb_ref[...], sh_ref[...], sc_ref[...],
                  wr_ref, xm_ref, f_ref, s_ref, alpha)


def _s5_post(geo, y2, u, d_skip, w_glu, b_glu, xs, modl, lng, lnb, w_router, alpha):
    D = geo.D
    E = w_router.shape[-1]
    out_specs, out_shape = _post_outs(geo, E)
    TM = geo.TM
    return pl.pallas_call(
        functools.partial(_s5_post_kernel, alpha=alpha), grid=(geo.B, geo.nt),
        in_specs=[geo.row_spec(D), geo.row_spec(D),
                  geo.row_spec(D), _full_spec((1, D)), _full_spec((D, 2 * D)), _full_spec((1, 2 * D)),
                  geo.row_spec(D), geo.mod_spec(2), _full_spec((1, D)), _full_spec((1, D)),
                  geo.mod_spec(3), geo.mod_spec(4), _full_spec((E, D))],
        out_specs=out_specs, out_shape=out_shape,
        compiler_params=_cp(2), name="s5_post",
    )(y2[0], y2[1], u, d_skip.reshape(1, D), w_glu.astype(BF16), b_glu.reshape(1, 2 * D), xs, modl,
      lng.reshape(1, D), lnb.reshape(1, D), modl, modl, w_router.T)


def _route_kernel(s_ref, b_ref, idx_ref, w_ref):
    sc = s_ref[0]
    E, TM = sc.shape
    biased = sc + b_ref[...]
    G = N_EXPERT_GROUPS
    per = E // G
    neg = -jnp.inf
    blocks, gs = [], []
    for g in range(G):
        blk = biased[g * per:(g + 1) * per]
        m1 = jnp.max(blk, axis=0, keepdims=True)
        is1 = blk == m1
        cnt = jnp.sum(is1.astype(F32), axis=0, keepdims=True)
        m2 = jnp.max(jnp.where(is1, neg, blk), axis=0, keepdims=True)
        blocks.append(blk)
        gs.append(m1 + jnp.where(cnt >= 2.0, m1, m2))
    masked = []
    for g in range(G):
        ahead = jnp.zeros((1, TM), F32)
        for h in range(G):
            if h < g:
                ahead = ahead + (gs[h] >= gs[g]).astype(F32)
            elif h > g:
                ahead = ahead + (gs[h] > gs[g]).astype(F32)
        masked.append(jnp.where(ahead < float(TOPK_GROUPS), blocks[g], neg))
    masked = jnp.concatenate(masked, axis=0)
    iota_e = lax.broadcasted_iota(jnp.int32, (E, TM), 0)
    idxs, ws = [], []
    for _ in range(TOP_K):
        m = jnp.max(masked, axis=0, keepdims=True)
        ik = jnp.min(jnp.where(masked == m, iota_e, E), axis=0, keepdims=True)
        sel = iota_e == ik
        ws.append(jnp.sum(jnp.where(sel, sc, 0.0), axis=0, keepdims=True))
        idxs.append(ik)
        masked = jnp.where(sel, neg, masked)
    tot = ws[0]
    for wk in ws[1:]:
        tot = tot + wk
    w = jnp.concatenate(ws, axis=0)
    idx_ref[0] = jnp.concatenate(idxs, axis=0)
    w_ref[0] = w / (tot + 1e-20) * ROUTED_SCALE


def _rank_kernel(idx_ref, rank_ref, cnt_ref, run_scr, *, E):
    first = jnp.logical_and(pl.program_id(0) == 0, pl.program_id(1) == 0)

    @pl.when(first)
    def _():
        run_scr[...] = jnp.zeros_like(run_scr)

    idx = idx_ref[0]
    K, TM = idx.shape
    iota_e = lax.broadcasted_iota(jnp.int32, (E, TM), 0)
    member = jnp.zeros((E, TM), F32)
    for k in range(K):
        member = member + (iota_e == idx[k:k + 1]).astype(F32)
    before = (lax.broadcasted_iota(jnp.int32, (TM, TM), 0)
              < lax.broadcasted_iota(jnp.int32, (TM, TM), 1)).astype(BF16)
    rank = jnp.dot(member.astype(BF16), before, preferred_element_type=F32) + run_scr[...]
    rows = [jnp.sum(jnp.where(iota_e == idx[k:k + 1], rank, 0.0), axis=0, keepdims=True) for k in range(K)]
    rank_ref[0] = jnp.concatenate(rows, axis=0).astype(jnp.int32)
    run_scr[...] = run_scr[...] + jnp.sum(member, axis=1, keepdims=True)
    cnt_ref[...] = run_scr[...]


def _dest_kernel(idx_ref, rank_ref, start_ref, dest_ref, *, E):
    idx = idx_ref[0]
    K, TM = idx.shape
    iota_e = lax.broadcasted_iota(jnp.int32, (E, TM), 0)
    start = start_ref[...]
    rows = [jnp.sum(jnp.where(iota_e == idx[k:k + 1], start, 0), axis=0, keepdims=True) for k in range(K)]
    dest_ref[0] = jnp.concatenate(rows, axis=0) + rank_ref[0]


def _route_dispatch(geo, scores, bias, blk):
    B, P = geo.B, geo.P
    E = scores.shape[1]
    K = TOP_K
    kspec = geo.col_spec(K)
    idx, w = pl.pallas_call(
        _route_kernel, grid=(B, geo.nt),
        in_specs=[geo.col_spec(E, whole=True), _full_spec((E, 1))],
        out_specs=[kspec, kspec],
        out_shape=[jax.ShapeDtypeStruct((B, K, P), jnp.int32), jax.ShapeDtypeStruct((B, K, P), F32)],
        compiler_params=_cp(2), name="moe_route",
    )(scores, bias.astype(F32).reshape(E, 1))
    rank, cnt = pl.pallas_call(
        functools.partial(_rank_kernel, E=E), grid=(B, geo.nt),
        in_specs=[kspec], out_specs=[kspec, _full_spec((E, 1))],
        out_shape=[jax.ShapeDtypeStruct((B, K, P), jnp.int32), jax.ShapeDtypeStruct((E, 1), F32)],
        scratch_shapes=[pltpu.VMEM((E, 1), F32)],
        compiler_params=_cp(2), name="moe_rank",
    )(idx)
    n_assign = B * P * K
    n_blocks = -(-(n_assign + E * (blk - 1)) // blk)
    counts = cnt[:, 0].astype(jnp.int32)
    pcounts = (counts + blk - 1) // blk * blk
    pends = jnp.cumsum(pcounts)
    starts = (pends - pcounts).astype(jnp.int32)
    n_used = pends[-1] // blk
    blk_e = jnp.minimum(jnp.searchsorted(pends, jnp.arange(n_blocks) * blk, side='right'), E - 1)
    blk_e = jnp.where(jnp.arange(n_blocks) < n_used, blk_e, blk_e[jnp.maximum(n_used - 1, 0)])
    dest = pl.pallas_call(
        functools.partial(_dest_kernel, E=E), grid=(B, geo.nt),
        in_specs=[kspec, kspec, _full_spec((E, 1))], out_specs=kspec,
        out_shape=jax.ShapeDtypeStruct((B, K, P), jnp.int32),
        compiler_params=_cp(2), name="moe_dest",
    )(idx, rank, starts.reshape(E, 1))
    return w, dest, blk_e.astype(jnp.int32), n_used.astype(jnp.int32).reshape(1), n_blocks


def _expert_kernel(be_ref, nu_ref, x_ref, wgu_ref, wd_ref, *rest):
    del be_ref
    o_ref = rest[-1]
    i = pl.program_id(0)

    @pl.when(i < nu_ref[0])
    def _():
        h = jnp.dot(x_ref[...], wgu_ref[...].astype(BF16), preferred_element_type=F32)
        Fh = h.shape[-1] // 2
        a = _silu(h[:, :Fh]) * h[:, Fh:]
        o_ref[...] = jnp.dot(a.astype(BF16), wd_ref[...].astype(BF16),
                             preferred_element_type=F32).astype(o_ref.dtype)

    @pl.when(i >= nu_ref[0])
    def _():
        o_ref[...] = jnp.zeros_like(o_ref)


def _expert_ffn(x_part, blk_e, n_used, w_gu, w_down, layer, blk, n_rows_all, blk0, prev):
    n_rows, D = x_part.shape
    F2 = w_gu.shape[-1]
    ins = [blk_e, n_used, x_part, w_gu, w_down]
    specs = [pl.BlockSpec((blk, D), lambda i, be, nu: (i, 0)),
             pl.BlockSpec((None, None, D, F2), lambda i, be, nu: (layer, be[i], 0, 0)),
             pl.BlockSpec((None, None, F2 // 2, D), lambda i, be, nu: (layer, be[i], 0, 0))]
    if prev is not None:
        ins.append(prev)
        specs.append(pl.BlockSpec(memory_space=pl.ANY))
    grid_spec = pltpu.PrefetchScalarGridSpec(
        num_scalar_prefetch=2, grid=(n_rows // blk,), in_specs=specs,
        out_specs=pl.BlockSpec((blk, D), lambda i, be, nu: (blk0 + i, 0)))
    return pl.pallas_call(
        _expert_kernel, grid_spec=grid_spec,
        out_shape=jax.ShapeDtypeStruct((n_rows_all, D), BF16),
        input_output_aliases={} if prev is None else {5: 0},
        compiler_params=_cp(1), name="moe_experts",
    )(*ins)


def _moe_final_kernel(*refs, alpha, aliased):
    if aliased:
        xm_ref, f_ref, ga_ref, w_ref, shgu_ref, shd_ref, gate_ref, lng_ref, lnb_ref, _, o_ref = refs
    else:
        xm_ref, f_ref, ga_ref, w_ref, shgu_ref, shd_ref, gate_ref, lng_ref, lnb_ref, o_ref = refs
    h = jnp.dot(f_ref[0], shgu_ref[...], preferred_element_type=F32)
    Fh = h.shape[-1] // 2
    a = _silu(h[:, :Fh]) * h[:, Fh:]
    y = jnp.dot(a.astype(BF16), shd_ref[...], preferred_element_type=F32)
    w = w_ref[0]
    for k in range(w.shape[-1]):
        y = y + w[:, k:k + 1] * ga_ref[k, 0].astype(F32)
    o_ref[0] = _layer_norm(alpha * xm_ref[0] + gate_ref[...] * y, lng_ref[...], lnb_ref[...])


def _moe_group(geo, xm, f, scores, modl, bias, w_gu, w_down, sh_gu, sh_down, lng, lnb, layer, alpha, blk, prev,
               tie=None):
    B, P, D = geo.B, geo.P, geo.D
    T = B * P
    K = TOP_K
    w, dest, blk_e, n_used, n_blocks = _route_dispatch(geo, scores, bias, blk)
    dest_flat = jnp.swapaxes(dest, 0, 1).reshape(K * T)
    t0 = geo.b0 * P
    tok = jnp.broadcast_to(jnp.arange(t0, t0 + T, dtype=jnp.int32)[None], (K, T)).reshape(K * T)
    hit = jnp.zeros((n_blocks * blk,), jnp.int32).at[dest_flat].add(
        tok + 1, unique_indices=True, mode='promise_in_bounds')
    row_tok = jnp.where(hit > 0, hit - 1, t0 + jnp.arange(n_blocks * blk, dtype=jnp.int32) % T)
    f2d = f.reshape(geo.B_all * P, D)
    y_sorted = None
    for part in range(MOE_ROW_PARTS):
        lo = n_blocks * part // MOE_ROW_PARTS
        hi = n_blocks * (part + 1) // MOE_ROW_PARTS
        x_part = f2d.at[row_tok[lo * blk:hi * blk]].get(mode='promise_in_bounds')
        used = jnp.clip(n_used - lo, 0, hi - lo)
        y_sorted = _expert_ffn(x_part, blk_e[lo:hi], used, w_gu, w_down, layer, blk, n_blocks * blk, lo, y_sorted)
    gathered = y_sorted.at[dest_flat].get(unique_indices=True, mode='promise_in_bounds').reshape(K, B, P, D)
    if tie is not None:
        gathered, tie = lax.optimization_barrier((gathered, tie))
    F2 = sh_gu.shape[-1]
    TM = geo.TM
    ins = [xm, f, gathered, jnp.swapaxes(w, 1, 2), sh_gu.astype(BF16), sh_down.astype(BF16), modl,
           lng.reshape(1, D), lnb.reshape(1, D)]
    specs = [geo.row_spec(D, whole=True), geo.row_spec(D, whole=True),
             pl.BlockSpec((K, 1, TM, D), lambda b, i: (0, b, i, 0)), geo.row_spec(K),
             _full_spec((D, F2)), _full_spec((F2 // 2, D)), geo.mod_spec(5),
             _full_spec((1, D)), _full_spec((1, D))]
    if prev is not None:
        ins.append(prev)
        specs.append(pl.BlockSpec(memory_space=pl.ANY))
    out = pl.pallas_call(
        functools.partial(_moe_final_kernel, alpha=alpha, aliased=prev is not None), grid=(B, geo.nt),
        in_specs=specs, out_specs=geo.row_spec(D, whole=True),
        out_shape=jax.ShapeDtypeStruct((geo.B_all, P, D), F32),
        input_output_aliases={} if prev is None else {9: 0},
        compiler_params=_cp(2), name="moe_final",
    )(*ins)
    return out, tie


def _moe(geo, xm, f, scores, modl, bias, w_gu, w_down, sh_gu, sh_down, lng, lnb, layer, alpha, blk, tie=None):
    n_groups = MOE_GROUPS if geo.B % MOE_GROUPS == 0 else 1
    nb = geo.B // n_groups
    out = None
    for g in range(n_groups):
        out, tie = _moe_group(geo.group(g * nb, nb), xm, f, scores, modl, bias, w_gu, w_down, sh_gu, sh_down,
                              lng, lnb, layer, alpha, blk, out, tie)
    return out, tie


def kernel(x, c, ctx, c_ctx, mod_w, mod_b, ln_g, ln_b, rg_w_in, rg_conv_w, rg_conv_b, rg_gate_w, rg_gate_b, rg_lam, rg_w_out, hy_w_in, hy_b_in, hy_short_w, hy_short_b, hy_f_w1, hy_f_b1, hy_f_w2, hy_f_b2, hy_f_w3, hy_f_b3, hy_f_w4, hy_f_freq, hy_f_decay, hy_f_bias, hy_w_out, hy_b_out, da_w_in, da_lam, da_subln, da_w_out, s5_a_re, s5_a_im, s5_log_step, s5_b_re, s5_b_im, s5_c_re, s5_c_im, s5_d, s5_w_glu, s5_b_glu, moe_w_router, moe_bias, moe_w_gu, moe_w_down, moe_sh_gu, moe_sh_down):
    B, S, D = x.shape
    C = ctx.shape[1]
    depth = mod_w.shape[0]
    alpha = (2 * depth) ** 0.25
    geo = _Geo(B, S, C, D)
    xs = jnp.concatenate([x, ctx], axis=1)
    R = -(-(B + 1) // SUBLANES) * SUBLANES
    cc = jnp.zeros((R, D), F32).at[:B].set(c).at[B].set(c_ctx)
    modt = _mod_table(cc, mod_w, mod_b).reshape(depth, 6, R, 1, D)
    blk = MOE_BLOCK
    s5_ops, ties = {}, {}
    for i in range(N_MIXERS - 1, depth, N_MIXERS):
        j = i // N_MIXERS
        s5_ops[i] = _s5_operators(s5_a_re[j], s5_a_im[j], s5_log_step[j], s5_b_re[j], s5_b_im[j],
                                  s5_c_re[j], s5_c_im[j])
        for d in range(2):
            ties[i - 2 + d] = (i, d)
    for i in range(depth):
        kind, j = i % N_MIXERS, i // N_MIXERS
        modl = modt[i]
        post = functools.partial(_post_mixer, geo, xs=xs, modl=modl, lng=ln_g[i, 0], lnb=ln_b[i, 0],
                                 w_router=moe_w_router[i], alpha=alpha)
        if kind == 0:
            y = _rglru_mixer(geo, xs, modl, rg_w_in[j], rg_conv_w[j], rg_conv_b[j], rg_gate_w[j],
                             rg_gate_b[j], rg_lam[j])
            xm, f, scores = post(y=y, w_out=rg_w_out[j], b_out=None)
        elif kind == 1:
            y = _hyena_mixer(geo, xs, modl, hy_w_in[j], hy_b_in[j], hy_short_w[j], hy_short_b[j],
                             hy_f_w1[j], hy_f_b1[j], hy_f_w2[j], hy_f_b2[j], hy_f_w3[j], hy_f_b3[j],
                             hy_f_w4[j], hy_f_freq[j], hy_f_decay[j], hy_f_bias[j])
            xm, f, scores = post(y=y, w_out=hy_w_out[j], b_out=hy_b_out[j])
        elif kind == 2:
            y = _diff_attention_mixer(geo, xs, modl, da_w_in[j], da_lam[j], da_subln[j], i)
            xm, f, scores = post(y=y, w_out=da_w_out[j], b_out=None)
        else:
            u, y2 = _s5_mixer(geo, xs, modl, s5_ops[i])
            xm, f, scores = _s5_post(geo, y2, u, s5_d[j], s5_w_glu[j], s5_b_glu[j], xs, modl,
                                     ln_g[i, 0], ln_b[i, 0], moe_w_router[i], alpha)
        tie = ties.get(i)
        xs, tied = _moe(geo, xm, f, scores, modl, moe_bias[i], moe_w_gu, moe_w_down, moe_sh_gu[i],
                        moe_sh_down[i], ln_g[i, 1], ln_b[i, 1], i, alpha, blk,
                        None if tie is None else s5_ops[tie[0]][tie[1]])
        if tie is not None:
            s5_ops[tie[0]][tie[1]] = tied
    return xs[:, :S]
```

```python
import functools
import math

import jax
import jax.numpy as jnp
from jax import lax
from jax.experimental import pallas as pl
from jax.experimental.pallas import tpu as pltpu

F32 = jnp.float32
BF16 = jnp.bfloat16
HIGHEST = lax.Precision.HIGHEST

N_MIXERS = 4
LN_EPS = 1e-6
LRU_C = 8.0
RG_BLOCK = 128
GRID_W = 64
DA_HEAD_DIM = 64
ROPE_THETA = 10000.0
S5_GROUP = 16
S5_CHUNK = 16
TOP_K = 8
N_EXPERT_GROUPS = 8
TOPK_GROUPS = 4
ROUTED_SCALE = 2.5
MOE_BLOCK = 512
MOE_GROUPS = 1
MOE_ROW_PARTS = 4
DA_Q_TILE = 1024

LANES = 128
SUBLANES = 8
VMEM_LIMIT = 56 * 1024 * 1024


def _cp(n_grid):
    return pltpu.CompilerParams(dimension_semantics=("arbitrary",) * n_grid,
                                vmem_limit_bytes=VMEM_LIMIT)


def _silu(x):
    return x * jax.nn.sigmoid(x)


def _mod_table_kernel(c_ref, w_ref, b_ref, o_ref):
    s = _silu(c_ref[...])
    o_ref[...] = jnp.dot(s, w_ref[...], precision=HIGHEST, preferred_element_type=F32) + b_ref[...]


def _mod_table(cc, mod_w, mod_b):
    depth, D, _ = mod_w.shape
    R = cc.shape[0]
    return pl.pallas_call(
        _mod_table_kernel,
        grid=(depth, 6),
        in_specs=[pl.BlockSpec((R, D), lambda i, k: (0, 0)),
                  pl.BlockSpec((None, D, D), lambda i, k: (i, 0, k)),
                  pl.BlockSpec((None, None, 1, D), lambda i, k: (i, k, 0, 0))],
        out_specs=pl.BlockSpec((None, None, R, D), lambda i, k: (i, k, 0, 0)),
        out_shape=jax.ShapeDtypeStruct((depth, 6, R, D), F32),
        compiler_params=_cp(2), name="mod_table",
    )(cc, mod_w, mod_b.reshape(depth, 6, 1, D))


class _Geo:
    def __init__(self, B, S, C, D, b0=0, B_all=None):
        self.B, self.S, self.C, self.D = B, S, C, D
        self.b0 = b0
        self.B_all = B if B_all is None else B_all
        self.P = S + C
        self.TM = math.gcd(S, C)
        while self.TM > 256:
            self.TM //= 2
        self.nt = self.P // self.TM
        self.n_lat = S // self.TM

    def group(self, b0, nb):
        return _Geo(nb, self.S, self.C, self.D, b0=b0, B_all=self.B_all)

    def mod_spec(self, k):
        D, b0, Ba, n_lat = self.D, self.b0, self.B_all, self.n_lat
        return pl.BlockSpec((None, None, 1, D),
                            lambda b, i, *_: (k, jnp.where(i < n_lat, b + b0, Ba), 0, 0))

    def row_spec(self, width, col=0, whole=False):
        b0 = self.b0 if whole else 0
        return pl.BlockSpec((1, self.TM, width), lambda b, i, *_: (b + b0, i, col))

    def col_spec(self, height, whole=False):
        b0 = self.b0 if whole else 0
        return pl.BlockSpec((1, height, self.TM), lambda b, i, *_: (b + b0, 0, i))


def _full_spec(shape):
    nd = len(shape)
    return pl.BlockSpec(shape, lambda *_: (0,) * nd)


def _layer_norm(z, g, b):
    mu = jnp.mean(z, axis=-1, keepdims=True)
    zc = z - mu
    var = jnp.mean(zc * zc, axis=-1, keepdims=True)
    return zc * lax.rsqrt(var + LN_EPS) * g + b


def _finish_mixer(y, x, gate, lng, lnb, sh, sc, wr_ref, xm_ref, f_ref, s_ref, alpha):
    xn = _layer_norm(alpha * x + gate * y, lng, lnb)
    xm_ref[0] = xn
    f = xn * (1.0 + sc) + sh
    f_ref[0] = f.astype(f_ref.dtype)
    def nt(a, b):
        return lax.dot_general(a, b, (((1,), (1,)), ((), ())), preferred_element_type=F32)

    w = wr_ref[...]
    w_hi = w.astype(BF16)
    w_lo = (w - w_hi.astype(F32)).astype(BF16)
    f_hi = f.astype(BF16)
    f_lo = (f - f_hi.astype(F32)).astype(BF16)
    logits = nt(w_hi, f_hi) + (nt(w_hi, f_lo) + nt(w_lo, f_hi))
    s_ref[0] = jax.nn.sigmoid(logits)


def _post_kernel(*refs, alpha, has_bias):
    if has_bias:
        (y_ref, w_ref, b_ref, x_ref, gate_ref, lng_ref, lnb_ref, sh_ref, sc_ref, wr_ref,
         xm_ref, f_ref, s_ref) = refs
    else:
        (y_ref, w_ref, x_ref, gate_ref, lng_ref, lnb_ref, sh_ref, sc_ref, wr_ref,
         xm_ref, f_ref, s_ref) = refs
    y = jnp.dot(y_ref[0].astype(BF16), w_ref[...], preferred_element_type=F32)
    if has_bias:
        y = y + b_ref[...]
    _finish_mixer(y, x_ref[0], gate_ref[...], lng_ref[...], lnb_ref[...], sh_ref[...], sc_ref[...],
                  wr_ref, xm_ref, f_ref, s_ref, alpha)


def _post_outs(geo, E):
    B, P, D = geo.B, geo.P, geo.D
    out_specs = [geo.row_spec(D), geo.row_spec(D), geo.col_spec(E)]
    out_shape = [jax.ShapeDtypeStruct((B, P, D), F32), jax.ShapeDtypeStruct((B, P, D), BF16),
                 jax.ShapeDtypeStruct((B, E, P), F32)]
    return out_specs, out_shape


def _post_mixer(geo, y, w_out, b_out, xs, modl, lng, lnb, w_router, alpha):
    D = geo.D
    Kd = y.shape[-1]
    E = w_router.shape[-1]
    has_bias = b_out is not None
    ins = [y, w_out.astype(BF16)]
    specs = [geo.row_spec(Kd), _full_spec((Kd, D))]
    if has_bias:
        ins.append(b_out.reshape(1, D))
        specs.append(_full_spec((1, D)))
    ins += [xs, modl, lng.reshape(1, D), lnb.reshape(1, D), modl, modl, w_router.T]
    specs += [geo.row_spec(D), geo.mod_spec(2), _full_spec((1, D)), _full_spec((1, D)),
              geo.mod_spec(3), geo.mod_spec(4), _full_spec((E, D))]
    out_specs, out_shape = _post_outs(geo, E)
    return pl.pallas_call(
        functools.partial(_post_kernel, alpha=alpha, has_bias=has_bias),
        grid=(geo.B, geo.nt), in_specs=specs, out_specs=out_specs, out_shape=out_shape,
        compiler_params=_cp(2), name="post_mixer",
    )(*ins)


def _dwconv_seg(r, pad_scr, cw, cb, S, C, lo):
    P = S + C
    n = r.shape[-1]
    pad_scr[0:SUBLANES, :] = jnp.zeros((SUBLANES, n), F32)
    pad_scr[SUBLANES + P:, :] = jnp.zeros((SUBLANES, n), F32)
    pad_scr[SUBLANES:SUBLANES + P, :] = r
    row = lax.broadcasted_iota(jnp.int32, r.shape, 0)
    tl = jnp.where(row < S, row, row - S)
    sl = jnp.where(row < S, S, C)
    acc = jnp.zeros_like(r) + cb
    for k in range(cw.shape[0]):
        off = k - lo
        if off == 0:
            term = r
        else:
            shifted = pad_scr[SUBLANES + off:SUBLANES + off + P, :]
            valid = jnp.logical_and(tl + off >= 0, tl + off < sl)
            term = jnp.where(valid, shifted, 0.0)
        acc = acc + cw[k:k + 1, :] * term
    return acc


def _rg_in_kernel(x_ref, sh_ref, sc_ref, wg_ref, wr_ref, g_ref, r_ref):
    h = (x_ref[0] * (1.0 + sc_ref[...]) + sh_ref[...]).astype(BF16)
    g = jnp.dot(h, wg_ref[...], preferred_element_type=F32)
    g_ref[0] = jax.nn.gelu(g).astype(g_ref.dtype)
    r_ref[0] = jnp.dot(h, wr_ref[...], preferred_element_type=F32)


def _rg_scan_kernel(r_ref, g_ref, cw_ref, cb_ref, gw_ref, gb_ref, lam_ref, o_ref,
                    a_scr, b_scr, pad_scr, al_scr, bl_scr, cin_scr, *, S, C):
    P = S + C
    NT = P // SUBLANES
    hsum = None
    n = r_ref.shape[-1]
    r = r_ref[0]
    rc = _dwconv_seg(r, pad_scr, cw_ref[...], cb_ref[...], S, C, cw_ref.shape[0] // 2)
    rcb = rc.astype(BF16)
    row = lax.broadcasted_iota(jnp.int32, (P, n), 0)
    sub = jnp.bitwise_and(row, SUBLANES - 1)
    for scr in (a_scr, b_scr):
        scr[0:SUBLANES, :] = jnp.zeros((SUBLANES, n), F32)
        scr[SUBLANES + P:, :] = jnp.zeros((SUBLANES, n), F32)
    for d in range(2):
        rev = d == 1
        gr = jax.nn.sigmoid(jnp.dot(rcb, gw_ref[d, 0], preferred_element_type=F32) + gb_ref[d, 0])
        gi = jax.nn.sigmoid(jnp.dot(rcb, gw_ref[d, 1], preferred_element_type=F32) + gb_ref[d, 1])
        nl = -lam_ref[d]
        sp = jnp.maximum(nl, 0.0) + jnp.log1p(jnp.exp(-jnp.abs(nl)))
        a = jnp.exp(-LRU_C * gr * sp)
        om = 1.0 - a * a
        bb = om * lax.rsqrt(jnp.maximum(om, 1e-30)) * gi * rc
        for s in (1, 2, 4):
            a_scr[SUBLANES:SUBLANES + P, :] = a
            b_scr[SUBLANES:SUBLANES + P, :] = bb
            lo = SUBLANES + (s if rev else -s)
            a_sh = a_scr[lo:lo + P, :]
            b_sh = b_scr[lo:lo + P, :]
            m = (sub < SUBLANES - s) if rev else (sub >= s)
            bb = jnp.where(m, a * b_sh + bb, bb)
            a = jnp.where(m, a * a_sh, a)
        a_scr[SUBLANES:SUBLANES + P, :] = a
        b_scr[SUBLANES:SUBLANES + P, :] = bb
        edge = SUBLANES + (0 if rev else SUBLANES - 1)
        al_scr[...] = a_scr[pl.ds(edge, NT, stride=SUBLANES), :]
        bl_scr[...] = b_scr[pl.ds(edge, NT, stride=SUBLANES), :]

        def chain(lo_tile, n_tiles, c0):
            def body(i, c):
                t = (lo_tile + n_tiles - 1 - i) if rev else (lo_tile + i)
                cin_scr[pl.ds(t, 1), :] = c
                return bl_scr[pl.ds(t, 1), :] + al_scr[pl.ds(t, 1), :] * c
            return lax.fori_loop(0, n_tiles, body, c0, unroll=4)

        c_ctx = chain(S // SUBLANES, C // SUBLANES, jnp.zeros((1, n), F32))
        chain(0, S // SUBLANES, c_ctx)
        cin = cin_scr[...]
        for j in range(SUBLANES):
            pad_scr[pl.ds(j, NT, stride=SUBLANES), :] = cin
        h = bb + a * pad_scr[0:P, :]
        hsum = h if d == 0 else hsum + h
    o_ref[0] = (g_ref[0].astype(F32) * hsum).astype(o_ref.dtype)


def _rglru_mixer(geo, xs, modl, w_in, conv_w, conv_b, gate_w, gate_b, lam):
    B, P, D, S, C = geo.B, geo.P, geo.D, geo.S, geo.C
    R = w_in.shape[1] // 2
    nb = R // RG_BLOCK
    w_in = w_in.astype(BF16)
    g, r = pl.pallas_call(
        _rg_in_kernel, grid=(B, geo.nt),
        in_specs=[geo.row_spec(D), geo.mod_spec(0), geo.mod_spec(1),
                  pl.BlockSpec((D, R), lambda b, i: (0, 0)), pl.BlockSpec((D, R), lambda b, i: (0, 1))],
        out_specs=[geo.row_spec(R), geo.row_spec(R)],
        out_shape=[jax.ShapeDtypeStruct((B, P, R), BF16), jax.ShapeDtypeStruct((B, P, R), F32)],
        compiler_params=_cp(2), name="rg_in",
    )(xs, modl, modl, w_in, w_in)
    K = conv_w.shape[0]
    seq_spec = pl.BlockSpec((1, P, RG_BLOCK), lambda b, n: (b, 0, n))
    y = pl.pallas_call(
        functools.partial(_rg_scan_kernel, S=S, C=C), grid=(B, nb),
        in_specs=[seq_spec, seq_spec,
                  pl.BlockSpec((K, RG_BLOCK), lambda b, n: (0, n)),
                  pl.BlockSpec((1, RG_BLOCK), lambda b, n: (0, n)),
                  pl.BlockSpec((2, 2, None, RG_BLOCK, RG_BLOCK), lambda b, n: (0, 0, n, 0, 0)),
                  pl.BlockSpec((2, 2, 1, RG_BLOCK), lambda b, n: (0, 0, 0, n)),
                  pl.BlockSpec((2, 1, RG_BLOCK), lambda b, n: (0, 0, n))],
        out_specs=seq_spec,
        out_shape=jax.ShapeDtypeStruct((B, P, R), BF16),
        scratch_shapes=([pltpu.VMEM((P + 2 * SUBLANES, RG_BLOCK), F32)] * 3
                        + [pltpu.VMEM((P // SUBLANES, RG_BLOCK), F32)] * 3),
        compiler_params=_cp(2), name="rg_scan",
    )(r, g, conv_w, conv_b.reshape(1, R), gate_w.astype(BF16), gate_b.reshape(2, 2, 1, R),
      lam.reshape(2, 1, R))
    return y


def _mm_bias_kernel(x_ref, sh_ref, sc_ref, w_ref, b_ref, o_ref):
    h = (x_ref[0] * (1.0 + sc_ref[...]) + sh_ref[...]).astype(BF16)
    o_ref[0] = (jnp.dot(h, w_ref[...], preferred_element_type=F32) + b_ref[...]).astype(o_ref.dtype)


def _short_conv_kernel(u_ref, cw_ref, cb_ref, o_ref, ob_ref, pad_scr, *, S, C):
    y = _dwconv_seg(u_ref[0], pad_scr, cw_ref[...], cb_ref[...], S, C, (cw_ref.shape[0] - 1) // 2)
    o_ref[0] = y
    ob_ref[0] = y.astype(BF16)


def _dft_table_kernel(c_ref, s_ref, st_ref, *, L, TF):
    i = pl.program_id(0)
    N = 2 * L
    f = lax.broadcasted_iota(jnp.int32, (TF, L), 0) + i * TF
    t = lax.broadcasted_iota(jnp.int32, (TF, L), 1)
    ang = jnp.bitwise_and(f * t, N - 1).astype(F32) * (2.0 * math.pi / N)
    c_ref[...] = jnp.cos(ang).astype(BF16)
    nyq_t = (1 - 2 * jnp.bitwise_and(t, 1)).astype(F32)
    s_ref[...] = jnp.where(f == 0, nyq_t, jnp.sin(ang)).astype(BF16)
    nyq_f = (1 - 2 * jnp.bitwise_and(f, 1)).astype(F32)
    st_ref[...] = jnp.where(t == 0, nyq_f, jnp.sin(ang)).astype(BF16)


def _dft_tables(L):
    TF = min(L, 256)
    shp = jax.ShapeDtypeStruct((L, L), BF16)
    spec = pl.BlockSpec((TF, L), lambda i: (i, 0))
    return pl.pallas_call(
        functools.partial(_dft_table_kernel, L=L, TF=TF), grid=(L // TF,),
        in_specs=[], out_specs=[spec, spec, spec], out_shape=[shp, shp, shp],
        compiler_params=_cp(1), name="dft_tables",
    )()


def _hy_filter_kernel(z_ref, w1_ref, b1_ref, w2_ref, b2_ref, w3_ref, b3_ref, fq_ref,
                      w4f_ref, w4b_ref, df_ref, db_ref, tn_ref, kp_ref, km_ref):
    fq = fq_ref[...]

    def lin(h, w_ref, b_ref):
        return jnp.dot(h, w_ref[...], precision=HIGHEST, preferred_element_type=F32) + b_ref[...]

    h = jnp.sin(fq * lin(z_ref[...], w1_ref, b1_ref))
    h = jnp.sin(fq * lin(h, w2_ref, b2_ref))
    h = jnp.sin(fq * lin(h, w3_ref, b3_ref))
    tn = tn_ref[...]
    hf = jnp.dot(h, w4f_ref[...], precision=HIGHEST, preferred_element_type=F32)
    hf = hf * jnp.exp(-tn * jnp.abs(df_ref[...]))
    hb = jnp.dot(h, w4b_ref[...], precision=HIGHEST, preferred_element_type=F32)
    hb = hb * jnp.exp(-tn * jnp.abs(db_ref[...]))
    row = lax.broadcasted_iota(jnp.int32, hb.shape, 0)
    hb = jnp.where(row == 0, 0.0, hb)
    nrm = lax.rsqrt(jnp.sum(hf * hf, axis=0, keepdims=True) + jnp.sum(hb * hb, axis=0, keepdims=True) + 1e-6)
    hf = hf * nrm
    hb = hb * nrm
    kp_ref[...] = (hf + hb).astype(BF16)
    km_ref[...] = (hf - hb).astype(BF16)


def _hy_spectrum_kernel(c_ref, s_ref, s0_ref, kp_ref, km_ref, ka_ref, kb_ref, kc_ref, *, L, TF):
    i = pl.program_id(0)
    inv_n = 1.0 / (2 * L)
    kr = jnp.dot(c_ref[...], kp_ref[...], preferred_element_type=F32)
    ks = jnp.dot(s_ref[...], km_ref[...], preferred_element_type=F32)
    nyq = jnp.dot(s0_ref[...], kp_ref[...], preferred_element_type=F32)[0:1]
    f = lax.broadcasted_iota(jnp.int32, kr.shape, 0) + i * TF
    dc = f == 0
    ka_ref[...] = jnp.where(dc, kr * inv_n, kr * (2.0 * inv_n))
    kb_ref[...] = jnp.where(dc, 0.0, ks * (-2.0 * inv_n))
    kc_ref[...] = jnp.where(dc, nyq * inv_n, kr * (2.0 * inv_n))


def _hy_filters(L, tabs, fw1, fb1, fw2, fb2, fw3, fb3, fw4, ffreq, fdecay, D):
    cm, sm, _ = tabs
    E = fw1.shape[0]
    Hd = fw1.shape[1]
    bands = (E - 1) // 2
    t = jnp.arange(L, dtype=F32)
    t_norm = t / max(L - 1, 1)
    fr = jnp.linspace(1e-4, bands - 1, bands, dtype=F32)
    ang = (2.0 * math.pi / L) * t[:, None] * fr[None, :]
    z = jnp.concatenate([t_norm[:, None], jnp.cos(ang), -jnp.sin(ang)], -1)
    Ep, Hp = -(-E // LANES) * LANES, -(-Hd // LANES) * LANES
    z = jnp.pad(z, ((0, 0), (0, Ep - E)))
    fw1 = jnp.pad(fw1, ((0, Ep - E), (0, Hp - Hd)))
    fw2 = jnp.pad(fw2, ((0, Hp - Hd), (0, Hp - Hd)))
    fw3 = jnp.pad(fw3, ((0, Hp - Hd), (0, Hp - Hd)))
    fw4 = jnp.pad(fw4, ((0, Hp - Hd), (0, 0)))
    fb1, fb2, fb3, ffreq = (jnp.pad(v, (0, Hp - Hd)) for v in (fb1, fb2, fb3, ffreq))
    E, Hd = Ep, Hp
    CT = min(2 * D, 512)
    nct = 2 * D // CT
    dec = fdecay.reshape(1, 4 * D)
    kp, km = pl.pallas_call(
        _hy_filter_kernel, grid=(nct,),
        in_specs=[_full_spec((L, E)), _full_spec((E, Hd)), _full_spec((1, Hd)), _full_spec((Hd, Hd)),
                  _full_spec((1, Hd)), _full_spec((Hd, Hd)), _full_spec((1, Hd)), _full_spec((1, Hd)),
                  pl.BlockSpec((Hd, CT), lambda j: (0, j)), pl.BlockSpec((Hd, CT), lambda j: (0, j + nct)),
                  pl.BlockSpec((1, CT), lambda j: (0, j)), pl.BlockSpec((1, CT), lambda j: (0, j + nct)),
                  _full_spec((L, 1))],
        out_specs=[pl.BlockSpec((L, CT), lambda j: (0, j))] * 2,
        out_shape=[jax.ShapeDtypeStruct((L, 2 * D), BF16)] * 2,
        compiler_params=_cp(1), name="hy_filter",
    )(z, fw1, fb1.reshape(1, Hd), fw2, fb2.reshape(1, Hd), fw3, fb3.reshape(1, Hd), ffreq.reshape(1, Hd),
      fw4, fw4, dec, dec, t_norm[:, None])
    TF = min(L, 256)
    spec_w = pl.BlockSpec((TF, L), lambda i, j: (i, 0))
    spec_k = pl.BlockSpec((L, CT), lambda i, j: (0, j))
    spec_o = pl.BlockSpec((TF, CT), lambda i, j: (i, j))
    shp = jax.ShapeDtypeStruct((L, 2 * D), F32)
    return pl.pallas_call(
        functools.partial(_hy_spectrum_kernel, L=L, TF=TF), grid=(L // TF, nct),
        in_specs=[spec_w, spec_w, pl.BlockSpec((SUBLANES, L), lambda i, j: (0, 0)), spec_k, spec_k],
        out_specs=[spec_o] * 3, out_shape=[shp] * 3,
        compiler_params=_cp(2), name="hy_spectrum",
    )(cm, sm, sm, kp, km)


def _hy_fwd_kernel(z_ref, c_ref, s_ref, ka_ref, kb_ref, kc_ref, p_ref):
    z = z_ref[0]
    zr = jnp.dot(c_ref[...], z, preferred_element_type=F32)
    zs = jnp.dot(s_ref[...], z, preferred_element_type=F32)
    kb = kb_ref[...]
    p_ref[0, 0] = (zr * ka_ref[...] + zs * kb).astype(BF16)
    p_ref[0, 1] = (zs * kc_ref[...] - zr * kb).astype(BF16)


def _hy_inv_kernel(p_ref, c_ref, st_ref, z_ref, x_ref, fb_ref, o_ref):
    y = jnp.dot(c_ref[...], p_ref[0, 0], preferred_element_type=F32)
    y = y + jnp.dot(st_ref[...], p_ref[0, 1], preferred_element_type=F32)
    o_ref[0] = (x_ref[0] * (y + z_ref[0].astype(F32) * fb_ref[...])).astype(o_ref.dtype)


def _hy_inv_kernel_alias(p_ref, c_ref, st_ref, z_ref, x_ref, fb_ref, prev_ref, o_ref):
    del prev_ref
    _hy_inv_kernel(p_ref, c_ref, st_ref, z_ref, x_ref, fb_ref, o_ref)


def _hy_conv(geo, L, off, tabs, z, z_col, kfilt, k_col, xmul, x_col, fbias, out):
    B, P, D = geo.B, geo.P, geo.D
    cm, sm, smt = tabs
    ka, kb, kc = kfilt
    rb = off // L
    TF = min(L, 256)
    spec_w = pl.BlockSpec((TF, L), lambda b, i: (i, 0))
    spec_k = pl.BlockSpec((TF, D), lambda b, i: (i, k_col))
    p = pl.pallas_call(
        _hy_fwd_kernel, grid=(B, L // TF),
        in_specs=[pl.BlockSpec((1, L, D), lambda b, i: (b, rb, z_col)), spec_w, spec_w,
                  spec_k, spec_k, spec_k],
        out_specs=pl.BlockSpec((1, 2, TF, D), lambda b, i: (b, 0, i, 0)),
        out_shape=jax.ShapeDtypeStruct((B, 2, L, D), BF16),
        compiler_params=_cp(2), name="hy_fwd",
    )(z, cm, sm, ka, kb, kc)
    CT = D
    nct = D // CT
    rt = off // TF
    spec_wi = pl.BlockSpec((TF, L), lambda b, j, i: (i, 0))
    return pl.pallas_call(
        _hy_inv_kernel_alias, grid=(B, nct, L // TF),
        in_specs=[pl.BlockSpec((1, 2, L, CT), lambda b, j, i: (b, 0, 0, j), pipeline_mode=pl.Buffered(1)),
                  spec_wi, spec_wi,
                  pl.BlockSpec((1, TF, CT), lambda b, j, i: (b, rt + i, z_col * nct + j)),
                  pl.BlockSpec((1, TF, CT), lambda b, j, i: (b, rt + i, x_col * nct + j)),
                  pl.BlockSpec((1, CT), lambda b, j, i: (0, k_col * nct + j)),
                  pl.BlockSpec(memory_space=pl.ANY)],
        out_specs=pl.BlockSpec((1, TF, CT), lambda b, j, i: (b, rt + i, j)),
        out_shape=jax.ShapeDtypeStruct((B, P, D), BF16),
        input_output_aliases={6: 0},
        compiler_params=_cp(3), name="hy_inv",
    )(p, cm, smt, z, xmul, fbias, out)


def _hyena_mixer(geo, xs, modl, w_in, b_in, short_w, short_b, fw1, fb1, fw2, fb2, fw3, fb3, fw4,
                 ffreq, fdecay, fbias):
    B, P, D, S, C = geo.B, geo.P, geo.D, geo.S, geo.C
    u0 = pl.pallas_call(
        _mm_bias_kernel, grid=(B, geo.nt, 3),
        in_specs=[geo.row_spec(D), geo.mod_spec(0), geo.mod_spec(1),
                  pl.BlockSpec((D, D), lambda b, i, j: (0, j)), pl.BlockSpec((1, D), lambda b, i, j: (0, j))],
        out_specs=pl.BlockSpec((1, geo.TM, D), lambda b, i, j: (b, i, j)),
        out_shape=jax.ShapeDtypeStruct((B, P, 3 * D), F32),
        compiler_params=_cp(3), name="hy_in",
    )(xs, modl, modl, w_in.astype(BF16), b_in.reshape(1, 3 * D))
    CT = min(D, 256)
    Ks = short_w.shape[0]
    spec = pl.BlockSpec((1, P, CT), lambda b, j: (b, 0, j))
    u, ub = pl.pallas_call(
        functools.partial(_short_conv_kernel, S=S, C=C), grid=(B, 3 * D // CT),
        in_specs=[spec, pl.BlockSpec((Ks, CT), lambda b, j: (0, j)), pl.BlockSpec((1, CT), lambda b, j: (0, j))],
        out_specs=[spec, spec],
        out_shape=[jax.ShapeDtypeStruct((B, P, 3 * D), F32), jax.ShapeDtypeStruct((B, P, 3 * D), BF16)],
        scratch_shapes=[pltpu.VMEM((P + 2 * SUBLANES, CT), F32)],
        compiler_params=_cp(2), name="hy_short",
    )(u0, short_w, short_b.reshape(1, 3 * D))
    fb = fbias.reshape(1, 2 * D)
    z1 = jnp.zeros((B, P, D), BF16)
    z2 = jnp.zeros((B, P, D), BF16)
    segs = [(S, 0), (C, S)]
    convs = []
    for L, off in segs:
        tabs = _dft_tables(L)
        kf = _hy_filters(L, tabs, fw1, fb1, fw2, fb2, fw3, fb3, fw4, ffreq, fdecay, D)
        convs.append((L, off, tabs, kf))
    for L, off, tabs, kf in convs:
        z1 = _hy_conv(geo, L, off, tabs, ub, 0, kf, 0, u, 1, fb, z1)
    for L, off, tabs, kf in convs:
        z2 = _hy_conv(geo, L, off, tabs, z1, 0, kf, 1, u, 2, fb, z2)
    return z2


def _rope_tables(S, D):
    t = jnp.arange(S)
    row = (t // GRID_W).astype(F32)
    col = (t % GRID_W).astype(F32)
    axis_dim = DA_HEAD_DIM // 2
    half = axis_dim // 2
    inv = ROPE_THETA ** (-jnp.arange(0, axis_dim, 2, dtype=F32) / axis_dim)
    lane = jnp.arange(D)
    within = lane % DA_HEAD_DIM
    pos = jnp.where((within // axis_dim)[None, :] == 0, row[:, None], col[:, None])
    ang = pos * inv[lane % half][None, :]
    sign = jnp.where((lane % axis_dim) < half, -1.0, 1.0)[None, :]
    return jnp.cos(ang), jnp.sin(ang) * sign


def _da_in_kernel(x_ref, sh_ref, sc_ref, w_ref, cos_ref, sin_ref, o_ref, *, n_lat):
    i = pl.program_id(1)
    j = pl.program_id(2)
    h = (x_ref[0] * (1.0 + sc_ref[...]) + sh_ref[...]).astype(BF16)
    acc = jnp.dot(h, w_ref[...], preferred_element_type=F32)
    acc = acc * jnp.where(j == 0, DA_HEAD_DIM ** -0.5, 1.0)
    rot = jnp.logical_and(i < n_lat, j < 2)

    @pl.when(rot)
    def _():
        Dn = acc.shape[-1]
        half = DA_HEAD_DIM // 4
        lane = lax.broadcasted_iota(jnp.int32, acc.shape, 1)
        up = pltpu.roll(acc, Dn - half, 1)
        dn = pltpu.roll(acc, half, 1)
        partner = jnp.where(jnp.bitwise_and(lane, 2 * half - 1) < half, up, dn)
        o_ref[0] = (acc * cos_ref[...] + partner * sin_ref[...]).astype(o_ref.dtype)

    @pl.when(jnp.logical_not(rot))
    def _():
        o_ref[0] = acc.astype(o_ref.dtype)


def _da_attn_kernel(*refs, kv_lo, nk, lam_init, aliased):
    if aliased:
        q_ref, k_ref, v_ref, lam_ref, sub_ref, _, o_ref, vx_scr, s_scr = refs
    else:
        q_ref, k_ref, v_ref, lam_ref, sub_ref, o_ref, vx_scr, s_scr = refs
    i = pl.program_id(2)
    HW = v_ref.shape[-1]
    TQ = q_ref.shape[1]
    lp = lam_ref[...]
    lam = (jnp.exp(jnp.sum(lp[0:1] * lp[1:2], axis=1, keepdims=True))
           - jnp.exp(jnp.sum(lp[2:3] * lp[3:4], axis=1, keepdims=True)) + lam_init)

    @pl.when(i == 0)
    def _():
        vx_scr[:, :HW] = v_ref[0]
        vx_scr[:, HW:] = jnp.ones((vx_scr.shape[0], HW), BF16)

    lane = lax.broadcasted_iota(jnp.int32, (TQ, HW), 1)
    q = q_ref[0]
    k = k_ref[0, kv_lo:kv_lo + nk, :]
    outs = []
    for c in range(2):
        qc = jnp.where((lane // DA_HEAD_DIM) == c, q, jnp.zeros_like(q))
        s_scr[...] = lax.dot_general(qc, k, (((1,), (1,)), ((), ())), preferred_element_type=F32)
        m = jnp.max(s_scr[...], axis=-1, keepdims=True)
        p = jnp.exp((s_scr[...] - m).astype(BF16))
        ov = jnp.dot(p, vx_scr[kv_lo:kv_lo + nk, :], preferred_element_type=F32)
        outs.append(ov[:, :HW] / ov[:, HW:HW + 1])
    o = outs[0] - lam * outs[1]
    o = o * lax.rsqrt(jnp.mean(o * o, axis=-1, keepdims=True) + 1e-5) * sub_ref[...] * (1.0 - lam_init)
    o_ref[0] = o.astype(o_ref.dtype)


def _diff_attention_mixer(geo, xs, modl, w_in, lam_p, subln_w, layer_idx):
    B, P, D, S, C = geo.B, geo.P, geo.D, geo.S, geo.C
    H = D // (2 * DA_HEAD_DIM)
    HW = 2 * DA_HEAD_DIM
    cos_t, sin_t = _rope_tables(S, D)
    n_lat = geo.n_lat
    tab_spec = pl.BlockSpec((geo.TM, D), lambda b, i, j: (jnp.minimum(i, n_lat - 1), 0))
    qkv = pl.pallas_call(
        functools.partial(_da_in_kernel, n_lat=n_lat), grid=(B, geo.nt, 3),
        in_specs=[geo.row_spec(D), geo.mod_spec(0), geo.mod_spec(1),
                  pl.BlockSpec((D, D), lambda b, i, j: (0, j)), tab_spec, tab_spec],
        out_specs=pl.BlockSpec((1, geo.TM, D), lambda b, i, j: (b, i, j)),
        out_shape=jax.ShapeDtypeStruct((B, P, 3 * D), BF16),
        compiler_params=_cp(3), name="da_in",
    )(xs, modl, modl, w_in.astype(BF16), cos_t, sin_t)
    lam_init = 0.8 - 0.6 * math.exp(-0.3 * layer_idx)
    def attend(TQ, row0, n_tiles, kv_lo, nk, prev):
        rb = row0 // TQ
        ins = [qkv, qkv, qkv, lam_p, subln_w.reshape(1, HW)]
        specs = [pl.BlockSpec((1, TQ, HW), lambda b, h, i: (b, rb + i, h)),
                 pl.BlockSpec((1, P, HW), lambda b, h, i: (b, 0, H + h)),
                 pl.BlockSpec((1, P, HW), lambda b, h, i: (b, 0, 2 * H + h)),
                 _full_spec((4, DA_HEAD_DIM)), _full_spec((1, HW))]
        if prev is not None:
            ins.append(prev)
            specs.append(pl.BlockSpec(memory_space=pl.ANY))
        return pl.pallas_call(
            functools.partial(_da_attn_kernel, kv_lo=kv_lo, nk=nk, lam_init=lam_init, aliased=prev is not None),
            grid=(B, H, n_tiles), in_specs=specs,
            out_specs=pl.BlockSpec((1, TQ, HW), lambda b, h, i: (b, rb + i, h)),
            out_shape=jax.ShapeDtypeStruct((B, P, D), BF16),
            scratch_shapes=[pltpu.VMEM((P, 2 * HW), BF16), pltpu.VMEM((TQ, nk), F32)],
            input_output_aliases={} if prev is None else {5: 0},
            compiler_params=_cp(3), name="da_attn",
        )(*ins)

    TQ = geo.TM
    while TQ < DA_Q_TILE and S % (2 * TQ) == 0:
        TQ *= 2
    y = attend(TQ, 0, S // TQ, 0, P, None)
    return attend(geo.TM, S, C // geo.TM, S, C, y)


def _s5_operators(a_re, a_im, log_step, b_re, b_im, c_re, c_im):
    T = S5_CHUNK
    G, Pst = a_re.shape[1], a_re.shape[2]
    Hg = S5_GROUP
    GL = LANES // Hg
    LB = G // GL
    lam = lax.complex(jnp.minimum(a_re.astype(F32), -1e-4), a_im.astype(F32))
    step = jnp.exp(log_step.astype(F32))[..., None]
    abar = jnp.exp(lam * step)
    bbar = ((abar - 1.0) / lam)[..., None] * lax.complex(b_re.astype(F32), b_im.astype(F32))
    cmat = lax.complex(c_re.astype(F32), c_im.astype(F32))
    pows = jnp.stack([abar ** l for l in range(T + 1)], axis=1)
    eye = jnp.eye(GL, dtype=F32)
    ar = jnp.arange(T)
    ops = []
    for d in range(2):
        pw = pows[d]
        kl = jnp.einsum('gjp,lgp,gph->lgjh', cmat[d], pw[:T], bbar[d]).real
        lag = (ar[None, :] - ar[:, None]) if d == 0 else (ar[:, None] - ar[None, :])
        tz = jnp.where((lag >= 0)[:, :, None, None, None], kl[jnp.clip(lag, 0, T - 1)], 0.0)
        tz = tz.reshape(T, T, LB, GL, Hg, Hg).astype(BF16)
        m = jnp.einsum('stbgjh,gk->bsghtkj', tz, eye.astype(BF16)).reshape(LB, T * LANES, T * LANES)
        e_in = (T - 1 - ar) if d == 0 else ar
        gc = pw[e_in][:, :, :, None] * bbar[d][None]
        gc = gc.reshape(T, LB, GL, Pst, Hg)
        eyeb = eye.astype(BF16)
        g_re = jnp.einsum('sbgph,gk->bsghkp', gc.real.astype(BF16), eyeb).reshape(LB, T * LANES, GL * Pst)
        g_im = jnp.einsum('sbgph,gk->bsghkp', gc.imag.astype(BF16), eyeb).reshape(LB, T * LANES, GL * Pst)
        e_out = (ar + 1) if d == 0 else (T - ar)
        hc = cmat[d][None] * pw[e_out][:, :, None, :]
        hc = hc.reshape(T, LB, GL, Hg, Pst)
        h_re = jnp.einsum('tbgjp,gk->bgptkj', hc.real.astype(BF16), eyeb).reshape(LB, GL * Pst, T * LANES)
        h_im = jnp.einsum('tbgjp,gk->bgptkj', (-hc.imag).astype(BF16), eyeb).reshape(LB, GL * Pst, T * LANES)
        at = pw[T].reshape(LB, 1, GL * Pst)
        ops.append((m, jnp.concatenate([g_re, g_im], axis=2), jnp.concatenate([h_re, h_im], axis=1),
                    jnp.concatenate([at.real, at.imag], axis=2).astype(F32)))
    return ops


def _modulate_kernel(x_ref, sh_ref, sc_ref, o_ref):
    o_ref[0] = x_ref[0] * (1.0 + sc_ref[...]) + sh_ref[...]


def _s5_kernel(u_ref, m_ref, g_ref, h_ref, a_ref, o_ref, gx_scr, sp_scr, *, S, C, reverse):
    T = S5_CHUNK
    P = S + C
    n = P // T
    n_lat = S // T
    NB = u_ref.shape[0]
    x = jnp.concatenate(
        [jnp.concatenate([u_ref[bi, pl.ds(s, n, stride=T), :] for s in range(T)], axis=1) for bi in range(NB)],
        axis=0).astype(BF16)
    gx_scr[...] = jnp.dot(x, g_ref[...], preferred_element_type=F32)
    ns = a_ref.shape[-1] // 2
    a_r = a_ref[:, :ns]
    a_i = a_ref[:, ns:]

    def scan(lo, cnt, rev, carry):
        def body(k, st):
            c = (lo + cnt - 1 - k) if rev else (lo + k)
            new = []
            for bi in range(NB):
                s_r, s_i = st[2 * bi], st[2 * bi + 1]
                sp_scr[pl.ds(bi * n + c, 1), :] = jnp.concatenate([s_r, s_i], axis=1)
                gx = gx_scr[pl.ds(bi * n + c, 1), :]
                new += [a_r * s_r - a_i * s_i + gx[:, :ns], a_r * s_i + a_i * s_r + gx[:, ns:]]
            return tuple(new)
        return lax.fori_loop(0, cnt, body, carry)

    zero = tuple(jnp.zeros((1, ns), F32) for _ in range(2 * NB))
    scan(0, n_lat, reverse, scan(n_lat, n - n_lat, reverse, zero))

    y = jnp.dot(x, m_ref[...], preferred_element_type=F32)
    y = y + jnp.dot(sp_scr[...].astype(BF16), h_ref[...], preferred_element_type=F32)
    for bi in range(NB):
        for s in range(T):
            o_ref[bi, pl.ds(s, n, stride=T), :] = y[bi * n:(bi + 1) * n, s * LANES:(s + 1) * LANES]


def _s5_mixer(geo, xs, modl, ops):
    B, P, D, S, C = geo.B, geo.P, geo.D, geo.S, geo.C
    u = pl.pallas_call(
        _modulate_kernel, grid=(B, geo.nt),
        in_specs=[geo.row_spec(D), geo.mod_spec(0), geo.mod_spec(1)],
        out_specs=geo.row_spec(D), out_shape=jax.ShapeDtypeStruct((B, P, D), F32),
        compiler_params=_cp(2), name="s5_modulate",
    )(xs, modl, modl)
    LB = D // LANES
    TL = S5_CHUNK * LANES
    NS = ops[0][1].shape[-1]
    n = P // S5_CHUNK
    NB = 2 if B % 2 == 0 else 1
    once = pl.Buffered(1)
    ys = []
    for d, (big_m, big_g, big_h, a_t) in enumerate(ops):
        ys.append(pl.pallas_call(
            functools.partial(_s5_kernel, S=S, C=C, reverse=d == 1), grid=(LB, B // NB),
            in_specs=[pl.BlockSpec((NB, P, LANES), lambda l, b: (b, 0, l)),
                      pl.BlockSpec((None, TL, TL), lambda l, b: (l, 0, 0), pipeline_mode=once),
                      pl.BlockSpec((None, TL, NS), lambda l, b: (l, 0, 0), pipeline_mode=once),
                      pl.BlockSpec((None, NS, TL), lambda l, b: (l, 0, 0), pipeline_mode=once),
                      pl.BlockSpec((None, 1, NS), lambda l, b: (l, 0, 0))],
            out_specs=pl.BlockSpec((NB, P, LANES), lambda l, b: (b, 0, l)),
            out_shape=jax.ShapeDtypeStruct((B, P, D), F32),
            scratch_shapes=[pltpu.VMEM((NB * n, NS), F32), pltpu.VMEM((NB * n, NS), F32)],
            compiler_params=_cp(2), name="s5_scan",
        )(u, big_m, big_g, big_h, a_t))
    return u, ys


def _s5_post_kernel(yf_ref, yb_ref, u_ref, d_ref, w_ref, b_ref, x_ref, gate_ref, lng_ref, lnb_ref,
                    sh_ref, sc_ref, wr_ref, xm_ref, f_ref, s_ref, *, alpha):
    g = jax.nn.gelu(yf_ref[0] + yb_ref[0] + d_ref[...] * u_ref[0])
    vg = jnp.dot(g.astype(BF16), w_ref[...], preferred_element_type=F32) + b_ref[...]
    Dn = vg.shape[-1] // 2
    y = vg[:, :Dn] * jax.nn.sigmoid(vg[:, Dn:])
    _finish_mixer(y, x_ref[0], gate_ref[...], lng_ref[...], lnb_ref[...], sh_ref[...], sc_ref[...],
                  wr_ref, xm_ref, f_ref, s_ref, alpha)


def _s5_post(geo, y2, u, d_skip, w_glu, b_glu, xs, modl, lng, lnb, w_router, alpha):
    D = geo.D
    E = w_router.shape[-1]
    out_specs, out_shape = _post_outs(geo, E)
    TM = geo.TM
    return pl.pallas_call(
        functools.partial(_s5_post_kernel, alpha=alpha), grid=(geo.B, geo.nt),
        in_specs=[geo.row_spec(D), geo.row_spec(D),
                  geo.row_spec(D), _full_spec((1, D)), _full_spec((D, 2 * D)), _full_spec((1, 2 * D)),
                  geo.row_spec(D), geo.mod_spec(2), _full_spec((1, D)), _full_spec((1, D)),
                  geo.mod_spec(3), geo.mod_spec(4), _full_spec((E, D))],
        out_specs=out_specs, out_shape=out_shape,
        compiler_params=_cp(2), name="s5_post",
    )(y2[0], y2[1], u, d_skip.reshape(1, D), w_glu.astype(BF16), b_glu.reshape(1, 2 * D), xs, modl,
      lng.reshape(1, D), lnb.reshape(1, D), modl, modl, w_router.T)


def _route_kernel(s_ref, b_ref, idx_ref, w_ref):
    sc = s_ref[0]
    E, TM = sc.shape
    biased = sc + b_ref[...]
    G = N_EXPERT_GROUPS
    per = E // G
    neg = -jnp.inf
    blocks, gs = [], []
    for g in range(G):
        blk = biased[g * per:(g + 1) * per]
        m1 = jnp.max(blk, axis=0, keepdims=True)
        is1 = blk == m1
        cnt = jnp.sum(is1.astype(F32), axis=0, keepdims=True)
        m2 = jnp.max(jnp.where(is1, neg, blk), axis=0, keepdims=True)
        blocks.append(blk)
        gs.append(m1 + jnp.where(cnt >= 2.0, m1, m2))
    masked = []
    for g in range(G):
        ahead = jnp.zeros((1, TM), F32)
        for h in range(G):
            if h < g:
                ahead = ahead + (gs[h] >= gs[g]).astype(F32)
            elif h > g:
                ahead = ahead + (gs[h] > gs[g]).astype(F32)
        masked.append(jnp.where(ahead < float(TOPK_GROUPS), blocks[g], neg))
    masked = jnp.concatenate(masked, axis=0)
    iota_e = lax.broadcasted_iota(jnp.int32, (E, TM), 0)
    idxs, ws = [], []
    for _ in range(TOP_K):
        m = jnp.max(masked, axis=0, keepdims=True)
        ik = jnp.min(jnp.where(masked == m, iota_e, E), axis=0, keepdims=True)
        sel = iota_e == ik
        ws.append(jnp.sum(jnp.where(sel, sc, 0.0), axis=0, keepdims=True))
        idxs.append(ik)
        masked = jnp.where(sel, neg, masked)
    tot = ws[0]
    for wk in ws[1:]:
        tot = tot + wk
    w = jnp.concatenate(ws, axis=0)
    idx_ref[0] = jnp.concatenate(idxs, axis=0)
    w_ref[0] = w / (tot + 1e-20) * ROUTED_SCALE


def _rank_kernel(idx_ref, rank_ref, cnt_ref, run_scr, *, E):
    first = jnp.logical_and(pl.program_id(0) == 0, pl.program_id(1) == 0)

    @pl.when(first)
    def _():
        run_scr[...] = jnp.zeros_like(run_scr)

    idx = idx_ref[0]
    K, TM = idx.shape
    iota_e = lax.broadcasted_iota(jnp.int32, (E, TM), 0)
    member = jnp.zeros((E, TM), F32)
    for k in range(K):
        member = member + (iota_e == idx[k:k + 1]).astype(F32)
    before = (lax.broadcasted_iota(jnp.int32, (TM, TM), 0)
              < lax.broadcasted_iota(jnp.int32, (TM, TM), 1)).astype(BF16)
    rank = jnp.dot(member.astype(BF16), before, preferred_element_type=F32) + run_scr[...]
    rows = [jnp.sum(jnp.where(iota_e == idx[k:k + 1], rank, 0.0), axis=0, keepdims=True) for k in range(K)]
    rank_ref[0] = jnp.concatenate(rows, axis=0).astype(jnp.int32)
    run_scr[...] = run_scr[...] + jnp.sum(member, axis=1, keepdims=True)
    cnt_ref[...] = run_scr[...]


def _dest_kernel(idx_ref, rank_ref, start_ref, dest_ref, *, E):
    idx = idx_ref[0]
    K, TM = idx.shape
    iota_e = lax.broadcasted_iota(jnp.int32, (E, TM), 0)
    start = start_ref[...]
    rows = [jnp.sum(jnp.where(iota_e == idx[k:k + 1], start, 0), axis=0, keepdims=True) for k in range(K)]
    dest_ref[0] = jnp.concatenate(rows, axis=0) + rank_ref[0]


def _route_dispatch(geo, scores, bias, blk):
    B, P = geo.B, geo.P
    E = scores.shape[1]
    K = TOP_K
    kspec = geo.col_spec(K)
    idx, w = pl.pallas_call(
        _route_kernel, grid=(B, geo.nt),
        in_specs=[geo.col_spec(E, whole=True), _full_spec((E, 1))],
        out_specs=[kspec, kspec],
        out_shape=[jax.ShapeDtypeStruct((B, K, P), jnp.int32), jax.ShapeDtypeStruct((B, K, P), F32)],
        compiler_params=_cp(2), name="moe_route",
    )(scores, bias.astype(F32).reshape(E, 1))
    rank, cnt = pl.pallas_call(
        functools.partial(_rank_kernel, E=E), grid=(B, geo.nt),
        in_specs=[kspec], out_specs=[kspec, _full_spec((E, 1))],
        out_shape=[jax.ShapeDtypeStruct((B, K, P), jnp.int32), jax.ShapeDtypeStruct((E, 1), F32)],
        scratch_shapes=[pltpu.VMEM((E, 1), F32)],
        compiler_params=_cp(2), name="moe_rank",
    )(idx)
    n_assign = B * P * K
    n_blocks = -(-(n_assign + E * (blk - 1)) // blk)
    counts = cnt[:, 0].astype(jnp.int32)
    pcounts = (counts + blk - 1) // blk * blk
    pends = jnp.cumsum(pcounts)
    starts = (pends - pcounts).astype(jnp.int32)
    n_used = pends[-1] // blk
    blk_e = jnp.minimum(jnp.searchsorted(pends, jnp.arange(n_blocks) * blk, side='right'), E - 1)
    blk_e = jnp.where(jnp.arange(n_blocks) < n_used, blk_e, blk_e[jnp.maximum(n_used - 1, 0)])
    dest = pl.pallas_call(
        functools.partial(_dest_kernel, E=E), grid=(B, geo.nt),
        in_specs=[kspec, kspec, _full_spec((E, 1))], out_specs=kspec,
        out_shape=jax.ShapeDtypeStruct((B, K, P), jnp.int32),
        compiler_params=_cp(2), name="moe_dest",
    )(idx, rank, starts.reshape(E, 1))
    return w, dest, blk_e.astype(jnp.int32), n_used.astype(jnp.int32).reshape(1), n_blocks


def _expert_kernel(be_ref, nu_ref, x_ref, wgu_ref, wd_ref, *rest):
    del be_ref
    o_ref = rest[-1]
    i = pl.program_id(0)

    @pl.when(i < nu_ref[0])
    def _():
        h = jnp.dot(x_ref[...], wgu_ref[...].astype(BF16), preferred_element_type=F32)
        Fh = h.shape[-1] // 2
        a = _silu(h[:, :Fh]) * h[:, Fh:]
        o_ref[...] = jnp.dot(a.astype(BF16), wd_ref[...].astype(BF16),
                             preferred_element_type=F32).astype(o_ref.dtype)

    @pl.when(i >= nu_ref[0])
    def _():
        o_ref[...] = jnp.zeros_like(o_ref)


def _expert_ffn(x_part, blk_e, n_used, w_gu, w_down, layer, blk, n_rows_all, blk0, prev):
    n_rows, D = x_part.shape
    F2 = w_gu.shape[-1]
    ins = [blk_e, n_used, x_part, w_gu, w_down]
    specs = [pl.BlockSpec((blk, D), lambda i, be, nu: (i, 0)),
             pl.BlockSpec((None, None, D, F2), lambda i, be, nu: (layer, be[i], 0, 0)),
             pl.BlockSpec((None, None, F2 // 2, D), lambda i, be, nu: (layer, be[i], 0, 0))]
    if prev is not None:
        ins.append(prev)
        specs.append(pl.BlockSpec(memory_space=pl.ANY))
    grid_spec = pltpu.PrefetchScalarGridSpec(
        num_scalar_prefetch=2, grid=(n_rows // blk,), in_specs=specs,
        out_specs=pl.BlockSpec((blk, D), lambda i, be, nu: (blk0 + i, 0)))
    return pl.pallas_call(
        _expert_kernel, grid_spec=grid_spec,
        out_shape=jax.ShapeDtypeStruct((n_rows_all, D), BF16),
        input_output_aliases={} if prev is None else {5: 0},
        compiler_params=_cp(1), name="moe_experts",
    )(*ins)


def _moe_final_kernel(*refs, alpha, aliased):
    if aliased:
        xm_ref, f_ref, ga_ref, w_ref, shgu_ref, shd_ref, gate_ref, lng_ref, lnb_ref, _, o_ref = refs
    else:
        xm_ref, f_ref, ga_ref, w_ref, shgu_ref, shd_ref, gate_ref, lng_ref, lnb_ref, o_ref = refs
    h = jnp.dot(f_ref[0], shgu_ref[...], preferred_element_type=F32)
    Fh = h.shape[-1] // 2
    a = _silu(h[:, :Fh]) * h[:, Fh:]
    y = jnp.dot(a.astype(BF16), shd_ref[...], preferred_element_type=F32)
    w = w_ref[0]
    for k in range(w.shape[-1]):
        y = y + w[:, k:k + 1] * ga_ref[k, 0].astype(F32)
    o_ref[0] = _layer_norm(alpha * xm_ref[0] + gate_ref[...] * y, lng_ref[...], lnb_ref[...])


def _moe_group(geo, xm, f, scores, modl, bias, w_gu, w_down, sh_gu, sh_down, lng, lnb, layer, alpha, blk, prev,
               tie=None):
    B, P, D = geo.B, geo.P, geo.D
    T = B * P
    K = TOP_K
    w, dest, blk_e, n_used, n_blocks = _route_dispatch(geo, scores, bias, blk)
    dest_flat = jnp.swapaxes(dest, 0, 1).reshape(K * T)
    t0 = geo.b0 * P
    tok = jnp.broadcast_to(jnp.arange(t0, t0 + T, dtype=jnp.int32)[None], (K, T)).reshape(K * T)
    hit = jnp.zeros((n_blocks * blk,), jnp.int32).at[dest_flat].add(
        tok + 1, unique_indices=True, mode='promise_in_bounds')
    row_tok = jnp.where(hit > 0, hit - 1, t0 + jnp.arange(n_blocks * blk, dtype=jnp.int32) % T)
    f2d = f.reshape(geo.B_all * P, D)
    y_sorted = None
    for part in range(MOE_ROW_PARTS):
        lo = n_blocks * part // MOE_ROW_PARTS
        hi = n_blocks * (part + 1) // MOE_ROW_PARTS
        x_part = f2d.at[row_tok[lo * blk:hi * blk]].get(mode='promise_in_bounds')
        used = jnp.clip(n_used - lo, 0, hi - lo)
        y_sorted = _expert_ffn(x_part, blk_e[lo:hi], used, w_gu, w_down, layer, blk, n_blocks * blk, lo, y_sorted)
    gathered = y_sorted.at[dest_flat].get(unique_indices=True, mode='promise_in_bounds').reshape(K, B, P, D)
    if tie is not None:
        gathered, tie = lax.optimization_barrier((gathered, tie))
    F2 = sh_gu.shape[-1]
    TM = geo.TM
    ins = [xm, f, gathered, jnp.swapaxes(w, 1, 2), sh_gu.astype(BF16), sh_down.astype(BF16), modl,
           lng.reshape(1, D), lnb.reshape(1, D)]
    specs = [geo.row_spec(D, whole=True), geo.row_spec(D, whole=True),
             pl.BlockSpec((K, 1, TM, D), lambda b, i: (0, b, i, 0)), geo.row_spec(K),
             _full_spec((D, F2)), _full_spec((F2 // 2, D)), geo.mod_spec(5),
             _full_spec((1, D)), _full_spec((1, D))]
    if prev is not None:
        ins.append(prev)
        specs.append(pl.BlockSpec(memory_space=pl.ANY))
    out = pl.pallas_call(
        functools.partial(_moe_final_kernel, alpha=alpha, aliased=prev is not None), grid=(B, geo.nt),
        in_specs=specs, out_specs=geo.row_spec(D, whole=True),
        out_shape=jax.ShapeDtypeStruct((geo.B_all, P, D), F32),
        input_output_aliases={} if prev is None else {9: 0},
        compiler_params=_cp(2), name="moe_final",
    )(*ins)
    return out, tie


def _moe(geo, xm, f, scores, modl, bias, w_gu, w_down, sh_gu, sh_down, lng, lnb, layer, alpha, blk, tie=None):
    n_groups = MOE_GROUPS if geo.B % MOE_GROUPS == 0 else 1
    nb = geo.B // n_groups
    out = None
    for g in range(n_groups):
        out, tie = _moe_group(geo.group(g * nb, nb), xm, f, scores, modl, bias, w_gu, w_down, sh_gu, sh_down,
                              lng, lnb, layer, alpha, blk, out, tie)
    return out, tie


def kernel(x, c, ctx, c_ctx, mod_w, mod_b, ln_g, ln_b, rg_w_in, rg_conv_w, rg_conv_b, rg_gate_w, rg_gate_b, rg_lam, rg_w_out, hy_w_in, hy_b_in, hy_short_w, hy_short_b, hy_f_w1, hy_f_b1, hy_f_w2, hy_f_b2, hy_f_w3, hy_f_b3, hy_f_w4, hy_f_freq, hy_f_decay, hy_f_bias, hy_w_out, hy_b_out, da_w_in, da_lam, da_subln, da_w_out, s5_a_re, s5_a_im, s5_log_step, s5_b_re, s5_b_im, s5_c_re, s5_c_im, s5_d, s5_w_glu, s5_b_glu, moe_w_router, moe_bias, moe_w_gu, moe_w_down, moe_sh_gu, moe_sh_down):
    B, S, D = x.shape
    C = ctx.shape[1]
    depth = mod_w.shape[0]
    alpha = (2 * depth) ** 0.25
    geo = _Geo(B, S, C, D)
    xs = jnp.concatenate([x, ctx], axis=1)
    R = -(-(B + 1) // SUBLANES) * SUBLANES
    cc = jnp.zeros((R, D), F32).at[:B].set(c).at[B].set(c_ctx)
    modt = _mod_table(cc, mod_w, mod_b).reshape(depth, 6, R, 1, D)
    blk = MOE_BLOCK
    s5_ops, ties = {}, {}
    for i in range(N_MIXERS - 1, depth, N_MIXERS):
        j = i // N_MIXERS
        s5_ops[i] = _s5_operators(s5_a_re[j], s5_a_im[j], s5_log_step[j], s5_b_re[j], s5_b_im[j],
                                  s5_c_re[j], s5_c_im[j])
        for d in range(2):
            ties[i - 2 + d] = (i, d)
    for i in range(depth):
        kind, j = i % N_MIXERS, i // N_MIXERS
        modl = modt[i]
        post = functools.partial(_post_mixer, geo, xs=xs, modl=modl, lng=ln_g[i, 0], lnb=ln_b[i, 0],
                                 w_router=moe_w_router[i], alpha=alpha)
        if kind == 0:
            y = _rglru_mixer(geo, xs, modl, rg_w_in[j], rg_conv_w[j], rg_conv_b[j], rg_gate_w[j],
                             rg_gate_b[j], rg_lam[j])
            xm, f, scores = post(y=y, w_out=rg_w_out[j], b_out=None)
        elif kind == 1:
            y = _hyena_mixer(geo, xs, modl, hy_w_in[j], hy_b_in[j], hy_short_w[j], hy_short_b[j],
                             hy_f_w1[j], hy_f_b1[j], hy_f_w2[j], hy_f_b2[j], hy_f_w3[j], hy_f_b3[j],
                             hy_f_w4[j], hy_f_freq[j], hy_f_decay[j], hy_f_bias[j])
            xm, f, scores = post(y=y, w_out=hy_w_out[j], b_out=hy_b_out[j])
        elif kind == 2:
            y = _diff_attention_mixer(geo, xs, modl, da_w_in[j], da_lam[j], da_subln[j], i)
            xm, f, scores = post(y=y, w_out=da_w_out[j], b_out=None)
        else:
            u, y2 = _s5_mixer(geo, xs, modl, s5_ops[i])
            xm, f, scores = _s5_post(geo, y2, u, s5_d[j], s5_w_glu[j], s5_b_glu[j], xs, modl,
                                     ln_g[i, 0], ln_b[i, 0], moe_w_router[i], alpha)
        tie = ties.get(i)
        xs, tied = _moe(geo, xm, f, scores, modl, moe_bias[i], moe_w_gu, moe_w_down, moe_sh_gu[i],
                        moe_sh_down[i], ln_g[i, 1], ln_b[i, 1], i, alpha, blk,
                        None if tie is None else s5_ops[tie[0]][tie[1]])
        if tie is not None:
            s5_ops[tie[0]][tie[1]] = tied
    return xs[:, :S]
```

```python
import functools
import math

import jax
import jax.numpy as jnp
from jax import lax
from jax.experimental import pallas as pl
from jax.experimental.pallas import tpu as pltpu

F32 = jnp.float32
BF16 = jnp.bfloat16
HIGHEST = lax.Precision.HIGHEST

N_MIXERS = 4
LN_EPS = 1e-6
LRU_C = 8.0
RG_BLOCK = 128
GRID_W = 64
DA_HEAD_DIM = 64
ROPE_THETA = 10000.0
S5_GROUP = 16
S5_CHUNK = 16
TOP_K = 8
N_EXPERT_GROUPS = 8
TOPK_GROUPS = 4
ROUTED_SCALE = 2.5
MOE_BLOCK = 512
MOE_GROUPS = 1
MOE_ROW_CUTS = (0, 1, 3, 7, 12)
DA_Q_TILE = 1024

LANES = 128
SUBLANES = 8
VMEM_LIMIT = 56 * 1024 * 1024


def _cp(n_grid):
    return pltpu.CompilerParams(dimension_semantics=("arbitrary",) * n_grid,
                                vmem_limit_bytes=VMEM_LIMIT)


def _silu(x):
    return x * jax.nn.sigmoid(x)


def _mod_table_kernel(c_ref, w_ref, b_ref, o_ref):
    s = _silu(c_ref[...])
    o_ref[...] = jnp.dot(s, w_ref[...], precision=HIGHEST, preferred_element_type=F32) + b_ref[...]


def _mod_table(cc, mod_w, mod_b):
    depth, D, _ = mod_w.shape
    R = cc.shape[0]
    return pl.pallas_call(
        _mod_table_kernel,
        grid=(depth, 6),
        in_specs=[pl.BlockSpec((R, D), lambda i, k: (0, 0)),
                  pl.BlockSpec((None, D, D), lambda i, k: (i, 0, k)),
                  pl.BlockSpec((None, None, 1, D), lambda i, k: (i, k, 0, 0))],
        out_specs=pl.BlockSpec((None, None, R, D), lambda i, k: (i, k, 0, 0)),
        out_shape=jax.ShapeDtypeStruct((depth, 6, R, D), F32),
        compiler_params=_cp(2), name="mod_table",
    )(cc, mod_w, mod_b.reshape(depth, 6, 1, D))


class _Geo:
    def __init__(self, B, S, C, D, b0=0, B_all=None):
        self.B, self.S, self.C, self.D = B, S, C, D
        self.b0 = b0
        self.B_all = B if B_all is None else B_all
        self.P = S + C
        self.TM = math.gcd(S, C)
        while self.TM > 256:
            self.TM //= 2
        self.nt = self.P // self.TM
        self.n_lat = S // self.TM

    def group(self, b0, nb):
        return _Geo(nb, self.S, self.C, self.D, b0=b0, B_all=self.B_all)

    def mod_spec(self, k):
        D, b0, Ba, n_lat = self.D, self.b0, self.B_all, self.n_lat
        return pl.BlockSpec((None, None, 1, D),
                            lambda b, i, *_: (k, jnp.where(i < n_lat, b + b0, Ba), 0, 0))

    def row_spec(self, width, col=0, whole=False):
        b0 = self.b0 if whole else 0
        return pl.BlockSpec((1, self.TM, width), lambda b, i, *_: (b + b0, i, col))

    def col_spec(self, height, whole=False):
        b0 = self.b0 if whole else 0
        return pl.BlockSpec((1, height, self.TM), lambda b, i, *_: (b + b0, 0, i))


def _full_spec(shape):
    nd = len(shape)
    return pl.BlockSpec(shape, lambda *_: (0,) * nd)


def _layer_norm(z, g, b):
    mu = jnp.mean(z, axis=-1, keepdims=True)
    zc = z - mu
    var = jnp.mean(zc * zc, axis=-1, keepdims=True)
    return zc * lax.rsqrt(var + LN_EPS) * g + b


def _finish_mixer(y, x, gate, lng, lnb, sh, sc, wr_ref, xm_ref, f_ref, s_ref, alpha):
    xn = _layer_norm(alpha * x + gate * y, lng, lnb)
    xm_ref[0] = xn
    f = xn * (1.0 + sc) + sh
    f_ref[0] = f.astype(f_ref.dtype)
    def nt(a, b):
        return lax.dot_general(a, b, (((1,), (1,)), ((), ())), preferred_element_type=F32)

    w = wr_ref[...]
    w_hi = w.astype(BF16)
    w_lo = (w - w_hi.astype(F32)).astype(BF16)
    f_hi = f.astype(BF16)
    f_lo = (f - f_hi.astype(F32)).astype(BF16)
    logits = nt(w_hi, f_hi) + (nt(w_hi, f_lo) + nt(w_lo, f_hi))
    s_ref[0] = jax.nn.sigmoid(logits)


def _post_kernel(*refs, alpha, has_bias):
    if has_bias:
        (y_ref, w_ref, b_ref, x_ref, gate_ref, lng_ref, lnb_ref, sh_ref, sc_ref, wr_ref,
         xm_ref, f_ref, s_ref) = refs
    else:
        (y_ref, w_ref, x_ref, gate_ref, lng_ref, lnb_ref, sh_ref, sc_ref, wr_ref,
         xm_ref, f_ref, s_ref) = refs
    y = jnp.dot(y_ref[0].astype(BF16), w_ref[...], preferred_element_type=F32)
    if has_bias:
        y = y + b_ref[...]
    _finish_mixer(y, x_ref[0], gate_ref[...], lng_ref[...], lnb_ref[...], sh_ref[...], sc_ref[...],
                  wr_ref, xm_ref, f_ref, s_ref, alpha)


def _post_outs(geo, E):
    B, P, D = geo.B, geo.P, geo.D
    out_specs = [geo.row_spec(D), geo.row_spec(D), geo.col_spec(E)]
    out_shape = [jax.ShapeDtypeStruct((B, P, D), F32), jax.ShapeDtypeStruct((B, P, D), BF16),
                 jax.ShapeDtypeStruct((B, E, P), F32)]
    return out_specs, out_shape


def _post_mixer(geo, y, w_out, b_out, xs, modl, lng, lnb, w_router, alpha):
    D = geo.D
    Kd = y.shape[-1]
    E = w_router.shape[-1]
    has_bias = b_out is not None
    ins = [y, w_out.astype(BF16)]
    specs = [geo.row_spec(Kd), _full_spec((Kd, D))]
    if has_bias:
        ins.append(b_out.reshape(1, D))
        specs.append(_full_spec((1, D)))
    ins += [xs, modl, lng.reshape(1, D), lnb.reshape(1, D), modl, modl, w_router.T]
    specs += [geo.row_spec(D), geo.mod_spec(2), _full_spec((1, D)), _full_spec((1, D)),
              geo.mod_spec(3), geo.mod_spec(4), _full_spec((E, D))]
    out_specs, out_shape = _post_outs(geo, E)
    return pl.pallas_call(
        functools.partial(_post_kernel, alpha=alpha, has_bias=has_bias),
        grid=(geo.B, geo.nt), in_specs=specs, out_specs=out_specs, out_shape=out_shape,
        compiler_params=_cp(2), name="post_mixer",
    )(*ins)


def _dwconv_seg(r, pad_scr, cw, cb, S, C, lo):
    P = S + C
    n = r.shape[-1]
    pad_scr[0:SUBLANES, :] = jnp.zeros((SUBLANES, n), F32)
    pad_scr[SUBLANES + P:, :] = jnp.zeros((SUBLANES, n), F32)
    pad_scr[SUBLANES:SUBLANES + P, :] = r
    row = lax.broadcasted_iota(jnp.int32, r.shape, 0)
    tl = jnp.where(row < S, row, row - S)
    sl = jnp.where(row < S, S, C)
    acc = jnp.zeros_like(r) + cb
    for k in range(cw.shape[0]):
        off = k - lo
        if off == 0:
            term = r
        else:
            shifted = pad_scr[SUBLANES + off:SUBLANES + off + P, :]
            valid = jnp.logical_and(tl + off >= 0, tl + off < sl)
            term = jnp.where(valid, shifted, 0.0)
        acc = acc + cw[k:k + 1, :] * term
    return acc


def _rg_in_kernel(x_ref, sh_ref, sc_ref, wg_ref, wr_ref, g_ref, r_ref):
    h = (x_ref[0] * (1.0 + sc_ref[...]) + sh_ref[...]).astype(BF16)
    g = jnp.dot(h, wg_ref[...], preferred_element_type=F32)
    g_ref[0] = jax.nn.gelu(g).astype(g_ref.dtype)
    r_ref[0] = jnp.dot(h, wr_ref[...], preferred_element_type=F32)


def _rg_scan_kernel(r_ref, g_ref, cw_ref, cb_ref, gw_ref, gb_ref, lam_ref, o_ref,
                    a_scr, b_scr, pad_scr, al_scr, bl_scr, cin_scr, *, S, C):
    P = S + C
    NT = P // SUBLANES
    hsum = None
    n = r_ref.shape[-1]
    r = r_ref[0]
    rc = _dwconv_seg(r, pad_scr, cw_ref[...], cb_ref[...], S, C, cw_ref.shape[0] // 2)
    rcb = rc.astype(BF16)
    row = lax.broadcasted_iota(jnp.int32, (P, n), 0)
    sub = jnp.bitwise_and(row, SUBLANES - 1)
    for scr in (a_scr, b_scr):
        scr[0:SUBLANES, :] = jnp.zeros((SUBLANES, n), F32)
        scr[SUBLANES + P:, :] = jnp.zeros((SUBLANES, n), F32)
    for d in range(2):
        rev = d == 1
        gr = jax.nn.sigmoid(jnp.dot(rcb, gw_ref[d, 0], preferred_element_type=F32) + gb_ref[d, 0])
        gi = jax.nn.sigmoid(jnp.dot(rcb, gw_ref[d, 1], preferred_element_type=F32) + gb_ref[d, 1])
        nl = -lam_ref[d]
        sp = jnp.maximum(nl, 0.0) + jnp.log1p(jnp.exp(-jnp.abs(nl)))
        a = jnp.exp(-LRU_C * gr * sp)
        om = 1.0 - a * a
        bb = om * lax.rsqrt(jnp.maximum(om, 1e-30)) * gi * rc
        for s in (1, 2, 4):
            a_scr[SUBLANES:SUBLANES + P, :] = a
            b_scr[SUBLANES:SUBLANES + P, :] = bb
            lo = SUBLANES + (s if rev else -s)
            a_sh = a_scr[lo:lo + P, :]
            b_sh = b_scr[lo:lo + P, :]
            m = (sub < SUBLANES - s) if rev else (sub >= s)
            bb = jnp.where(m, a * b_sh + bb, bb)
            a = jnp.where(m, a * a_sh, a)
        a_scr[SUBLANES:SUBLANES + P, :] = a
        b_scr[SUBLANES:SUBLANES + P, :] = bb
        edge = SUBLANES + (0 if rev else SUBLANES - 1)
        al_scr[...] = a_scr[pl.ds(edge, NT, stride=SUBLANES), :]
        bl_scr[...] = b_scr[pl.ds(edge, NT, stride=SUBLANES), :]

        def chain(lo_tile, n_tiles, c0):
            def body(i, c):
                t = (lo_tile + n_tiles - 1 - i) if rev else (lo_tile + i)
                cin_scr[pl.ds(t, 1), :] = c
                return bl_scr[pl.ds(t, 1), :] + al_scr[pl.ds(t, 1), :] * c
            return lax.fori_loop(0, n_tiles, body, c0, unroll=4)

        c_ctx = chain(S // SUBLANES, C // SUBLANES, jnp.zeros((1, n), F32))
        chain(0, S // SUBLANES, c_ctx)
        cin = cin_scr[...]
        for j in range(SUBLANES):
            pad_scr[pl.ds(j, NT, stride=SUBLANES), :] = cin
        h = bb + a * pad_scr[0:P, :]
        hsum = h if d == 0 else hsum + h
    o_ref[0] = (g_ref[0].astype(F32) * hsum).astype(o_ref.dtype)


def _rglru_mixer(geo, xs, modl, w_in, conv_w, conv_b, gate_w, gate_b, lam):
    B, P, D, S, C = geo.B, geo.P, geo.D, geo.S, geo.C
    R = w_in.shape[1] // 2
    nb = R // RG_BLOCK
    w_in = w_in.astype(BF16)
    g, r = pl.pallas_call(
        _rg_in_kernel, grid=(B, geo.nt),
        in_specs=[geo.row_spec(D), geo.mod_spec(0), geo.mod_spec(1),
                  pl.BlockSpec((D, R), lambda b, i: (0, 0)), pl.BlockSpec((D, R), lambda b, i: (0, 1))],
        out_specs=[geo.row_spec(R), geo.row_spec(R)],
        out_shape=[jax.ShapeDtypeStruct((B, P, R), BF16), jax.ShapeDtypeStruct((B, P, R), F32)],
        compiler_params=_cp(2), name="rg_in",
    )(xs, modl, modl, w_in, w_in)
    K = conv_w.shape[0]
    seq_spec = pl.BlockSpec((1, P, RG_BLOCK), lambda b, n: (b, 0, n))
    y = pl.pallas_call(
        functools.partial(_rg_scan_kernel, S=S, C=C), grid=(B, nb),
        in_specs=[seq_spec, seq_spec,
                  pl.BlockSpec((K, RG_BLOCK), lambda b, n: (0, n)),
                  pl.BlockSpec((1, RG_BLOCK), lambda b, n: (0, n)),
                  pl.BlockSpec((2, 2, None, RG_BLOCK, RG_BLOCK), lambda b, n: (0, 0, n, 0, 0)),
                  pl.BlockSpec((2, 2, 1, RG_BLOCK), lambda b, n: (0, 0, 0, n)),
                  pl.BlockSpec((2, 1, RG_BLOCK), lambda b, n: (0, 0, n))],
        out_specs=seq_spec,
        out_shape=jax.ShapeDtypeStruct((B, P, R), BF16),
        scratch_shapes=([pltpu.VMEM((P + 2 * SUBLANES, RG_BLOCK), F32)] * 3
                        + [pltpu.VMEM((P // SUBLANES, RG_BLOCK), F32)] * 3),
        compiler_params=_cp(2), name="rg_scan",
    )(r, g, conv_w, conv_b.reshape(1, R), gate_w.astype(BF16), gate_b.reshape(2, 2, 1, R),
      lam.reshape(2, 1, R))
    return y


def _mm_bias_kernel(x_ref, sh_ref, sc_ref, w_ref, b_ref, o_ref):
    h = (x_ref[0] * (1.0 + sc_ref[...]) + sh_ref[...]).astype(BF16)
    o_ref[0] = (jnp.dot(h, w_ref[...], preferred_element_type=F32) + b_ref[...]).astype(o_ref.dtype)


def _short_conv_kernel(u_ref, cw_ref, cb_ref, o_ref, ob_ref, pad_scr, *, S, C):
    y = _dwconv_seg(u_ref[0], pad_scr, cw_ref[...], cb_ref[...], S, C, (cw_ref.shape[0] - 1) // 2)
    o_ref[0] = y
    ob_ref[0] = y.astype(BF16)


def _dft_table_kernel(c_ref, s_ref, st_ref, *, L, TF):
    i = pl.program_id(0)
    N = 2 * L
    f = lax.broadcasted_iota(jnp.int32, (TF, L), 0) + i * TF
    t = lax.broadcasted_iota(jnp.int32, (TF, L), 1)
    ang = jnp.bitwise_and(f * t, N - 1).astype(F32) * (2.0 * math.pi / N)
    c_ref[...] = jnp.cos(ang).astype(BF16)
    nyq_t = (1 - 2 * jnp.bitwise_and(t, 1)).astype(F32)
    s_ref[...] = jnp.where(f == 0, nyq_t, jnp.sin(ang)).astype(BF16)
    nyq_f = (1 - 2 * jnp.bitwise_and(f, 1)).astype(F32)
    st_ref[...] = jnp.where(t == 0, nyq_f, jnp.sin(ang)).astype(BF16)


def _dft_tables(L):
    TF = min(L, 256)
    shp = jax.ShapeDtypeStruct((L, L), BF16)
    spec = pl.BlockSpec((TF, L), lambda i: (i, 0))
    return pl.pallas_call(
        functools.partial(_dft_table_kernel, L=L, TF=TF), grid=(L // TF,),
        in_specs=[], out_specs=[spec, spec, spec], out_shape=[shp, shp, shp],
        compiler_params=_cp(1), name="dft_tables",
    )()


def _hy_filter_kernel(z_ref, w1_ref, b1_ref, w2_ref, b2_ref, w3_ref, b3_ref, fq_ref,
                      w4f_ref, w4b_ref, df_ref, db_ref, tn_ref, kp_ref, km_ref):
    fq = fq_ref[...]

    def lin(h, w_ref, b_ref):
        return jnp.dot(h, w_ref[...], precision=HIGHEST, preferred_element_type=F32) + b_ref[...]

    h = jnp.sin(fq * lin(z_ref[...], w1_ref, b1_ref))
    h = jnp.sin(fq * lin(h, w2_ref, b2_ref))
    h = jnp.sin(fq * lin(h, w3_ref, b3_ref))
    tn = tn_ref[...]
    hf = jnp.dot(h, w4f_ref[...], precision=HIGHEST, preferred_element_type=F32)
    hf = hf * jnp.exp(-tn * jnp.abs(df_ref[...]))
    hb = jnp.dot(h, w4b_ref[...], precision=HIGHEST, preferred_element_type=F32)
    hb = hb * jnp.exp(-tn * jnp.abs(db_ref[...]))
    row = lax.broadcasted_iota(jnp.int32, hb.shape, 0)
    hb = jnp.where(row == 0, 0.0, hb)
    nrm = lax.rsqrt(jnp.sum(hf * hf, axis=0, keepdims=True) + jnp.sum(hb * hb, axis=0, keepdims=True) + 1e-6)
    hf = hf * nrm
    hb = hb * nrm
    kp_ref[...] = (hf + hb).astype(BF16)
    km_ref[...] = (hf - hb).astype(BF16)


def _hy_spectrum_kernel(c_ref, s_ref, s0_ref, kp_ref, km_ref, ka_ref, kb_ref, kc_ref, *, L, TF):
    i = pl.program_id(0)
    inv_n = 1.0 / (2 * L)
    kr = jnp.dot(c_ref[...], kp_ref[...], preferred_element_type=F32)
    ks = jnp.dot(s_ref[...], km_ref[...], preferred_element_type=F32)
    nyq = jnp.dot(s0_ref[...], kp_ref[...], preferred_element_type=F32)[0:1]
    f = lax.broadcasted_iota(jnp.int32, kr.shape, 0) + i * TF
    dc = f == 0
    ka_ref[...] = jnp.where(dc, kr * inv_n, kr * (2.0 * inv_n))
    kb_ref[...] = jnp.where(dc, 0.0, ks * (-2.0 * inv_n))
    kc_ref[...] = jnp.where(dc, nyq * inv_n, kr * (2.0 * inv_n))


def _hy_filters(L, tabs, fw1, fb1, fw2, fb2, fw3, fb3, fw4, ffreq, fdecay, D):
    cm, sm, _ = tabs
    E = fw1.shape[0]
    Hd = fw1.shape[1]
    bands = (E - 1) // 2
    t = jnp.arange(L, dtype=F32)
    t_norm = t / max(L - 1, 1)
    fr = jnp.linspace(1e-4, bands - 1, bands, dtype=F32)
    ang = (2.0 * math.pi / L) * t[:, None] * fr[None, :]
    z = jnp.concatenate([t_norm[:, None], jnp.cos(ang), -jnp.sin(ang)], -1)
    Ep, Hp = -(-E // LANES) * LANES, -(-Hd // LANES) * LANES
    z = jnp.pad(z, ((0, 0), (0, Ep - E)))
    fw1 = jnp.pad(fw1, ((0, Ep - E), (0, Hp - Hd)))
    fw2 = jnp.pad(fw2, ((0, Hp - Hd), (0, Hp - Hd)))
    fw3 = jnp.pad(fw3, ((0, Hp - Hd), (0, Hp - Hd)))
    fw4 = jnp.pad(fw4, ((0, Hp - Hd), (0, 0)))
    fb1, fb2, fb3, ffreq = (jnp.pad(v, (0, Hp - Hd)) for v in (fb1, fb2, fb3, ffreq))
    E, Hd = Ep, Hp
    CT = min(2 * D, 512)
    nct = 2 * D // CT
    dec = fdecay.reshape(1, 4 * D)
    kp, km = pl.pallas_call(
        _hy_filter_kernel, grid=(nct,),
        in_specs=[_full_spec((L, E)), _full_spec((E, Hd)), _full_spec((1, Hd)), _full_spec((Hd, Hd)),
                  _full_spec((1, Hd)), _full_spec((Hd, Hd)), _full_spec((1, Hd)), _full_spec((1, Hd)),
                  pl.BlockSpec((Hd, CT), lambda j: (0, j)), pl.BlockSpec((Hd, CT), lambda j: (0, j + nct)),
                  pl.BlockSpec((1, CT), lambda j: (0, j)), pl.BlockSpec((1, CT), lambda j: (0, j + nct)),
                  _full_spec((L, 1))],
        out_specs=[pl.BlockSpec((L, CT), lambda j: (0, j))] * 2,
        out_shape=[jax.ShapeDtypeStruct((L, 2 * D), BF16)] * 2,
        compiler_params=_cp(1), name="hy_filter",
    )(z, fw1, fb1.reshape(1, Hd), fw2, fb2.reshape(1, Hd), fw3, fb3.reshape(1, Hd), ffreq.reshape(1, Hd),
      fw4, fw4, dec, dec, t_norm[:, None])
    TF = min(L, 256)
    spec_w = pl.BlockSpec((TF, L), lambda i, j: (i, 0))
    spec_k = pl.BlockSpec((L, CT), lambda i, j: (0, j))
    spec_o = pl.BlockSpec((TF, CT), lambda i, j: (i, j))
    shp = jax.ShapeDtypeStruct((L, 2 * D), F32)
    return pl.pallas_call(
        functools.partial(_hy_spectrum_kernel, L=L, TF=TF), grid=(L // TF, nct),
        in_specs=[spec_w, spec_w, pl.BlockSpec((SUBLANES, L), lambda i, j: (0, 0)), spec_k, spec_k],
        out_specs=[spec_o] * 3, out_shape=[shp] * 3,
        compiler_params=_cp(2), name="hy_spectrum",
    )(cm, sm, sm, kp, km)


def _hy_fwd_kernel(z_ref, c_ref, s_ref, ka_ref, kb_ref, kc_ref, p_ref):
    z = z_ref[0]
    zr = jnp.dot(c_ref[...], z, preferred_element_type=F32)
    zs = jnp.dot(s_ref[...], z, preferred_element_type=F32)
    kb = kb_ref[...]
    p_ref[0, 0] = (zr * ka_ref[...] + zs * kb).astype(BF16)
    p_ref[0, 1] = (zs * kc_ref[...] - zr * kb).astype(BF16)


def _hy_inv_kernel(p_ref, c_ref, st_ref, z_ref, x_ref, fb_ref, o_ref):
    y = jnp.dot(c_ref[...], p_ref[0, 0], preferred_element_type=F32)
    y = y + jnp.dot(st_ref[...], p_ref[0, 1], preferred_element_type=F32)
    o_ref[0] = (x_ref[0] * (y + z_ref[0].astype(F32) * fb_ref[...])).astype(o_ref.dtype)


def _hy_inv_kernel_alias(p_ref, c_ref, st_ref, z_ref, x_ref, fb_ref, prev_ref, o_ref):
    del prev_ref
    _hy_inv_kernel(p_ref, c_ref, st_ref, z_ref, x_ref, fb_ref, o_ref)


def _hy_conv(geo, L, off, tabs, z, z_col, kfilt, k_col, xmul, x_col, fbias, out):
    B, P, D = geo.B, geo.P, geo.D
    cm, sm, smt = tabs
    ka, kb, kc = kfilt
    rb = off // L
    TF = min(L, 256)
    spec_w = pl.BlockSpec((TF, L), lambda b, i: (i, 0))
    spec_k = pl.BlockSpec((TF, D), lambda b, i: (i, k_col))
    p = pl.pallas_call(
        _hy_fwd_kernel, grid=(B, L // TF),
        in_specs=[pl.BlockSpec((1, L, D), lambda b, i: (b, rb, z_col)), spec_w, spec_w,
                  spec_k, spec_k, spec_k],
        out_specs=pl.BlockSpec((1, 2, TF, D), lambda b, i: (b, 0, i, 0)),
        out_shape=jax.ShapeDtypeStruct((B, 2, L, D), BF16),
        compiler_params=_cp(2), name="hy_fwd",
    )(z, cm, sm, ka, kb, kc)
    CT = D
    nct = D // CT
    rt = off // TF
    spec_wi = pl.BlockSpec((TF, L), lambda b, j, i: (i, 0))
    return pl.pallas_call(
        _hy_inv_kernel_alias, grid=(B, nct, L // TF),
        in_specs=[pl.BlockSpec((1, 2, L, CT), lambda b, j, i: (b, 0, 0, j), pipeline_mode=pl.Buffered(1)),
                  spec_wi, spec_wi,
                  pl.BlockSpec((1, TF, CT), lambda b, j, i: (b, rt + i, z_col * nct + j)),
                  pl.BlockSpec((1, TF, CT), lambda b, j, i: (b, rt + i, x_col * nct + j)),
                  pl.BlockSpec((1, CT), lambda b, j, i: (0, k_col * nct + j)),
                  pl.BlockSpec(memory_space=pl.ANY)],
        out_specs=pl.BlockSpec((1, TF, CT), lambda b, j, i: (b, rt + i, j)),
        out_shape=jax.ShapeDtypeStruct((B, P, D), BF16),
        input_output_aliases={6: 0},
        compiler_params=_cp(3), name="hy_inv",
    )(p, cm, smt, z, xmul, fbias, out)


def _hyena_mixer(geo, xs, modl, w_in, b_in, short_w, short_b, fw1, fb1, fw2, fb2, fw3, fb3, fw4,
                 ffreq, fdecay, fbias):
    B, P, D, S, C = geo.B, geo.P, geo.D, geo.S, geo.C
    u0 = pl.pallas_call(
        _mm_bias_kernel, grid=(B, geo.nt, 3),
        in_specs=[geo.row_spec(D), geo.mod_spec(0), geo.mod_spec(1),
                  pl.BlockSpec((D, D), lambda b, i, j: (0, j)), pl.BlockSpec((1, D), lambda b, i, j: (0, j))],
        out_specs=pl.BlockSpec((1, geo.TM, D), lambda b, i, j: (b, i, j)),
        out_shape=jax.ShapeDtypeStruct((B, P, 3 * D), F32),
        compiler_params=_cp(3), name="hy_in",
    )(xs, modl, modl, w_in.astype(BF16), b_in.reshape(1, 3 * D))
    CT = min(D, 256)
    Ks = short_w.shape[0]
    spec = pl.BlockSpec((1, P, CT), lambda b, j: (b, 0, j))
    u, ub = pl.pallas_call(
        functools.partial(_short_conv_kernel, S=S, C=C), grid=(B, 3 * D // CT),
        in_specs=[spec, pl.BlockSpec((Ks, CT), lambda b, j: (0, j)), pl.BlockSpec((1, CT), lambda b, j: (0, j))],
        out_specs=[spec, spec],
        out_shape=[jax.ShapeDtypeStruct((B, P, 3 * D), F32), jax.ShapeDtypeStruct((B, P, 3 * D), BF16)],
        scratch_shapes=[pltpu.VMEM((P + 2 * SUBLANES, CT), F32)],
        compiler_params=_cp(2), name="hy_short",
    )(u0, short_w, short_b.reshape(1, 3 * D))
    fb = fbias.reshape(1, 2 * D)
    z1 = jnp.zeros((B, P, D), BF16)
    z2 = jnp.zeros((B, P, D), BF16)
    segs = [(S, 0), (C, S)]
    convs = []
    for L, off in segs:
        tabs = _dft_tables(L)
        kf = _hy_filters(L, tabs, fw1, fb1, fw2, fb2, fw3, fb3, fw4, ffreq, fdecay, D)
        convs.append((L, off, tabs, kf))
    for L, off, tabs, kf in convs:
        z1 = _hy_conv(geo, L, off, tabs, ub, 0, kf, 0, u, 1, fb, z1)
    for L, off, tabs, kf in convs:
        z2 = _hy_conv(geo, L, off, tabs, z1, 0, kf, 1, u, 2, fb, z2)
    return z2


def _rope_tables(S, D):
    t = jnp.arange(S)
    row = (t // GRID_W).astype(F32)
    col = (t % GRID_W).astype(F32)
    axis_dim = DA_HEAD_DIM // 2
    half = axis_dim // 2
    inv = ROPE_THETA ** (-jnp.arange(0, axis_dim, 2, dtype=F32) / axis_dim)
    lane = jnp.arange(D)
    within = lane % DA_HEAD_DIM
    pos = jnp.where((within // axis_dim)[None, :] == 0, row[:, None], col[:, None])
    ang = pos * inv[lane % half][None, :]
    sign = jnp.where((lane % axis_dim) < half, -1.0, 1.0)[None, :]
    return jnp.cos(ang), jnp.sin(ang) * sign


def _da_in_kernel(x_ref, sh_ref, sc_ref, w_ref, cos_ref, sin_ref, o_ref, *, n_lat):
    i = pl.program_id(1)
    j = pl.program_id(2)
    h = (x_ref[0] * (1.0 + sc_ref[...]) + sh_ref[...]).astype(BF16)
    acc = jnp.dot(h, w_ref[...], preferred_element_type=F32)
    acc = acc * jnp.where(j == 0, DA_HEAD_DIM ** -0.5, 1.0)
    rot = jnp.logical_and(i < n_lat, j < 2)

    @pl.when(rot)
    def _():
        Dn = acc.shape[-1]
        half = DA_HEAD_DIM // 4
        lane = lax.broadcasted_iota(jnp.int32, acc.shape, 1)
        up = pltpu.roll(acc, Dn - half, 1)
        dn = pltpu.roll(acc, half, 1)
        partner = jnp.where(jnp.bitwise_and(lane, 2 * half - 1) < half, up, dn)
        o_ref[0] = (acc * cos_ref[...] + partner * sin_ref[...]).astype(o_ref.dtype)

    @pl.when(jnp.logical_not(rot))
    def _():
        o_ref[0] = acc.astype(o_ref.dtype)


def _da_attn_kernel(*refs, kv_lo, nk, lam_init, aliased):
    if aliased:
        q_ref, k_ref, v_ref, lam_ref, sub_ref, _, o_ref, vx_scr, s_scr = refs
    else:
        q_ref, k_ref, v_ref, lam_ref, sub_ref, o_ref, vx_scr, s_scr = refs
    i = pl.program_id(2)
    HW = v_ref.shape[-1]
    TQ = q_ref.shape[1]
    lp = lam_ref[...]
    lam = (jnp.exp(jnp.sum(lp[0:1] * lp[1:2], axis=1, keepdims=True))
           - jnp.exp(jnp.sum(lp[2:3] * lp[3:4], axis=1, keepdims=True)) + lam_init)

    @pl.when(i == 0)
    def _():
        vx_scr[:, :HW] = v_ref[0]
        vx_scr[:, HW:] = jnp.ones((vx_scr.shape[0], HW), BF16)

    lane = lax.broadcasted_iota(jnp.int32, (TQ, HW), 1)
    q = q_ref[0]
    k = k_ref[0, kv_lo:kv_lo + nk, :]
    outs = []
    for c in range(2):
        qc = jnp.where((lane // DA_HEAD_DIM) == c, q, jnp.zeros_like(q))
        s_scr[...] = lax.dot_general(qc, k, (((1,), (1,)), ((), ())), preferred_element_type=F32)
        m = jnp.max(s_scr[...], axis=-1, keepdims=True)
        p = jnp.exp((s_scr[...] - m).astype(BF16))
        ov = jnp.dot(p, vx_scr[kv_lo:kv_lo + nk, :], preferred_element_type=F32)
        outs.append(ov[:, :HW] / ov[:, HW:HW + 1])
    o = outs[0] - lam * outs[1]
    o = o * lax.rsqrt(jnp.mean(o * o, axis=-1, keepdims=True) + 1e-5) * sub_ref[...] * (1.0 - lam_init)
    o_ref[0] = o.astype(o_ref.dtype)


def _diff_attention_mixer(geo, xs, modl, w_in, lam_p, subln_w, layer_idx):
    B, P, D, S, C = geo.B, geo.P, geo.D, geo.S, geo.C
    H = D // (2 * DA_HEAD_DIM)
    HW = 2 * DA_HEAD_DIM
    cos_t, sin_t = _rope_tables(S, D)
    n_lat = geo.n_lat
    tab_spec = pl.BlockSpec((geo.TM, D), lambda b, i, j: (jnp.minimum(i, n_lat - 1), 0))
    qkv = pl.pallas_call(
        functools.partial(_da_in_kernel, n_lat=n_lat), grid=(B, geo.nt, 3),
        in_specs=[geo.row_spec(D), geo.mod_spec(0), geo.mod_spec(1),
                  pl.BlockSpec((D, D), lambda b, i, j: (0, j)), tab_spec, tab_spec],
        out_specs=pl.BlockSpec((1, geo.TM, D), lambda b, i, j: (b, i, j)),
        out_shape=jax.ShapeDtypeStruct((B, P, 3 * D), BF16),
        compiler_params=_cp(3), name="da_in",
    )(xs, modl, modl, w_in.astype(BF16), cos_t, sin_t)
    lam_init = 0.8 - 0.6 * math.exp(-0.3 * layer_idx)
    def attend(TQ, row0, n_tiles, kv_lo, nk, prev):
        rb = row0 // TQ
        ins = [qkv, qkv, qkv, lam_p, subln_w.reshape(1, HW)]
        specs = [pl.BlockSpec((1, TQ, HW), lambda b, h, i: (b, rb + i, h)),
                 pl.BlockSpec((1, P, HW), lambda b, h, i: (b, 0, H + h)),
                 pl.BlockSpec((1, P, HW), lambda b, h, i: (b, 0, 2 * H + h)),
                 _full_spec((4, DA_HEAD_DIM)), _full_spec((1, HW))]
        if prev is not None:
            ins.append(prev)
            specs.append(pl.BlockSpec(memory_space=pl.ANY))
        return pl.pallas_call(
            functools.partial(_da_attn_kernel, kv_lo=kv_lo, nk=nk, lam_init=lam_init, aliased=prev is not None),
            grid=(B, H, n_tiles), in_specs=specs,
            out_specs=pl.BlockSpec((1, TQ, HW), lambda b, h, i: (b, rb + i, h)),
            out_shape=jax.ShapeDtypeStruct((B, P, D), BF16),
            scratch_shapes=[pltpu.VMEM((P, 2 * HW), BF16), pltpu.VMEM((TQ, nk), F32)],
            input_output_aliases={} if prev is None else {5: 0},
            compiler_params=_cp(3), name="da_attn",
        )(*ins)

    TQ = geo.TM
    while TQ < DA_Q_TILE and S % (2 * TQ) == 0:
        TQ *= 2
    y = attend(TQ, 0, S // TQ, 0, P, None)
    return attend(geo.TM, S, C // geo.TM, S, C, y)


def _s5_operators(a_re, a_im, log_step, b_re, b_im, c_re, c_im):
    T = S5_CHUNK
    G, Pst = a_re.shape[1], a_re.shape[2]
    Hg = S5_GROUP
    GL = LANES // Hg
    LB = G // GL
    lam = lax.complex(jnp.minimum(a_re.astype(F32), -1e-4), a_im.astype(F32))
    step = jnp.exp(log_step.astype(F32))[..., None]
    abar = jnp.exp(lam * step)
    bbar = ((abar - 1.0) / lam)[..., None] * lax.complex(b_re.astype(F32), b_im.astype(F32))
    cmat = lax.complex(c_re.astype(F32), c_im.astype(F32))
    pows = jnp.stack([abar ** l for l in range(T + 1)], axis=1)
    eye = jnp.eye(GL, dtype=F32)
    ar = jnp.arange(T)
    ops = []
    for d in range(2):
        pw = pows[d]
        kl = jnp.einsum('gjp,lgp,gph->lgjh', cmat[d], pw[:T], bbar[d]).real
        lag = (ar[None, :] - ar[:, None]) if d == 0 else (ar[:, None] - ar[None, :])
        tz = jnp.where((lag >= 0)[:, :, None, None, None], kl[jnp.clip(lag, 0, T - 1)], 0.0)
        tz = tz.reshape(T, T, LB, GL, Hg, Hg).astype(BF16)
        m = jnp.einsum('stbgjh,gk->bsghtkj', tz, eye.astype(BF16)).reshape(LB, T * LANES, T * LANES)
        e_in = (T - 1 - ar) if d == 0 else ar
        gc = pw[e_in][:, :, :, None] * bbar[d][None]
        gc = gc.reshape(T, LB, GL, Pst, Hg)
        eyeb = eye.astype(BF16)
        g_re = jnp.einsum('sbgph,gk->bsghkp', gc.real.astype(BF16), eyeb).reshape(LB, T * LANES, GL * Pst)
        g_im = jnp.einsum('sbgph,gk->bsghkp', gc.imag.astype(BF16), eyeb).reshape(LB, T * LANES, GL * Pst)
        e_out = (ar + 1) if d == 0 else (T - ar)
        hc = cmat[d][None] * pw[e_out][:, :, None, :]
        hc = hc.reshape(T, LB, GL, Hg, Pst)
        h_re = jnp.einsum('tbgjp,gk->bgptkj', hc.real.astype(BF16), eyeb).reshape(LB, GL * Pst, T * LANES)
        h_im = jnp.einsum('tbgjp,gk->bgptkj', (-hc.imag).astype(BF16), eyeb).reshape(LB, GL * Pst, T * LANES)
        at = pw[T].reshape(LB, 1, GL * Pst)
        ops.append((m, jnp.concatenate([g_re, g_im], axis=2), jnp.concatenate([h_re, h_im], axis=1),
                    jnp.concatenate([at.real, at.imag], axis=2).astype(F32)))
    return ops


def _modulate_kernel(x_ref, sh_ref, sc_ref, o_ref):
    o_ref[0] = x_ref[0] * (1.0 + sc_ref[...]) + sh_ref[...]


def _s5_kernel(u_ref, m_ref, g_ref, h_ref, a_ref, o_ref, gx_scr, sp_scr, *, S, C, reverse):
    T = S5_CHUNK
    P = S + C
    n = P // T
    n_lat = S // T
    NB = u_ref.shape[0]
    x = jnp.concatenate(
        [jnp.concatenate([u_ref[bi, pl.ds(s, n, stride=T), :] for s in range(T)], axis=1) for bi in range(NB)],
        axis=0).astype(BF16)
    gx_scr[...] = jnp.dot(x, g_ref[...], preferred_element_type=F32)
    ns = a_ref.shape[-1] // 2
    a_r = a_ref[:, :ns]
    a_i = a_ref[:, ns:]

    def scan(lo, cnt, rev, carry):
        def body(k, st):
            c = (lo + cnt - 1 - k) if rev else (lo + k)
            new = []
            for bi in range(NB):
                s_r, s_i = st[2 * bi], st[2 * bi + 1]
                sp_scr[pl.ds(bi * n + c, 1), :] = jnp.concatenate([s_r, s_i], axis=1)
                gx = gx_scr[pl.ds(bi * n + c, 1), :]
                new += [a_r * s_r - a_i * s_i + gx[:, :ns], a_r * s_i + a_i * s_r + gx[:, ns:]]
            return tuple(new)
        return lax.fori_loop(0, cnt, body, carry)

    zero = tuple(jnp.zeros((1, ns), F32) for _ in range(2 * NB))
    scan(0, n_lat, reverse, scan(n_lat, n - n_lat, reverse, zero))

    y = jnp.dot(x, m_ref[...], preferred_element_type=F32)
    y = y + jnp.dot(sp_scr[...].astype(BF16), h_ref[...], preferred_element_type=F32)
    for bi in range(NB):
        for s in range(T):
            o_ref[bi, pl.ds(s, n, stride=T), :] = y[bi * n:(bi + 1) * n, s * LANES:(s + 1) * LANES]


def _s5_mixer(geo, xs, modl, ops):
    B, P, D, S, C = geo.B, geo.P, geo.D, geo.S, geo.C
    u = pl.pallas_call(
        _modulate_kernel, grid=(B, geo.nt),
        in_specs=[geo.row_spec(D), geo.mod_spec(0), geo.mod_spec(1)],
        out_specs=geo.row_spec(D), out_shape=jax.ShapeDtypeStruct((B, P, D), F32),
        compiler_params=_cp(2), name="s5_modulate",
    )(xs, modl, modl)
    LB = D // LANES
    TL = S5_CHUNK * LANES
    NS = ops[0][1].shape[-1]
    n = P // S5_CHUNK
    NB = 2 if B % 2 == 0 else 1
    once = pl.Buffered(1)
    ys = []
    for d, (big_m, big_g, big_h, a_t) in enumerate(ops):
        ys.append(pl.pallas_call(
            functools.partial(_s5_kernel, S=S, C=C, reverse=d == 1), grid=(LB, B // NB),
            in_specs=[pl.BlockSpec((NB, P, LANES), lambda l, b: (b, 0, l)),
                      pl.BlockSpec((None, TL, TL), lambda l, b: (l, 0, 0), pipeline_mode=once),
                      pl.BlockSpec((None, TL, NS), lambda l, b: (l, 0, 0), pipeline_mode=once),
                      pl.BlockSpec((None, NS, TL), lambda l, b: (l, 0, 0), pipeline_mode=once),
                      pl.BlockSpec((None, 1, NS), lambda l, b: (l, 0, 0))],
            out_specs=pl.BlockSpec((NB, P, LANES), lambda l, b: (b, 0, l)),
            out_shape=jax.ShapeDtypeStruct((B, P, D), F32),
            scratch_shapes=[pltpu.VMEM((NB * n, NS), F32), pltpu.VMEM((NB * n, NS), F32)],
            compiler_params=_cp(2), name="s5_scan",
        )(u, big_m, big_g, big_h, a_t))
    return u, ys


def _s5_post_kernel(yf_ref, yb_ref, u_ref, d_ref, w_ref, b_ref, x_ref, gate_ref, lng_ref, lnb_ref,
                    sh_ref, sc_ref, wr_ref, xm_ref, f_ref, s_ref, *, alpha):
    g = jax.nn.gelu(yf_ref[0] + yb_ref[0] + d_ref[...] * u_ref[0])
    vg = jnp.dot(g.astype(BF16), w_ref[...], preferred_element_type=F32) + b_ref[...]
    Dn = vg.shape[-1] // 2
    y = vg[:, :Dn] * jax.nn.sigmoid(vg[:, Dn:])
    _finish_mixer(y, x_ref[0], gate_ref[...], lng_ref[...], lnb_ref[...], sh_ref[...], sc_ref[...],
                  wr_ref, xm_ref, f_ref, s_ref, alpha)


def _s5_post(geo, y2, u, d_skip, w_glu, b_glu, xs, modl, lng, lnb, w_router, alpha):
    D = geo.D
    E = w_router.shape[-1]
    out_specs, out_shape = _post_outs(geo, E)
    TM = geo.TM
    return pl.pallas_call(
        functools.partial(_s5_post_kernel, alpha=alpha), grid=(geo.B, geo.nt),
        in_specs=[geo.row_spec(D), geo.row_spec(D),
                  geo.row_spec(D), _full_spec((1, D)), _full_spec((D, 2 * D)), _full_spec((1, 2 * D)),
                  geo.row_spec(D), geo.mod_spec(2), _full_spec((1, D)), _full_spec((1, D)),
                  geo.mod_spec(3), geo.mod_spec(4), _full_spec((E, D))],
        out_specs=out_specs, out_shape=out_shape,
        compiler_params=_cp(2), name="s5_post",
    )(y2[0], y2[1], u, d_skip.reshape(1, D), w_glu.astype(BF16), b_glu.reshape(1, 2 * D), xs, modl,
      lng.reshape(1, D), lnb.reshape(1, D), modl, modl, w_router.T)


def _route_kernel(s_ref, b_ref, idx_ref, w_ref):
    sc = s_ref[0]
    E, TM = sc.shape
    biased = sc + b_ref[...]
    G = N_EXPERT_GROUPS
    per = E // G
    neg = -jnp.inf
    blocks, gs = [], []
    for g in range(G):
        blk = biased[g * per:(g + 1) * per]
        m1 = jnp.max(blk, axis=0, keepdims=True)
        is1 = blk == m1
        cnt = jnp.sum(is1.astype(F32), axis=0, keepdims=True)
        m2 = jnp.max(jnp.where(is1, neg, blk), axis=0, keepdims=True)
        blocks.append(blk)
        gs.append(m1 + jnp.where(cnt >= 2.0, m1, m2))
    masked = []
    for g in range(G):
        ahead = jnp.zeros((1, TM), F32)
        for h in range(G):
            if h < g:
                ahead = ahead + (gs[h] >= gs[g]).astype(F32)
            elif h > g:
                ahead = ahead + (gs[h] > gs[g]).astype(F32)
        masked.append(jnp.where(ahead < float(TOPK_GROUPS), blocks[g], neg))
    masked = jnp.concatenate(masked, axis=0)
    iota_e = lax.broadcasted_iota(jnp.int32, (E, TM), 0)
    idxs, ws = [], []
    for _ in range(TOP_K):
        m = jnp.max(masked, axis=0, keepdims=True)
        ik = jnp.min(jnp.where(masked == m, iota_e, E), axis=0, keepdims=True)
        sel = iota_e == ik
        ws.append(jnp.sum(jnp.where(sel, sc, 0.0), axis=0, keepdims=True))
        idxs.append(ik)
        masked = jnp.where(sel, neg, masked)
    tot = ws[0]
    for wk in ws[1:]:
        tot = tot + wk
    w = jnp.concatenate(ws, axis=0)
    idx_ref[0] = jnp.concatenate(idxs, axis=0)
    w_ref[0] = w / (tot + 1e-20) * ROUTED_SCALE


def _rank_kernel(idx_ref, rank_ref, cnt_ref, run_scr, *, E):
    first = jnp.logical_and(pl.program_id(0) == 0, pl.program_id(1) == 0)

    @pl.when(first)
    def _():
        run_scr[...] = jnp.zeros_like(run_scr)

    idx = idx_ref[0]
    K, TM = idx.shape
    iota_e = lax.broadcasted_iota(jnp.int32, (E, TM), 0)
    member = jnp.zeros((E, TM), F32)
    for k in range(K):
        member = member + (iota_e == idx[k:k + 1]).astype(F32)
    before = (lax.broadcasted_iota(jnp.int32, (TM, TM), 0)
              < lax.broadcasted_iota(jnp.int32, (TM, TM), 1)).astype(BF16)
    rank = jnp.dot(member.astype(BF16), before, preferred_element_type=F32) + run_scr[...]
    rows = [jnp.sum(jnp.where(iota_e == idx[k:k + 1], rank, 0.0), axis=0, keepdims=True) for k in range(K)]
    rank_ref[0] = jnp.concatenate(rows, axis=0).astype(jnp.int32)
    run_scr[...] = run_scr[...] + jnp.sum(member, axis=1, keepdims=True)
    cnt_ref[...] = run_scr[...]


def _dest_kernel(idx_ref, rank_ref, start_ref, dest_ref, *, E):
    idx = idx_ref[0]
    K, TM = idx.shape
    iota_e = lax.broadcasted_iota(jnp.int32, (E, TM), 0)
    start = start_ref[...]
    rows = [jnp.sum(jnp.where(iota_e == idx[k:k + 1], start, 0), axis=0, keepdims=True) for k in range(K)]
    dest_ref[0] = jnp.concatenate(rows, axis=0) + rank_ref[0]


def _route_dispatch(geo, scores, bias, blk):
    B, P = geo.B, geo.P
    E = scores.shape[1]
    K = TOP_K
    kspec = geo.col_spec(K)
    idx, w = pl.pallas_call(
        _route_kernel, grid=(B, geo.nt),
        in_specs=[geo.col_spec(E, whole=True), _full_spec((E, 1))],
        out_specs=[kspec, kspec],
        out_shape=[jax.ShapeDtypeStruct((B, K, P), jnp.int32), jax.ShapeDtypeStruct((B, K, P), F32)],
        compiler_params=_cp(2), name="moe_route",
    )(scores, bias.astype(F32).reshape(E, 1))
    rank, cnt = pl.pallas_call(
        functools.partial(_rank_kernel, E=E), grid=(B, geo.nt),
        in_specs=[kspec], out_specs=[kspec, _full_spec((E, 1))],
        out_shape=[jax.ShapeDtypeStruct((B, K, P), jnp.int32), jax.ShapeDtypeStruct((E, 1), F32)],
        scratch_shapes=[pltpu.VMEM((E, 1), F32)],
        compiler_params=_cp(2), name="moe_rank",
    )(idx)
    n_assign = B * P * K
    n_blocks = -(-(n_assign + E * (blk - 1)) // blk)
    counts = cnt[:, 0].astype(jnp.int32)
    pcounts = (counts + blk - 1) // blk * blk
    pends = jnp.cumsum(pcounts)
    starts = (pends - pcounts).astype(jnp.int32)
    n_used = pends[-1] // blk
    blk_e = jnp.minimum(jnp.searchsorted(pends, jnp.arange(n_blocks) * blk, side='right'), E - 1)
    blk_e = jnp.where(jnp.arange(n_blocks) < n_used, blk_e, blk_e[jnp.maximum(n_used - 1, 0)])
    dest = pl.pallas_call(
        functools.partial(_dest_kernel, E=E), grid=(B, geo.nt),
        in_specs=[kspec, kspec, _full_spec((E, 1))], out_specs=kspec,
        out_shape=jax.ShapeDtypeStruct((B, K, P), jnp.int32),
        compiler_params=_cp(2), name="moe_dest",
    )(idx, rank, starts.reshape(E, 1))
    return w, dest, blk_e.astype(jnp.int32), n_used.astype(jnp.int32).reshape(1), n_blocks


def _expert_kernel(be_ref, nu_ref, x_ref, wgu_ref, wd_ref, *rest):
    del be_ref
    o_ref = rest[-1]
    i = pl.program_id(0)

    @pl.when(i < nu_ref[0])
    def _():
        h = jnp.dot(x_ref[...], wgu_ref[...].astype(BF16), preferred_element_type=F32)
        Fh = h.shape[-1] // 2
        a = _silu(h[:, :Fh]) * h[:, Fh:]
        o_ref[...] = jnp.dot(a.astype(BF16), wd_ref[...].astype(BF16),
                             preferred_element_type=F32).astype(o_ref.dtype)

    @pl.when(i >= nu_ref[0])
    def _():
        o_ref[...] = jnp.zeros_like(o_ref)


def _expert_ffn(x_part, blk_e, n_used, w_gu, w_down, layer, blk, n_rows_all, blk0, prev):
    n_rows, D = x_part.shape
    F2 = w_gu.shape[-1]
    ins = [blk_e, n_used, x_part, w_gu, w_down]
    specs = [pl.BlockSpec((blk, D), lambda i, be, nu: (i, 0)),
             pl.BlockSpec((None, None, D, F2), lambda i, be, nu: (layer, be[i], 0, 0)),
             pl.BlockSpec((None, None, F2 // 2, D), lambda i, be, nu: (layer, be[i], 0, 0))]
    if prev is not None:
        ins.append(prev)
        specs.append(pl.BlockSpec(memory_space=pl.ANY))
    grid_spec = pltpu.PrefetchScalarGridSpec(
        num_scalar_prefetch=2, grid=(n_rows // blk,), in_specs=specs,
        out_specs=pl.BlockSpec((blk, D), lambda i, be, nu: (blk0 + i, 0)))
    return pl.pallas_call(
        _expert_kernel, grid_spec=grid_spec,
        out_shape=jax.ShapeDtypeStruct((n_rows_all, D), BF16),
        input_output_aliases={} if prev is None else {5: 0},
        compiler_params=_cp(1), name="moe_experts",
    )(*ins)


def _moe_final_kernel(*refs, alpha, aliased):
    if aliased:
        xm_ref, f_ref, ga_ref, w_ref, shgu_ref, shd_ref, gate_ref, lng_ref, lnb_ref, _, o_ref = refs
    else:
        xm_ref, f_ref, ga_ref, w_ref, shgu_ref, shd_ref, gate_ref, lng_ref, lnb_ref, o_ref = refs
    h = jnp.dot(f_ref[0], shgu_ref[...], preferred_element_type=F32)
    Fh = h.shape[-1] // 2
    a = _silu(h[:, :Fh]) * h[:, Fh:]
    y = jnp.dot(a.astype(BF16), shd_ref[...], preferred_element_type=F32)
    w = w_ref[0]
    for k in range(w.shape[-1]):
        y = y + w[:, k:k + 1] * ga_ref[k, 0].astype(F32)
    o_ref[0] = _layer_norm(alpha * xm_ref[0] + gate_ref[...] * y, lng_ref[...], lnb_ref[...])


def _moe_group(geo, xm, f, scores, modl, bias, w_gu, w_down, sh_gu, sh_down, lng, lnb, layer, alpha, blk, prev,
               tie=None, latent_only=False):
    B, P, D = geo.B, geo.P, geo.D
    T = B * P
    K = TOP_K
    w, dest, blk_e, n_used, n_blocks = _route_dispatch(geo, scores, bias, blk)
    dest_flat = jnp.swapaxes(dest, 0, 1).reshape(K * T)
    t0 = geo.b0 * P
    tok = jnp.broadcast_to(jnp.arange(t0, t0 + T, dtype=jnp.int32)[None], (K, T)).reshape(K * T)
    hit = jnp.zeros((n_blocks * blk,), jnp.int32).at[dest_flat].add(
        tok + 1, unique_indices=True, mode='promise_in_bounds')
    row_tok = jnp.where(hit > 0, hit - 1, t0 + jnp.arange(n_blocks * blk, dtype=jnp.int32) % T)
    f2d = f.reshape(geo.B_all * P, D)
    y_sorted = None
    cuts = [n_blocks * c // MOE_ROW_CUTS[-1] for c in MOE_ROW_CUTS]
    for lo, hi in zip(cuts[:-1], cuts[1:]):
        if hi == lo:
            continue
        x_part = f2d.at[row_tok[lo * blk:hi * blk]].get(mode='promise_in_bounds')
        used = jnp.clip(n_used - lo, 0, hi - lo)
        y_sorted = _expert_ffn(x_part, blk_e[lo:hi], used, w_gu, w_down, layer, blk, n_blocks * blk, lo, y_sorted)
    gathered = y_sorted.at[dest_flat].get(unique_indices=True, mode='promise_in_bounds').reshape(K, B, P, D)
    if tie is not None:
        gathered, tie = lax.optimization_barrier((gathered, tie))
    F2 = sh_gu.shape[-1]
    TM = geo.TM
    ins = [xm, f, gathered, jnp.swapaxes(w, 1, 2), sh_gu.astype(BF16), sh_down.astype(BF16), modl,
           lng.reshape(1, D), lnb.reshape(1, D)]
    specs = [geo.row_spec(D, whole=True), geo.row_spec(D, whole=True),
             pl.BlockSpec((K, 1, TM, D), lambda b, i: (0, b, i, 0)), geo.row_spec(K),
             _full_spec((D, F2)), _full_spec((F2 // 2, D)), geo.mod_spec(5),
             _full_spec((1, D)), _full_spec((1, D))]
    if prev is not None:
        ins.append(prev)
        specs.append(pl.BlockSpec(memory_space=pl.ANY))
    n_tiles, rows_out = (geo.n_lat, geo.S) if latent_only else (geo.nt, P)
    out = pl.pallas_call(
        functools.partial(_moe_final_kernel, alpha=alpha, aliased=prev is not None), grid=(B, n_tiles),
        in_specs=specs, out_specs=geo.row_spec(D, whole=True),
        out_shape=jax.ShapeDtypeStruct((geo.B_all, rows_out, D), F32),
        input_output_aliases={} if prev is None else {9: 0},
        compiler_params=_cp(2), name="moe_final",
    )(*ins)
    return out, tie


def _moe(geo, xm, f, scores, modl, bias, w_gu, w_down, sh_gu, sh_down, lng, lnb, layer, alpha, blk, tie=None,
         latent_only=False):
    n_groups = MOE_GROUPS if geo.B % MOE_GROUPS == 0 else 1
    nb = geo.B // n_groups
    out = None
    for g in range(n_groups):
        out, tie = _moe_group(geo.group(g * nb, nb), xm, f, scores, modl, bias, w_gu, w_down, sh_gu, sh_down,
                              lng, lnb, layer, alpha, blk, out, tie, latent_only)
    return out, tie


def kernel(x, c, ctx, c_ctx, mod_w, mod_b, ln_g, ln_b, rg_w_in, rg_conv_w, rg_conv_b, rg_gate_w, rg_gate_b, rg_lam, rg_w_out, hy_w_in, hy_b_in, hy_short_w, hy_short_b, hy_f_w1, hy_f_b1, hy_f_w2, hy_f_b2, hy_f_w3, hy_f_b3, hy_f_w4, hy_f_freq, hy_f_decay, hy_f_bias, hy_w_out, hy_b_out, da_w_in, da_lam, da_subln, da_w_out, s5_a_re, s5_a_im, s5_log_step, s5_b_re, s5_b_im, s5_c_re, s5_c_im, s5_d, s5_w_glu, s5_b_glu, moe_w_router, moe_bias, moe_w_gu, moe_w_down, moe_sh_gu, moe_sh_down):
    B, S, D = x.shape
    C = ctx.shape[1]
    depth = mod_w.shape[0]
    alpha = (2 * depth) ** 0.25
    geo = _Geo(B, S, C, D)
    xs = jnp.concatenate([x, ctx], axis=1)
    R = -(-(B + 1) // SUBLANES) * SUBLANES
    cc = jnp.zeros((R, D), F32).at[:B].set(c).at[B].set(c_ctx)
    modt = _mod_table(cc, mod_w, mod_b).reshape(depth, 6, R, 1, D)
    blk = MOE_BLOCK
    s5_ops, ties = {}, {}
    for i in range(N_MIXERS - 1, depth, N_MIXERS):
        j = i // N_MIXERS
        s5_ops[i] = _s5_operators(s5_a_re[j], s5_a_im[j], s5_log_step[j], s5_b_re[j], s5_b_im[j],
                                  s5_c_re[j], s5_c_im[j])
        for d in range(2):
            ties[i - 2 + d] = (i, d)
    for i in range(depth):
        kind, j = i % N_MIXERS, i // N_MIXERS
        modl = modt[i]
        post = functools.partial(_post_mixer, geo, xs=xs, modl=modl, lng=ln_g[i, 0], lnb=ln_b[i, 0],
                                 w_router=moe_w_router[i], alpha=alpha)
        if kind == 0:
            y = _rglru_mixer(geo, xs, modl, rg_w_in[j], rg_conv_w[j], rg_conv_b[j], rg_gate_w[j],
                             rg_gate_b[j], rg_lam[j])
            xm, f, scores = post(y=y, w_out=rg_w_out[j], b_out=None)
        elif kind == 1:
            y = _hyena_mixer(geo, xs, modl, hy_w_in[j], hy_b_in[j], hy_short_w[j], hy_short_b[j],
                             hy_f_w1[j], hy_f_b1[j], hy_f_w2[j], hy_f_b2[j], hy_f_w3[j], hy_f_b3[j],
                             hy_f_w4[j], hy_f_freq[j], hy_f_decay[j], hy_f_bias[j])
            xm, f, scores = post(y=y, w_out=hy_w_out[j], b_out=hy_b_out[j])
        elif kind == 2:
            y = _diff_attention_mixer(geo, xs, modl, da_w_in[j], da_lam[j], da_subln[j], i)
            xm, f, scores = post(y=y, w_out=da_w_out[j], b_out=None)
        else:
            u, y2 = _s5_mixer(geo, xs, modl, s5_ops[i])
            xm, f, scores = _s5_post(geo, y2, u, s5_d[j], s5_w_glu[j], s5_b_glu[j], xs, modl,
                                     ln_g[i, 0], ln_b[i, 0], moe_w_router[i], alpha)
        tie = ties.get(i)
        xs, tied = _moe(geo, xm, f, scores, modl, moe_bias[i], moe_w_gu, moe_w_down, moe_sh_gu[i],
                        moe_sh_down[i], ln_g[i, 1], ln_b[i, 1], i, alpha, blk,
                        None if tie is None else s5_ops[tie[0]][tie[1]], latent_only=i == depth - 1)
        if tie is not None:
            s5_ops[tie[0]][tie[1]] = tied
    return xs
```

```python
import functools
import math

import jax
import jax.numpy as jnp
from jax import lax
from jax.experimental import pallas as pl
from jax.experimental.pallas import tpu as pltpu

F32 = jnp.float32
BF16 = jnp.bfloat16
HIGHEST = lax.Precision.HIGHEST

N_MIXERS = 4
LN_EPS = 1e-6
LRU_C = 8.0
RG_BLOCK = 128
GRID_W = 64
DA_HEAD_DIM = 64
ROPE_THETA = 10000.0
S5_GROUP = 16
S5_CHUNK = 16
TOP_K = 8
N_EXPERT_GROUPS = 8
TOPK_GROUPS = 4
ROUTED_SCALE = 2.5
MOE_BLOCK = 512
MOE_GROUPS = 1
MOE_ROW_CUTS = (0, 1, 3, 7, 12)
MOE_COMBINE_GROUPS = 2
DA_Q_TILE = 1024

LANES = 128
SUBLANES = 8
VMEM_LIMIT = 56 * 1024 * 1024


def _cp(n_grid):
    return pltpu.CompilerParams(dimension_semantics=("arbitrary",) * n_grid,
                                vmem_limit_bytes=VMEM_LIMIT)


def _silu(x):
    return x * jax.nn.sigmoid(x)


def _mod_table_kernel(c_ref, w_ref, b_ref, o_ref):
    s = _silu(c_ref[...])
    o_ref[...] = jnp.dot(s, w_ref[...], precision=HIGHEST, preferred_element_type=F32) + b_ref[...]


def _mod_table(cc, mod_w, mod_b):
    depth, D, _ = mod_w.shape
    R = cc.shape[0]
    return pl.pallas_call(
        _mod_table_kernel,
        grid=(depth, 6),
        in_specs=[pl.BlockSpec((R, D), lambda i, k: (0, 0)),
                  pl.BlockSpec((None, D, D), lambda i, k: (i, 0, k)),
                  pl.BlockSpec((None, None, 1, D), lambda i, k: (i, k, 0, 0))],
        out_specs=pl.BlockSpec((None, None, R, D), lambda i, k: (i, k, 0, 0)),
        out_shape=jax.ShapeDtypeStruct((depth, 6, R, D), F32),
        compiler_params=_cp(2), name="mod_table",
    )(cc, mod_w, mod_b.reshape(depth, 6, 1, D))


class _Geo:
    def __init__(self, B, S, C, D, b0=0, B_all=None):
        self.B, self.S, self.C, self.D = B, S, C, D
        self.b0 = b0
        self.B_all = B if B_all is None else B_all
        self.P = S + C
        self.TM = math.gcd(S, C)
        while self.TM > 256:
            self.TM //= 2
        self.nt = self.P // self.TM
        self.n_lat = S // self.TM

    def group(self, b0, nb):
        return _Geo(nb, self.S, self.C, self.D, b0=b0, B_all=self.B_all)

    def mod_spec(self, k):
        D, b0, Ba, n_lat = self.D, self.b0, self.B_all, self.n_lat
        return pl.BlockSpec((None, None, 1, D),
                            lambda b, i, *_: (k, jnp.where(i < n_lat, b + b0, Ba), 0, 0))

    def row_spec(self, width, col=0, whole=False):
        b0 = self.b0 if whole else 0
        return pl.BlockSpec((1, self.TM, width), lambda b, i, *_: (b + b0, i, col))

    def col_spec(self, height, whole=False):
        b0 = self.b0 if whole else 0
        return pl.BlockSpec((1, height, self.TM), lambda b, i, *_: (b + b0, 0, i))


def _full_spec(shape):
    nd = len(shape)
    return pl.BlockSpec(shape, lambda *_: (0,) * nd)


def _layer_norm(z, g, b):
    mu = jnp.mean(z, axis=-1, keepdims=True)
    zc = z - mu
    var = jnp.mean(zc * zc, axis=-1, keepdims=True)
    return zc * lax.rsqrt(var + LN_EPS) * g + b


def _finish_mixer(y, x, gate, lng, lnb, sh, sc, wr_ref, xm_ref, f_ref, s_ref, alpha):
    xn = _layer_norm(alpha * x + gate * y, lng, lnb)
    xm_ref[0] = xn
    f = xn * (1.0 + sc) + sh
    f_ref[0] = f.astype(f_ref.dtype)
    def nt(a, b):
        return lax.dot_general(a, b, (((1,), (1,)), ((), ())), preferred_element_type=F32)

    w = wr_ref[...]
    w_hi = w.astype(BF16)
    w_lo = (w - w_hi.astype(F32)).astype(BF16)
    f_hi = f.astype(BF16)
    f_lo = (f - f_hi.astype(F32)).astype(BF16)
    logits = nt(w_hi, f_hi) + (nt(w_hi, f_lo) + nt(w_lo, f_hi))
    s_ref[0] = jax.nn.sigmoid(logits)


def _post_kernel(*refs, alpha, has_bias):
    if has_bias:
        (y_ref, w_ref, b_ref, x_ref, gate_ref, lng_ref, lnb_ref, sh_ref, sc_ref, wr_ref,
         xm_ref, f_ref, s_ref) = refs
    else:
        (y_ref, w_ref, x_ref, gate_ref, lng_ref, lnb_ref, sh_ref, sc_ref, wr_ref,
         xm_ref, f_ref, s_ref) = refs
    y = jnp.dot(y_ref[0].astype(BF16), w_ref[...], preferred_element_type=F32)
    if has_bias:
        y = y + b_ref[...]
    _finish_mixer(y, x_ref[0], gate_ref[...], lng_ref[...], lnb_ref[...], sh_ref[...], sc_ref[...],
                  wr_ref, xm_ref, f_ref, s_ref, alpha)


def _post_outs(geo, E):
    B, P, D = geo.B, geo.P, geo.D
    out_specs = [geo.row_spec(D), geo.row_spec(D), geo.col_spec(E)]
    out_shape = [jax.ShapeDtypeStruct((B, P, D), F32), jax.ShapeDtypeStruct((B, P, D), BF16),
                 jax.ShapeDtypeStruct((B, E, P), F32)]
    return out_specs, out_shape


def _post_mixer(geo, y, w_out, b_out, xs, modl, lng, lnb, w_router, alpha):
    D = geo.D
    Kd = y.shape[-1]
    E = w_router.shape[-1]
    has_bias = b_out is not None
    ins = [y, w_out.astype(BF16)]
    specs = [geo.row_spec(Kd), _full_spec((Kd, D))]
    if has_bias:
        ins.append(b_out.reshape(1, D))
        specs.append(_full_spec((1, D)))
    ins += [xs, modl, lng.reshape(1, D), lnb.reshape(1, D), modl, modl, w_router.T]
    specs += [geo.row_spec(D), geo.mod_spec(2), _full_spec((1, D)), _full_spec((1, D)),
              geo.mod_spec(3), geo.mod_spec(4), _full_spec((E, D))]
    out_specs, out_shape = _post_outs(geo, E)
    return pl.pallas_call(
        functools.partial(_post_kernel, alpha=alpha, has_bias=has_bias),
        grid=(geo.B, geo.nt), in_specs=specs, out_specs=out_specs, out_shape=out_shape,
        compiler_params=_cp(2), name="post_mixer",
    )(*ins)


def _dwconv_seg(r, pad_scr, cw, cb, S, C, lo):
    P = S + C
    n = r.shape[-1]
    pad_scr[0:SUBLANES, :] = jnp.zeros((SUBLANES, n), F32)
    pad_scr[SUBLANES + P:, :] = jnp.zeros((SUBLANES, n), F32)
    pad_scr[SUBLANES:SUBLANES + P, :] = r
    row = lax.broadcasted_iota(jnp.int32, r.shape, 0)
    tl = jnp.where(row < S, row, row - S)
    sl = jnp.where(row < S, S, C)
    acc = jnp.zeros_like(r) + cb
    for k in range(cw.shape[0]):
        off = k - lo
        if off == 0:
            term = r
        else:
            shifted = pad_scr[SUBLANES + off:SUBLANES + off + P, :]
            valid = jnp.logical_and(tl + off >= 0, tl + off < sl)
            term = jnp.where(valid, shifted, 0.0)
        acc = acc + cw[k:k + 1, :] * term
    return acc


def _rg_in_kernel(x_ref, sh_ref, sc_ref, wg_ref, wr_ref, g_ref, r_ref):
    h = (x_ref[0] * (1.0 + sc_ref[...]) + sh_ref[...]).astype(BF16)
    g = jnp.dot(h, wg_ref[...], preferred_element_type=F32)
    g_ref[0] = jax.nn.gelu(g).astype(g_ref.dtype)
    r_ref[0] = jnp.dot(h, wr_ref[...], preferred_element_type=F32)


def _rg_scan_kernel(r_ref, g_ref, cw_ref, cb_ref, gw_ref, gb_ref, lam_ref, o_ref,
                    a_scr, b_scr, pad_scr, al_scr, bl_scr, cin_scr, *, S, C):
    P = S + C
    NT = P // SUBLANES
    hsum = None
    n = r_ref.shape[-1]
    r = r_ref[0]
    rc = _dwconv_seg(r, pad_scr, cw_ref[...], cb_ref[...], S, C, cw_ref.shape[0] // 2)
    rcb = rc.astype(BF16)
    row = lax.broadcasted_iota(jnp.int32, (P, n), 0)
    sub = jnp.bitwise_and(row, SUBLANES - 1)
    for scr in (a_scr, b_scr):
        scr[0:SUBLANES, :] = jnp.zeros((SUBLANES, n), F32)
        scr[SUBLANES + P:, :] = jnp.zeros((SUBLANES, n), F32)
    for d in range(2):
        rev = d == 1
        gr = jax.nn.sigmoid(jnp.dot(rcb, gw_ref[d, 0], preferred_element_type=F32) + gb_ref[d, 0])
        gi = jax.nn.sigmoid(jnp.dot(rcb, gw_ref[d, 1], preferred_element_type=F32) + gb_ref[d, 1])
        nl = -lam_ref[d]
        sp = jnp.maximum(nl, 0.0) + jnp.log1p(jnp.exp(-jnp.abs(nl)))
        a = jnp.exp(-LRU_C * gr * sp)
        om = 1.0 - a * a
        bb = om * lax.rsqrt(jnp.maximum(om, 1e-30)) * gi * rc
        for s in (1, 2, 4):
            a_scr[SUBLANES:SUBLANES + P, :] = a
            b_scr[SUBLANES:SUBLANES + P, :] = bb
            lo = SUBLANES + (s if rev else -s)
            a_sh = a_scr[lo:lo + P, :]
            b_sh = b_scr[lo:lo + P, :]
            m = (sub < SUBLANES - s) if rev else (sub >= s)
            bb = jnp.where(m, a * b_sh + bb, bb)
            a = jnp.where(m, a * a_sh, a)
        a_scr[SUBLANES:SUBLANES + P, :] = a
        b_scr[SUBLANES:SUBLANES + P, :] = bb
        edge = SUBLANES + (0 if rev else SUBLANES - 1)
        al_scr[...] = a_scr[pl.ds(edge, NT, stride=SUBLANES), :]
        bl_scr[...] = b_scr[pl.ds(edge, NT, stride=SUBLANES), :]

        def chain(lo_tile, n_tiles, c0):
            def body(i, c):
                t = (lo_tile + n_tiles - 1 - i) if rev else (lo_tile + i)
                cin_scr[pl.ds(t, 1), :] = c
                return bl_scr[pl.ds(t, 1), :] + al_scr[pl.ds(t, 1), :] * c
            return lax.fori_loop(0, n_tiles, body, c0, unroll=4)

        c_ctx = chain(S // SUBLANES, C // SUBLANES, jnp.zeros((1, n), F32))
        chain(0, S // SUBLANES, c_ctx)
        cin = cin_scr[...]
        for j in range(SUBLANES):
            pad_scr[pl.ds(j, NT, stride=SUBLANES), :] = cin
        h = bb + a * pad_scr[0:P, :]
        hsum = h if d == 0 else hsum + h
    o_ref[0] = (g_ref[0].astype(F32) * hsum).astype(o_ref.dtype)


def _rglru_mixer(geo, xs, modl, w_in, conv_w, conv_b, gate_w, gate_b, lam):
    B, P, D, S, C = geo.B, geo.P, geo.D, geo.S, geo.C
    R = w_in.shape[1] // 2
    nb = R // RG_BLOCK
    w_in = w_in.astype(BF16)
    g, r = pl.pallas_call(
        _rg_in_kernel, grid=(B, geo.nt),
        in_specs=[geo.row_spec(D), geo.mod_spec(0), geo.mod_spec(1),
                  pl.BlockSpec((D, R), lambda b, i: (0, 0)), pl.BlockSpec((D, R), lambda b, i: (0, 1))],
        out_specs=[geo.row_spec(R), geo.row_spec(R)],
        out_shape=[jax.ShapeDtypeStruct((B, P, R), BF16), jax.ShapeDtypeStruct((B, P, R), F32)],
        compiler_params=_cp(2), name="rg_in",
    )(xs, modl, modl, w_in, w_in)
    K = conv_w.shape[0]
    seq_spec = pl.BlockSpec((1, P, RG_BLOCK), lambda b, n: (b, 0, n))
    y = pl.pallas_call(
        functools.partial(_rg_scan_kernel, S=S, C=C), grid=(B, nb),
        in_specs=[seq_spec, seq_spec,
                  pl.BlockSpec((K, RG_BLOCK), lambda b, n: (0, n)),
                  pl.BlockSpec((1, RG_BLOCK), lambda b, n: (0, n)),
                  pl.BlockSpec((2, 2, None, RG_BLOCK, RG_BLOCK), lambda b, n: (0, 0, n, 0, 0)),
                  pl.BlockSpec((2, 2, 1, RG_BLOCK), lambda b, n: (0, 0, 0, n)),
                  pl.BlockSpec((2, 1, RG_BLOCK), lambda b, n: (0, 0, n))],
        out_specs=seq_spec,
        out_shape=jax.ShapeDtypeStruct((B, P, R), BF16),
        scratch_shapes=([pltpu.VMEM((P + 2 * SUBLANES, RG_BLOCK), F32)] * 3
                        + [pltpu.VMEM((P // SUBLANES, RG_BLOCK), F32)] * 3),
        compiler_params=_cp(2), name="rg_scan",
    )(r, g, conv_w, conv_b.reshape(1, R), gate_w.astype(BF16), gate_b.reshape(2, 2, 1, R),
      lam.reshape(2, 1, R))
    return y


def _mm_bias_kernel(x_ref, sh_ref, sc_ref, w_ref, b_ref, o_ref):
    h = (x_ref[0] * (1.0 + sc_ref[...]) + sh_ref[...]).astype(BF16)
    o_ref[0] = (jnp.dot(h, w_ref[...], preferred_element_type=F32) + b_ref[...]).astype(o_ref.dtype)


def _short_conv_kernel(u_ref, cw_ref, cb_ref, o_ref, ob_ref, pad_scr, *, S, C):
    y = _dwconv_seg(u_ref[0], pad_scr, cw_ref[...], cb_ref[...], S, C, (cw_ref.shape[0] - 1) // 2)
    o_ref[0] = y
    ob_ref[0] = y.astype(BF16)


def _dft_table_kernel(c_ref, s_ref, st_ref, *, L, TF):
    i = pl.program_id(0)
    N = 2 * L
    f = lax.broadcasted_iota(jnp.int32, (TF, L), 0) + i * TF
    t = lax.broadcasted_iota(jnp.int32, (TF, L), 1)
    ang = jnp.bitwise_and(f * t, N - 1).astype(F32) * (2.0 * math.pi / N)
    c_ref[...] = jnp.cos(ang).astype(BF16)
    nyq_t = (1 - 2 * jnp.bitwise_and(t, 1)).astype(F32)
    s_ref[...] = jnp.where(f == 0, nyq_t, jnp.sin(ang)).astype(BF16)
    nyq_f = (1 - 2 * jnp.bitwise_and(f, 1)).astype(F32)
    st_ref[...] = jnp.where(t == 0, nyq_f, jnp.sin(ang)).astype(BF16)


def _dft_tables(L):
    TF = min(L, 256)
    shp = jax.ShapeDtypeStruct((L, L), BF16)
    spec = pl.BlockSpec((TF, L), lambda i: (i, 0))
    return pl.pallas_call(
        functools.partial(_dft_table_kernel, L=L, TF=TF), grid=(L // TF,),
        in_specs=[], out_specs=[spec, spec, spec], out_shape=[shp, shp, shp],
        compiler_params=_cp(1), name="dft_tables",
    )()


def _hy_filter_kernel(z_ref, w1_ref, b1_ref, w2_ref, b2_ref, w3_ref, b3_ref, fq_ref,
                      w4f_ref, w4b_ref, df_ref, db_ref, tn_ref, kp_ref, km_ref):
    fq = fq_ref[...]

    def lin(h, w_ref, b_ref):
        return jnp.dot(h, w_ref[...], precision=HIGHEST, preferred_element_type=F32) + b_ref[...]

    h = jnp.sin(fq * lin(z_ref[...], w1_ref, b1_ref))
    h = jnp.sin(fq * lin(h, w2_ref, b2_ref))
    h = jnp.sin(fq * lin(h, w3_ref, b3_ref))
    tn = tn_ref[...]
    hf = jnp.dot(h, w4f_ref[...], precision=HIGHEST, preferred_element_type=F32)
    hf = hf * jnp.exp(-tn * jnp.abs(df_ref[...]))
    hb = jnp.dot(h, w4b_ref[...], precision=HIGHEST, preferred_element_type=F32)
    hb = hb * jnp.exp(-tn * jnp.abs(db_ref[...]))
    row = lax.broadcasted_iota(jnp.int32, hb.shape, 0)
    hb = jnp.where(row == 0, 0.0, hb)
    nrm = lax.rsqrt(jnp.sum(hf * hf, axis=0, keepdims=True) + jnp.sum(hb * hb, axis=0, keepdims=True) + 1e-6)
    hf = hf * nrm
    hb = hb * nrm
    kp_ref[...] = (hf + hb).astype(BF16)
    km_ref[...] = (hf - hb).astype(BF16)


def _hy_spectrum_kernel(c_ref, s_ref, s0_ref, kp_ref, km_ref, ka_ref, kb_ref, kc_ref, *, L, TF):
    i = pl.program_id(0)
    inv_n = 1.0 / (2 * L)
    kr = jnp.dot(c_ref[...], kp_ref[...], preferred_element_type=F32)
    ks = jnp.dot(s_ref[...], km_ref[...], preferred_element_type=F32)
    nyq = jnp.dot(s0_ref[...], kp_ref[...], preferred_element_type=F32)[0:1]
    f = lax.broadcasted_iota(jnp.int32, kr.shape, 0) + i * TF
    dc = f == 0
    ka_ref[...] = jnp.where(dc, kr * inv_n, kr * (2.0 * inv_n))
    kb_ref[...] = jnp.where(dc, 0.0, ks * (-2.0 * inv_n))
    kc_ref[...] = jnp.where(dc, nyq * inv_n, kr * (2.0 * inv_n))


def _hy_filters(L, tabs, fw1, fb1, fw2, fb2, fw3, fb3, fw4, ffreq, fdecay, D):
    cm, sm, _ = tabs
    E = fw1.shape[0]
    Hd = fw1.shape[1]
    bands = (E - 1) // 2
    t = jnp.arange(L, dtype=F32)
    t_norm = t / max(L - 1, 1)
    fr = jnp.linspace(1e-4, bands - 1, bands, dtype=F32)
    ang = (2.0 * math.pi / L) * t[:, None] * fr[None, :]
    z = jnp.concatenate([t_norm[:, None], jnp.cos(ang), -jnp.sin(ang)], -1)
    Ep, Hp = -(-E // LANES) * LANES, -(-Hd // LANES) * LANES
    z = jnp.pad(z, ((0, 0), (0, Ep - E)))
    fw1 = jnp.pad(fw1, ((0, Ep - E), (0, Hp - Hd)))
    fw2 = jnp.pad(fw2, ((0, Hp - Hd), (0, Hp - Hd)))
    fw3 = jnp.pad(fw3, ((0, Hp - Hd), (0, Hp - Hd)))
    fw4 = jnp.pad(fw4, ((0, Hp - Hd), (0, 0)))
    fb1, fb2, fb3, ffreq = (jnp.pad(v, (0, Hp - Hd)) for v in (fb1, fb2, fb3, ffreq))
    E, Hd = Ep, Hp
    CT = min(2 * D, 512)
    nct = 2 * D // CT
    dec = fdecay.reshape(1, 4 * D)
    kp, km = pl.pallas_call(
        _hy_filter_kernel, grid=(nct,),
        in_specs=[_full_spec((L, E)), _full_spec((E, Hd)), _full_spec((1, Hd)), _full_spec((Hd, Hd)),
                  _full_spec((1, Hd)), _full_spec((Hd, Hd)), _full_spec((1, Hd)), _full_spec((1, Hd)),
                  pl.BlockSpec((Hd, CT), lambda j: (0, j)), pl.BlockSpec((Hd, CT), lambda j: (0, j + nct)),
                  pl.BlockSpec((1, CT), lambda j: (0, j)), pl.BlockSpec((1, CT), lambda j: (0, j + nct)),
                  _full_spec((L, 1))],
        out_specs=[pl.BlockSpec((L, CT), lambda j: (0, j))] * 2,
        out_shape=[jax.ShapeDtypeStruct((L, 2 * D), BF16)] * 2,
        compiler_params=_cp(1), name="hy_filter",
    )(z, fw1, fb1.reshape(1, Hd), fw2, fb2.reshape(1, Hd), fw3, fb3.reshape(1, Hd), ffreq.reshape(1, Hd),
      fw4, fw4, dec, dec, t_norm[:, None])
    TF = min(L, 256)
    spec_w = pl.BlockSpec((TF, L), lambda i, j: (i, 0))
    spec_k = pl.BlockSpec((L, CT), lambda i, j: (0, j))
    spec_o = pl.BlockSpec((TF, CT), lambda i, j: (i, j))
    shp = jax.ShapeDtypeStruct((L, 2 * D), F32)
    return pl.pallas_call(
        functools.partial(_hy_spectrum_kernel, L=L, TF=TF), grid=(L // TF, nct),
        in_specs=[spec_w, spec_w, pl.BlockSpec((SUBLANES, L), lambda i, j: (0, 0)), spec_k, spec_k],
        out_specs=[spec_o] * 3, out_shape=[shp] * 3,
        compiler_params=_cp(2), name="hy_spectrum",
    )(cm, sm, sm, kp, km)


def _hy_fwd_kernel(z_ref, c_ref, s_ref, ka_ref, kb_ref, kc_ref, p_ref):
    z = z_ref[0]
    zr = jnp.dot(c_ref[...], z, preferred_element_type=F32)
    zs = jnp.dot(s_ref[...], z, preferred_element_type=F32)
    kb = kb_ref[...]
    p_ref[0, 0] = (zr * ka_ref[...] + zs * kb).astype(BF16)
    p_ref[0, 1] = (zs * kc_ref[...] - zr * kb).astype(BF16)


def _hy_inv_kernel(p_ref, c_ref, st_ref, z_ref, x_ref, fb_ref, o_ref):
    y = jnp.dot(c_ref[...], p_ref[0, 0], preferred_element_type=F32)
    y = y + jnp.dot(st_ref[...], p_ref[0, 1], preferred_element_type=F32)
    o_ref[0] = (x_ref[0] * (y + z_ref[0].astype(F32) * fb_ref[...])).astype(o_ref.dtype)


def _hy_inv_kernel_alias(p_ref, c_ref, st_ref, z_ref, x_ref, fb_ref, prev_ref, o_ref):
    del prev_ref
    _hy_inv_kernel(p_ref, c_ref, st_ref, z_ref, x_ref, fb_ref, o_ref)


def _hy_conv(geo, L, off, tabs, z, z_col, kfilt, k_col, xmul, x_col, fbias, out):
    B, P, D = geo.B, geo.P, geo.D
    cm, sm, smt = tabs
    ka, kb, kc = kfilt
    rb = off // L
    TF = min(L, 256)
    spec_w = pl.BlockSpec((TF, L), lambda b, i: (i, 0))
    spec_k = pl.BlockSpec((TF, D), lambda b, i: (i, k_col))
    p = pl.pallas_call(
        _hy_fwd_kernel, grid=(B, L // TF),
        in_specs=[pl.BlockSpec((1, L, D), lambda b, i: (b, rb, z_col)), spec_w, spec_w,
                  spec_k, spec_k, spec_k],
        out_specs=pl.BlockSpec((1, 2, TF, D), lambda b, i: (b, 0, i, 0)),
        out_shape=jax.ShapeDtypeStruct((B, 2, L, D), BF16),
        compiler_params=_cp(2), name="hy_fwd",
    )(z, cm, sm, ka, kb, kc)
    CT = D
    nct = D // CT
    rt = off // TF
    spec_wi = pl.BlockSpec((TF, L), lambda b, j, i: (i, 0))
    return pl.pallas_call(
        _hy_inv_kernel_alias, grid=(B, nct, L // TF),
        in_specs=[pl.BlockSpec((1, 2, L, CT), lambda b, j, i: (b, 0, 0, j), pipeline_mode=pl.Buffered(1)),
                  spec_wi, spec_wi,
                  pl.BlockSpec((1, TF, CT), lambda b, j, i: (b, rt + i, z_col * nct + j)),
                  pl.BlockSpec((1, TF, CT), lambda b, j, i: (b, rt + i, x_col * nct + j)),
                  pl.BlockSpec((1, CT), lambda b, j, i: (0, k_col * nct + j)),
                  pl.BlockSpec(memory_space=pl.ANY)],
        out_specs=pl.BlockSpec((1, TF, CT), lambda b, j, i: (b, rt + i, j)),
        out_shape=jax.ShapeDtypeStruct((B, P, D), BF16),
        input_output_aliases={6: 0},
        compiler_params=_cp(3), name="hy_inv",
    )(p, cm, smt, z, xmul, fbias, out)


def _hyena_mixer(geo, xs, modl, w_in, b_in, short_w, short_b, fw1, fb1, fw2, fb2, fw3, fb3, fw4,
                 ffreq, fdecay, fbias):
    B, P, D, S, C = geo.B, geo.P, geo.D, geo.S, geo.C
    u0 = pl.pallas_call(
        _mm_bias_kernel, grid=(B, geo.nt, 3),
        in_specs=[geo.row_spec(D), geo.mod_spec(0), geo.mod_spec(1),
                  pl.BlockSpec((D, D), lambda b, i, j: (0, j)), pl.BlockSpec((1, D), lambda b, i, j: (0, j))],
        out_specs=pl.BlockSpec((1, geo.TM, D), lambda b, i, j: (b, i, j)),
        out_shape=jax.ShapeDtypeStruct((B, P, 3 * D), F32),
        compiler_params=_cp(3), name="hy_in",
    )(xs, modl, modl, w_in.astype(BF16), b_in.reshape(1, 3 * D))
    CT = min(D, 256)
    Ks = short_w.shape[0]
    spec = pl.BlockSpec((1, P, CT), lambda b, j: (b, 0, j))
    u, ub = pl.pallas_call(
        functools.partial(_short_conv_kernel, S=S, C=C), grid=(B, 3 * D // CT),
        in_specs=[spec, pl.BlockSpec((Ks, CT), lambda b, j: (0, j)), pl.BlockSpec((1, CT), lambda b, j: (0, j))],
        out_specs=[spec, spec],
        out_shape=[jax.ShapeDtypeStruct((B, P, 3 * D), F32), jax.ShapeDtypeStruct((B, P, 3 * D), BF16)],
        scratch_shapes=[pltpu.VMEM((P + 2 * SUBLANES, CT), F32)],
        compiler_params=_cp(2), name="hy_short",
    )(u0, short_w, short_b.reshape(1, 3 * D))
    fb = fbias.reshape(1, 2 * D)
    z1 = jnp.zeros((B, P, D), BF16)
    z2 = jnp.zeros((B, P, D), BF16)
    segs = [(S, 0), (C, S)]
    convs = []
    for L, off in segs:
        tabs = _dft_tables(L)
        kf = _hy_filters(L, tabs, fw1, fb1, fw2, fb2, fw3, fb3, fw4, ffreq, fdecay, D)
        convs.append((L, off, tabs, kf))
    for L, off, tabs, kf in convs:
        z1 = _hy_conv(geo, L, off, tabs, ub, 0, kf, 0, u, 1, fb, z1)
    for L, off, tabs, kf in convs:
        z2 = _hy_conv(geo, L, off, tabs, z1, 0, kf, 1, u, 2, fb, z2)
    return z2


def _rope_tables(S, D):
    t = jnp.arange(S)
    row = (t // GRID_W).astype(F32)
    col = (t % GRID_W).astype(F32)
    axis_dim = DA_HEAD_DIM // 2
    half = axis_dim // 2
    inv = ROPE_THETA ** (-jnp.arange(0, axis_dim, 2, dtype=F32) / axis_dim)
    lane = jnp.arange(D)
    within = lane % DA_HEAD_DIM
    pos = jnp.where((within // axis_dim)[None, :] == 0, row[:, None], col[:, None])
    ang = pos * inv[lane % half][None, :]
    sign = jnp.where((lane % axis_dim) < half, -1.0, 1.0)[None, :]
    return jnp.cos(ang), jnp.sin(ang) * sign


def _da_in_kernel(x_ref, sh_ref, sc_ref, w_ref, cos_ref, sin_ref, o_ref, *, n_lat):
    i = pl.program_id(1)
    j = pl.program_id(2)
    h = (x_ref[0] * (1.0 + sc_ref[...]) + sh_ref[...]).astype(BF16)
    acc = jnp.dot(h, w_ref[...], preferred_element_type=F32)
    acc = acc * jnp.where(j == 0, DA_HEAD_DIM ** -0.5, 1.0)
    rot = jnp.logical_and(i < n_lat, j < 2)

    @pl.when(rot)
    def _():
        Dn = acc.shape[-1]
        half = DA_HEAD_DIM // 4
        lane = lax.broadcasted_iota(jnp.int32, acc.shape, 1)
        up = pltpu.roll(acc, Dn - half, 1)
        dn = pltpu.roll(acc, half, 1)
        partner = jnp.where(jnp.bitwise_and(lane, 2 * half - 1) < half, up, dn)
        o_ref[0] = (acc * cos_ref[...] + partner * sin_ref[...]).astype(o_ref.dtype)

    @pl.when(jnp.logical_not(rot))
    def _():
        o_ref[0] = acc.astype(o_ref.dtype)


def _da_attn_kernel(*refs, kv_lo, nk, lam_init, aliased):
    if aliased:
        q_ref, k_ref, v_ref, lam_ref, sub_ref, _, o_ref, vx_scr, s_scr = refs
    else:
        q_ref, k_ref, v_ref, lam_ref, sub_ref, o_ref, vx_scr, s_scr = refs
    i = pl.program_id(2)
    HW = v_ref.shape[-1]
    TQ = q_ref.shape[1]
    lp = lam_ref[...]
    lam = (jnp.exp(jnp.sum(lp[0:1] * lp[1:2], axis=1, keepdims=True))
           - jnp.exp(jnp.sum(lp[2:3] * lp[3:4], axis=1, keepdims=True)) + lam_init)

    @pl.when(i == 0)
    def _():
        vx_scr[:, :HW] = v_ref[0]
        vx_scr[:, HW:] = jnp.ones((vx_scr.shape[0], HW), BF16)

    lane = lax.broadcasted_iota(jnp.int32, (TQ, HW), 1)
    q = q_ref[0]
    k = k_ref[0, kv_lo:kv_lo + nk, :]
    outs = []
    for c in range(2):
        qc = jnp.where((lane // DA_HEAD_DIM) == c, q, jnp.zeros_like(q))
        s_scr[...] = lax.dot_general(qc, k, (((1,), (1,)), ((), ())), preferred_element_type=F32)
        m = jnp.max(s_scr[...], axis=-1, keepdims=True)
        p = jnp.exp((s_scr[...] - m).astype(BF16))
        ov = jnp.dot(p, vx_scr[kv_lo:kv_lo + nk, :], preferred_element_type=F32)
        outs.append(ov[:, :HW] / ov[:, HW:HW + 1])
    o = outs[0] - lam * outs[1]
    o = o * lax.rsqrt(jnp.mean(o * o, axis=-1, keepdims=True) + 1e-5) * sub_ref[...] * (1.0 - lam_init)
    o_ref[0] = o.astype(o_ref.dtype)


def _diff_attention_mixer(geo, xs, modl, w_in, lam_p, subln_w, layer_idx):
    B, P, D, S, C = geo.B, geo.P, geo.D, geo.S, geo.C
    H = D // (2 * DA_HEAD_DIM)
    HW = 2 * DA_HEAD_DIM
    cos_t, sin_t = _rope_tables(S, D)
    n_lat = geo.n_lat
    tab_spec = pl.BlockSpec((geo.TM, D), lambda b, i, j: (jnp.minimum(i, n_lat - 1), 0))
    qkv = pl.pallas_call(
        functools.partial(_da_in_kernel, n_lat=n_lat), grid=(B, geo.nt, 3),
        in_specs=[geo.row_spec(D), geo.mod_spec(0), geo.mod_spec(1),
                  pl.BlockSpec((D, D), lambda b, i, j: (0, j)), tab_spec, tab_spec],
        out_specs=pl.BlockSpec((1, geo.TM, D), lambda b, i, j: (b, i, j)),
        out_shape=jax.ShapeDtypeStruct((B, P, 3 * D), BF16),
        compiler_params=_cp(3), name="da_in",
    )(xs, modl, modl, w_in.astype(BF16), cos_t, sin_t)
    lam_init = 0.8 - 0.6 * math.exp(-0.3 * layer_idx)
    def attend(TQ, row0, n_tiles, kv_lo, nk, prev):
        rb = row0 // TQ
        ins = [qkv, qkv, qkv, lam_p, subln_w.reshape(1, HW)]
        specs = [pl.BlockSpec((1, TQ, HW), lambda b, h, i: (b, rb + i, h)),
                 pl.BlockSpec((1, P, HW), lambda b, h, i: (b, 0, H + h)),
                 pl.BlockSpec((1, P, HW), lambda b, h, i: (b, 0, 2 * H + h)),
                 _full_spec((4, DA_HEAD_DIM)), _full_spec((1, HW))]
        if prev is not None:
            ins.append(prev)
            specs.append(pl.BlockSpec(memory_space=pl.ANY))
        return pl.pallas_call(
            functools.partial(_da_attn_kernel, kv_lo=kv_lo, nk=nk, lam_init=lam_init, aliased=prev is not None),
            grid=(B, H, n_tiles), in_specs=specs,
            out_specs=pl.BlockSpec((1, TQ, HW), lambda b, h, i: (b, rb + i, h)),
            out_shape=jax.ShapeDtypeStruct((B, P, D), BF16),
            scratch_shapes=[pltpu.VMEM((P, 2 * HW), BF16), pltpu.VMEM((TQ, nk), F32)],
            input_output_aliases={} if prev is None else {5: 0},
            compiler_params=_cp(3), name="da_attn",
        )(*ins)

    TQ = geo.TM
    while TQ < DA_Q_TILE and S % (2 * TQ) == 0:
        TQ *= 2
    y = attend(TQ, 0, S // TQ, 0, P, None)
    return attend(geo.TM, S, C // geo.TM, S, C, y)


def _s5_operators(a_re, a_im, log_step, b_re, b_im, c_re, c_im):
    T = S5_CHUNK
    G, Pst = a_re.shape[1], a_re.shape[2]
    Hg = S5_GROUP
    GL = LANES // Hg
    LB = G // GL
    lam = lax.complex(jnp.minimum(a_re.astype(F32), -1e-4), a_im.astype(F32))
    step = jnp.exp(log_step.astype(F32))[..., None]
    abar = jnp.exp(lam * step)
    bbar = ((abar - 1.0) / lam)[..., None] * lax.complex(b_re.astype(F32), b_im.astype(F32))
    cmat = lax.complex(c_re.astype(F32), c_im.astype(F32))
    pows = jnp.stack([abar ** l for l in range(T + 1)], axis=1)
    eye = jnp.eye(GL, dtype=F32)
    ar = jnp.arange(T)
    ops = []
    for d in range(2):
        pw = pows[d]
        kl = jnp.einsum('gjp,lgp,gph->lgjh', cmat[d], pw[:T], bbar[d]).real
        lag = (ar[None, :] - ar[:, None]) if d == 0 else (ar[:, None] - ar[None, :])
        tz = jnp.where((lag >= 0)[:, :, None, None, None], kl[jnp.clip(lag, 0, T - 1)], 0.0)
        tz = tz.reshape(T, T, LB, GL, Hg, Hg).astype(BF16)
        m = jnp.einsum('stbgjh,gk->bsghtkj', tz, eye.astype(BF16)).reshape(LB, T * LANES, T * LANES)
        e_in = (T - 1 - ar) if d == 0 else ar
        gc = pw[e_in][:, :, :, None] * bbar[d][None]
        gc = gc.reshape(T, LB, GL, Pst, Hg)
        eyeb = eye.astype(BF16)
        g_re = jnp.einsum('sbgph,gk->bsghkp', gc.real.astype(BF16), eyeb).reshape(LB, T * LANES, GL * Pst)
        g_im = jnp.einsum('sbgph,gk->bsghkp', gc.imag.astype(BF16), eyeb).reshape(LB, T * LANES, GL * Pst)
        e_out = (ar + 1) if d == 0 else (T - ar)
        hc = cmat[d][None] * pw[e_out][:, :, None, :]
        hc = hc.reshape(T, LB, GL, Hg, Pst)
        h_re = jnp.einsum('tbgjp,gk->bgptkj', hc.real.astype(BF16), eyeb).reshape(LB, GL * Pst, T * LANES)
        h_im = jnp.einsum('tbgjp,gk->bgptkj', (-hc.imag).astype(BF16), eyeb).reshape(LB, GL * Pst, T * LANES)
        at = pw[T].reshape(LB, 1, GL * Pst)
        ops.append((m, jnp.concatenate([g_re, g_im], axis=2), jnp.concatenate([h_re, h_im], axis=1),
                    jnp.concatenate([at.real, at.imag], axis=2).astype(F32)))
    return ops


def _modulate_kernel(x_ref, sh_ref, sc_ref, o_ref):
    o_ref[0] = x_ref[0] * (1.0 + sc_ref[...]) + sh_ref[...]


def _s5_kernel(u_ref, m_ref, g_ref, h_ref, a_ref, o_ref, gx_scr, sp_scr, *, S, C, reverse):
    T = S5_CHUNK
    P = S + C
    n = P // T
    n_lat = S // T
    NB = u_ref.shape[0]
    x = jnp.concatenate(
        [jnp.concatenate([u_ref[bi, pl.ds(s, n, stride=T), :] for s in range(T)], axis=1) for bi in range(NB)],
        axis=0).astype(BF16)
    gx_scr[...] = jnp.dot(x, g_ref[...], preferred_element_type=F32)
    ns = a_ref.shape[-1] // 2
    a_r = a_ref[:, :ns]
    a_i = a_ref[:, ns:]

    def scan(lo, cnt, rev, carry):
        def body(k, st):
            c = (lo + cnt - 1 - k) if rev else (lo + k)
            new = []
            for bi in range(NB):
                s_r, s_i = st[2 * bi], st[2 * bi + 1]
                sp_scr[pl.ds(bi * n + c, 1), :] = jnp.concatenate([s_r, s_i], axis=1)
                gx = gx_scr[pl.ds(bi * n + c, 1), :]
                new += [a_r * s_r - a_i * s_i + gx[:, :ns], a_r * s_i + a_i * s_r + gx[:, ns:]]
            return tuple(new)
        return lax.fori_loop(0, cnt, body, carry)

    zero = tuple(jnp.zeros((1, ns), F32) for _ in range(2 * NB))
    scan(0, n_lat, reverse, scan(n_lat, n - n_lat, reverse, zero))

    y = jnp.dot(x, m_ref[...], preferred_element_type=F32)
    y = y + jnp.dot(sp_scr[...].astype(BF16), h_ref[...], preferred_element_type=F32)
    for bi in range(NB):
        for s in range(T):
            o_ref[bi, pl.ds(s, n, stride=T), :] = y[bi * n:(bi + 1) * n, s * LANES:(s + 1) * LANES]


def _s5_mixer(geo, xs, modl, ops):
    B, P, D, S, C = geo.B, geo.P, geo.D, geo.S, geo.C
    u = pl.pallas_call(
        _modulate_kernel, grid=(B, geo.nt),
        in_specs=[geo.row_spec(D), geo.mod_spec(0), geo.mod_spec(1)],
        out_specs=geo.row_spec(D), out_shape=jax.ShapeDtypeStruct((B, P, D), F32),
        compiler_params=_cp(2), name="s5_modulate",
    )(xs, modl, modl)
    LB = D // LANES
    TL = S5_CHUNK * LANES
    NS = ops[0][1].shape[-1]
    n = P // S5_CHUNK
    NB = 2 if B % 2 == 0 else 1
    once = pl.Buffered(1)
    ys = []
    for d, (big_m, big_g, big_h, a_t) in enumerate(ops):
        ys.append(pl.pallas_call(
            functools.partial(_s5_kernel, S=S, C=C, reverse=d == 1), grid=(LB, B // NB),
            in_specs=[pl.BlockSpec((NB, P, LANES), lambda l, b: (b, 0, l)),
                      pl.BlockSpec((None, TL, TL), lambda l, b: (l, 0, 0), pipeline_mode=once),
                      pl.BlockSpec((None, TL, NS), lambda l, b: (l, 0, 0), pipeline_mode=once),
                      pl.BlockSpec((None, NS, TL), lambda l, b: (l, 0, 0), pipeline_mode=once),
                      pl.BlockSpec((None, 1, NS), lambda l, b: (l, 0, 0))],
            out_specs=pl.BlockSpec((NB, P, LANES), lambda l, b: (b, 0, l)),
            out_shape=jax.ShapeDtypeStruct((B, P, D), F32),
            scratch_shapes=[pltpu.VMEM((NB * n, NS), F32), pltpu.VMEM((NB * n, NS), F32)],
            compiler_params=_cp(2), name="s5_scan",
        )(u, big_m, big_g, big_h, a_t))
    return u, ys


def _s5_post_kernel(yf_ref, yb_ref, u_ref, d_ref, w_ref, b_ref, x_ref, gate_ref, lng_ref, lnb_ref,
                    sh_ref, sc_ref, wr_ref, xm_ref, f_ref, s_ref, *, alpha):
    g = jax.nn.gelu(yf_ref[0] + yb_ref[0] + d_ref[...] * u_ref[0])
    vg = jnp.dot(g.astype(BF16), w_ref[...], preferred_element_type=F32) + b_ref[...]
    Dn = vg.shape[-1] // 2
    y = vg[:, :Dn] * jax.nn.sigmoid(vg[:, Dn:])
    _finish_mixer(y, x_ref[0], gate_ref[...], lng_ref[...], lnb_ref[...], sh_ref[...], sc_ref[...],
                  wr_ref, xm_ref, f_ref, s_ref, alpha)


def _s5_post(geo, y2, u, d_skip, w_glu, b_glu, xs, modl, lng, lnb, w_router, alpha):
    D = geo.D
    E = w_router.shape[-1]
    out_specs, out_shape = _post_outs(geo, E)
    TM = geo.TM
    return pl.pallas_call(
        functools.partial(_s5_post_kernel, alpha=alpha), grid=(geo.B, geo.nt),
        in_specs=[geo.row_spec(D), geo.row_spec(D),
                  geo.row_spec(D), _full_spec((1, D)), _full_spec((D, 2 * D)), _full_spec((1, 2 * D)),
                  geo.row_spec(D), geo.mod_spec(2), _full_spec((1, D)), _full_spec((1, D)),
                  geo.mod_spec(3), geo.mod_spec(4), _full_spec((E, D))],
        out_specs=out_specs, out_shape=out_shape,
        compiler_params=_cp(2), name="s5_post",
    )(y2[0], y2[1], u, d_skip.reshape(1, D), w_glu.astype(BF16), b_glu.reshape(1, 2 * D), xs, modl,
      lng.reshape(1, D), lnb.reshape(1, D), modl, modl, w_router.T)


def _route_kernel(s_ref, b_ref, idx_ref, w_ref):
    sc = s_ref[0]
    E, TM = sc.shape
    biased = sc + b_ref[...]
    G = N_EXPERT_GROUPS
    per = E // G
    neg = -jnp.inf
    blocks, gs = [], []
    for g in range(G):
        blk = biased[g * per:(g + 1) * per]
        m1 = jnp.max(blk, axis=0, keepdims=True)
        is1 = blk == m1
        cnt = jnp.sum(is1.astype(F32), axis=0, keepdims=True)
        m2 = jnp.max(jnp.where(is1, neg, blk), axis=0, keepdims=True)
        blocks.append(blk)
        gs.append(m1 + jnp.where(cnt >= 2.0, m1, m2))
    masked = []
    for g in range(G):
        ahead = jnp.zeros((1, TM), F32)
        for h in range(G):
            if h < g:
                ahead = ahead + (gs[h] >= gs[g]).astype(F32)
            elif h > g:
                ahead = ahead + (gs[h] > gs[g]).astype(F32)
        masked.append(jnp.where(ahead < float(TOPK_GROUPS), blocks[g], neg))
    masked = jnp.concatenate(masked, axis=0)
    iota_e = lax.broadcasted_iota(jnp.int32, (E, TM), 0)
    idxs, ws = [], []
    for _ in range(TOP_K):
        m = jnp.max(masked, axis=0, keepdims=True)
        ik = jnp.min(jnp.where(masked == m, iota_e, E), axis=0, keepdims=True)
        sel = iota_e == ik
        ws.append(jnp.sum(jnp.where(sel, sc, 0.0), axis=0, keepdims=True))
        idxs.append(ik)
        masked = jnp.where(sel, neg, masked)
    tot = ws[0]
    for wk in ws[1:]:
        tot = tot + wk
    w = jnp.concatenate(ws, axis=0)
    idx_ref[0] = jnp.concatenate(idxs, axis=0)
    w_ref[0] = w / (tot + 1e-20) * ROUTED_SCALE


def _rank_kernel(idx_ref, rank_ref, cnt_ref, run_scr, *, E):
    first = jnp.logical_and(pl.program_id(0) == 0, pl.program_id(1) == 0)

    @pl.when(first)
    def _():
        run_scr[...] = jnp.zeros_like(run_scr)

    idx = idx_ref[0]
    K, TM = idx.shape
    iota_e = lax.broadcasted_iota(jnp.int32, (E, TM), 0)
    member = jnp.zeros((E, TM), F32)
    for k in range(K):
        member = member + (iota_e == idx[k:k + 1]).astype(F32)
    before = (lax.broadcasted_iota(jnp.int32, (TM, TM), 0)
              < lax.broadcasted_iota(jnp.int32, (TM, TM), 1)).astype(BF16)
    rank = jnp.dot(member.astype(BF16), before, preferred_element_type=F32) + run_scr[...]
    rows = [jnp.sum(jnp.where(iota_e == idx[k:k + 1], rank, 0.0), axis=0, keepdims=True) for k in range(K)]
    rank_ref[0] = jnp.concatenate(rows, axis=0).astype(jnp.int32)
    run_scr[...] = run_scr[...] + jnp.sum(member, axis=1, keepdims=True)
    cnt_ref[...] = run_scr[...]


def _dest_kernel(idx_ref, rank_ref, start_ref, dest_ref, *, E):
    idx = idx_ref[0]
    K, TM = idx.shape
    iota_e = lax.broadcasted_iota(jnp.int32, (E, TM), 0)
    start = start_ref[...]
    rows = [jnp.sum(jnp.where(iota_e == idx[k:k + 1], start, 0), axis=0, keepdims=True) for k in range(K)]
    dest_ref[0] = jnp.concatenate(rows, axis=0) + rank_ref[0]


def _route_dispatch(geo, scores, bias, blk):
    B, P = geo.B, geo.P
    E = scores.shape[1]
    K = TOP_K
    kspec = geo.col_spec(K)
    idx, w = pl.pallas_call(
        _route_kernel, grid=(B, geo.nt),
        in_specs=[geo.col_spec(E, whole=True), _full_spec((E, 1))],
        out_specs=[kspec, kspec],
        out_shape=[jax.ShapeDtypeStruct((B, K, P), jnp.int32), jax.ShapeDtypeStruct((B, K, P), F32)],
        compiler_params=_cp(2), name="moe_route",
    )(scores, bias.astype(F32).reshape(E, 1))
    rank, cnt = pl.pallas_call(
        functools.partial(_rank_kernel, E=E), grid=(B, geo.nt),
        in_specs=[kspec], out_specs=[kspec, _full_spec((E, 1))],
        out_shape=[jax.ShapeDtypeStruct((B, K, P), jnp.int32), jax.ShapeDtypeStruct((E, 1), F32)],
        scratch_shapes=[pltpu.VMEM((E, 1), F32)],
        compiler_params=_cp(2), name="moe_rank",
    )(idx)
    n_assign = B * P * K
    n_blocks = -(-(n_assign + E * (blk - 1)) // blk)
    counts = cnt[:, 0].astype(jnp.int32)
    pcounts = (counts + blk - 1) // blk * blk
    pends = jnp.cumsum(pcounts)
    starts = (pends - pcounts).astype(jnp.int32)
    n_used = pends[-1] // blk
    blk_e = jnp.minimum(jnp.searchsorted(pends, jnp.arange(n_blocks) * blk, side='right'), E - 1)
    blk_e = jnp.where(jnp.arange(n_blocks) < n_used, blk_e, blk_e[jnp.maximum(n_used - 1, 0)])
    dest = pl.pallas_call(
        functools.partial(_dest_kernel, E=E), grid=(B, geo.nt),
        in_specs=[kspec, kspec, _full_spec((E, 1))], out_specs=kspec,
        out_shape=jax.ShapeDtypeStruct((B, K, P), jnp.int32),
        compiler_params=_cp(2), name="moe_dest",
    )(idx, rank, starts.reshape(E, 1))
    return w, dest, blk_e.astype(jnp.int32), n_used.astype(jnp.int32).reshape(1), n_blocks


def _expert_kernel(be_ref, nu_ref, x_ref, wgu_ref, wd_ref, *rest):
    del be_ref
    o_ref = rest[-1]
    i = pl.program_id(0)

    @pl.when(i < nu_ref[0])
    def _():
        h = jnp.dot(x_ref[...], wgu_ref[...].astype(BF16), preferred_element_type=F32)
        Fh = h.shape[-1] // 2
        a = _silu(h[:, :Fh]) * h[:, Fh:]
        o_ref[...] = jnp.dot(a.astype(BF16), wd_ref[...].astype(BF16),
                             preferred_element_type=F32).astype(o_ref.dtype)

    @pl.when(i >= nu_ref[0])
    def _():
        o_ref[...] = jnp.zeros_like(o_ref)


def _expert_ffn(x_part, blk_e, n_used, w_gu, w_down, layer, blk, n_rows_all, blk0, prev):
    n_rows, D = x_part.shape
    F2 = w_gu.shape[-1]
    ins = [blk_e, n_used, x_part, w_gu, w_down]
    specs = [pl.BlockSpec((blk, D), lambda i, be, nu: (i, 0)),
             pl.BlockSpec((None, None, D, F2), lambda i, be, nu: (layer, be[i], 0, 0)),
             pl.BlockSpec((None, None, F2 // 2, D), lambda i, be, nu: (layer, be[i], 0, 0))]
    if prev is not None:
        ins.append(prev)
        specs.append(pl.BlockSpec(memory_space=pl.ANY))
    grid_spec = pltpu.PrefetchScalarGridSpec(
        num_scalar_prefetch=2, grid=(n_rows // blk,), in_specs=specs,
        out_specs=pl.BlockSpec((blk, D), lambda i, be, nu: (blk0 + i, 0)))
    return pl.pallas_call(
        _expert_kernel, grid_spec=grid_spec,
        out_shape=jax.ShapeDtypeStruct((n_rows_all, D), BF16),
        input_output_aliases={} if prev is None else {5: 0},
        compiler_params=_cp(1), name="moe_experts",
    )(*ins)


def _moe_final_kernel(*refs, alpha, aliased):
    if aliased:
        xm_ref, f_ref, ga_ref, w_ref, shgu_ref, shd_ref, gate_ref, lng_ref, lnb_ref, _, o_ref = refs
    else:
        xm_ref, f_ref, ga_ref, w_ref, shgu_ref, shd_ref, gate_ref, lng_ref, lnb_ref, o_ref = refs
    h = jnp.dot(f_ref[0], shgu_ref[...], preferred_element_type=F32)
    Fh = h.shape[-1] // 2
    a = _silu(h[:, :Fh]) * h[:, Fh:]
    y = jnp.dot(a.astype(BF16), shd_ref[...], preferred_element_type=F32)
    w = w_ref[0]
    for k in range(w.shape[-1]):
        y = y + w[:, k:k + 1] * ga_ref[k, 0].astype(F32)
    o_ref[0] = _layer_norm(alpha * xm_ref[0] + gate_ref[...] * y, lng_ref[...], lnb_ref[...])


def _moe_group(geo, xm, f, scores, modl, bias, w_gu, w_down, sh_gu, sh_down, lng, lnb, layer, alpha, blk, prev,
               tie=None, latent_only=False):
    B, P, D = geo.B, geo.P, geo.D
    T = B * P
    K = TOP_K
    w, dest, blk_e, n_used, n_blocks = _route_dispatch(geo, scores, bias, blk)
    dest_flat = jnp.swapaxes(dest, 0, 1).reshape(K * T)
    t0 = geo.b0 * P
    tok = jnp.broadcast_to(jnp.arange(t0, t0 + T, dtype=jnp.int32)[None], (K, T)).reshape(K * T)
    hit = jnp.zeros((n_blocks * blk,), jnp.int32).at[dest_flat].add(
        tok + 1, unique_indices=True, mode='promise_in_bounds')
    row_tok = jnp.where(hit > 0, hit - 1, t0 + jnp.arange(n_blocks * blk, dtype=jnp.int32) % T)
    f2d = f.reshape(geo.B_all * P, D)
    y_sorted = None
    cuts = [n_blocks * c // MOE_ROW_CUTS[-1] for c in MOE_ROW_CUTS]
    for lo, hi in zip(cuts[:-1], cuts[1:]):
        if hi == lo:
            continue
        x_part = f2d.at[row_tok[lo * blk:hi * blk]].get(mode='promise_in_bounds')
        used = jnp.clip(n_used - lo, 0, hi - lo)
        y_sorted = _expert_ffn(x_part, blk_e[lo:hi], used, w_gu, w_down, layer, blk, n_blocks * blk, lo, y_sorted)
    F2 = sh_gu.shape[-1]
    TM = geo.TM
    n_tiles, rows_out = (geo.n_lat, geo.S) if latent_only else (geo.nt, P)
    n_sub = MOE_COMBINE_GROUPS if B % MOE_COMBINE_GROUPS == 0 else 1
    nb = B // n_sub
    out = prev
    for g in range(n_sub):
        sub = geo.group(geo.b0 + g * nb, nb)
        dest_g = jnp.swapaxes(dest[g * nb:(g + 1) * nb], 0, 1).reshape(K * nb * P)
        gathered = y_sorted.at[dest_g].get(unique_indices=True, mode='promise_in_bounds').reshape(K, nb, P, D)
        if tie is not None and g == 0:
            gathered, tie = lax.optimization_barrier((gathered, tie))
        ins = [xm, f, gathered, jnp.swapaxes(w[g * nb:(g + 1) * nb], 1, 2), sh_gu.astype(BF16),
               sh_down.astype(BF16), modl, lng.reshape(1, D), lnb.reshape(1, D)]
        specs = [sub.row_spec(D, whole=True), sub.row_spec(D, whole=True),
                 pl.BlockSpec((K, 1, TM, D), lambda b, i: (0, b, i, 0)), sub.row_spec(K),
                 _full_spec((D, F2)), _full_spec((F2 // 2, D)), sub.mod_spec(5),
                 _full_spec((1, D)), _full_spec((1, D))]
        if out is not None:
            ins.append(out)
            specs.append(pl.BlockSpec(memory_space=pl.ANY))
        out = pl.pallas_call(
            functools.partial(_moe_final_kernel, alpha=alpha, aliased=out is not None), grid=(nb, n_tiles),
            in_specs=specs, out_specs=sub.row_spec(D, whole=True),
            out_shape=jax.ShapeDtypeStruct((geo.B_all, rows_out, D), F32),
            input_output_aliases={} if out is None else {9: 0},
            compiler_params=_cp(2), name="moe_final",
        )(*ins)
    return out, tie


def _moe(geo, xm, f, scores, modl, bias, w_gu, w_down, sh_gu, sh_down, lng, lnb, layer, alpha, blk, tie=None,
         latent_only=False):
    n_groups = MOE_GROUPS if geo.B % MOE_GROUPS == 0 else 1
    nb = geo.B // n_groups
    out = None
    for g in range(n_groups):
        out, tie = _moe_group(geo.group(g * nb, nb), xm, f, scores, modl, bias, w_gu, w_down, sh_gu, sh_down,
                              lng, lnb, layer, alpha, blk, out, tie, latent_only)
    return out, tie


def kernel(x, c, ctx, c_ctx, mod_w, mod_b, ln_g, ln_b, rg_w_in, rg_conv_w, rg_conv_b, rg_gate_w, rg_gate_b, rg_lam, rg_w_out, hy_w_in, hy_b_in, hy_short_w, hy_short_b, hy_f_w1, hy_f_b1, hy_f_w2, hy_f_b2, hy_f_w3, hy_f_b3, hy_f_w4, hy_f_freq, hy_f_decay, hy_f_bias, hy_w_out, hy_b_out, da_w_in, da_lam, da_subln, da_w_out, s5_a_re, s5_a_im, s5_log_step, s5_b_re, s5_b_im, s5_c_re, s5_c_im, s5_d, s5_w_glu, s5_b_glu, moe_w_router, moe_bias, moe_w_gu, moe_w_down, moe_sh_gu, moe_sh_down):
    B, S, D = x.shape
    C = ctx.shape[1]
    depth = mod_w.shape[0]
    alpha = (2 * depth) ** 0.25
    geo = _Geo(B, S, C, D)
    xs = jnp.concatenate([x, ctx], axis=1)
    R = -(-(B + 1) // SUBLANES) * SUBLANES
    cc = jnp.zeros((R, D), F32).at[:B].set(c).at[B].set(c_ctx)
    modt = _mod_table(cc, mod_w, mod_b).reshape(depth, 6, R, 1, D)
    blk = MOE_BLOCK
    s5_ops, ties = {}, {}
    for i in range(N_MIXERS - 1, depth, N_MIXERS):
        j = i // N_MIXERS
        s5_ops[i] = _s5_operators(s5_a_re[j], s5_a_im[j], s5_log_step[j], s5_b_re[j], s5_b_im[j],
                                  s5_c_re[j], s5_c_im[j])
        for d in range(2):
            ties[i - 2 + d] = (i, d)
    for i in range(depth):
        kind, j = i % N_MIXERS, i // N_MIXERS
        modl = modt[i]
        post = functools.partial(_post_mixer, geo, xs=xs, modl=modl, lng=ln_g[i, 0], lnb=ln_b[i, 0],
                                 w_router=moe_w_router[i], alpha=alpha)
        if kind == 0:
            y = _rglru_mixer(geo, xs, modl, rg_w_in[j], rg_conv_w[j], rg_conv_b[j], rg_gate_w[j],
                             rg_gate_b[j], rg_lam[j])
            xm, f, scores = post(y=y, w_out=rg_w_out[j], b_out=None)
        elif kind == 1:
            y = _hyena_mixer(geo, xs, modl, hy_w_in[j], hy_b_in[j], hy_short_w[j], hy_short_b[j],
                             hy_f_w1[j], hy_f_b1[j], hy_f_w2[j], hy_f_b2[j], hy_f_w3[j], hy_f_b3[j],
                             hy_f_w4[j], hy_f_freq[j], hy_f_decay[j], hy_f_bias[j])
            xm, f, scores = post(y=y, w_out=hy_w_out[j], b_out=hy_b_out[j])
        elif kind == 2:
            y = _diff_attention_mixer(geo, xs, modl, da_w_in[j], da_lam[j], da_subln[j], i)
            xm, f, scores = post(y=y, w_out=da_w_out[j], b_out=None)
        else:
            u, y2 = _s5_mixer(geo, xs, modl, s5_ops[i])
            xm, f, scores = _s5_post(geo, y2, u, s5_d[j], s5_w_glu[j], s5_b_glu[j], xs, modl,
                                     ln_g[i, 0], ln_b[i, 0], moe_w_router[i], alpha)
        tie = ties.get(i)
        xs, tied = _moe(geo, xm, f, scores, modl, moe_bias[i], moe_w_gu, moe_w_down, moe_sh_gu[i],
                        moe_sh_down[i], ln_g[i, 1], ln_b[i, 1], i, alpha, blk,
                        None if tie is None else s5_ops[tie[0]][tie[1]], latent_only=i == depth - 1)
        if tie is not None:
            s5_ops[tie[0]][tie[1]] = tied
    return xs
```
